```python
import jax, jax.numpy as jnp
from jax import lax
import numpy as np

D_MODEL = 1024
BATCH = 32
SEQ = 2048
DEPTH = 2

D_MIX = D_MODEL
HEAD_DIM = 64
A_HEADS = 6
A_DIM = A_HEADS * HEAD_DIM
IDX_HEADS = 4
IDX_DIM = 64
TOPK_MAX = 256
Q_BLOCK = 128
B_HEADS = 6
B_DK = 64
B_DV = 64
B_DIM = B_HEADS * B_DV
HGRN_CHUNK = 64
C_GROUPS = 4
C_DIM = D_MIX - A_DIM - B_DIM
C_GROUP_DIM = C_DIM // C_GROUPS
C_CHUNK = 128
ROPE_THETA = 10000.0
F_DENSE = 2816
N_EXPERTS = 8
TOP_K = 2
F_EXPERT = 3584
N_DENSE = (DEPTH + 1) // 2
N_MOE = DEPTH // 2
EPS = 1e-6
IN_WIDTHS = (A_DIM, HEAD_DIM, HEAD_DIM, IDX_HEADS * IDX_DIM, IDX_DIM, IDX_HEADS, B_HEADS * B_DK, B_HEADS * B_DK, B_HEADS * B_DV, B_HEADS * B_DV, 2 * C_DIM)
IN_DIM = sum(IN_WIDTHS)

kernel_name = "hybrid_dsa_hgrn2_gmlp_moe_block"


def rms_norm(x, g):
    xf = x.astype(jnp.float32)
    y = xf * lax.rsqrt(jnp.mean(xf * xf, axis=-1, keepdims=True) + EPS)
    return (y * g.astype(jnp.float32)).astype(x.dtype)


def layer_norm(x, g, b):
    xf = x.astype(jnp.float32)
    mu = jnp.mean(xf, axis=-1, keepdims=True)
    var = jnp.mean(jnp.square(xf - mu), axis=-1, keepdims=True)
    y = (xf - mu) * lax.rsqrt(var + EPS)
    return (y * g.astype(jnp.float32) + b.astype(jnp.float32)).astype(x.dtype)


def rope(x, pos):
    d = x.shape[-1]
    inv = ROPE_THETA ** (-jnp.arange(0, d, 2, dtype=jnp.float32) / d)
    ang = pos.astype(jnp.float32)[..., None] * inv
    cos = jnp.cos(ang)[:, :, None, :]
    sin = jnp.sin(ang)[:, :, None, :]
    xf = x.astype(jnp.float32)
    x1, x2 = xf[..., : d // 2], xf[..., d // 2:]
    return jnp.concatenate([x1 * cos - x2 * sin, x2 * cos + x1 * sin], axis=-1).astype(x.dtype)


def dsa_mixer(q, k, v, iq, ik, iw, positions):
    Bn, S = q.shape[0], q.shape[1]
    q = rope(q, positions)
    k = rope(k[:, :, None, :], positions)[:, :, 0, :]
    iq = rope(iq, positions)
    ik = rope(ik[:, :, None, :], positions)[:, :, 0, :]
    n_top = min(TOPK_MAX, S // 4)
    nb = S // Q_BLOCK
    to_blocks = lambda a: jnp.swapaxes(a.reshape((Bn, nb, Q_BLOCK) + a.shape[2:]), 0, 1)
    s_idx = jnp.arange(S)

    def one_block(args):
        qj, iqj, iwj, j = args
        t = j * Q_BLOCK + jnp.arange(Q_BLOCK)
        rel = jax.nn.relu(jnp.einsum('bqhd,bsd->bqhs', iqj, ik).astype(jnp.float32) * IDX_DIM ** -0.5)
        score = jnp.einsum('bqh,bqhs->bqs', iwj.astype(jnp.float32) * IDX_HEADS ** -0.5, rel)
        causal = s_idx[None, :] <= t[:, None]
        score = jnp.where(causal[None], score, -jnp.inf)
        _, idx = lax.top_k(score, n_top)
        flat = idx.reshape(Bn, Q_BLOCK * n_top, 1)
        kg = jnp.take_along_axis(k, flat, axis=1).reshape(Bn, Q_BLOCK, n_top, HEAD_DIM)
        vg = jnp.take_along_axis(v, flat, axis=1).reshape(Bn, Q_BLOCK, n_top, HEAD_DIM)
        logits = jnp.einsum('bqhd,bqkd->bqhk', qj, kg).astype(jnp.float32) * HEAD_DIM ** -0.5
        valid = idx <= t[None, :, None]
        logits = jnp.where(valid[:, :, None, :], logits, -jnp.inf)
        p = jax.nn.softmax(logits, axis=-1).astype(v.dtype)
        return jnp.einsum('bqhk,bqkd->bqhd', p, vg)

    out = lax.map(one_block, (to_blocks(q), to_blocks(iq), to_blocks(iw), jnp.arange(nb)))
    return jnp.swapaxes(out, 0, 1).reshape(Bn, S, A_HEADS * HEAD_DIM)


def hgrn2_mixer(q, f_logit, i, g, lb, onorm_g):
    Bn, S, H, dk = q.shape
    dv = i.shape[-1]
    lb = lb.reshape(H, dk).astype(jnp.float32)
    log_f = jnp.logaddexp(jnp.log(lb), jnp.log1p(-lb) + jax.nn.log_sigmoid(f_logit.astype(jnp.float32)))
    kk = -jnp.expm1(log_f)
    nc = S // HGRN_CHUNK
    chunk = lambda a: a.astype(jnp.float32).reshape(Bn, nc, HGRN_CHUNK, H, a.shape[-1]).transpose(1, 0, 3, 2, 4)
    qc, kc, vc, lfc = chunk(q), chunk(kk), chunk(i), chunk(log_f)
    bc = jnp.cumsum(lfc, axis=3)
    tril = jnp.arange(HGRN_CHUNK)[:, None] >= jnp.arange(HGRN_CHUNK)[None, :]

    def step(state, xs):
        qj, kj, vj, bj = xs
        diff = bj[:, :, :, None, :] - bj[:, :, None, :, :]
        decay = jnp.exp(jnp.where(tril[:, :, None], diff, -jnp.inf))
        attn = jnp.einsum('bhtd,bhsd,bhtsd->bhts', qj, kj, decay)
        o = jnp.einsum('bhts,bhsv->bhtv', attn, vj) + jnp.einsum('bhtd,bhdv->bhtv', qj * jnp.exp(bj), state)
        b_last = bj[:, :, -1:, :]
        new_state = jnp.exp(b_last[:, :, 0, :])[..., None] * state + jnp.einsum('bhsd,bhsv->bhdv', kj * jnp.exp(b_last - bj), vj)
        return new_state, o

    s0 = jnp.zeros((Bn, H, dk, dv), jnp.float32)
    _, o = lax.scan(step, s0, (qc, kc, vc, bc))
    o = o.transpose(1, 0, 3, 2, 4).reshape(Bn, S, H, dv)
    o = rms_norm(o, onorm_g) * jax.nn.silu(g.astype(jnp.float32))
    return o.reshape(Bn, S, H * dv).astype(q.dtype)


def gmlp_mixer(uv, vn_g, vn_b, ws, bs):
    Bn, S = uv.shape[0], uv.shape[1]
    uv = jax.nn.gelu(uv, approximate=False)
    u, v = uv[..., :C_DIM], uv[..., C_DIM:]
    v = layer_norm(v, vn_g, vn_b)
    nc = S // C_CHUNK
    v = v.reshape(Bn, nc, C_CHUNK, C_GROUPS, C_GROUP_DIM)
    mask = jnp.tril(jnp.ones((C_CHUNK, C_CHUNK), ws.dtype))
    w = ws * mask[None]
    mixed = jnp.einsum('gts,bnsgd->bntgd', w, v) + bs.T[None, None, :, :, None]
    return u * mixed.reshape(Bn, S, C_DIM)


def swiglu(h, w_gate, w_up, w_down):
    return (jax.nn.silu(h @ w_gate) * (h @ w_up)) @ w_down


def moe_swiglu(h, w_router, w_gate, w_up, w_down):
    logits = jnp.einsum('bsd,de->bse', h, w_router).astype(jnp.float32)
    top_v, top_i = lax.top_k(logits, TOP_K)
    wts = jax.nn.softmax(top_v, axis=-1)
    gates = jnp.sum(jax.nn.one_hot(top_i, N_EXPERTS, dtype=jnp.float32) * wts[..., None], axis=-2)

    def per_seq(args):
        hb, gb = args
        a = jnp.einsum('sd,edf->sef', hb, w_gate)
        u = jnp.einsum('sd,edf->sef', hb, w_up)
        act = jax.nn.silu(a) * u * gb[:, :, None]
        return jnp.einsum('sef,efd->sd', act, w_down)

    return lax.map(per_seq, (h, gates.astype(h.dtype)))


def setup_inputs(seed: int = 0) -> dict:
    key = jax.random.key(seed)
    ks = jax.random.split(key, 24)
    nrm = lambda k, shape, s: jax.random.normal(k, shape, jnp.float32) * s
    x = nrm(ks[0], (BATCH, SEQ, D_MODEL), 1.0)
    c = nrm(ks[1], (BATCH, D_MODEL), 1.0)
    offsets = jax.random.randint(ks[2], (BATCH, 1), 0, 4096, dtype=jnp.int32)
    positions = offsets + jnp.arange(SEQ, dtype=jnp.int32)[None, :]
    return {
        "x": x,
        "c": c,
        "positions": positions,
        "w_ada": nrm(ks[3], (DEPTH, D_MODEL, 6 * D_MODEL), 0.5 * D_MODEL ** -0.5),
        "b_ada": nrm(ks[4], (DEPTH, 6 * D_MODEL), 0.02),
        "g_norm_mix": 1.0 + nrm(ks[5], (DEPTH, D_MODEL), 0.02),
        "g_norm_ffn": 1.0 + nrm(ks[6], (DEPTH, D_MODEL), 0.02),
        "w_in": nrm(ks[7], (DEPTH, D_MODEL, IN_DIM), D_MODEL ** -0.5),
        "w_out": nrm(ks[8], (DEPTH, D_MIX, D_MODEL), D_MIX ** -0.5),
        "hgrn_lb_logits": nrm(ks[9], (DEPTH, B_HEADS * B_DK), 0.5),
        "hgrn_out_norm": 1.0 + nrm(ks[10], (DEPTH, B_DV), 0.02),
        "gmlp_vnorm_g": 1.0 + nrm(ks[11], (DEPTH, C_DIM), 0.02),
        "gmlp_vnorm_b": nrm(ks[12], (DEPTH, C_DIM), 0.02),
        "gmlp_w_s": nrm(ks[13], (DEPTH, C_GROUPS, C_CHUNK, C_CHUNK), C_CHUNK ** -0.5),
        "gmlp_b_s": 1.0 + nrm(ks[14], (DEPTH, C_GROUPS, C_CHUNK), 0.02),
        "ffn_w_gate": nrm(ks[15], (N_DENSE, D_MODEL, F_DENSE), D_MODEL ** -0.5),
        "ffn_w_up": nrm(ks[16], (N_DENSE, D_MODEL, F_DENSE), D_MODEL ** -0.5),
        "ffn_w_down": nrm(ks[17], (N_DENSE, F_DENSE, D_MODEL), F_DENSE ** -0.5),
        "moe_w_router": nrm(ks[18], (N_MOE, D_MODEL, N_EXPERTS), D_MODEL ** -0.5),
        "moe_w_gate": nrm(ks[19], (N_MOE, N_EXPERTS, D_MODEL, F_EXPERT), D_MODEL ** -0.5),
        "moe_w_up": nrm(ks[20], (N_MOE, N_EXPERTS, D_MODEL, F_EXPERT), D_MODEL ** -0.5),
        "moe_w_down": nrm(ks[21], (N_MOE, N_EXPERTS, F_EXPERT, D_MODEL), F_EXPERT ** -0.5),
        "g_final": 1.0 + nrm(ks[22], (D_MODEL,), 0.02),
    }


def reference(x, c, positions, w_ada, b_ada, g_norm_mix, g_norm_ffn, w_in, w_out, hgrn_lb_logits, hgrn_out_norm, gmlp_vnorm_g, gmlp_vnorm_b, gmlp_w_s, gmlp_b_s, ffn_w_gate, ffn_w_up, ffn_w_down, moe_w_router, moe_w_gate, moe_w_up, moe_w_down, g_final):
    Bn, S = x.shape[0], x.shape[1]
    p_lb = jax.nn.softmax(hgrn_lb_logits.astype(jnp.float32), axis=0)
    cum = jnp.cumsum(p_lb, axis=0)
    lower_bounds = cum - cum[0:1]
    splits = [int(s) for s in np.cumsum(IN_WIDTHS)[:-1]]
    cond = jax.nn.silu(c)
    for l in range(DEPTH):
        mod = cond @ w_ada[l] + b_ada[l]
        sh1, sc1, g1, sh2, sc2, g2 = [m[:, None, :] for m in jnp.split(mod, 6, axis=-1)]
        h = rms_norm(x, g_norm_mix[l]) * (1 + sc1) + sh1
        z = h @ w_in[l]
        aq, ak, av, iq, ik, iw, bq, bf, bi, bg, cuv = jnp.split(z, splits, axis=-1)
        a_out = dsa_mixer(aq.reshape(Bn, S, A_HEADS, HEAD_DIM), ak, av, iq.reshape(Bn, S, IDX_HEADS, IDX_DIM), ik, iw, positions)
        b_out = hgrn2_mixer(bq.reshape(Bn, S, B_HEADS, B_DK), bf.reshape(Bn, S, B_HEADS, B_DK), bi.reshape(Bn, S, B_HEADS, B_DV), bg.reshape(Bn, S, B_HEADS, B_DV), lower_bounds[l], hgrn_out_norm[l])
        c_out = gmlp_mixer(cuv, gmlp_vnorm_g[l], gmlp_vnorm_b[l], gmlp_w_s[l], gmlp_b_s[l])
        mix = jnp.concatenate([a_out, b_out, c_out], axis=-1) @ w_out[l]
        x = x + g1 * mix
        h = rms_norm(x, g_norm_ffn[l]) * (1 + sc2) + sh2
        if l % 2 == 0:
            y = swiglu(h, ffn_w_gate[l // 2], ffn_w_up[l // 2], ffn_w_down[l // 2])
        else:
            y = moe_swiglu(h, moe_w_router[l // 2], moe_w_gate[l // 2], moe_w_up[l // 2], moe_w_down[l // 2])
        x = x + g2 * y
    return rms_norm(x, g_final)
```

```python
import functools

import numpy as np
import jax
import jax.numpy as jnp
from jax import lax
from jax.experimental import pallas as pl
from jax.experimental.pallas import tpu as pltpu

F32 = jnp.float32
BF16 = jnp.bfloat16
I32 = jnp.int32

HEAD_DIM = 64
A_HEADS = 6
A_DIM = A_HEADS * HEAD_DIM
IDX_HEADS = 4
IDX_DIM = 64
TOPK_MAX = 256
B_HEADS = 6
B_DK = 64
B_DV = 64
B_DIM = B_HEADS * B_DV
HGRN_CHUNK = 64
C_GROUPS = 4
C_GROUP_DIM = 64
C_DIM = C_GROUPS * C_GROUP_DIM
C_CHUNK = 128
ROPE_THETA = 10000.0
N_EXPERTS = 8
EPS = 1e-6
IN_WIDTHS = (A_DIM, HEAD_DIM, HEAD_DIM, IDX_HEADS * IDX_DIM, IDX_DIM, IDX_HEADS,
             B_DIM, B_DIM, B_DIM, B_DIM, 2 * C_DIM)

LANES = 128
QB = 128
VMEM_LIMIT = 48 * 1024 * 1024
HGRN_FAST_SPAN = 60.0
INT_MIN = -2 ** 31

NT_DIMS = (((1,), (1,)), ((), ()))
TN_DIMS = (((0,), (0,)), ((), ()))


def _cparams(*sem):
    return pltpu.CompilerParams(dimension_semantics=sem, vmem_limit_bytes=VMEM_LIMIT)


def _split2(x):
    hi = x.astype(BF16)
    lo = (x - hi.astype(F32)).astype(BF16)
    return hi, lo


def _split3(x):
    hi = x.astype(BF16)
    r1 = x - hi.astype(F32)
    mid = r1.astype(BF16)
    lo = (r1 - mid.astype(F32)).astype(BF16)
    return hi, mid, lo


def _silu(x):
    return x * (1.0 / (1.0 + jnp.exp(-x)))


def _modulated_rmsnorm(x, g, sc, sh):
    var = jnp.mean(x * x, axis=-1, keepdims=True)
    y = x * lax.rsqrt(var + EPS)
    return (y * g) * (1.0 + sc) + sh


def _ada_kernel(c_ref, w_ref, b_ref, o_ref):
    cond = _silu(c_ref[...])
    o_ref[0] = jnp.dot(cond, w_ref[0], precision=lax.Precision.HIGHEST,
                       preferred_element_type=F32) + b_ref[0]


def _ada_mod(c, w_ada, b_ada):
    depth, d, d6 = w_ada.shape
    bn = c.shape[0]
    tn = 1536
    return pl.pallas_call(
        _ada_kernel,
        grid=(depth, d6 // tn),
        in_specs=[pl.BlockSpec((bn, d), lambda l, n: (0, 0)),
                  pl.BlockSpec((1, d, tn), lambda l, n: (l, 0, n)),
                  pl.BlockSpec((1, 1, tn), lambda l, n: (l, 0, n))],
        out_specs=pl.BlockSpec((1, bn, tn), lambda l, n: (l, 0, n)),
        out_shape=jax.ShapeDtypeStruct((depth, bn, d6), F32),
        compiler_params=_cparams("parallel", "parallel"),
    )(c, w_ada, b_ada.reshape(depth, 1, d6))


def _rope_kernel(posn_ref, post_ref, invn_ref, signn_ref, invt_ref,
                 cosn_ref, sinn_ref, cost_ref, sint_ref):
    ang_n = posn_ref[0].astype(F32) * invn_ref[...]
    cosn_ref[0] = jnp.cos(ang_n)
    sinn_ref[0] = jnp.sin(ang_n) * signn_ref[...]
    ang_t = invt_ref[...] * post_ref[0].astype(F32)
    cost_ref[0] = jnp.cos(ang_t)
    sint_ref[0] = jnp.sin(ang_t)


def _rope_tables(positions):
    bn, s = positions.shape
    half = HEAD_DIM // 2
    inv = ROPE_THETA ** (-jnp.arange(0, HEAD_DIM, 2, dtype=F32) / HEAD_DIM)
    inv_n = jnp.tile(inv, LANES // half).reshape(1, LANES)
    sign_n = jnp.tile(jnp.concatenate([-jnp.ones((half,), F32), jnp.ones((half,), F32)]),
                      LANES // HEAD_DIM).reshape(1, LANES)
    inv_t = inv.reshape(half, 1)
    full = lambda shape: pl.BlockSpec(shape, lambda b: (0,) * len(shape))
    return pl.pallas_call(
        _rope_kernel,
        grid=(bn,),
        in_specs=[pl.BlockSpec((1, s, 1), lambda b: (b, 0, 0)),
                  pl.BlockSpec((1, 1, s), lambda b: (b, 0, 0)),
                  full((1, LANES)), full((1, LANES)), full((half, 1))],
        out_specs=[pl.BlockSpec((1, s, LANES), lambda b: (b, 0, 0)),
                   pl.BlockSpec((1, s, LANES), lambda b: (b, 0, 0)),
                   pl.BlockSpec((1, half, s), lambda b: (b, 0, 0)),
                   pl.BlockSpec((1, half, s), lambda b: (b, 0, 0))],
        out_shape=[jax.ShapeDtypeStruct((bn, s, LANES), F32),
                   jax.ShapeDtypeStruct((bn, s, LANES), F32),
                   jax.ShapeDtypeStruct((bn, half, s), F32),
                   jax.ShapeDtypeStruct((bn, half, s), F32)],
        compiler_params=_cparams("parallel"),
    )(positions.reshape(bn, s, 1), positions.reshape(bn, 1, s), inv_n, sign_n, inv_t)


N_KZ = 0
N_IK = 2 * LANES
N_BQ = 4 * LANES
N_BF = N_BQ + B_DIM
N_BI = N_BF + B_DIM
N_BG = N_BI + B_DIM
N_CUV = N_BG + B_DIM
N_COLS = N_CUV + 2 * C_DIM
T_Q = 0
T_IQ = A_DIM
T_V = T_IQ + IDX_HEADS * IDX_DIM
T_IW = T_V + HEAD_DIM
T_ROWS = T_IW + 8


def _prep_in_weights(w_in_l):
    offs = np.concatenate([[0], np.cumsum(IN_WIDTHS)])
    sl = lambda i: w_in_l[:, int(offs[i]):int(offs[i + 1])]
    aq, ak, av, iq, ik, iw, bq, bf, bi, bg, cuv = [sl(i) for i in range(11)]
    half = HEAD_DIM // 2
    rot = lambda w: jnp.concatenate([w[:, half:], w[:, :half]], axis=1)
    z = jnp.zeros_like(ak)
    w_n = jnp.concatenate([ak, z, rot(ak), z, ik, ik, rot(ik), rot(ik), bq, bf, bi, bg, cuv], axis=1)
    w_t = jnp.concatenate([aq, iq, av, iw, jnp.zeros((w_in_l.shape[0], 4), w_in_l.dtype)], axis=1).T
    return w_n.astype(BF16), w_t.astype(BF16)


def _inproj_kernel(x_ref, sh_ref, sc_ref, g_ref, wn_ref, wt_ref, cosn_ref, sinn_ref, cost_ref, sint_ref,
                   kz_ref, ik3_ref, bq_ref, bf_ref, bi_ref, bg_ref, cuv_ref,
                   qt_ref, iq3t_ref, vt_ref, iwt_ref):
    tm = x_ref.shape[1]
    half = HEAD_DIM // 2
    h = _modulated_rmsnorm(x_ref[0], g_ref[...], sc_ref[0], sh_ref[0]).astype(BF16)

    cosn = cosn_ref[0]
    sinn = sinn_ref[0]
    zk = jnp.dot(h, wn_ref[:, N_KZ:N_KZ + 2 * LANES], preferred_element_type=F32)
    kz_ref[0] = (zk[:, :LANES] * cosn + zk[:, LANES:] * sinn).astype(BF16)
    zi = jnp.dot(h, wn_ref[:, N_IK:N_IK + 2 * LANES], preferred_element_type=F32)
    ik2 = zi[:, :LANES] * cosn + zi[:, LANES:] * sinn
    hi, lo = _split2(ik2)
    lane = lax.broadcasted_iota(I32, (tm, LANES), 1)
    ik3_ref[0, :, :LANES] = jnp.where(lane < IDX_DIM, hi, lo)
    ik3_ref[0, :, LANES:] = hi
    bq_ref[0] = jnp.dot(h, wn_ref[:, N_BQ:N_BF], preferred_element_type=F32).astype(BF16)
    bf_ref[0] = jnp.dot(h, wn_ref[:, N_BF:N_BI], preferred_element_type=F32)
    bi_ref[0] = jnp.dot(h, wn_ref[:, N_BI:N_BG], preferred_element_type=F32).astype(BF16)
    bg_ref[0] = jnp.dot(h, wn_ref[:, N_BG:N_CUV], preferred_element_type=F32).astype(BF16)
    cuv_ref[0] = jnp.dot(h, wn_ref[:, N_CUV:N_COLS], preferred_element_type=F32).astype(BF16)

    zt = lax.dot_general(wt_ref[...], h, NT_DIMS, preferred_element_type=F32)
    cost = cost_ref[0]
    sint = sint_ref[0]
    qscale = HEAD_DIM ** -0.5
    for hh in range(A_HEADS):
        r0 = T_Q + hh * HEAD_DIM
        x1 = zt[r0:r0 + half]
        x2 = zt[r0 + half:r0 + HEAD_DIM]
        qt_ref[0, hh * HEAD_DIM:hh * HEAD_DIM + half] = ((x1 * cost - x2 * sint) * qscale).astype(BF16)
        qt_ref[0, hh * HEAD_DIM + half:(hh + 1) * HEAD_DIM] = ((x2 * cost + x1 * sint) * qscale).astype(BF16)
    iscale = IDX_DIM ** -0.5
    zero = jnp.zeros((IDX_DIM, tm), BF16)
    for hh in range(IDX_HEADS):
        r0 = T_IQ + hh * IDX_DIM
        x1 = zt[r0:r0 + half]
        x2 = zt[r0 + half:r0 + IDX_DIM]
        y = jnp.concatenate([(x1 * cost - x2 * sint) * iscale, (x2 * cost + x1 * sint) * iscale], axis=0)
        hi, lo = _split2(y)
        iq3t_ref[0, hh, 0 * IDX_DIM:1 * IDX_DIM] = hi
        iq3t_ref[0, hh, 1 * IDX_DIM:2 * IDX_DIM] = hi
        iq3t_ref[0, hh, 2 * IDX_DIM:3 * IDX_DIM] = lo
        iq3t_ref[0, hh, 3 * IDX_DIM:4 * IDX_DIM] = zero
    vt = zt[T_V:T_V + HEAD_DIM].astype(BF16)
    for i in range(tm // QB):
        vt_ref[0, i] = vt[:, i * QB:(i + 1) * QB]
    iwt_ref[0] = zt[T_IW:T_IW + 8] * (IDX_HEADS ** -0.5)


def _inproj(x, sh, sc, g, w_n, w_t, cosn, sinn, cost, sint, tm):
    bn, s, d = x.shape
    nt = s // tm
    half = HEAD_DIM // 2
    tok = lambda w: pl.BlockSpec((1, tm, w), lambda b, t: (b, t, 0))
    vec = pl.BlockSpec((1, 1, d), lambda b, t: (b, 0, 0))
    full2 = lambda a: pl.BlockSpec(a.shape, lambda b, t: (0, 0))
    out_shapes = [
        jax.ShapeDtypeStruct((bn, s, LANES), BF16),
        jax.ShapeDtypeStruct((bn, s, 2 * LANES), BF16),
        jax.ShapeDtypeStruct((bn, s, B_DIM), BF16),
        jax.ShapeDtypeStruct((bn, s, B_DIM), F32),
        jax.ShapeDtypeStruct((bn, s, B_DIM), BF16),
        jax.ShapeDtypeStruct((bn, s, B_DIM), BF16),
        jax.ShapeDtypeStruct((bn, s, 2 * C_DIM), BF16),
        jax.ShapeDtypeStruct((bn, A_DIM, s), BF16),
        jax.ShapeDtypeStruct((bn, IDX_HEADS, 4 * IDX_DIM, s), BF16),
        jax.ShapeDtypeStruct((bn, s // QB, HEAD_DIM, QB), BF16),
        jax.ShapeDtypeStruct((bn, 8, s), F32),
    ]
    out_specs = [
        tok(LANES), tok(2 * LANES), tok(B_DIM), tok(B_DIM), tok(B_DIM), tok(B_DIM), tok(2 * C_DIM),
        pl.BlockSpec((1, A_DIM, tm), lambda b, t: (b, 0, t)),
        pl.BlockSpec((1, IDX_HEADS, 4 * IDX_DIM, tm), lambda b, t: (b, 0, 0, t)),
        pl.BlockSpec((1, tm // QB, HEAD_DIM, QB), lambda b, t: (b, t, 0, 0)),
        pl.BlockSpec((1, 8, tm), lambda b, t: (b, 0, t)),
    ]
    return pl.pallas_call(
        _inproj_kernel,
        grid=(bn, nt),
        in_specs=[tok(d), vec, vec, full2(g), full2(w_n), full2(w_t),
                  tok(LANES), tok(LANES),
                  pl.BlockSpec((1, half, tm), lambda b, t: (b, 0, t)),
                  pl.BlockSpec((1, half, tm), lambda b, t: (b, 0, t))],
        out_specs=out_specs,
        out_shape=out_shapes,
        compiler_params=_cparams("parallel", "parallel"),
    )(x, sh, sc, g, w_n, w_t, cosn, sinn, cost, sint)


def _dsa_kernel(n_top, ik3_ref, kz_ref, vt_ref, iq3t_ref, iwt_ref, qt_ref, out_ref, key_scr, lg_scr):
    j = pl.program_id(1)
    nkc = j + 1
    row = lax.broadcasted_iota(I32, (QB, QB), 0)
    col = lax.broadcasted_iota(I32, (QB, QB), 1)
    int_min = jnp.int32(INT_MIN)
    sub = QB // 8

    def score_body(c, carry):
        ks = pl.multiple_of(c * QB, QB)
        ikc = ik3_ref[0, pl.ds(ks, QB), :]
        sc = jnp.zeros((QB, QB), F32)
        for hh in range(IDX_HEADS):
            rel = jnp.dot(ikc, iq3t_ref[0, hh], preferred_element_type=F32)
            sc = sc + jnp.maximum(rel, 0.0) * iwt_ref[0, hh:hh + 1, :]
        sc = jnp.where(sc == 0.0, 0.0, sc)
        bits = pltpu.bitcast(sc, I32)
        key = bits ^ ((bits >> 31) & jnp.int32(0x7FFFFFFF))
        causal = (ks + row) <= (j * QB + col)
        key_scr[pl.ds(ks, QB), :] = jnp.where(causal, key, int_min)
        return carry

    lax.fori_loop(0, nkc, score_body, 0)

    def count(pred_fn):
        def body(c, acc):
            ks = pl.multiple_of(c * QB, QB)
            m = jnp.where(pred_fn(key_scr[pl.ds(ks, QB), :]), 1, 0).astype(I32)
            return acc + jnp.sum(m.reshape(sub, 8, QB), axis=0)
        acc = lax.fori_loop(0, nkc, body, jnp.zeros((8, QB), I32))
        return jnp.sum(acc, axis=0, keepdims=True)

    def bit_body(i, t_u):
        cand = t_u | jnp.left_shift(jnp.int32(1), 31 - i)
        cand_s = cand ^ int_min
        cnt = count(lambda k: k >= cand_s)
        return jnp.where(cnt >= n_top, cand, t_u)

    t_u = lax.fori_loop(0, 32, bit_body, jnp.zeros((1, QB), I32))
    thr = t_u ^ int_min
    n_gt = count(lambda k: k > thr)
    n_tie = (n_top - n_gt).astype(F32)
    thr_valid = jnp.where(thr > int_min, 1.0, 0.0)

    ltri = jnp.where(row >= col, 1.0, 0.0).astype(BF16)
    zpad = jnp.zeros((HEAD_DIM, QB), BF16)
    qpad = [jnp.concatenate([qt_ref[0, hh * HEAD_DIM:(hh + 1) * HEAD_DIM, :], zpad], axis=0)
            for hh in range(A_HEADS)]
    neg_inf = jnp.float32(-jnp.inf)

    def pass_a(c, carry):
        tie_cnt, ms = carry
        ks = pl.multiple_of(c * QB, QB)
        kc = key_scr[pl.ds(ks, QB), :]
        eqf = jnp.where(kc == thr, thr_valid, 0.0)
        pref = jnp.dot(ltri, eqf.astype(BF16), preferred_element_type=F32)
        keep_tie = eqf * jnp.where((tie_cnt + pref) <= n_tie, 1.0, 0.0)
        sel = jnp.where(kc > thr, 1.0, keep_tie) > 0.5
        tie_cnt = tie_cnt + pref[QB - 1:QB, :]
        kzc = kz_ref[0, pl.ds(ks, QB), :]
        new_ms = []
        for hh in range(A_HEADS):
            lt = jnp.dot(kzc, qpad[hh], preferred_element_type=F32)
            lt = jnp.where(sel, lt, neg_inf)
            lg_scr[hh, pl.ds(ks, QB), :] = lt
            new_ms.append(jnp.maximum(ms[hh], jnp.max(lt, axis=0, keepdims=True)))
        return tie_cnt, tuple(new_ms)

    init_m = tuple(jnp.full((1, QB), neg_inf, F32) for _ in range(A_HEADS))
    _, ms = lax.fori_loop(0, nkc, pass_a, (jnp.zeros((1, QB), F32), init_m))

    def pass_b(c, carry):
        ls, accs = carry
        ks = pl.multiple_of(c * QB, QB)
        vtc = vt_ref[0, c]
        new_ls, new_accs = [], []
        for hh in range(A_HEADS):
            p = jnp.exp(lg_scr[hh, pl.ds(ks, QB), :] - ms[hh])
            new_ls.append(ls[hh] + jnp.sum(p, axis=0, keepdims=True))
            new_accs.append(accs[hh] + jnp.dot(vtc, p.astype(BF16), preferred_element_type=F32))
        return tuple(new_ls), tuple(new_accs)

    init_l = tuple(jnp.zeros((1, QB), F32) for _ in range(A_HEADS))
    init_acc = tuple(jnp.zeros((HEAD_DIM, QB), F32) for _ in range(A_HEADS))
    ls, accs = lax.fori_loop(0, nkc, pass_b, (init_l, init_acc))
    o_t = jnp.concatenate([accs[hh] / ls[hh] for hh in range(A_HEADS)], axis=0)
    out_ref[0] = o_t.T.astype(BF16)


def _dsa(ik3, kz, vt4, iq3t, iwt, qt):
    bn, s, _ = kz.shape
    n_top = min(TOPK_MAX, s // 4)
    nq = s // QB
    return pl.pallas_call(
        functools.partial(_dsa_kernel, n_top),
        grid=(bn, nq),
        in_specs=[pl.BlockSpec((1, s, 2 * LANES), lambda b, q: (b, 0, 0)),
                  pl.BlockSpec((1, s, LANES), lambda b, q: (b, 0, 0)),
                  pl.BlockSpec((1, nq, HEAD_DIM, QB), lambda b, q: (b, 0, 0, 0)),
                  pl.BlockSpec((1, IDX_HEADS, 4 * IDX_DIM, QB), lambda b, q: (b, 0, 0, q)),
                  pl.BlockSpec((1, 8, QB), lambda b, q: (b, 0, q)),
                  pl.BlockSpec((1, A_DIM, QB), lambda b, q: (b, 0, q))],
        out_specs=pl.BlockSpec((1, QB, A_DIM), lambda b, q: (b, q, 0)),
        out_shape=jax.ShapeDtypeStruct((bn, s, A_DIM), BF16),
        scratch_shapes=[pltpu.VMEM((s, QB), I32), pltpu.VMEM((A_HEADS, s, QB), F32)],
        compiler_params=_cparams("parallel", "arbitrary"),
    )(ik3, kz, vt4, iq3t, iwt, qt)


def _hgrn_kernel(q_ref, f_ref, i_ref, g_ref, lb_ref, gn_ref, o_ref, st_scr, oi_scr):
    s = q_ref.shape[1]
    ch = HGRN_CHUNK
    nchunks = s // ch
    lane = lax.broadcasted_iota(I32, (ch, LANES), 1)
    head0 = lane < B_DK
    r_t = lax.broadcasted_iota(I32, (ch, ch), 0)
    r_s = lax.broadcasted_iota(I32, (ch, ch), 1)
    tril = r_t >= r_s
    ltri = jnp.where(tril, 1.0, 0.0).astype(BF16)
    bd_r = lax.broadcasted_iota(I32, (LANES, LANES), 0)
    bd_c = lax.broadcasted_iota(I32, (LANES, LANES), 1)
    same_head = (bd_r < B_DK) == (bd_c < B_DK)
    ones_bd = jnp.where(same_head, 1.0, 0.0).astype(BF16)
    s_iota = lax.broadcasted_iota(I32, (ch, LANES), 0)

    lb = lb_ref[...]
    log_lb = jnp.log(lb)
    log_1mlb = jnp.log1p(-lb)
    gn = gn_ref[...]
    st_scr[...] = jnp.zeros_like(st_scr)

    def chunk_body(c, carry):
        t0 = pl.multiple_of(c * ch, ch)
        z = f_ref[0, pl.ds(t0, ch), :]
        q = q_ref[0, pl.ds(t0, ch), :].astype(F32)
        v = i_ref[0, pl.ds(t0, ch), :]
        softplus_tail = jnp.log1p(jnp.exp(-jnp.abs(z)))
        log_sig = -(jnp.maximum(-z, 0.0) + softplus_tail)
        x2 = log_1mlb + log_sig
        amax = jnp.maximum(log_lb, x2)
        log_f = amax + jnp.log1p(jnp.exp(-jnp.abs(log_lb - x2)))
        kk = (1.0 - lb) * jnp.exp(-(jnp.maximum(z, 0.0) + softplus_tail))
        f_hi, f_mid, f_lo = _split3(log_f)
        b = (jnp.dot(ltri, f_hi, preferred_element_type=F32)
             + jnp.dot(ltri, f_mid, preferred_element_type=F32)
             + jnp.dot(ltri, f_lo, preferred_element_type=F32))
        b_last = b[ch - 1:ch, :]
        qt = q * jnp.exp(b)
        qt_b = qt.astype(BF16)
        st = st_scr[...]
        o_inter = lax.dot_general(qt_b, st.astype(BF16), NT_DIMS, preferred_element_type=F32)

        span_ok = jnp.min(b_last) >= -HGRN_FAST_SPAN

        @pl.when(span_ok)
        def _():
            kt = (kk * jnp.exp(-b)).astype(BF16)
            a0 = lax.dot_general(jnp.where(head0, qt_b, jnp.zeros_like(qt_b)), kt, NT_DIMS,
                                 preferred_element_type=F32)
            a1 = lax.dot_general(jnp.where(head0, jnp.zeros_like(qt_b), qt_b), kt, NT_DIMS,
                                 preferred_element_type=F32)
            a0 = jnp.where(tril, a0, 0.0).astype(BF16)
            a1 = jnp.where(tril, a1, 0.0).astype(BF16)
            o0 = jnp.dot(a0, v, preferred_element_type=F32)
            o1 = jnp.dot(a1, v, preferred_element_type=F32)
            oi_scr[...] = jnp.where(head0, o0, o1)

        @pl.when(jnp.logical_not(span_ok))
        def _():
            vf = v.astype(F32)

            def t_body(t, carry2):
                onehot = jnp.where(s_iota == t, 1.0, 0.0)
                b_t = jnp.sum(onehot * b, axis=0, keepdims=True)
                q_t = jnp.sum(onehot * q, axis=0, keepdims=True)
                dec = jnp.exp(jnp.where(s_iota <= t, b_t - b, -jnp.inf))
                w = q_t * kk * dec
                w0 = jnp.sum(jnp.where(head0, w, 0.0), axis=1, keepdims=True)
                w1 = jnp.sum(jnp.where(head0, 0.0, w), axis=1, keepdims=True)
                a_col = jnp.where(head0, w0, w1)
                oi_scr[pl.ds(t, 1), :] = jnp.sum(a_col * vf, axis=0, keepdims=True)
                return carry2

            lax.fori_loop(0, ch, t_body, 0)

        o = oi_scr[...] + o_inter
        kh = (kk * jnp.exp(b_last - b)).astype(BF16)
        upd = lax.dot_general(v, kh, TN_DIMS, preferred_element_type=F32)
        st_scr[...] = st * jnp.exp(b_last) + jnp.where(same_head, upd, 0.0)

        o2_hi, o2_lo = _split2(o * o)
        ss = (jnp.dot(o2_hi, ones_bd, preferred_element_type=F32)
              + jnp.dot(o2_lo, ones_bd, preferred_element_type=F32))
        y = (o * lax.rsqrt(ss * (1.0 / B_DV) + EPS)) * gn
        g = g_ref[0, pl.ds(t0, ch), :].astype(F32)
        o_ref[0, pl.ds(t0, ch), :] = (y * _silu(g)).astype(o_ref.dtype)
        return carry

    lax.fori_loop(0, nchunks, chunk_body, 0)


def _hgrn(bq, bf, bi, bg, lb, gn):
    bn, s, _ = bq.shape
    npair = B_DIM // LANES
    tok = pl.BlockSpec((1, s, LANES), lambda b, p: (b, 0, p))
    return pl.pallas_call(
        _hgrn_kernel,
        grid=(bn, npair),
        in_specs=[tok, tok, tok, tok,
                  pl.BlockSpec((1, LANES), lambda b, p: (0, p)),
                  pl.BlockSpec((1, LANES), lambda b, p: (0, 0))],
        out_specs=tok,
        out_shape=jax.ShapeDtypeStruct((bn, s, B_DIM), BF16),
        scratch_shapes=[pltpu.VMEM((LANES, LANES), F32), pltpu.VMEM((HGRN_CHUNK, LANES), F32)],
        compiler_params=_cparams("parallel", "parallel"),
    )(bq, bf, bi, bg, lb, gn)


def _gmlp_kernel(uv_ref, vg_ref, vb_ref, ws_ref, bias_ref, o_ref):
    uv = uv_ref[0].astype(F32)
    uv = 0.5 * uv * (1.0 + lax.erf(uv * (2.0 ** -0.5)))
    u = uv[:, :C_DIM]
    v = uv[:, C_DIM:]
    mu = jnp.mean(v, axis=-1, keepdims=True)
    var = jnp.mean(jnp.square(v - mu), axis=-1, keepdims=True)
    vn = ((v - mu) * lax.rsqrt(var + EPS)) * vg_ref[...] + vb_ref[...]
    vn_b = vn.astype(BF16)
    r_t = lax.broadcasted_iota(I32, (C_CHUNK, C_CHUNK), 0)
    r_s = lax.broadcasted_iota(I32, (C_CHUNK, C_CHUNK), 1)
    lane = lax.broadcasted_iota(I32, (C_CHUNK, C_DIM), 1)
    mixed = jnp.zeros((C_CHUNK, C_DIM), F32)
    for gi in range(C_GROUPS):
        w = jnp.where(r_t >= r_s, ws_ref[gi], 0.0).astype(BF16)
        m = jnp.dot(w, vn_b, preferred_element_type=F32)
        in_group = (lane >= gi * C_GROUP_DIM) & (lane < (gi + 1) * C_GROUP_DIM)
        mixed = jnp.where(in_group, m, mixed)
    o_ref[0] = (u * (mixed + bias_ref[...])).astype(o_ref.dtype)


def _gmlp(cuv, vg, vb, ws, bs):
    bn, s, _ = cuv.shape
    bias = jnp.repeat(bs.T, C_GROUP_DIM, axis=1)
    full = lambda a: pl.BlockSpec(a.shape, lambda b, t: (0,) * a.ndim)
    return pl.pallas_call(
        _gmlp_kernel,
        grid=(bn, s // C_CHUNK),
        in_specs=[pl.BlockSpec((1, C_CHUNK, 2 * C_DIM), lambda b, t: (b, t, 0)),
                  full(vg), full(vb), full(ws), full(bias)],
        out_specs=pl.BlockSpec((1, C_CHUNK, C_DIM), lambda b, t: (b, t, 0)),
        out_shape=jax.ShapeDtypeStruct((bn, s, C_DIM), BF16),
        compiler_params=_cparams("parallel", "parallel"),
    )(cuv, vg, vb, ws, bias)


def _outproj_kernel(x_ref, a_ref, b_ref, c_ref, wa_ref, wb_ref, wc_ref, g1_ref, o_ref):
    mix = (jnp.dot(a_ref[0], wa_ref[...], preferred_element_type=F32)
           + jnp.dot(b_ref[0], wb_ref[...], preferred_element_type=F32)
           + jnp.dot(c_ref[0], wc_ref[...], preferred_element_type=F32))
    o_ref[0] = x_ref[0] + g1_ref[0] * mix


def _outproj(x, a, b, c, w_out_l, g1, tm):
    bn, s, d = x.shape
    wa = w_out_l[:A_DIM].astype(BF16)
    wb = w_out_l[A_DIM:A_DIM + B_DIM].astype(BF16)
    wc = w_out_l[A_DIM + B_DIM:].astype(BF16)
    tok = lambda w: pl.BlockSpec((1, tm, w), lambda bb, t: (bb, t, 0))
    full = lambda arr: pl.BlockSpec(arr.shape, lambda bb, t: (0, 0))
    return pl.pallas_call(
        _outproj_kernel,
        grid=(bn, s // tm),
        in_specs=[tok(d), tok(A_DIM), tok(B_DIM), tok(C_DIM), full(wa), full(wb), full(wc),
                  pl.BlockSpec((1, 1, d), lambda bb, t: (bb, 0, 0))],
        out_specs=tok(d),
        out_shape=jax.ShapeDtypeStruct((bn, s, d), F32),
        compiler_params=_cparams("parallel", "parallel"),
    )(x, a, b, c, wa, wb, wc, g1)


def _ffn_kernel(x_ref, sh_ref, sc_ref, gate_ref, g_ref, wg_ref, wu_ref, wd_ref, o_ref, h_scr, acc_scr):
    f = pl.program_id(2)

    @pl.when(f == 0)
    def _():
        h_scr[...] = _modulated_rmsnorm(x_ref[0], g_ref[...], sc_ref[0], sh_ref[0]).astype(BF16)
        acc_scr[...] = jnp.zeros_like(acc_scr)

    h = h_scr[...]
    a = jnp.dot(h, wg_ref[...], preferred_element_type=F32)
    u = jnp.dot(h, wu_ref[...], preferred_element_type=F32)
    act = (_silu(a) * u).astype(BF16)
    acc_scr[...] += jnp.dot(act, wd_ref[...], preferred_element_type=F32)

    @pl.when(f == pl.num_programs(2) - 1)
    def _():
        o_ref[0] = x_ref[0] + gate_ref[0] * acc_scr[...]


def _ffn(x, sh, sc, gate, g, wg, wu, wd, tm, tf):
    bn, s, d = x.shape
    fdim = wg.shape[1]
    tok = pl.BlockSpec((1, tm, d), lambda b, t, f: (b, t, 0))
    vec = pl.BlockSpec((1, 1, d), lambda b, t, f: (b, 0, 0))
    return pl.pallas_call(
        _ffn_kernel,
        grid=(bn, s // tm, fdim // tf),
        in_specs=[tok, vec, vec, vec,
                  pl.BlockSpec((1, d), lambda b, t, f: (0, 0)),
                  pl.BlockSpec((d, tf), lambda b, t, f: (0, f)),
                  pl.BlockSpec((d, tf), lambda b, t, f: (0, f)),
                  pl.BlockSpec((tf, d), lambda b, t, f: (f, 0))],
        out_specs=tok,
        out_shape=jax.ShapeDtypeStruct((bn, s, d), F32),
        scratch_shapes=[pltpu.VMEM((tm, d), BF16), pltpu.VMEM((tm, d), F32)],
        compiler_params=_cparams("parallel", "parallel", "arbitrary"),
    )(x, sh, sc, gate, g, wg, wu, wd)


def _moe_kernel(x_ref, sh_ref, sc_ref, gate_ref, g_ref, wr_ref, wg_ref, wu_ref, wd_ref, o_ref,
                h_scr, acc_scr, gates_scr, ge_scr):
    e = pl.program_id(2)
    f = pl.program_id(3)
    tm = x_ref.shape[1]
    lane = lax.broadcasted_iota(I32, (tm, LANES), 1)

    @pl.when((e == 0) & (f == 0))
    def _():
        h = _modulated_rmsnorm(x_ref[0], g_ref[...], sc_ref[0], sh_ref[0])
        h_scr[...] = h.astype(BF16)
        acc_scr[...] = jnp.zeros_like(acc_scr)
        logits = jnp.dot(h, wr_ref[...], precision=lax.Precision.HIGHEST, preferred_element_type=F32)
        logits = jnp.where(lane < N_EXPERTS, logits, -jnp.inf)
        m1 = jnp.max(logits, axis=-1, keepdims=True)
        i1 = jnp.min(jnp.where(logits == m1, lane, LANES), axis=-1, keepdims=True)
        rest = jnp.where(lane == i1, -jnp.inf, logits)
        m2 = jnp.max(rest, axis=-1, keepdims=True)
        i2 = jnp.min(jnp.where(rest == m2, lane, LANES), axis=-1, keepdims=True)
        e2 = jnp.exp(m2 - m1)
        den = 1.0 + e2
        gates_scr[...] = jnp.where(lane == i1, 1.0 / den, jnp.where(lane == i2, e2 / den, 0.0))

    @pl.when(f == 0)
    def _():
        ge = jnp.sum(jnp.where(lane == e, gates_scr[...], 0.0), axis=-1, keepdims=True)
        ge_scr[...] = jnp.broadcast_to(ge, ge_scr.shape)

    h = h_scr[...]
    a = jnp.dot(h, wg_ref[0], preferred_element_type=F32)
    u = jnp.dot(h, wu_ref[0], preferred_element_type=F32)
    act = (_silu(a) * u * ge_scr[:, 0:1]).astype(BF16)
    acc_scr[...] += jnp.dot(act, wd_ref[0], preferred_element_type=F32)

    @pl.when((e == pl.num_programs(2) - 1) & (f == pl.num_programs(3) - 1))
    def _():
        o_ref[0] = x_ref[0] + gate_ref[0] * acc_scr[...]


def _moe(x, sh, sc, gate, g, w_router, wg, wu, wd, tm, tf):
    bn, s, d = x.shape
    ne, _, fdim = wg.shape
    wr = jnp.zeros((d, LANES), F32).at[:, :ne].set(w_router)
    tok = pl.BlockSpec((1, tm, d), lambda b, t, e, f: (b, t, 0))
    vec = pl.BlockSpec((1, 1, d), lambda b, t, e, f: (b, 0, 0))
    return pl.pallas_call(
        _moe_kernel,
        grid=(bn, s // tm, ne, fdim // tf),
        in_specs=[tok, vec, vec, vec,
                  pl.BlockSpec((1, d), lambda b, t, e, f: (0, 0)),
                  pl.BlockSpec((d, LANES), lambda b, t, e, f: (0, 0)),
                  pl.BlockSpec((1, d, tf), lambda b, t, e, f: (e, 0, f)),
                  pl.BlockSpec((1, d, tf), lambda b, t, e, f: (e, 0, f)),
                  pl.BlockSpec((1, tf, d), lambda b, t, e, f: (e, f, 0))],
        out_specs=tok,
        out_shape=jax.ShapeDtypeStruct((bn, s, d), F32),
        scratch_shapes=[pltpu.VMEM((tm, d), BF16), pltpu.VMEM((tm, d), F32),
                        pltpu.VMEM((tm, LANES), F32), pltpu.VMEM((tm, LANES), F32)],
        compiler_params=_cparams("parallel", "parallel", "arbitrary", "arbitrary"),
    )(x, sh, sc, gate, g, wr, wg, wu, wd)


def _final_norm_kernel(x_ref, g_ref, o_ref):
    x = x_ref[0]
    var = jnp.mean(x * x, axis=-1, keepdims=True)
    o_ref[0] = (x * lax.rsqrt(var + EPS)) * g_ref[...]


def _final_norm(x, g, tm):
    bn, s, d = x.shape
    tok = pl.BlockSpec((1, tm, d), lambda b, t: (b, t, 0))
    return pl.pallas_call(
        _final_norm_kernel,
        grid=(bn, s // tm),
        in_specs=[tok, pl.BlockSpec((1, d), lambda b, t: (0, 0))],
        out_specs=tok,
        out_shape=jax.ShapeDtypeStruct((bn, s, d), F32),
        compiler_params=_cparams("parallel", "parallel"),
    )(x, g)


def _token_tile(s, want):
    tm = min(want, s)
    assert s % tm == 0 and tm % QB == 0
    return tm


def kernel(x, c, positions, w_ada, b_ada, g_norm_mix, g_norm_ffn, w_in, w_out, hgrn_lb_logits, hgrn_out_norm, gmlp_vnorm_g, gmlp_vnorm_b, gmlp_w_s, gmlp_b_s, ffn_w_gate, ffn_w_up, ffn_w_down, moe_w_router, moe_w_gate, moe_w_up, moe_w_down, g_final):
    bn, s, d = x.shape
    depth = w_in.shape[0]
    assert s % QB == 0 and d == sum(IN_WIDTHS[:1]) + B_DIM + C_DIM
    tm_proj = _token_tile(s, 512)
    tm_ffn = _token_tile(s, 1024)

    p_lb = jax.nn.softmax(hgrn_lb_logits.astype(F32), axis=0)
    cum = jnp.cumsum(p_lb, axis=0)
    lower_bounds = cum - cum[0:1]

    mod = _ada_mod(c, w_ada, b_ada)
    cosn, sinn, cost, sint = _rope_tables(positions)

    for l in range(depth):
        sh1, sc1, g1, sh2, sc2, g2 = [mod[l, :, i * d:(i + 1) * d].reshape(bn, 1, d) for i in range(6)]
        w_n, w_t = _prep_in_weights(w_in[l])
        (kz, ik3, bq, bf, bi, bg, cuv, qt, iq3t, vt4, iwt) = _inproj(
            x, sh1, sc1, g_norm_mix[l].reshape(1, d), w_n, w_t, cosn, sinn, cost, sint, tm_proj)
        a_out = _dsa(ik3, kz, vt4, iq3t, iwt, qt)
        b_out = _hgrn(bq, bf, bi, bg, lower_bounds[l].reshape(1, B_DIM),
                      jnp.tile(hgrn_out_norm[l], LANES // B_DV).reshape(1, LANES))
        c_out = _gmlp(cuv, gmlp_vnorm_g[l].reshape(1, C_DIM), gmlp_vnorm_b[l].reshape(1, C_DIM),
                      gmlp_w_s[l], gmlp_b_s[l])
        x = _outproj(x, a_out, b_out, c_out, w_out[l], g1, tm_proj)
        gf = g_norm_ffn[l].reshape(1, d)
        if l % 2 == 0:
            i = l // 2
            x = _ffn(x, sh2, sc2, g2, gf, ffn_w_gate[i].astype(BF16), ffn_w_up[i].astype(BF16),
                     ffn_w_down[i].astype(BF16), tm_ffn, 256)
        else:
            i = l // 2
            x = _moe(x, sh2, sc2, g2, gf, moe_w_router[i], moe_w_gate[i].astype(BF16),
                     moe_w_up[i].astype(BF16), moe_w_down[i].astype(BF16), tm_ffn, 512)
    return _final_norm(x, g_final.reshape(1, d), tm_proj)
```

```python
import functools

import numpy as np
import jax
import jax.numpy as jnp
from jax import lax
from jax.experimental import pallas as pl
from jax.experimental.pallas import tpu as pltpu

F32 = jnp.float32
BF16 = jnp.bfloat16
I32 = jnp.int32

HEAD_DIM = 64
A_HEADS = 6
A_DIM = A_HEADS * HEAD_DIM
IDX_HEADS = 4
IDX_DIM = 64
TOPK_MAX = 256
B_HEADS = 6
B_DK = 64
B_DV = 64
B_DIM = B_HEADS * B_DV
HGRN_CHUNK = 64
C_GROUPS = 4
C_GROUP_DIM = 64
C_DIM = C_GROUPS * C_GROUP_DIM
C_CHUNK = 128
ROPE_THETA = 10000.0
N_EXPERTS = 8
EPS = 1e-6
IN_WIDTHS = (A_DIM, HEAD_DIM, HEAD_DIM, IDX_HEADS * IDX_DIM, IDX_DIM, IDX_HEADS,
             B_DIM, B_DIM, B_DIM, B_DIM, 2 * C_DIM)

LANES = 128
QB = 128
VMEM_LIMIT = 48 * 1024 * 1024
HGRN_FAST_SPAN = 80.0
INT_MIN = -2 ** 31

NT_DIMS = (((1,), (1,)), ((), ()))
TN_DIMS = (((0,), (0,)), ((), ()))


def _cparams(*sem):
    return pltpu.CompilerParams(dimension_semantics=sem, vmem_limit_bytes=VMEM_LIMIT)


def _split2(x):
    hi = x.astype(BF16)
    lo = (x - hi.astype(F32)).astype(BF16)
    return hi, lo


def _split3(x):
    hi = x.astype(BF16)
    r1 = x - hi.astype(F32)
    mid = r1.astype(BF16)
    lo = (r1 - mid.astype(F32)).astype(BF16)
    return hi, mid, lo


def _silu(x):
    return x * (1.0 / (1.0 + jnp.exp(-x)))


def _modulated_rmsnorm(x, g, sc, sh):
    var = jnp.mean(x * x, axis=-1, keepdims=True)
    y = x * lax.rsqrt(var + EPS)
    return (y * g) * (1.0 + sc) + sh


def _ada_kernel(c_ref, w_ref, b_ref, o_ref):
    cond = _silu(c_ref[...])
    o_ref[0] = jnp.dot(cond, w_ref[0], precision=lax.Precision.HIGHEST,
                       preferred_element_type=F32) + b_ref[0]


def _ada_mod(c, w_ada, b_ada):
    depth, d, d6 = w_ada.shape
    bn = c.shape[0]
    tn = 1536
    return pl.pallas_call(
        _ada_kernel,
        grid=(depth, d6 // tn),
        in_specs=[pl.BlockSpec((bn, d), lambda l, n: (0, 0)),
                  pl.BlockSpec((1, d, tn), lambda l, n: (l, 0, n)),
                  pl.BlockSpec((1, 1, tn), lambda l, n: (l, 0, n))],
        out_specs=pl.BlockSpec((1, bn, tn), lambda l, n: (l, 0, n)),
        out_shape=jax.ShapeDtypeStruct((depth, bn, d6), F32),
        compiler_params=_cparams("parallel", "parallel"),
    )(c, w_ada, b_ada.reshape(depth, 1, d6))


def _rope_kernel(posn_ref, post_ref, invn_ref, signn_ref, invt_ref,
                 cosn_ref, sinn_ref, cost_ref, sint_ref):
    ang_n = posn_ref[0].astype(F32) * invn_ref[...]
    cosn_ref[0] = jnp.cos(ang_n)
    sinn_ref[0] = jnp.sin(ang_n) * signn_ref[...]
    ang_t = invt_ref[...] * post_ref[0].astype(F32)
    cost_ref[0] = jnp.cos(ang_t)
    sint_ref[0] = jnp.sin(ang_t)


def _rope_tables(positions):
    bn, s = positions.shape
    half = HEAD_DIM // 2
    inv = ROPE_THETA ** (-jnp.arange(0, HEAD_DIM, 2, dtype=F32) / HEAD_DIM)
    inv_n = jnp.tile(inv, LANES // half).reshape(1, LANES)
    sign_n = jnp.tile(jnp.concatenate([-jnp.ones((half,), F32), jnp.ones((half,), F32)]),
                      LANES // HEAD_DIM).reshape(1, LANES)
    inv_t = inv.reshape(half, 1)
    full = lambda shape: pl.BlockSpec(shape, lambda b: (0,) * len(shape))
    return pl.pallas_call(
        _rope_kernel,
        grid=(bn,),
        in_specs=[pl.BlockSpec((1, s, 1), lambda b: (b, 0, 0)),
                  pl.BlockSpec((1, 1, s), lambda b: (b, 0, 0)),
                  full((1, LANES)), full((1, LANES)), full((half, 1))],
        out_specs=[pl.BlockSpec((1, s, LANES), lambda b: (b, 0, 0)),
                   pl.BlockSpec((1, s, LANES), lambda b: (b, 0, 0)),
                   pl.BlockSpec((1, half, s), lambda b: (b, 0, 0)),
                   pl.BlockSpec((1, half, s), lambda b: (b, 0, 0))],
        out_shape=[jax.ShapeDtypeStruct((bn, s, LANES), F32),
                   jax.ShapeDtypeStruct((bn, s, LANES), F32),
                   jax.ShapeDtypeStruct((bn, half, s), F32),
                   jax.ShapeDtypeStruct((bn, half, s), F32)],
        compiler_params=_cparams("parallel"),
    )(positions.reshape(bn, s, 1), positions.reshape(bn, 1, s), inv_n, sign_n, inv_t)


N_KZ = 0
N_IK = 2 * LANES
N_BQ = 4 * LANES
N_BF = N_BQ + B_DIM
N_BI = N_BF + B_DIM
N_BG = N_BI + B_DIM
N_CUV = N_BG + B_DIM
N_COLS = N_CUV + 2 * C_DIM
T_Q = 0
T_IQ = A_DIM
T_V = T_IQ + IDX_HEADS * IDX_DIM
T_IW = T_V + HEAD_DIM
T_ROWS = T_IW + 8


def _prep_in_weights(w_in_l):
    offs = np.concatenate([[0], np.cumsum(IN_WIDTHS)])
    sl = lambda i: w_in_l[:, int(offs[i]):int(offs[i + 1])]
    aq, ak, av, iq, ik, iw, bq, bf, bi, bg, cuv = [sl(i) for i in range(11)]
    half = HEAD_DIM // 2
    rot = lambda w: jnp.concatenate([w[:, half:], w[:, :half]], axis=1)
    z = jnp.zeros_like(ak)
    w_n = jnp.concatenate([ak, z, rot(ak), z, ik, ik, rot(ik), rot(ik), bq, bf, bi, bg, cuv], axis=1)
    w_t = jnp.concatenate([aq, iq, av, iw, jnp.zeros((w_in_l.shape[0], 4), w_in_l.dtype)], axis=1).T
    return w_n.astype(BF16), w_t.astype(BF16)


def _inproj_kernel(x_ref, sh_ref, sc_ref, g_ref, wn_ref, wt_ref, cosn_ref, sinn_ref, cost_ref, sint_ref,
                   kz_ref, ik3_ref, bq_ref, bf_ref, bi_ref, bg_ref, cuv_ref,
                   qt_ref, iq3t_ref, vt_ref, iwt_ref):
    tm = x_ref.shape[1]
    half = HEAD_DIM // 2
    h = _modulated_rmsnorm(x_ref[0], g_ref[...], sc_ref[0], sh_ref[0]).astype(BF16)

    cosn = cosn_ref[0]
    sinn = sinn_ref[0]
    zk = jnp.dot(h, wn_ref[:, N_KZ:N_KZ + 2 * LANES], preferred_element_type=F32)
    kz_ref[0] = (zk[:, :LANES] * cosn + zk[:, LANES:] * sinn).astype(BF16)
    zi = jnp.dot(h, wn_ref[:, N_IK:N_IK + 2 * LANES], preferred_element_type=F32)
    ik2 = zi[:, :LANES] * cosn + zi[:, LANES:] * sinn
    hi, lo = _split2(ik2)
    lane = lax.broadcasted_iota(I32, (tm, LANES), 1)
    ik3_ref[0, :, :LANES] = jnp.where(lane < IDX_DIM, hi, lo)
    ik3_ref[0, :, LANES:] = hi
    bq_ref[0] = jnp.dot(h, wn_ref[:, N_BQ:N_BF], preferred_element_type=F32).astype(BF16)
    bf_ref[0] = jnp.dot(h, wn_ref[:, N_BF:N_BI], preferred_element_type=F32)
    bi_ref[0] = jnp.dot(h, wn_ref[:, N_BI:N_BG], preferred_element_type=F32).astype(BF16)
    bg_ref[0] = jnp.dot(h, wn_ref[:, N_BG:N_CUV], preferred_element_type=F32).astype(BF16)
    cuv_ref[0] = jnp.dot(h, wn_ref[:, N_CUV:N_COLS], preferred_element_type=F32).astype(BF16)

    zt = lax.dot_general(wt_ref[...], h, NT_DIMS, preferred_element_type=F32)
    cost = cost_ref[0]
    sint = sint_ref[0]
    qscale = HEAD_DIM ** -0.5
    for hh in range(A_HEADS):
        r0 = T_Q + hh * HEAD_DIM
        x1 = zt[r0:r0 + half]
        x2 = zt[r0 + half:r0 + HEAD_DIM]
        qt_ref[0, hh * HEAD_DIM:hh * HEAD_DIM + half] = ((x1 * cost - x2 * sint) * qscale).astype(BF16)
        qt_ref[0, hh * HEAD_DIM + half:(hh + 1) * HEAD_DIM] = ((x2 * cost + x1 * sint) * qscale).astype(BF16)
    iscale = IDX_DIM ** -0.5
    zero = jnp.zeros((IDX_DIM, tm), BF16)
    for hh in range(IDX_HEADS):
        r0 = T_IQ + hh * IDX_DIM
        x1 = zt[r0:r0 + half]
        x2 = zt[r0 + half:r0 + IDX_DIM]
        y = jnp.concatenate([(x1 * cost - x2 * sint) * iscale, (x2 * cost + x1 * sint) * iscale], axis=0)
        hi, lo = _split2(y)
        iq3t_ref[0, hh, 0 * IDX_DIM:1 * IDX_DIM] = hi
        iq3t_ref[0, hh, 1 * IDX_DIM:2 * IDX_DIM] = hi
        iq3t_ref[0, hh, 2 * IDX_DIM:3 * IDX_DIM] = lo
        iq3t_ref[0, hh, 3 * IDX_DIM:4 * IDX_DIM] = zero
    vt = zt[T_V:T_V + HEAD_DIM].astype(BF16)
    for i in range(tm // QB):
        vt_ref[0, i] = vt[:, i * QB:(i + 1) * QB]
    iwt_ref[0] = zt[T_IW:T_IW + 8] * (IDX_HEADS ** -0.5)


def _inproj(x, sh, sc, g, w_n, w_t, cosn, sinn, cost, sint, tm):
    bn, s, d = x.shape
    nt = s // tm
    half = HEAD_DIM // 2
    tok = lambda w: pl.BlockSpec((1, tm, w), lambda b, t: (b, t, 0))
    vec = pl.BlockSpec((1, 1, d), lambda b, t: (b, 0, 0))
    full2 = lambda a: pl.BlockSpec(a.shape, lambda b, t: (0, 0))
    out_shapes = [
        jax.ShapeDtypeStruct((bn, s, LANES), BF16),
        jax.ShapeDtypeStruct((bn, s, 2 * LANES), BF16),
        jax.ShapeDtypeStruct((bn, s, B_DIM), BF16),
        jax.ShapeDtypeStruct((bn, s, B_DIM), F32),
        jax.ShapeDtypeStruct((bn, s, B_DIM), BF16),
        jax.ShapeDtypeStruct((bn, s, B_DIM), BF16),
        jax.ShapeDtypeStruct((bn, s, 2 * C_DIM), BF16),
        jax.ShapeDtypeStruct((bn, A_DIM, s), BF16),
        jax.ShapeDtypeStruct((bn, IDX_HEADS, 4 * IDX_DIM, s), BF16),
        jax.ShapeDtypeStruct((bn, s // QB, HEAD_DIM, QB), BF16),
        jax.ShapeDtypeStruct((bn, 8, s), F32),
    ]
    out_specs = [
        tok(LANES), tok(2 * LANES), tok(B_DIM), tok(B_DIM), tok(B_DIM), tok(B_DIM), tok(2 * C_DIM),
        pl.BlockSpec((1, A_DIM, tm), lambda b, t: (b, 0, t)),
        pl.BlockSpec((1, IDX_HEADS, 4 * IDX_DIM, tm), lambda b, t: (b, 0, 0, t)),
        pl.BlockSpec((1, tm // QB, HEAD_DIM, QB), lambda b, t: (b, t, 0, 0)),
        pl.BlockSpec((1, 8, tm), lambda b, t: (b, 0, t)),
    ]
    return pl.pallas_call(
        _inproj_kernel,
        grid=(bn, nt),
        in_specs=[tok(d), vec, vec, full2(g), full2(w_n), full2(w_t),
                  tok(LANES), tok(LANES),
                  pl.BlockSpec((1, half, tm), lambda b, t: (b, 0, t)),
                  pl.BlockSpec((1, half, tm), lambda b, t: (b, 0, t))],
        out_specs=out_specs,
        out_shape=out_shapes,
        compiler_params=_cparams("parallel", "parallel"),
    )(x, sh, sc, g, w_n, w_t, cosn, sinn, cost, sint)


def _dsa_kernel(n_top, ik3_ref, kz_ref, vt_ref, iq3t_ref, iwt_ref, qt_ref, out_ref, key_scr, lg_scr):
    j = pl.program_id(1)
    nkc = j + 1
    row = lax.broadcasted_iota(I32, (QB, QB), 0)
    col = lax.broadcasted_iota(I32, (QB, QB), 1)
    int_min = jnp.int32(INT_MIN)
    sub = QB // 8

    def score_body(c, carry):
        ks = pl.multiple_of(c * QB, QB)
        ikc = ik3_ref[0, pl.ds(ks, QB), :]
        sc = jnp.zeros((QB, QB), F32)
        for hh in range(IDX_HEADS):
            rel = jnp.dot(ikc, iq3t_ref[0, hh], preferred_element_type=F32)
            sc = sc + jnp.maximum(rel, 0.0) * iwt_ref[0, hh:hh + 1, :]
        sc = jnp.where(sc == 0.0, 0.0, sc)
        bits = pltpu.bitcast(sc, I32)
        key = bits ^ ((bits >> 31) & jnp.int32(0x7FFFFFFF))
        causal = (ks + row) <= (j * QB + col)
        key_scr[pl.ds(ks, QB), :] = jnp.where(causal, key, int_min)
        return carry

    lax.fori_loop(0, nkc, score_body, 0)

    def count(pred_fn):
        def body(c, acc):
            ks = pl.multiple_of(c * QB, QB)
            m = jnp.where(pred_fn(key_scr[pl.ds(ks, QB), :]), 1, 0).astype(I32)
            return acc + jnp.sum(m.reshape(sub, 8, QB), axis=0)
        acc = lax.fori_loop(0, nkc, body, jnp.zeros((8, QB), I32))
        return jnp.sum(acc, axis=0, keepdims=True)

    def bit_body(i, t_u):
        cand = t_u | jnp.left_shift(jnp.int32(1), 31 - i)
        cand_s = cand ^ int_min
        cnt = count(lambda k: k >= cand_s)
        return jnp.where(cnt >= n_top, cand, t_u)

    t_u = lax.fori_loop(0, 32, bit_body, jnp.zeros((1, QB), I32))
    thr = t_u ^ int_min
    n_gt = count(lambda k: k > thr)
    n_tie = (n_top - n_gt).astype(F32)
    thr_valid = jnp.where(thr > int_min, 1.0, 0.0)

    ltri = jnp.where(row >= col, 1.0, 0.0).astype(BF16)
    zpad = jnp.zeros((HEAD_DIM, QB), BF16)
    qpad = [jnp.concatenate([qt_ref[0, hh * HEAD_DIM:(hh + 1) * HEAD_DIM, :], zpad], axis=0)
            for hh in range(A_HEADS)]
    neg_inf = jnp.float32(-jnp.inf)

    def pass_a(c, carry):
        tie_cnt, ms = carry
        ks = pl.multiple_of(c * QB, QB)
        kc = key_scr[pl.ds(ks, QB), :]
        eqf = jnp.where(kc == thr, thr_valid, 0.0)
        pref = jnp.dot(ltri, eqf.astype(BF16), preferred_element_type=F32)
        keep_tie = eqf * jnp.where((tie_cnt + pref) <= n_tie, 1.0, 0.0)
        sel = jnp.where(kc > thr, 1.0, keep_tie) > 0.5
        tie_cnt = tie_cnt + pref[QB - 1:QB, :]
        kzc = kz_ref[0, pl.ds(ks, QB), :]
        new_ms = []
        for hh in range(A_HEADS):
            lt = jnp.dot(kzc, qpad[hh], preferred_element_type=F32)
            lt = jnp.where(sel, lt, neg_inf)
            lg_scr[hh, pl.ds(ks, QB), :] = lt
            new_ms.append(jnp.maximum(ms[hh], jnp.max(lt, axis=0, keepdims=True)))
        return tie_cnt, tuple(new_ms)

    init_m = tuple(jnp.full((1, QB), neg_inf, F32) for _ in range(A_HEADS))
    _, ms = lax.fori_loop(0, nkc, pass_a, (jnp.zeros((1, QB), F32), init_m))

    def pass_b(c, carry):
        ls, accs = carry
        ks = pl.multiple_of(c * QB, QB)
        vtc = vt_ref[0, c]
        new_ls, new_accs = [], []
        for hh in range(A_HEADS):
            p = jnp.exp(lg_scr[hh, pl.ds(ks, QB), :] - ms[hh])
            new_ls.append(ls[hh] + jnp.sum(p, axis=0, keepdims=True))
            new_accs.append(accs[hh] + jnp.dot(vtc, p.astype(BF16), preferred_element_type=F32))
        return tuple(new_ls), tuple(new_accs)

    init_l = tuple(jnp.zeros((1, QB), F32) for _ in range(A_HEADS))
    init_acc = tuple(jnp.zeros((HEAD_DIM, QB), F32) for _ in range(A_HEADS))
    ls, accs = lax.fori_loop(0, nkc, pass_b, (init_l, init_acc))
    o_t = jnp.concatenate([accs[hh] / ls[hh] for hh in range(A_HEADS)], axis=0)
    out_ref[0] = o_t.T.astype(BF16)


def _dsa(ik3, kz, vt4, iq3t, iwt, qt):
    bn, s, _ = kz.shape
    n_top = min(TOPK_MAX, s // 4)
    nq = s // QB
    return pl.pallas_call(
        functools.partial(_dsa_kernel, n_top),
        grid=(bn, nq),
        in_specs=[pl.BlockSpec((1, s, 2 * LANES), lambda b, q: (b, 0, 0)),
                  pl.BlockSpec((1, s, LANES), lambda b, q: (b, 0, 0)),
                  pl.BlockSpec((1, nq, HEAD_DIM, QB), lambda b, q: (b, 0, 0, 0)),
                  pl.BlockSpec((1, IDX_HEADS, 4 * IDX_DIM, QB), lambda b, q: (b, 0, 0, q)),
                  pl.BlockSpec((1, 8, QB), lambda b, q: (b, 0, q)),
                  pl.BlockSpec((1, A_DIM, QB), lambda b, q: (b, 0, q))],
        out_specs=pl.BlockSpec((1, QB, A_DIM), lambda b, q: (b, q, 0)),
        out_shape=jax.ShapeDtypeStruct((bn, s, A_DIM), BF16),
        scratch_shapes=[pltpu.VMEM((s, QB), I32), pltpu.VMEM((A_HEADS, s, QB), F32)],
        compiler_params=_cparams("parallel", "arbitrary"),
    )(ik3, kz, vt4, iq3t, iwt, qt)


def _hgrn_kernel(q_ref, f_ref, i_ref, g_ref, lb_ref, gn_ref, o_ref, st_scr, oi_scr):
    s = q_ref.shape[1]
    ch = HGRN_CHUNK
    nchunks = s // ch
    lane = lax.broadcasted_iota(I32, (ch, LANES), 1)
    head0 = lane < B_DK
    r_t = lax.broadcasted_iota(I32, (ch, ch), 0)
    r_s = lax.broadcasted_iota(I32, (ch, ch), 1)
    tril = r_t >= r_s
    ltri = jnp.where(tril, 1.0, 0.0).astype(BF16)
    bd_r = lax.broadcasted_iota(I32, (LANES, LANES), 0)
    bd_c = lax.broadcasted_iota(I32, (LANES, LANES), 1)
    same_head = (bd_r < B_DK) == (bd_c < B_DK)
    ones_bd = jnp.where(same_head, 1.0, 0.0).astype(BF16)
    s_iota = lax.broadcasted_iota(I32, (ch, LANES), 0)

    lb = lb_ref[...]
    log_lb = jnp.log(lb)
    log_1mlb = jnp.log1p(-lb)
    gn = gn_ref[...]
    st_scr[...] = jnp.zeros_like(st_scr)

    def chunk_body(c, carry):
        t0 = pl.multiple_of(c * ch, ch)
        z = f_ref[0, pl.ds(t0, ch), :]
        q = q_ref[0, pl.ds(t0, ch), :].astype(F32)
        v = i_ref[0, pl.ds(t0, ch), :]
        softplus_tail = jnp.log1p(jnp.exp(-jnp.abs(z)))
        log_sig = -(jnp.maximum(-z, 0.0) + softplus_tail)
        x2 = log_1mlb + log_sig
        amax = jnp.maximum(log_lb, x2)
        log_f = amax + jnp.log1p(jnp.exp(-jnp.abs(log_lb - x2)))
        kk = (1.0 - lb) * jnp.exp(-(jnp.maximum(z, 0.0) + softplus_tail))
        f_hi, f_mid, f_lo = _split3(log_f)
        b = (jnp.dot(ltri, f_hi, preferred_element_type=F32)
             + jnp.dot(ltri, f_mid, preferred_element_type=F32)
             + jnp.dot(ltri, f_lo, preferred_element_type=F32))
        b_last = b[ch - 1:ch, :]
        qt = q * jnp.exp(b)
        qt_b = qt.astype(BF16)
        st = st_scr[...]
        o_inter = lax.dot_general(qt_b, st.astype(BF16), NT_DIMS, preferred_element_type=F32)

        span_ok = jnp.min(b_last) >= -HGRN_FAST_SPAN

        @pl.when(span_ok)
        def _():
            kt = (kk * jnp.exp(-b)).astype(BF16)
            a0 = lax.dot_general(jnp.where(head0, qt_b, jnp.zeros_like(qt_b)), kt, NT_DIMS,
                                 preferred_element_type=F32)
            a1 = lax.dot_general(jnp.where(head0, jnp.zeros_like(qt_b), qt_b), kt, NT_DIMS,
                                 preferred_element_type=F32)
            a0 = jnp.where(tril, a0, 0.0).astype(BF16)
            a1 = jnp.where(tril, a1, 0.0).astype(BF16)
            o0 = jnp.dot(a0, v, preferred_element_type=F32)
            o1 = jnp.dot(a1, v, preferred_element_type=F32)
            oi_scr[...] = jnp.where(head0, o0, o1)

        @pl.when(jnp.logical_not(span_ok))
        def _():
            vf = v.astype(F32)

            def t_body(t, carry2):
                onehot = jnp.where(s_iota == t, 1.0, 0.0)
                b_t = jnp.sum(onehot * b, axis=0, keepdims=True)
                q_t = jnp.sum(onehot * q, axis=0, keepdims=True)
                dec = jnp.exp(jnp.where(s_iota <= t, b_t - b, -jnp.inf))
                w = q_t * kk * dec
                w0 = jnp.sum(jnp.where(head0, w, 0.0), axis=1, keepdims=True)
                w1 = jnp.sum(jnp.where(head0, 0.0, w), axis=1, keepdims=True)
                a_col = jnp.where(head0, w0, w1)
                oi_scr[pl.ds(t, 1), :] = jnp.sum(a_col * vf, axis=0, keepdims=True)
                return carry2

            lax.fori_loop(0, ch, t_body, 0)

        o = oi_scr[...] + o_inter
        kh = (kk * jnp.exp(b_last - b)).astype(BF16)
        upd = lax.dot_general(v, kh, TN_DIMS, preferred_element_type=F32)
        st_scr[...] = st * jnp.exp(b_last) + jnp.where(same_head, upd, 0.0)

        o2_hi, o2_lo = _split2(o * o)
        ss = (jnp.dot(o2_hi, ones_bd, preferred_element_type=F32)
              + jnp.dot(o2_lo, ones_bd, preferred_element_type=F32))
        y = (o * lax.rsqrt(ss * (1.0 / B_DV) + EPS)) * gn
        g = g_ref[0, pl.ds(t0, ch), :].astype(F32)
        o_ref[0, pl.ds(t0, ch), :] = (y * _silu(g)).astype(o_ref.dtype)
        return carry

    lax.fori_loop(0, nchunks, chunk_body, 0)


def _hgrn(bq, bf, bi, bg, lb, gn):
    bn, s, _ = bq.shape
    npair = B_DIM // LANES
    tok = pl.BlockSpec((1, s, LANES), lambda b, p: (b, 0, p))
    return pl.pallas_call(
        _hgrn_kernel,
        grid=(bn, npair),
        in_specs=[tok, tok, tok, tok,
                  pl.BlockSpec((1, LANES), lambda b, p: (0, p)),
                  pl.BlockSpec((1, LANES), lambda b, p: (0, 0))],
        out_specs=tok,
        out_shape=jax.ShapeDtypeStruct((bn, s, B_DIM), BF16),
        scratch_shapes=[pltpu.VMEM((LANES, LANES), F32), pltpu.VMEM((HGRN_CHUNK, LANES), F32)],
        compiler_params=_cparams("parallel", "parallel"),
    )(bq, bf, bi, bg, lb, gn)


def _gmlp_kernel(uv_ref, vg_ref, vb_ref, ws_ref, bias_ref, o_ref):
    uv = uv_ref[0].astype(F32)
    uv = 0.5 * uv * (1.0 + lax.erf(uv * (2.0 ** -0.5)))
    u = uv[:, :C_DIM]
    v = uv[:, C_DIM:]
    mu = jnp.mean(v, axis=-1, keepdims=True)
    var = jnp.mean(jnp.square(v - mu), axis=-1, keepdims=True)
    vn = ((v - mu) * lax.rsqrt(var + EPS)) * vg_ref[...] + vb_ref[...]
    vn_b = vn.astype(BF16)
    r_t = lax.broadcasted_iota(I32, (C_CHUNK, C_CHUNK), 0)
    r_s = lax.broadcasted_iota(I32, (C_CHUNK, C_CHUNK), 1)
    lane = lax.broadcasted_iota(I32, (C_CHUNK, C_DIM), 1)
    mixed = jnp.zeros((C_CHUNK, C_DIM), F32)
    for gi in range(C_GROUPS):
        w = jnp.where(r_t >= r_s, ws_ref[gi], 0.0).astype(BF16)
        m = jnp.dot(w, vn_b, preferred_element_type=F32)
        in_group = (lane >= gi * C_GROUP_DIM) & (lane < (gi + 1) * C_GROUP_DIM)
        mixed = jnp.where(in_group, m, mixed)
    o_ref[0] = (u * (mixed + bias_ref[...])).astype(o_ref.dtype)


def _gmlp(cuv, vg, vb, ws, bs):
    bn, s, _ = cuv.shape
    bias = jnp.repeat(bs.T, C_GROUP_DIM, axis=1)
    full = lambda a: pl.BlockSpec(a.shape, lambda b, t: (0,) * a.ndim)
    return pl.pallas_call(
        _gmlp_kernel,
        grid=(bn, s // C_CHUNK),
        in_specs=[pl.BlockSpec((1, C_CHUNK, 2 * C_DIM), lambda b, t: (b, t, 0)),
                  full(vg), full(vb), full(ws), full(bias)],
        out_specs=pl.BlockSpec((1, C_CHUNK, C_DIM), lambda b, t: (b, t, 0)),
        out_shape=jax.ShapeDtypeStruct((bn, s, C_DIM), BF16),
        compiler_params=_cparams("parallel", "parallel"),
    )(cuv, vg, vb, ws, bias)


def _outproj_kernel(x_ref, a_ref, b_ref, c_ref, wa_ref, wb_ref, wc_ref, g1_ref, o_ref):
    mix = (jnp.dot(a_ref[0], wa_ref[...], preferred_element_type=F32)
           + jnp.dot(b_ref[0], wb_ref[...], preferred_element_type=F32)
           + jnp.dot(c_ref[0], wc_ref[...], preferred_element_type=F32))
    o_ref[0] = x_ref[0] + g1_ref[0] * mix


def _outproj(x, a, b, c, w_out_l, g1, tm):
    bn, s, d = x.shape
    wa = w_out_l[:A_DIM].astype(BF16)
    wb = w_out_l[A_DIM:A_DIM + B_DIM].astype(BF16)
    wc = w_out_l[A_DIM + B_DIM:].astype(BF16)
    tok = lambda w: pl.BlockSpec((1, tm, w), lambda bb, t: (bb, t, 0))
    full = lambda arr: pl.BlockSpec(arr.shape, lambda bb, t: (0, 0))
    return pl.pallas_call(
        _outproj_kernel,
        grid=(bn, s // tm),
        in_specs=[tok(d), tok(A_DIM), tok(B_DIM), tok(C_DIM), full(wa), full(wb), full(wc),
                  pl.BlockSpec((1, 1, d), lambda bb, t: (bb, 0, 0))],
        out_specs=tok(d),
        out_shape=jax.ShapeDtypeStruct((bn, s, d), F32),
        compiler_params=_cparams("parallel", "parallel"),
    )(x, a, b, c, wa, wb, wc, g1)


def _ffn_kernel(x_ref, sh_ref, sc_ref, gate_ref, g_ref, wg_ref, wu_ref, wd_ref, o_ref, h_scr, acc_scr):
    f = pl.program_id(2)

    @pl.when(f == 0)
    def _():
        h_scr[...] = _modulated_rmsnorm(x_ref[0], g_ref[...], sc_ref[0], sh_ref[0]).astype(BF16)
        acc_scr[...] = jnp.zeros_like(acc_scr)

    h = h_scr[...]
    a = jnp.dot(h, wg_ref[...], preferred_element_type=F32)
    u = jnp.dot(h, wu_ref[...], preferred_element_type=F32)
    act = (_silu(a) * u).astype(BF16)
    acc_scr[...] += jnp.dot(act, wd_ref[...], preferred_element_type=F32)

    @pl.when(f == pl.num_programs(2) - 1)
    def _():
        o_ref[0] = x_ref[0] + gate_ref[0] * acc_scr[...]


def _ffn(x, sh, sc, gate, g, wg, wu, wd, tm, tf):
    bn, s, d = x.shape
    fdim = wg.shape[1]
    tok = pl.BlockSpec((1, tm, d), lambda b, t, f: (b, t, 0))
    vec = pl.BlockSpec((1, 1, d), lambda b, t, f: (b, 0, 0))
    return pl.pallas_call(
        _ffn_kernel,
        grid=(bn, s // tm, fdim // tf),
        in_specs=[tok, vec, vec, vec,
                  pl.BlockSpec((1, d), lambda b, t, f: (0, 0)),
                  pl.BlockSpec((d, tf), lambda b, t, f: (0, f)),
                  pl.BlockSpec((d, tf), lambda b, t, f: (0, f)),
                  pl.BlockSpec((tf, d), lambda b, t, f: (f, 0))],
        out_specs=tok,
        out_shape=jax.ShapeDtypeStruct((bn, s, d), F32),
        scratch_shapes=[pltpu.VMEM((tm, d), BF16), pltpu.VMEM((tm, d), F32)],
        compiler_params=_cparams("parallel", "parallel", "arbitrary"),
    )(x, sh, sc, gate, g, wg, wu, wd)


MOE_RC = 256


def _moe_kernel(x_ref, sh_ref, sc_ref, gate_ref, g_ref, wrt_ref, wg_ref, wu_ref, wd_ref, o_ref,
                h_scr, xe_scr, ye_scr, gcol_scr, gates_scr, rsel_scr, rcol_scr, cnt_smem):
    e = pl.program_id(2)
    f = pl.program_id(3)
    tm = x_ref.shape[1]
    rc_rows = MOE_RC
    neg_inf = jnp.float32(-jnp.inf)

    @pl.when((e == 0) & (f == 0))
    def _route():
        h = _modulated_rmsnorm(x_ref[0], g_ref[...], sc_ref[0], sh_ref[0])
        h_scr[...] = h.astype(BF16)
        o_ref[0] = jnp.zeros((tm, o_ref.shape[2]), F32)
        logits = lax.dot_general(wrt_ref[...], h, NT_DIMS, precision=lax.Precision.HIGHEST,
                                 preferred_element_type=F32)
        ridx = lax.broadcasted_iota(I32, (N_EXPERTS, tm), 0)
        m1 = jnp.max(logits, axis=0, keepdims=True)
        i1 = jnp.min(jnp.where(logits == m1, ridx, N_EXPERTS), axis=0, keepdims=True)
        rest = jnp.where(ridx == i1, neg_inf, logits)
        m2 = jnp.max(rest, axis=0, keepdims=True)
        i2 = jnp.min(jnp.where(rest == m2, ridx, N_EXPERTS), axis=0, keepdims=True)
        e2 = jnp.exp(m2 - m1)
        den = 1.0 + e2
        gates_scr[...] = jnp.where(ridx == i1, 1.0 / den, jnp.where(ridx == i2, e2 / den, 0.0))
        sel = jnp.where(ridx == i1, 1.0, jnp.where(ridx == i2, 1.0, 0.0))
        ur = lax.broadcasted_iota(I32, (LANES, LANES), 0)
        uc = lax.broadcasted_iota(I32, (LANES, LANES), 1)
        utri = jnp.where(ur <= uc, 1.0, 0.0).astype(BF16)
        carry = jnp.zeros((N_EXPERTS, 1), F32)
        for kb in range(tm // LANES):
            sb = sel[:, kb * LANES:(kb + 1) * LANES]
            pref = jnp.dot(sb.astype(BF16), utri, preferred_element_type=F32) + carry
            rsel_scr[:, kb * LANES:(kb + 1) * LANES] = jnp.where(sb > 0.0, pref, -1.0)
            carry = pref[:, LANES - 1:LANES]
        r8 = lax.broadcasted_iota(I32, (N_EXPERTS, 1), 0)
        for ee in range(N_EXPERTS):
            cnt_smem[ee] = jnp.sum(jnp.where(r8 == ee, carry, 0.0)).astype(I32)

    n_rc = (cnt_smem[e] + (rc_rows - 1)) // rc_rows

    @pl.when(f == 0)
    def _gather():
        rsel_row = rsel_scr[pl.ds(e, 1), :]
        gate_row = gates_scr[pl.ds(e, 1), :]
        r8 = lax.broadcasted_iota(I32, (N_EXPERTS, LANES), 0)
        pick = jnp.where(r8 == e, 1.0, 0.0)
        rcol_scr[...] = lax.dot_general(rsel_scr[...], pick, TN_DIMS, precision=lax.Precision.HIGHEST,
                                        preferred_element_type=F32)
        h = h_scr[...]

        def body(rc, carry):
            r0 = pl.multiple_of(rc * rc_rows, rc_rows)
            want = (r0 + 1 + lax.broadcasted_iota(I32, (rc_rows, tm), 0)).astype(F32)
            pm = rsel_row == want
            pb = jnp.where(pm, 1.0, 0.0).astype(BF16)
            xe_scr[pl.ds(r0, rc_rows), :] = jnp.dot(pb, h, preferred_element_type=F32).astype(BF16)
            gcol = jnp.sum(jnp.where(pm, gate_row, 0.0), axis=1, keepdims=True)
            gcol_scr[pl.ds(r0, rc_rows), :] = jnp.broadcast_to(gcol, (rc_rows, LANES))
            ye_scr[pl.ds(r0, rc_rows), :] = jnp.zeros((rc_rows, ye_scr.shape[1]), F32)
            return carry

        lax.fori_loop(0, n_rc, body, 0)

    def ffn_body(rc, carry):
        r0 = pl.multiple_of(rc * rc_rows, rc_rows)
        xe = xe_scr[pl.ds(r0, rc_rows), :]
        a = jnp.dot(xe, wg_ref[0], preferred_element_type=F32)
        u = jnp.dot(xe, wu_ref[0], preferred_element_type=F32)
        act = (_silu(a) * u * gcol_scr[pl.ds(r0, rc_rows), 0:1]).astype(BF16)
        ye_scr[pl.ds(r0, rc_rows), :] += jnp.dot(act, wd_ref[0], preferred_element_type=F32)
        return carry

    lax.fori_loop(0, n_rc, ffn_body, 0)

    @pl.when(f == pl.num_programs(3) - 1)
    def _scatter():
        rcol = rcol_scr[:, 0:1]

        def body(rc, carry):
            r0 = pl.multiple_of(rc * rc_rows, rc_rows)
            want = (r0 + 1 + lax.broadcasted_iota(I32, (tm, rc_rows), 1)).astype(F32)
            pt = jnp.where(rcol == want, 1.0, 0.0).astype(BF16)
            ye = ye_scr[pl.ds(r0, rc_rows), :].astype(BF16)
            o_ref[0] += jnp.dot(pt, ye, preferred_element_type=F32)
            return carry

        lax.fori_loop(0, n_rc, body, 0)

    @pl.when((e == pl.num_programs(2) - 1) & (f == pl.num_programs(3) - 1))
    def _residual():
        o_ref[0] = x_ref[0] + gate_ref[0] * o_ref[0]


def _moe(x, sh, sc, gate, g, w_router, wg, wu, wd, tm, tf):
    bn, s, d = x.shape
    ne, _, fdim = wg.shape
    assert ne == N_EXPERTS and tm % MOE_RC == 0
    tok_in = pl.BlockSpec((1, tm, d), lambda b, t, e, f: (b, t, 0), pipeline_mode=pl.Buffered(1))
    tok_out = pl.BlockSpec((1, tm, d), lambda b, t, e, f: (b, t, 0))
    vec = pl.BlockSpec((1, 1, d), lambda b, t, e, f: (b, 0, 0))
    return pl.pallas_call(
        _moe_kernel,
        grid=(bn, s // tm, ne, fdim // tf),
        in_specs=[tok_in, vec, vec, vec,
                  pl.BlockSpec((1, d), lambda b, t, e, f: (0, 0)),
                  pl.BlockSpec((ne, d), lambda b, t, e, f: (0, 0)),
                  pl.BlockSpec((1, d, tf), lambda b, t, e, f: (e, 0, f)),
                  pl.BlockSpec((1, d, tf), lambda b, t, e, f: (e, 0, f)),
                  pl.BlockSpec((1, tf, d), lambda b, t, e, f: (e, f, 0))],
        out_specs=tok_out,
        out_shape=jax.ShapeDtypeStruct((bn, s, d), F32),
        scratch_shapes=[pltpu.VMEM((tm, d), BF16),
                        pltpu.VMEM((tm, d), BF16),
                        pltpu.VMEM((tm, d), F32),
                        pltpu.VMEM((tm, LANES), F32),
                        pltpu.VMEM((ne, tm), F32),
                        pltpu.VMEM((ne, tm), F32),
                        pltpu.VMEM((tm, LANES), F32),
                        pltpu.SMEM((ne,), I32)],
        compiler_params=pltpu.CompilerParams(
            dimension_semantics=("parallel", "parallel", "arbitrary", "arbitrary"),
            vmem_limit_bytes=56 * 1024 * 1024),
    )(x, sh, sc, gate, g, w_router.T, wg, wu, wd)


def _final_norm_kernel(x_ref, g_ref, o_ref):
    x = x_ref[0]
    var = jnp.mean(x * x, axis=-1, keepdims=True)
    o_ref[0] = (x * lax.rsqrt(var + EPS)) * g_ref[...]


def _final_norm(x, g, tm):
    bn, s, d = x.shape
    tok = pl.BlockSpec((1, tm, d), lambda b, t: (b, t, 0))
    return pl.pallas_call(
        _final_norm_kernel,
        grid=(bn, s // tm),
        in_specs=[tok, pl.BlockSpec((1, d), lambda b, t: (0, 0))],
        out_specs=tok,
        out_shape=jax.ShapeDtypeStruct((bn, s, d), F32),
        compiler_params=_cparams("parallel", "parallel"),
    )(x, g)


def _token_tile(s, want):
    tm = min(want, s)
    assert s % tm == 0 and tm % QB == 0
    return tm


def kernel(x, c, positions, w_ada, b_ada, g_norm_mix, g_norm_ffn, w_in, w_out, hgrn_lb_logits, hgrn_out_norm, gmlp_vnorm_g, gmlp_vnorm_b, gmlp_w_s, gmlp_b_s, ffn_w_gate, ffn_w_up, ffn_w_down, moe_w_router, moe_w_gate, moe_w_up, moe_w_down, g_final):
    bn, s, d = x.shape
    depth = w_in.shape[0]
    assert s % QB == 0 and d == sum(IN_WIDTHS[:1]) + B_DIM + C_DIM
    tm_proj = _token_tile(s, 512)
    tm_ffn = _token_tile(s, 1024)

    p_lb = jax.nn.softmax(hgrn_lb_logits.astype(F32), axis=0)
    cum = jnp.cumsum(p_lb, axis=0)
    lower_bounds = cum - cum[0:1]

    mod = _ada_mod(c, w_ada, b_ada)
    cosn, sinn, cost, sint = _rope_tables(positions)

    for l in range(depth):
        sh1, sc1, g1, sh2, sc2, g2 = [mod[l, :, i * d:(i + 1) * d].reshape(bn, 1, d) for i in range(6)]
        w_n, w_t = _prep_in_weights(w_in[l])
        (kz, ik3, bq, bf, bi, bg, cuv, qt, iq3t, vt4, iwt) = _inproj(
            x, sh1, sc1, g_norm_mix[l].reshape(1, d), w_n, w_t, cosn, sinn, cost, sint, tm_proj)
        a_out = _dsa(ik3, kz, vt4, iq3t, iwt, qt)
        b_out = _hgrn(bq, bf, bi, bg, lower_bounds[l].reshape(1, B_DIM),
                      jnp.tile(hgrn_out_norm[l], LANES // B_DV).reshape(1, LANES))
        c_out = _gmlp(cuv, gmlp_vnorm_g[l].reshape(1, C_DIM), gmlp_vnorm_b[l].reshape(1, C_DIM),
                      gmlp_w_s[l], gmlp_b_s[l])
        x = _outproj(x, a_out, b_out, c_out, w_out[l], g1, tm_proj)
        gf = g_norm_ffn[l].reshape(1, d)
        if l % 2 == 0:
            i = l // 2
            x = _ffn(x, sh2, sc2, g2, gf, ffn_w_gate[i].astype(BF16), ffn_w_up[i].astype(BF16),
                     ffn_w_down[i].astype(BF16), tm_ffn, 256)
        else:
            i = l // 2
            x = _moe(x, sh2, sc2, g2, gf, moe_w_router[i], moe_w_gate[i].astype(BF16),
                     moe_w_up[i].astype(BF16), moe_w_down[i].astype(BF16), _token_tile(s, 2048), 512)
    return _final_norm(x, g_final.reshape(1, d), tm_proj)
```

```python
import functools

import numpy as np
import jax
import jax.numpy as jnp
from jax import lax
from jax.experimental import pallas as pl
from jax.experimental.pallas import tpu as pltpu

F32 = jnp.float32
BF16 = jnp.bfloat16
I32 = jnp.int32

HEAD_DIM = 64
A_HEADS = 6
A_DIM = A_HEADS * HEAD_DIM
IDX_HEADS = 4
IDX_DIM = 64
TOPK_MAX = 256
B_HEADS = 6
B_DK = 64
B_DV = 64
B_DIM = B_HEADS * B_DV
HGRN_CHUNK = 64
C_GROUPS = 4
C_GROUP_DIM = 64
C_DIM = C_GROUPS * C_GROUP_DIM
C_CHUNK = 128
ROPE_THETA = 10000.0
N_EXPERTS = 8
EPS = 1e-6
IN_WIDTHS = (A_DIM, HEAD_DIM, HEAD_DIM, IDX_HEADS * IDX_DIM, IDX_DIM, IDX_HEADS,
             B_DIM, B_DIM, B_DIM, B_DIM, 2 * C_DIM)

LANES = 128
QB = 128
VMEM_LIMIT = 48 * 1024 * 1024
HGRN_FAST_SPAN = 80.0
HGRN_SUB = 16
HGRN_GROUP = 4
INT_MIN = -2 ** 31

NT_DIMS = (((1,), (1,)), ((), ()))
TN_DIMS = (((0,), (0,)), ((), ()))


def _cparams(*sem):
    return pltpu.CompilerParams(dimension_semantics=sem, vmem_limit_bytes=VMEM_LIMIT)


def _split2(x):
    hi = x.astype(BF16)
    lo = (x - hi.astype(F32)).astype(BF16)
    return hi, lo


def _split3(x):
    hi = x.astype(BF16)
    r1 = x - hi.astype(F32)
    mid = r1.astype(BF16)
    lo = (r1 - mid.astype(F32)).astype(BF16)
    return hi, mid, lo


def _silu(x):
    return x * (1.0 / (1.0 + jnp.exp(-x)))


def _modulated_rmsnorm(x, g, sc, sh):
    var = jnp.mean(x * x, axis=-1, keepdims=True)
    y = x * lax.rsqrt(var + EPS)
    return (y * g) * (1.0 + sc) + sh


def _ada_kernel(c_ref, w_ref, b_ref, o_ref):
    cond = _silu(c_ref[...])
    o_ref[0] = jnp.dot(cond, w_ref[0], precision=lax.Precision.HIGHEST,
                       preferred_element_type=F32) + b_ref[0]


def _ada_mod(c, w_ada, b_ada):
    depth, d, d6 = w_ada.shape
    bn = c.shape[0]
    tn = 1536
    return pl.pallas_call(
        _ada_kernel,
        grid=(depth, d6 // tn),
        in_specs=[pl.BlockSpec((bn, d), lambda l, n: (0, 0)),
                  pl.BlockSpec((1, d, tn), lambda l, n: (l, 0, n)),
                  pl.BlockSpec((1, 1, tn), lambda l, n: (l, 0, n))],
        out_specs=pl.BlockSpec((1, bn, tn), lambda l, n: (l, 0, n)),
        out_shape=jax.ShapeDtypeStruct((depth, bn, d6), F32),
        compiler_params=_cparams("parallel", "parallel"),
    )(c, w_ada, b_ada.reshape(depth, 1, d6))


def _rope_kernel(posn_ref, post_ref, invn_ref, signn_ref, invt_ref,
                 cosn_ref, sinn_ref, cost_ref, sint_ref):
    ang_n = posn_ref[0].astype(F32) * invn_ref[...]
    cosn_ref[0] = jnp.cos(ang_n)
    sinn_ref[0] = jnp.sin(ang_n) * signn_ref[...]
    ang_t = invt_ref[...] * post_ref[0].astype(F32)
    cost_ref[0] = jnp.cos(ang_t)
    sint_ref[0] = jnp.sin(ang_t)


def _rope_tables(positions):
    bn, s = positions.shape
    half = HEAD_DIM // 2
    inv = ROPE_THETA ** (-jnp.arange(0, HEAD_DIM, 2, dtype=F32) / HEAD_DIM)
    inv_n = jnp.tile(inv, LANES // half).reshape(1, LANES)
    sign_n = jnp.tile(jnp.concatenate([-jnp.ones((half,), F32), jnp.ones((half,), F32)]),
                      LANES // HEAD_DIM).reshape(1, LANES)
    inv_t = inv.reshape(half, 1)
    full = lambda shape: pl.BlockSpec(shape, lambda b: (0,) * len(shape))
    return pl.pallas_call(
        _rope_kernel,
        grid=(bn,),
        in_specs=[pl.BlockSpec((1, s, 1), lambda b: (b, 0, 0)),
                  pl.BlockSpec((1, 1, s), lambda b: (b, 0, 0)),
                  full((1, LANES)), full((1, LANES)), full((half, 1))],
        out_specs=[pl.BlockSpec((1, s, LANES), lambda b: (b, 0, 0)),
                   pl.BlockSpec((1, s, LANES), lambda b: (b, 0, 0)),
                   pl.BlockSpec((1, half, s), lambda b: (b, 0, 0)),
                   pl.BlockSpec((1, half, s), lambda b: (b, 0, 0))],
        out_shape=[jax.ShapeDtypeStruct((bn, s, LANES), F32),
                   jax.ShapeDtypeStruct((bn, s, LANES), F32),
                   jax.ShapeDtypeStruct((bn, half, s), F32),
                   jax.ShapeDtypeStruct((bn, half, s), F32)],
        compiler_params=_cparams("parallel"),
    )(positions.reshape(bn, s, 1), positions.reshape(bn, 1, s), inv_n, sign_n, inv_t)


N_KZ = 0
N_IK = 2 * LANES
N_BQ = 4 * LANES
N_BF = N_BQ + B_DIM
N_BI = N_BF + B_DIM
N_BG = N_BI + B_DIM
N_CUV = N_BG + B_DIM
N_COLS = N_CUV + 2 * C_DIM
T_Q = 0
T_IQ = A_DIM
T_V = T_IQ + IDX_HEADS * IDX_DIM
T_IW = T_V + HEAD_DIM
T_ROWS = T_IW + 8


def _prep_in_weights(w_in_l):
    offs = np.concatenate([[0], np.cumsum(IN_WIDTHS)])
    sl = lambda i: w_in_l[:, int(offs[i]):int(offs[i + 1])]
    aq, ak, av, iq, ik, iw, bq, bf, bi, bg, cuv = [sl(i) for i in range(11)]
    half = HEAD_DIM // 2
    rot = lambda w: jnp.concatenate([w[:, half:], w[:, :half]], axis=1)
    z = jnp.zeros_like(ak)
    w_n = jnp.concatenate([ak, z, rot(ak), z, ik, ik, rot(ik), rot(ik), bq, bf, bi, bg, cuv], axis=1)
    w_t = jnp.concatenate([aq, iq, av, iw, jnp.zeros((w_in_l.shape[0], 4), w_in_l.dtype)], axis=1).T
    return w_n.astype(BF16), w_t.astype(BF16)


def _inproj_kernel(x_ref, sh_ref, sc_ref, g_ref, wn_ref, wt_ref, cosn_ref, sinn_ref, cost_ref, sint_ref,
                   kz_ref, ik3_ref, bq_ref, bf_ref, bi_ref, bg_ref, cuv_ref,
                   qt_ref, iq3t_ref, vt_ref, iwt_ref):
    tm = x_ref.shape[1]
    half = HEAD_DIM // 2
    h = _modulated_rmsnorm(x_ref[0], g_ref[...], sc_ref[0], sh_ref[0]).astype(BF16)

    cosn = cosn_ref[0]
    sinn = sinn_ref[0]
    zk = jnp.dot(h, wn_ref[:, N_KZ:N_KZ + 2 * LANES], preferred_element_type=F32)
    kz_ref[0] = (zk[:, :LANES] * cosn + zk[:, LANES:] * sinn).astype(BF16)
    zi = jnp.dot(h, wn_ref[:, N_IK:N_IK + 2 * LANES], preferred_element_type=F32)
    ik2 = zi[:, :LANES] * cosn + zi[:, LANES:] * sinn
    hi, lo = _split2(ik2)
    lane = lax.broadcasted_iota(I32, (tm, LANES), 1)
    ik3_ref[0, :, :LANES] = jnp.where(lane < IDX_DIM, hi, lo)
    ik3_ref[0, :, LANES:] = hi
    bq_ref[0] = jnp.dot(h, wn_ref[:, N_BQ:N_BF], preferred_element_type=F32).astype(BF16)
    bf_ref[0] = jnp.dot(h, wn_ref[:, N_BF:N_BI], preferred_element_type=F32)
    bi_ref[0] = jnp.dot(h, wn_ref[:, N_BI:N_BG], preferred_element_type=F32).astype(BF16)
    bg_ref[0] = jnp.dot(h, wn_ref[:, N_BG:N_CUV], preferred_element_type=F32).astype(BF16)
    cuv_ref[0] = jnp.dot(h, wn_ref[:, N_CUV:N_COLS], preferred_element_type=F32).astype(BF16)

    zt = lax.dot_general(wt_ref[...], h, NT_DIMS, preferred_element_type=F32)
    cost = cost_ref[0]
    sint = sint_ref[0]
    qscale = HEAD_DIM ** -0.5
    for hh in range(A_HEADS):
        r0 = T_Q + hh * HEAD_DIM
        x1 = zt[r0:r0 + half]
        x2 = zt[r0 + half:r0 + HEAD_DIM]
        qt_ref[0, hh * HEAD_DIM:hh * HEAD_DIM + half] = ((x1 * cost - x2 * sint) * qscale).astype(BF16)
        qt_ref[0, hh * HEAD_DIM + half:(hh + 1) * HEAD_DIM] = ((x2 * cost + x1 * sint) * qscale).astype(BF16)
    iscale = IDX_DIM ** -0.5
    zero = jnp.zeros((IDX_DIM, tm), BF16)
    for hh in range(IDX_HEADS):
        r0 = T_IQ + hh * IDX_DIM
        x1 = zt[r0:r0 + half]
        x2 = zt[r0 + half:r0 + IDX_DIM]
        y = jnp.concatenate([(x1 * cost - x2 * sint) * iscale, (x2 * cost + x1 * sint) * iscale], axis=0)
        hi, lo = _split2(y)
        iq3t_ref[0, hh, 0 * IDX_DIM:1 * IDX_DIM] = hi
        iq3t_ref[0, hh, 1 * IDX_DIM:2 * IDX_DIM] = hi
        iq3t_ref[0, hh, 2 * IDX_DIM:3 * IDX_DIM] = lo
        iq3t_ref[0, hh, 3 * IDX_DIM:4 * IDX_DIM] = zero
    vt = zt[T_V:T_V + HEAD_DIM].astype(BF16)
    for i in range(tm // QB):
        vt_ref[0, i] = vt[:, i * QB:(i + 1) * QB]
    iwt_ref[0] = zt[T_IW:T_IW + 8] * (IDX_HEADS ** -0.5)


def _inproj(x, sh, sc, g, w_n, w_t, cosn, sinn, cost, sint, tm):
    bn, s, d = x.shape
    nt = s // tm
    half = HEAD_DIM // 2
    tok = lambda w: pl.BlockSpec((1, tm, w), lambda b, t: (b, t, 0))
    vec = pl.BlockSpec((1, 1, d), lambda b, t: (b, 0, 0))
    full2 = lambda a: pl.BlockSpec(a.shape, lambda b, t: (0, 0))
    out_shapes = [
        jax.ShapeDtypeStruct((bn, s, LANES), BF16),
        jax.ShapeDtypeStruct((bn, s, 2 * LANES), BF16),
        jax.ShapeDtypeStruct((bn, s, B_DIM), BF16),
        jax.ShapeDtypeStruct((bn, s, B_DIM), F32),
        jax.ShapeDtypeStruct((bn, s, B_DIM), BF16),
        jax.ShapeDtypeStruct((bn, s, B_DIM), BF16),
        jax.ShapeDtypeStruct((bn, s, 2 * C_DIM), BF16),
        jax.ShapeDtypeStruct((bn, A_DIM, s), BF16),
        jax.ShapeDtypeStruct((bn, IDX_HEADS, 4 * IDX_DIM, s), BF16),
        jax.ShapeDtypeStruct((bn, s // QB, HEAD_DIM, QB), BF16),
        jax.ShapeDtypeStruct((bn, 8, s), F32),
    ]
    out_specs = [
        tok(LANES), tok(2 * LANES), tok(B_DIM), tok(B_DIM), tok(B_DIM), tok(B_DIM), tok(2 * C_DIM),
        pl.BlockSpec((1, A_DIM, tm), lambda b, t: (b, 0, t)),
        pl.BlockSpec((1, IDX_HEADS, 4 * IDX_DIM, tm), lambda b, t: (b, 0, 0, t)),
        pl.BlockSpec((1, tm // QB, HEAD_DIM, QB), lambda b, t: (b, t, 0, 0)),
        pl.BlockSpec((1, 8, tm), lambda b, t: (b, 0, t)),
    ]
    return pl.pallas_call(
        _inproj_kernel,
        grid=(bn, nt),
        in_specs=[tok(d), vec, vec, full2(g), full2(w_n), full2(w_t),
                  tok(LANES), tok(LANES),
                  pl.BlockSpec((1, half, tm), lambda b, t: (b, 0, t)),
                  pl.BlockSpec((1, half, tm), lambda b, t: (b, 0, t))],
        out_specs=out_specs,
        out_shape=out_shapes,
        compiler_params=_cparams("parallel", "parallel"),
    )(x, sh, sc, g, w_n, w_t, cosn, sinn, cost, sint)


def _dsa_kernel(n_top, ik3_ref, kz_ref, vt_ref, iq3t_ref, iwt_ref, qt_ref, out_ref, key_scr, lg_scr):
    j = pl.program_id(1)
    nkc = j + 1
    row = lax.broadcasted_iota(I32, (QB, QB), 0)
    col = lax.broadcasted_iota(I32, (QB, QB), 1)
    int_min = jnp.int32(INT_MIN)
    sub = QB // 8

    def score_body(c, carry):
        ks = pl.multiple_of(c * QB, QB)
        ikc = ik3_ref[0, pl.ds(ks, QB), :]
        sc = jnp.zeros((QB, QB), F32)
        for hh in range(IDX_HEADS):
            rel = jnp.dot(ikc, iq3t_ref[0, hh], preferred_element_type=F32)
            sc = sc + jnp.maximum(rel, 0.0) * iwt_ref[0, hh:hh + 1, :]
        sc = jnp.where(sc == 0.0, 0.0, sc)
        bits = pltpu.bitcast(sc, I32)
        key = bits ^ ((bits >> 31) & jnp.int32(0x7FFFFFFF))
        causal = (ks + row) <= (j * QB + col)
        key_scr[pl.ds(ks, QB), :] = jnp.where(causal, key, int_min)
        return carry

    lax.fori_loop(0, nkc, score_body, 0)

    def count(pred_fn):
        def body(c, acc):
            ks = pl.multiple_of(c * QB, QB)
            m = jnp.where(pred_fn(key_scr[pl.ds(ks, QB), :]), 1, 0).astype(I32)
            return acc + jnp.sum(m.reshape(sub, 8, QB), axis=0)
        acc = lax.fori_loop(0, nkc, body, jnp.zeros((8, QB), I32))
        return jnp.sum(acc, axis=0, keepdims=True)

    def bit_body(i, t_u):
        cand = t_u | jnp.left_shift(jnp.int32(1), 31 - i)
        cand_s = cand ^ int_min
        cnt = count(lambda k: k >= cand_s)
        return jnp.where(cnt >= n_top, cand, t_u)

    t_u = lax.fori_loop(0, 32, bit_body, jnp.zeros((1, QB), I32))
    thr = t_u ^ int_min
    n_gt = count(lambda k: k > thr)
    n_tie = (n_top - n_gt).astype(F32)
    thr_valid = jnp.where(thr > int_min, 1.0, 0.0)

    ltri = jnp.where(row >= col, 1.0, 0.0).astype(BF16)
    zpad = jnp.zeros((HEAD_DIM, QB), BF16)
    qpad = [jnp.concatenate([qt_ref[0, hh * HEAD_DIM:(hh + 1) * HEAD_DIM, :], zpad], axis=0)
            for hh in range(A_HEADS)]
    neg_inf = jnp.float32(-jnp.inf)

    def pass_a(c, carry):
        tie_cnt, ms = carry
        ks = pl.multiple_of(c * QB, QB)
        kc = key_scr[pl.ds(ks, QB), :]
        eqf = jnp.where(kc == thr, thr_valid, 0.0)
        pref = jnp.dot(ltri, eqf.astype(BF16), preferred_element_type=F32)
        keep_tie = eqf * jnp.where((tie_cnt + pref) <= n_tie, 1.0, 0.0)
        sel = jnp.where(kc > thr, 1.0, keep_tie) > 0.5
        tie_cnt = tie_cnt + pref[QB - 1:QB, :]
        kzc = kz_ref[0, pl.ds(ks, QB), :]
        new_ms = []
        for hh in range(A_HEADS):
            lt = jnp.dot(kzc, qpad[hh], preferred_element_type=F32)
            lt = jnp.where(sel, lt, neg_inf)
            lg_scr[hh, pl.ds(ks, QB), :] = lt
            new_ms.append(jnp.maximum(ms[hh], jnp.max(lt, axis=0, keepdims=True)))
        return tie_cnt, tuple(new_ms)

    init_m = tuple(jnp.full((1, QB), neg_inf, F32) for _ in range(A_HEADS))
    _, ms = lax.fori_loop(0, nkc, pass_a, (jnp.zeros((1, QB), F32), init_m))

    def pass_b(c, carry):
        ls, accs = carry
        ks = pl.multiple_of(c * QB, QB)
        vtc = vt_ref[0, c]
        new_ls, new_accs = [], []
        for hh in range(A_HEADS):
            p = jnp.exp(lg_scr[hh, pl.ds(ks, QB), :] - ms[hh])
            new_ls.append(ls[hh] + jnp.sum(p, axis=0, keepdims=True))
            new_accs.append(accs[hh] + jnp.dot(vtc, p.astype(BF16), preferred_element_type=F32))
        return tuple(new_ls), tuple(new_accs)

    init_l = tuple(jnp.zeros((1, QB), F32) for _ in range(A_HEADS))
    init_acc = tuple(jnp.zeros((HEAD_DIM, QB), F32) for _ in range(A_HEADS))
    ls, accs = lax.fori_loop(0, nkc, pass_b, (init_l, init_acc))
    o_t = jnp.concatenate([accs[hh] / ls[hh] for hh in range(A_HEADS)], axis=0)
    out_ref[0] = o_t.T.astype(BF16)


def _dsa(ik3, kz, vt4, iq3t, iwt, qt):
    bn, s, _ = kz.shape
    n_top = min(TOPK_MAX, s // 4)
    nq = s // QB
    return pl.pallas_call(
        functools.partial(_dsa_kernel, n_top),
        grid=(bn, nq),
        in_specs=[pl.BlockSpec((1, s, 2 * LANES), lambda b, q: (b, 0, 0)),
                  pl.BlockSpec((1, s, LANES), lambda b, q: (b, 0, 0)),
                  pl.BlockSpec((1, nq, HEAD_DIM, QB), lambda b, q: (b, 0, 0, 0)),
                  pl.BlockSpec((1, IDX_HEADS, 4 * IDX_DIM, QB), lambda b, q: (b, 0, 0, q)),
                  pl.BlockSpec((1, 8, QB), lambda b, q: (b, 0, q)),
                  pl.BlockSpec((1, A_DIM, QB), lambda b, q: (b, 0, q))],
        out_specs=pl.BlockSpec((1, QB, A_DIM), lambda b, q: (b, q, 0)),
        out_shape=jax.ShapeDtypeStruct((bn, s, A_DIM), BF16),
        scratch_shapes=[pltpu.VMEM((s, QB), I32), pltpu.VMEM((A_HEADS, s, QB), F32)],
        compiler_params=_cparams("parallel", "arbitrary"),
    )(ik3, kz, vt4, iq3t, iwt, qt)


def _hgrn_kernel(q_ref, f_ref, i_ref, g_ref, lb_ref, gn_ref, o_ref, st_scr, oi_scr):
    s = q_ref.shape[1]
    ch = HGRN_CHUNK
    sb = HGRN_SUB
    nsb = ch // sb
    rows = HGRN_GROUP * ch
    lane = lax.broadcasted_iota(I32, (ch, LANES), 1)
    head0 = lane < B_DK
    lr = lax.broadcasted_iota(I32, (2 * rows, rows), 0)
    lc = lax.broadcasted_iota(I32, (2 * rows, rows), 1)
    lt = jnp.where(lr >= rows, lr - rows, lr)
    same_chunk = (lt // ch) == (lc // ch)
    upper = jnp.where(lr >= rows, (lt // sb) * sb - 1, lt)
    lcat = jnp.where(same_chunk & (lc <= upper), 1.0, 0.0).astype(BF16)
    ar = lax.broadcasted_iota(I32, (2 * ch, nsb * ch), 0) % ch
    ac = lax.broadcasted_iota(I32, (2 * ch, nsb * ch), 1)
    att_mask = ((ar // sb) == (ac // ch)) & ((ac % ch) <= ar)
    bd_r = lax.broadcasted_iota(I32, (LANES, LANES), 0)
    bd_c = lax.broadcasted_iota(I32, (LANES, LANES), 1)
    same_head = (bd_r < B_DK) == (bd_c < B_DK)
    ones_bd = jnp.where(same_head, 1.0, 0.0).astype(BF16)
    s_iota = lax.broadcasted_iota(I32, (ch, LANES), 0)

    lb = lb_ref[...]
    log_lb = jnp.log(lb)
    log_1mlb = jnp.log1p(-lb)
    gn = gn_ref[...]
    st_scr[...] = jnp.zeros_like(st_scr)

    def group_body(it, carry):
        t0 = pl.multiple_of(it * rows, rows)
        z = f_ref[0, pl.ds(t0, rows), :]
        q = q_ref[0, pl.ds(t0, rows), :].astype(F32)
        v = i_ref[0, pl.ds(t0, rows), :]
        softplus_tail = jnp.log1p(jnp.exp(-jnp.abs(z)))
        log_sig = -(jnp.maximum(-z, 0.0) + softplus_tail)
        x2 = log_1mlb + log_sig
        amax = jnp.maximum(log_lb, x2)
        log_f = amax + jnp.log1p(jnp.exp(-jnp.abs(log_lb - x2)))
        kk = (1.0 - lb) * jnp.exp(-(jnp.maximum(z, 0.0) + softplus_tail))
        f_hi, f_mid, f_lo = _split3(log_f)
        br = (jnp.dot(lcat, f_hi, preferred_element_type=F32)
              + jnp.dot(lcat, f_mid, preferred_element_type=F32)
              + jnp.dot(lcat, f_lo, preferred_element_type=F32))
        b = br[:rows]
        ref = br[rows:]
        q_loc = q * jnp.exp(b - ref)
        q_chk = (q * jnp.exp(b)).astype(BF16)
        span = ref - b

        st = st_scr[...]
        o_inter = []
        for c in range(HGRN_GROUP):
            lo_r = c * ch
            b_c = b[lo_r:lo_r + ch]
            ref_c = ref[lo_r:lo_r + ch]
            kk_c = kk[lo_r:lo_r + ch]
            v_c = v[lo_r:lo_r + ch]
            b_last = b_c[ch - 1:ch]
            kcat = jnp.concatenate(
                [kk_c * jnp.exp(jnp.minimum(ref_c[i * sb:i * sb + 1] - b_c, HGRN_FAST_SPAN))
                 for i in range(nsb)], axis=0).astype(BF16)
            q_c = q_loc[lo_r:lo_r + ch]
            qm = jnp.concatenate([jnp.where(head0, q_c, 0.0), jnp.where(head0, 0.0, q_c)],
                                 axis=0).astype(BF16)
            att = lax.dot_general(qm, kcat, NT_DIMS, preferred_element_type=F32)
            att = jnp.where(att_mask, att, 0.0).astype(BF16)
            o2 = jnp.dot(att, jnp.concatenate([v_c] * nsb, axis=0), preferred_element_type=F32)
            oi_scr[lo_r:lo_r + ch, :] = jnp.where(head0, o2[:ch], o2[ch:])
            o_inter.append(lax.dot_general(q_chk[lo_r:lo_r + ch], st.astype(BF16), NT_DIMS,
                                           preferred_element_type=F32))
            kh = (kk_c * jnp.exp(b_last - b_c)).astype(BF16)
            upd = lax.dot_general(v_c, kh, TN_DIMS, preferred_element_type=F32)
            st = st * jnp.exp(b_last) + jnp.where(same_head, upd, 0.0)
        st_scr[...] = st

        for c in range(HGRN_GROUP):
            lo_r = c * ch

            @pl.when(jnp.max(span[lo_r:lo_r + ch]) > HGRN_FAST_SPAN)
            def _():
                b_c = b[lo_r:lo_r + ch]
                q_c = q[lo_r:lo_r + ch]
                kk_c = kk[lo_r:lo_r + ch]
                vf = v[lo_r:lo_r + ch].astype(F32)

                def t_body(t, carry2):
                    onehot = jnp.where(s_iota == t, 1.0, 0.0)
                    b_t = jnp.sum(onehot * b_c, axis=0, keepdims=True)
                    q_t = jnp.sum(onehot * q_c, axis=0, keepdims=True)
                    dec = jnp.exp(jnp.where(s_iota <= t, b_t - b_c, -jnp.inf))
                    w = q_t * kk_c * dec
                    w0 = jnp.sum(jnp.where(head0, w, 0.0), axis=1, keepdims=True)
                    w1 = jnp.sum(jnp.where(head0, 0.0, w), axis=1, keepdims=True)
                    a_col = jnp.where(head0, w0, w1)
                    oi_scr[pl.ds(lo_r + t, 1), :] = jnp.sum(a_col * vf, axis=0, keepdims=True)
                    return carry2

                lax.fori_loop(0, ch, t_body, 0)

        o = oi_scr[...] + jnp.concatenate(o_inter, axis=0)
        o2_hi, o2_lo = _split2(o * o)
        ss = (jnp.dot(o2_hi, ones_bd, preferred_element_type=F32)
              + jnp.dot(o2_lo, ones_bd, preferred_element_type=F32))
        y = (o * lax.rsqrt(ss * (1.0 / B_DV) + EPS)) * gn
        g = g_ref[0, pl.ds(t0, rows), :].astype(F32)
        o_ref[0, pl.ds(t0, rows), :] = (y * _silu(g)).astype(o_ref.dtype)
        return carry

    lax.fori_loop(0, s // rows, group_body, 0)


def _hgrn(bq, bf, bi, bg, lb, gn):
    bn, s, _ = bq.shape
    assert s % (HGRN_GROUP * HGRN_CHUNK) == 0
    npair = B_DIM // LANES
    tok = pl.BlockSpec((1, s, LANES), lambda b, p: (b, 0, p))
    return pl.pallas_call(
        _hgrn_kernel,
        grid=(bn, npair),
        in_specs=[tok, tok, tok, tok,
                  pl.BlockSpec((1, LANES), lambda b, p: (0, p)),
                  pl.BlockSpec((1, LANES), lambda b, p: (0, 0))],
        out_specs=tok,
        out_shape=jax.ShapeDtypeStruct((bn, s, B_DIM), BF16),
        scratch_shapes=[pltpu.VMEM((LANES, LANES), F32),
                        pltpu.VMEM((HGRN_GROUP * HGRN_CHUNK, LANES), F32)],
        compiler_params=_cparams("parallel", "parallel"),
    )(bq, bf, bi, bg, lb, gn)


def _gmlp_kernel(uv_ref, vg_ref, vb_ref, ws_ref, bias_ref, o_ref):
    uv = uv_ref[0].astype(F32)
    uv = 0.5 * uv * (1.0 + lax.erf(uv * (2.0 ** -0.5)))
    u = uv[:, :C_DIM]
    v = uv[:, C_DIM:]
    mu = jnp.mean(v, axis=-1, keepdims=True)
    var = jnp.mean(jnp.square(v - mu), axis=-1, keepdims=True)
    vn = ((v - mu) * lax.rsqrt(var + EPS)) * vg_ref[...] + vb_ref[...]
    vn_b = vn.astype(BF16)
    r_t = lax.broadcasted_iota(I32, (C_CHUNK, C_CHUNK), 0)
    r_s = lax.broadcasted_iota(I32, (C_CHUNK, C_CHUNK), 1)
    lane = lax.broadcasted_iota(I32, (C_CHUNK, C_DIM), 1)
    mixed = jnp.zeros((C_CHUNK, C_DIM), F32)
    for gi in range(C_GROUPS):
        w = jnp.where(r_t >= r_s, ws_ref[gi], 0.0).astype(BF16)
        m = jnp.dot(w, vn_b, preferred_element_type=F32)
        in_group = (lane >= gi * C_GROUP_DIM) & (lane < (gi + 1) * C_GROUP_DIM)
        mixed = jnp.where(in_group, m, mixed)
    o_ref[0] = (u * (mixed + bias_ref[...])).astype(o_ref.dtype)


def _gmlp(cuv, vg, vb, ws, bs):
    bn, s, _ = cuv.shape
    bias = jnp.repeat(bs.T, C_GROUP_DIM, axis=1)
    full = lambda a: pl.BlockSpec(a.shape, lambda b, t: (0,) * a.ndim)
    return pl.pallas_call(
        _gmlp_kernel,
        grid=(bn, s // C_CHUNK),
        in_specs=[pl.BlockSpec((1, C_CHUNK, 2 * C_DIM), lambda b, t: (b, t, 0)),
                  full(vg), full(vb), full(ws), full(bias)],
        out_specs=pl.BlockSpec((1, C_CHUNK, C_DIM), lambda b, t: (b, t, 0)),
        out_shape=jax.ShapeDtypeStruct((bn, s, C_DIM), BF16),
        compiler_params=_cparams("parallel", "parallel"),
    )(cuv, vg, vb, ws, bias)


def _outproj_kernel(x_ref, a_ref, b_ref, c_ref, wa_ref, wb_ref, wc_ref, g1_ref, o_ref):
    mix = (jnp.dot(a_ref[0], wa_ref[...], preferred_element_type=F32)
           + jnp.dot(b_ref[0], wb_ref[...], preferred_element_type=F32)
           + jnp.dot(c_ref[0], wc_ref[...], preferred_element_type=F32))
    o_ref[0] = x_ref[0] + g1_ref[0] * mix


def _outproj(x, a, b, c, w_out_l, g1, tm):
    bn, s, d = x.shape
    wa = w_out_l[:A_DIM].astype(BF16)
    wb = w_out_l[A_DIM:A_DIM + B_DIM].astype(BF16)
    wc = w_out_l[A_DIM + B_DIM:].astype(BF16)
    tok = lambda w: pl.BlockSpec((1, tm, w), lambda bb, t: (bb, t, 0))
    full = lambda arr: pl.BlockSpec(arr.shape, lambda bb, t: (0, 0))
    return pl.pallas_call(
        _outproj_kernel,
        grid=(bn, s // tm),
        in_specs=[tok(d), tok(A_DIM), tok(B_DIM), tok(C_DIM), full(wa), full(wb), full(wc),
                  pl.BlockSpec((1, 1, d), lambda bb, t: (bb, 0, 0))],
        out_specs=tok(d),
        out_shape=jax.ShapeDtypeStruct((bn, s, d), F32),
        compiler_params=_cparams("parallel", "parallel"),
    )(x, a, b, c, wa, wb, wc, g1)


def _ffn_kernel(x_ref, sh_ref, sc_ref, gate_ref, g_ref, wg_ref, wu_ref, wd_ref, o_ref, h_scr, acc_scr):
    f = pl.program_id(2)

    @pl.when(f == 0)
    def _():
        h_scr[...] = _modulated_rmsnorm(x_ref[0], g_ref[...], sc_ref[0], sh_ref[0]).astype(BF16)
        acc_scr[...] = jnp.zeros_like(acc_scr)

    h = h_scr[...]
    a = jnp.dot(h, wg_ref[...], preferred_element_type=F32)
    u = jnp.dot(h, wu_ref[...], preferred_element_type=F32)
    act = (_silu(a) * u).astype(BF16)
    acc_scr[...] += jnp.dot(act, wd_ref[...], preferred_element_type=F32)

    @pl.when(f == pl.num_programs(2) - 1)
    def _():
        o_ref[0] = x_ref[0] + gate_ref[0] * acc_scr[...]


def _ffn(x, sh, sc, gate, g, wg, wu, wd, tm, tf):
    bn, s, d = x.shape
    fdim = wg.shape[1]
    tok = pl.BlockSpec((1, tm, d), lambda b, t, f: (b, t, 0))
    vec = pl.BlockSpec((1, 1, d), lambda b, t, f: (b, 0, 0))
    return pl.pallas_call(
        _ffn_kernel,
        grid=(bn, s // tm, fdim // tf),
        in_specs=[tok, vec, vec, vec,
                  pl.BlockSpec((1, d), lambda b, t, f: (0, 0)),
                  pl.BlockSpec((d, tf), lambda b, t, f: (0, f)),
                  pl.BlockSpec((d, tf), lambda b, t, f: (0, f)),
                  pl.BlockSpec((tf, d), lambda b, t, f: (f, 0))],
        out_specs=tok,
        out_shape=jax.ShapeDtypeStruct((bn, s, d), F32),
        scratch_shapes=[pltpu.VMEM((tm, d), BF16), pltpu.VMEM((tm, d), F32)],
        compiler_params=_cparams("parallel", "parallel", "arbitrary"),
    )(x, sh, sc, gate, g, wg, wu, wd)


MOE_RC = 256


def _moe_kernel(x_ref, sh_ref, sc_ref, gate_ref, g_ref, wrt_ref, wg_ref, wu_ref, wd_ref, o_ref,
                h_scr, xe_scr, ye_scr, gcol_scr, gates_scr, rsel_scr, rcol_scr, cnt_smem):
    e = pl.program_id(2)
    f = pl.program_id(3)
    tm = x_ref.shape[1]
    rc_rows = MOE_RC
    neg_inf = jnp.float32(-jnp.inf)

    @pl.when((e == 0) & (f == 0))
    def _route():
        h = _modulated_rmsnorm(x_ref[0], g_ref[...], sc_ref[0], sh_ref[0])
        h_scr[...] = h.astype(BF16)
        o_ref[0] = jnp.zeros((tm, o_ref.shape[2]), F32)
        logits = lax.dot_general(wrt_ref[...], h, NT_DIMS, precision=lax.Precision.HIGHEST,
                                 preferred_element_type=F32)
        ridx = lax.broadcasted_iota(I32, (N_EXPERTS, tm), 0)
        m1 = jnp.max(logits, axis=0, keepdims=True)
        i1 = jnp.min(jnp.where(logits == m1, ridx, N_EXPERTS), axis=0, keepdims=True)
        rest = jnp.where(ridx == i1, neg_inf, logits)
        m2 = jnp.max(rest, axis=0, keepdims=True)
        i2 = jnp.min(jnp.where(rest == m2, ridx, N_EXPERTS), axis=0, keepdims=True)
        e2 = jnp.exp(m2 - m1)
        den = 1.0 + e2
        gates_scr[...] = jnp.where(ridx == i1, 1.0 / den, jnp.where(ridx == i2, e2 / den, 0.0))
        sel = jnp.where(ridx == i1, 1.0, jnp.where(ridx == i2, 1.0, 0.0))
        ur = lax.broadcasted_iota(I32, (LANES, LANES), 0)
        uc = lax.broadcasted_iota(I32, (LANES, LANES), 1)
        utri = jnp.where(ur <= uc, 1.0, 0.0).astype(BF16)
        carry = jnp.zeros((N_EXPERTS, 1), F32)
        for kb in range(tm // LANES):
            sb = sel[:, kb * LANES:(kb + 1) * LANES]
            pref = jnp.dot(sb.astype(BF16), utri, preferred_element_type=F32) + carry
            rsel_scr[:, kb * LANES:(kb + 1) * LANES] = jnp.where(sb > 0.0, pref, -1.0)
            carry = pref[:, LANES - 1:LANES]
        r8 = lax.broadcasted_iota(I32, (N_EXPERTS, 1), 0)
        for ee in range(N_EXPERTS):
            cnt_smem[ee] = jnp.sum(jnp.where(r8 == ee, carry, 0.0)).astype(I32)

    n_rc = (cnt_smem[e] + (rc_rows - 1)) // rc_rows

    @pl.when(f == 0)
    def _gather():
        rsel_row = rsel_scr[pl.ds(e, 1), :]
        gate_row = gates_scr[pl.ds(e, 1), :]
        r8 = lax.broadcasted_iota(I32, (N_EXPERTS, LANES), 0)
        pick = jnp.where(r8 == e, 1.0, 0.0)
        rcol_scr[...] = lax.dot_general(rsel_scr[...], pick, TN_DIMS, precision=lax.Precision.HIGHEST,
                                        preferred_element_type=F32)
        h = h_scr[...]

        def body(rc, carry):
            r0 = pl.multiple_of(rc * rc_rows, rc_rows)
            want = (r0 + 1 + lax.broadcasted_iota(I32, (rc_rows, tm), 0)).astype(F32)
            pm = rsel_row == want
            pb = jnp.where(pm, 1.0, 0.0).astype(BF16)
            xe_scr[pl.ds(r0, rc_rows), :] = jnp.dot(pb, h, preferred_element_type=F32).astype(BF16)
            gcol = jnp.sum(jnp.where(pm, gate_row, 0.0), axis=1, keepdims=True)
            gcol_scr[pl.ds(r0, rc_rows), :] = jnp.broadcast_to(gcol, (rc_rows, LANES))
            ye_scr[pl.ds(r0, rc_rows), :] = jnp.zeros((rc_rows, ye_scr.shape[1]), F32)
            return carry

        lax.fori_loop(0, n_rc, body, 0)

    def ffn_body(rc, carry):
        r0 = pl.multiple_of(rc * rc_rows, rc_rows)
        xe = xe_scr[pl.ds(r0, rc_rows), :]
        a = jnp.dot(xe, wg_ref[0], preferred_element_type=F32)
        u = jnp.dot(xe, wu_ref[0], preferred_element_type=F32)
        act = (_silu(a) * u * gcol_scr[pl.ds(r0, rc_rows), 0:1]).astype(BF16)
        ye_scr[pl.ds(r0, rc_rows), :] += jnp.dot(act, wd_ref[0], preferred_element_type=F32)
        return carry

    lax.fori_loop(0, n_rc, ffn_body, 0)

    @pl.when(f == pl.num_programs(3) - 1)
    def _scatter():
        rcol = rcol_scr[:, 0:1]

        def body(rc, carry):
            r0 = pl.multiple_of(rc * rc_rows, rc_rows)
            want = (r0 + 1 + lax.broadcasted_iota(I32, (tm, rc_rows), 1)).astype(F32)
            pt = jnp.where(rcol == want, 1.0, 0.0).astype(BF16)
            ye = ye_scr[pl.ds(r0, rc_rows), :].astype(BF16)
            o_ref[0] += jnp.dot(pt, ye, preferred_element_type=F32)
            return carry

        lax.fori_loop(0, n_rc, body, 0)

    @pl.when((e == pl.num_programs(2) - 1) & (f == pl.num_programs(3) - 1))
    def _residual():
        o_ref[0] = x_ref[0] + gate_ref[0] * o_ref[0]


def _moe(x, sh, sc, gate, g, w_router, wg, wu, wd, tm, tf):
    bn, s, d = x.shape
    ne, _, fdim = wg.shape
    assert ne == N_EXPERTS and tm % MOE_RC == 0
    tok_in = pl.BlockSpec((1, tm, d), lambda b, t, e, f: (b, t, 0), pipeline_mode=pl.Buffered(1))
    tok_out = pl.BlockSpec((1, tm, d), lambda b, t, e, f: (b, t, 0))
    vec = pl.BlockSpec((1, 1, d), lambda b, t, e, f: (b, 0, 0))
    return pl.pallas_call(
        _moe_kernel,
        grid=(bn, s // tm, ne, fdim // tf),
        in_specs=[tok_in, vec, vec, vec,
                  pl.BlockSpec((1, d), lambda b, t, e, f: (0, 0)),
                  pl.BlockSpec((ne, d), lambda b, t, e, f: (0, 0)),
                  pl.BlockSpec((1, d, tf), lambda b, t, e, f: (e, 0, f)),
                  pl.BlockSpec((1, d, tf), lambda b, t, e, f: (e, 0, f)),
                  pl.BlockSpec((1, tf, d), lambda b, t, e, f: (e, f, 0))],
        out_specs=tok_out,
        out_shape=jax.ShapeDtypeStruct((bn, s, d), F32),
        scratch_shapes=[pltpu.VMEM((tm, d), BF16),
                        pltpu.VMEM((tm, d), BF16),
                        pltpu.VMEM((tm, d), F32),
                        pltpu.VMEM((tm, LANES), F32),
                        pltpu.VMEM((ne, tm), F32),
                        pltpu.VMEM((ne, tm), F32),
                        pltpu.VMEM((tm, LANES), F32),
                        pltpu.SMEM((ne,), I32)],
        compiler_params=pltpu.CompilerParams(
            dimension_semantics=("parallel", "parallel", "arbitrary", "arbitrary"),
            vmem_limit_bytes=56 * 1024 * 1024),
    )(x, sh, sc, gate, g, w_router.T, wg, wu, wd)


def _final_norm_kernel(x_ref, g_ref, o_ref):
    x = x_ref[0]
    var = jnp.mean(x * x, axis=-1, keepdims=True)
    o_ref[0] = (x * lax.rsqrt(var + EPS)) * g_ref[...]


def _final_norm(x, g, tm):
    bn, s, d = x.shape
    tok = pl.BlockSpec((1, tm, d), lambda b, t: (b, t, 0))
    return pl.pallas_call(
        _final_norm_kernel,
        grid=(bn, s // tm),
        in_specs=[tok, pl.BlockSpec((1, d), lambda b, t: (0, 0))],
        out_specs=tok,
        out_shape=jax.ShapeDtypeStruct((bn, s, d), F32),
        compiler_params=_cparams("parallel", "parallel"),
    )(x, g)


def _token_tile(s, want):
    tm = min(want, s)
    assert s % tm == 0 and tm % QB == 0
    return tm


def kernel(x, c, positions, w_ada, b_ada, g_norm_mix, g_norm_ffn, w_in, w_out, hgrn_lb_logits, hgrn_out_norm, gmlp_vnorm_g, gmlp_vnorm_b, gmlp_w_s, gmlp_b_s, ffn_w_gate, ffn_w_up, ffn_w_down, moe_w_router, moe_w_gate, moe_w_up, moe_w_down, g_final):
    bn, s, d = x.shape
    depth = w_in.shape[0]
    assert s % QB == 0 and d == sum(IN_WIDTHS[:1]) + B_DIM + C_DIM
    tm_proj = _token_tile(s, 512)
    tm_ffn = _token_tile(s, 1024)

    p_lb = jax.nn.softmax(hgrn_lb_logits.astype(F32), axis=0)
    cum = jnp.cumsum(p_lb, axis=0)
    lower_bounds = cum - cum[0:1]

    mod = _ada_mod(c, w_ada, b_ada)
    cosn, sinn, cost, sint = _rope_tables(positions)

    for l in range(depth):
        sh1, sc1, g1, sh2, sc2, g2 = [mod[l, :, i * d:(i + 1) * d].reshape(bn, 1, d) for i in range(6)]
        w_n, w_t = _prep_in_weights(w_in[l])
        (kz, ik3, bq, bf, bi, bg, cuv, qt, iq3t, vt4, iwt) = _inproj(
            x, sh1, sc1, g_norm_mix[l].reshape(1, d), w_n, w_t, cosn, sinn, cost, sint, tm_proj)
        a_out = _dsa(ik3, kz, vt4, iq3t, iwt, qt)
        b_out = _hgrn(bq, bf, bi, bg, lower_bounds[l].reshape(1, B_DIM),
                      jnp.tile(hgrn_out_norm[l], LANES // B_DV).reshape(1, LANES))
        c_out = _gmlp(cuv, gmlp_vnorm_g[l].reshape(1, C_DIM), gmlp_vnorm_b[l].reshape(1, C_DIM),
                      gmlp_w_s[l], gmlp_b_s[l])
        x = _outproj(x, a_out, b_out, c_out, w_out[l], g1, tm_proj)
        gf = g_norm_ffn[l].reshape(1, d)
        if l % 2 == 0:
            i = l // 2
            x = _ffn(x, sh2, sc2, g2, gf, ffn_w_gate[i].astype(BF16), ffn_w_up[i].astype(BF16),
                     ffn_w_down[i].astype(BF16), tm_ffn, 256)
        else:
            i = l // 2
            x = _moe(x, sh2, sc2, g2, gf, moe_w_router[i], moe_w_gate[i].astype(BF16),
                     moe_w_up[i].astype(BF16), moe_w_down[i].astype(BF16), _token_tile(s, 2048), 512)
    return _final_norm(x, g_final.reshape(1, d), tm_proj)
```

```python
import functools

import numpy as np
import jax
import jax.numpy as jnp
from jax import lax
from jax.experimental import pallas as pl
from jax.experimental.pallas import tpu as pltpu

F32 = jnp.float32
BF16 = jnp.bfloat16
I32 = jnp.int32
I16 = jnp.int16

HEAD_DIM = 64
A_HEADS = 6
A_DIM = A_HEADS * HEAD_DIM
IDX_HEADS = 4
IDX_DIM = 64
TOPK_MAX = 256
B_HEADS = 6
B_DK = 64
B_DV = 64
B_DIM = B_HEADS * B_DV
HGRN_CHUNK = 64
C_GROUPS = 4
C_GROUP_DIM = 64
C_DIM = C_GROUPS * C_GROUP_DIM
C_CHUNK = 128
ROPE_THETA = 10000.0
N_EXPERTS = 8
EPS = 1e-6
IN_WIDTHS = (A_DIM, HEAD_DIM, HEAD_DIM, IDX_HEADS * IDX_DIM, IDX_DIM, IDX_HEADS,
             B_DIM, B_DIM, B_DIM, B_DIM, 2 * C_DIM)

LANES = 128
KC = 128
QB = 2 * KC
VMEM_LIMIT = 48 * 1024 * 1024
HGRN_FAST_SPAN = 80.0
HGRN_SUB = 16
HGRN_GROUP = 4
INT_MIN = -2 ** 31

NT_DIMS = (((1,), (1,)), ((), ()))
TN_DIMS = (((0,), (0,)), ((), ()))


def _cparams(*sem):
    return pltpu.CompilerParams(dimension_semantics=sem, vmem_limit_bytes=VMEM_LIMIT)


def _split2(x):
    hi = x.astype(BF16)
    lo = (x - hi.astype(F32)).astype(BF16)
    return hi, lo


def _split3(x):
    hi = x.astype(BF16)
    r1 = x - hi.astype(F32)
    mid = r1.astype(BF16)
    lo = (r1 - mid.astype(F32)).astype(BF16)
    return hi, mid, lo


def _silu(x):
    return x * (1.0 / (1.0 + jnp.exp(-x)))


def _modulated_rmsnorm(x, g, sc, sh):
    var = jnp.mean(x * x, axis=-1, keepdims=True)
    y = x * lax.rsqrt(var + EPS)
    return (y * g) * (1.0 + sc) + sh


def _ada_kernel(c_ref, w_ref, b_ref, o_ref):
    cond = _silu(c_ref[...])
    o_ref[0] = jnp.dot(cond, w_ref[0], precision=lax.Precision.HIGHEST,
                       preferred_element_type=F32) + b_ref[0]


def _ada_mod(c, w_ada, b_ada):
    depth, d, d6 = w_ada.shape
    bn = c.shape[0]
    tn = 1536
    return pl.pallas_call(
        _ada_kernel,
        grid=(depth, d6 // tn),
        in_specs=[pl.BlockSpec((bn, d), lambda l, n: (0, 0)),
                  pl.BlockSpec((1, d, tn), lambda l, n: (l, 0, n)),
                  pl.BlockSpec((1, 1, tn), lambda l, n: (l, 0, n))],
        out_specs=pl.BlockSpec((1, bn, tn), lambda l, n: (l, 0, n)),
        out_shape=jax.ShapeDtypeStruct((depth, bn, d6), F32),
        compiler_params=_cparams("parallel", "parallel"),
    )(c, w_ada, b_ada.reshape(depth, 1, d6))


def _rope_kernel(posn_ref, post_ref, invn_ref, signn_ref, invt_ref,
                 cosn_ref, sinn_ref, cost_ref, sint_ref):
    ang_n = posn_ref[0].astype(F32) * invn_ref[...]
    cosn_ref[0] = jnp.cos(ang_n)
    sinn_ref[0] = jnp.sin(ang_n) * signn_ref[...]
    ang_t = invt_ref[...] * post_ref[0].astype(F32)
    cost_ref[0] = jnp.cos(ang_t)
    sint_ref[0] = jnp.sin(ang_t)


def _rope_tables(positions):
    bn, s = positions.shape
    half = HEAD_DIM // 2
    inv = ROPE_THETA ** (-jnp.arange(0, HEAD_DIM, 2, dtype=F32) / HEAD_DIM)
    inv_n = jnp.tile(inv, LANES // half).reshape(1, LANES)
    sign_n = jnp.tile(jnp.concatenate([-jnp.ones((half,), F32), jnp.ones((half,), F32)]),
                      LANES // HEAD_DIM).reshape(1, LANES)
    inv_t = inv.reshape(half, 1)
    full = lambda shape: pl.BlockSpec(shape, lambda b: (0,) * len(shape))
    return pl.pallas_call(
        _rope_kernel,
        grid=(bn,),
        in_specs=[pl.BlockSpec((1, s, 1), lambda b: (b, 0, 0)),
                  pl.BlockSpec((1, 1, s), lambda b: (b, 0, 0)),
                  full((1, LANES)), full((1, LANES)), full((half, 1))],
        out_specs=[pl.BlockSpec((1, s, LANES), lambda b: (b, 0, 0)),
                   pl.BlockSpec((1, s, LANES), lambda b: (b, 0, 0)),
                   pl.BlockSpec((1, half, s), lambda b: (b, 0, 0)),
                   pl.BlockSpec((1, half, s), lambda b: (b, 0, 0))],
        out_shape=[jax.ShapeDtypeStruct((bn, s, LANES), F32),
                   jax.ShapeDtypeStruct((bn, s, LANES), F32),
                   jax.ShapeDtypeStruct((bn, half, s), F32),
                   jax.ShapeDtypeStruct((bn, half, s), F32)],
        compiler_params=_cparams("parallel"),
    )(positions.reshape(bn, s, 1), positions.reshape(bn, 1, s), inv_n, sign_n, inv_t)


N_KZ = 0
N_IK = 2 * LANES
N_BQ = 4 * LANES
N_BF = N_BQ + B_DIM
N_BI = N_BF + B_DIM
N_BG = N_BI + B_DIM
N_CUV = N_BG + B_DIM
N_COLS = N_CUV + 2 * C_DIM
T_Q = 0
T_IQ = A_DIM
T_V = T_IQ + IDX_HEADS * IDX_DIM
T_IW = T_V + HEAD_DIM
T_ROWS = T_IW + 8


def _prep_in_weights(w_in_l):
    offs = np.concatenate([[0], np.cumsum(IN_WIDTHS)])
    sl = lambda i: w_in_l[:, int(offs[i]):int(offs[i + 1])]
    aq, ak, av, iq, ik, iw, bq, bf, bi, bg, cuv = [sl(i) for i in range(11)]
    half = HEAD_DIM // 2
    rot = lambda w: jnp.concatenate([w[:, half:], w[:, :half]], axis=1)
    z = jnp.zeros_like(ak)
    w_n = jnp.concatenate([ak, z, rot(ak), z, ik, ik, rot(ik), rot(ik), bq, bf, bi, bg, cuv], axis=1)
    w_t = jnp.concatenate([aq, iq, av, iw, jnp.zeros((w_in_l.shape[0], 4), w_in_l.dtype)], axis=1).T
    return w_n.astype(BF16), w_t.astype(BF16)


def _inproj_kernel(x_ref, sh_ref, sc_ref, g_ref, wn_ref, wt_ref, cosn_ref, sinn_ref, cost_ref, sint_ref,
                   kz_ref, ik3_ref, bq_ref, bf_ref, bi_ref, bg_ref, cuv_ref,
                   qt_ref, iq3t_ref, vt_ref, iwt_ref):
    tm = x_ref.shape[1]
    half = HEAD_DIM // 2
    h = _modulated_rmsnorm(x_ref[0], g_ref[...], sc_ref[0], sh_ref[0]).astype(BF16)

    cosn = cosn_ref[0]
    sinn = sinn_ref[0]
    zk = jnp.dot(h, wn_ref[:, N_KZ:N_KZ + 2 * LANES], preferred_element_type=F32)
    kz_ref[0] = (zk[:, :LANES] * cosn + zk[:, LANES:] * sinn).astype(BF16)
    zi = jnp.dot(h, wn_ref[:, N_IK:N_IK + 2 * LANES], preferred_element_type=F32)
    ik2 = zi[:, :LANES] * cosn + zi[:, LANES:] * sinn
    hi, lo = _split2(ik2)
    lane = lax.broadcasted_iota(I32, (tm, LANES), 1)
    ik3_ref[0, :, :LANES] = jnp.where(lane < IDX_DIM, hi, lo)
    ik3_ref[0, :, LANES:] = hi
    bq_ref[0] = jnp.dot(h, wn_ref[:, N_BQ:N_BF], preferred_element_type=F32).astype(BF16)
    bf_ref[0] = jnp.dot(h, wn_ref[:, N_BF:N_BI], preferred_element_type=F32)
    bi_ref[0] = jnp.dot(h, wn_ref[:, N_BI:N_BG], preferred_element_type=F32).astype(BF16)
    bg_ref[0] = jnp.dot(h, wn_ref[:, N_BG:N_CUV], preferred_element_type=F32).astype(BF16)
    cuv_ref[0] = jnp.dot(h, wn_ref[:, N_CUV:N_COLS], preferred_element_type=F32).astype(BF16)

    zt = lax.dot_general(wt_ref[...], h, NT_DIMS, preferred_element_type=F32)
    cost = cost_ref[0]
    sint = sint_ref[0]
    qscale = HEAD_DIM ** -0.5
    for hh in range(A_HEADS):
        r0 = T_Q + hh * HEAD_DIM
        x1 = zt[r0:r0 + half]
        x2 = zt[r0 + half:r0 + HEAD_DIM]
        qt_ref[0, hh * HEAD_DIM:hh * HEAD_DIM + half] = ((x1 * cost - x2 * sint) * qscale).astype(BF16)
        qt_ref[0, hh * HEAD_DIM + half:(hh + 1) * HEAD_DIM] = ((x2 * cost + x1 * sint) * qscale).astype(BF16)
    iscale = IDX_DIM ** -0.5
    zero = jnp.zeros((IDX_DIM, tm), BF16)
    for hh in range(IDX_HEADS):
        r0 = T_IQ + hh * IDX_DIM
        x1 = zt[r0:r0 + half]
        x2 = zt[r0 + half:r0 + IDX_DIM]
        y = jnp.concatenate([(x1 * cost - x2 * sint) * iscale, (x2 * cost + x1 * sint) * iscale], axis=0)
        hi, lo = _split2(y)
        iq3t_ref[0, hh, 0 * IDX_DIM:1 * IDX_DIM] = hi
        iq3t_ref[0, hh, 1 * IDX_DIM:2 * IDX_DIM] = hi
        iq3t_ref[0, hh, 2 * IDX_DIM:3 * IDX_DIM] = lo
        iq3t_ref[0, hh, 3 * IDX_DIM:4 * IDX_DIM] = zero
    vt = zt[T_V:T_V + HEAD_DIM].astype(BF16)
    for i in range(tm // KC):
        vt_ref[0, i] = vt[:, i * KC:(i + 1) * KC]
    iwt_ref[0] = zt[T_IW:T_IW + 8] * (IDX_HEADS ** -0.5)


def _inproj(x, sh, sc, g, w_n, w_t, cosn, sinn, cost, sint, tm):
    bn, s, d = x.shape
    nt = s // tm
    half = HEAD_DIM // 2
    tok = lambda w: pl.BlockSpec((1, tm, w), lambda b, t: (b, t, 0))
    vec = pl.BlockSpec((1, 1, d), lambda b, t: (b, 0, 0))
    full2 = lambda a: pl.BlockSpec(a.shape, lambda b, t: (0, 0))
    out_shapes = [
        jax.ShapeDtypeStruct((bn, s, LANES), BF16),
        jax.ShapeDtypeStruct((bn, s, 2 * LANES), BF16),
        jax.ShapeDtypeStruct((bn, s, B_DIM), BF16),
        jax.ShapeDtypeStruct((bn, s, B_DIM), F32),
        jax.ShapeDtypeStruct((bn, s, B_DIM), BF16),
        jax.ShapeDtypeStruct((bn, s, B_DIM), BF16),
        jax.ShapeDtypeStruct((bn, s, 2 * C_DIM), BF16),
        jax.ShapeDtypeStruct((bn, A_DIM, s), BF16),
        jax.ShapeDtypeStruct((bn, IDX_HEADS, 4 * IDX_DIM, s), BF16),
        jax.ShapeDtypeStruct((bn, s // KC, HEAD_DIM, KC), BF16),
        jax.ShapeDtypeStruct((bn, 8, s), F32),
    ]
    out_specs = [
        tok(LANES), tok(2 * LANES), tok(B_DIM), tok(B_DIM), tok(B_DIM), tok(B_DIM), tok(2 * C_DIM),
        pl.BlockSpec((1, A_DIM, tm), lambda b, t: (b, 0, t)),
        pl.BlockSpec((1, IDX_HEADS, 4 * IDX_DIM, tm), lambda b, t: (b, 0, 0, t)),
        pl.BlockSpec((1, tm // KC, HEAD_DIM, KC), lambda b, t: (b, t, 0, 0)),
        pl.BlockSpec((1, 8, tm), lambda b, t: (b, 0, t)),
    ]
    return pl.pallas_call(
        _inproj_kernel,
        grid=(bn, nt),
        in_specs=[tok(d), vec, vec, full2(g), full2(w_n), full2(w_t),
                  tok(LANES), tok(LANES),
                  pl.BlockSpec((1, half, tm), lambda b, t: (b, 0, t)),
                  pl.BlockSpec((1, half, tm), lambda b, t: (b, 0, t))],
        out_specs=out_specs,
        out_shape=out_shapes,
        compiler_params=_cparams("parallel", "parallel"),
    )(x, sh, sc, g, w_n, w_t, cosn, sinn, cost, sint)


def _dsa_kernel(n_top, ik3_ref, kz_ref, vt_ref, iq3t_ref, iwt_ref, qt_ref, out_ref,
                key_scr, hi_scr, lo_scr, lg_scr, acc_scr):
    j = pl.program_id(1)
    npair = j + 1
    pair = 2 * KC
    row = lax.broadcasted_iota(I32, (KC, QB), 0)
    col = lax.broadcasted_iota(I32, (KC, QB), 1)
    int_min = jnp.int32(INT_MIN)
    i16_min = jnp.int16(-2 ** 15)

    def score_pair(p, diagonal):
        for u in range(2):
            ks = pl.multiple_of(p * pair + u * KC, KC)
            ikc = ik3_ref[0, pl.ds(ks, KC), :]
            sc = jnp.zeros((KC, QB), F32)
            for hh in range(IDX_HEADS):
                rel = jnp.dot(ikc, iq3t_ref[0, hh], preferred_element_type=F32)
                sc = sc + jnp.maximum(rel, 0.0) * iwt_ref[0, hh:hh + 1, :]
            sc = jnp.where(sc == 0.0, 0.0, sc)
            bits = pltpu.bitcast(sc, I32)
            key = bits ^ ((bits >> 31) & jnp.int32(0x7FFFFFFF))
            if diagonal:
                key = jnp.where((u * KC + row) <= col, key, int_min)
            key_scr[pl.ds(ks, KC), :] = key
            hi_scr[pl.ds(ks, KC), :] = (key >> 16).astype(I16)
            lo_scr[pl.ds(ks, KC), :] = ((key & jnp.int32(0xFFFF)) - 32768).astype(I16)

    def score_body(p, carry):
        score_pair(p, False)
        return carry

    lax.fori_loop(0, j, score_body, 0)
    score_pair(j, True)

    def count16(ref, pred_fn):
        def body(p, acc):
            ks = pl.multiple_of(p * pair, pair)
            m = jnp.where(pred_fn(ref[pl.ds(ks, pair), :]), jnp.int16(1), jnp.int16(0))
            parts = [m[16 * i:16 * (i + 1)] for i in range(pair // 16)]
            while len(parts) > 1:
                parts = [parts[i] + parts[i + 1] for i in range(0, len(parts), 2)]
            return acc + parts[0]
        acc = lax.fori_loop(0, npair, body, jnp.zeros((16, QB), I16))
        return jnp.sum(acc.astype(I32), axis=0, keepdims=True)

    def bisect16(ref, k_needed):
        def bit_body(i, t_u):
            cand = t_u | jnp.left_shift(jnp.int32(1), 15 - i)
            cand16 = (cand - 32768).astype(I16)
            cnt = count16(ref, lambda x: x >= cand16)
            return jnp.where(cnt >= k_needed, cand, t_u)
        return lax.fori_loop(0, 16, bit_body, jnp.zeros((1, QB), I32))

    t_hi = bisect16(hi_scr, n_top)
    t_hi16 = (t_hi - 32768).astype(I16)
    n_hi_gt = count16(hi_scr, lambda x: x > t_hi16)

    def bucket_body(p, carry):
        ks = pl.multiple_of(p * pair, pair)
        in_bucket = hi_scr[pl.ds(ks, pair), :] == t_hi16
        lo_scr[pl.ds(ks, pair), :] = jnp.where(in_bucket, lo_scr[pl.ds(ks, pair), :], i16_min)
        return carry

    lax.fori_loop(0, npair, bucket_body, 0)
    t_lo = bisect16(lo_scr, n_top - n_hi_gt)
    t_lo16 = (t_lo - 32768).astype(I16)
    n_gt = n_hi_gt + count16(lo_scr, lambda x: x > t_lo16)
    thr = jnp.left_shift(t_hi - 32768, 16) | t_lo
    n_tie = (n_top - n_gt).astype(F32)
    thr_valid = jnp.where(thr > int_min, 1.0, 0.0)

    tr = lax.broadcasted_iota(I32, (KC, KC), 0)
    tc = lax.broadcasted_iota(I32, (KC, KC), 1)
    ltri = jnp.where(tr >= tc, 1.0, 0.0).astype(BF16)
    zpad = jnp.zeros((HEAD_DIM, QB), BF16)
    qpad = [jnp.concatenate([qt_ref[0, hh * HEAD_DIM:(hh + 1) * HEAD_DIM, :], zpad], axis=0)
            for hh in range(A_HEADS)]
    neg_inf = jnp.float32(-jnp.inf)

    def pass_a(p, carry):
        tie_cnt, ms = carry
        ms = list(ms)
        for u in range(2):
            ks = pl.multiple_of(p * pair + u * KC, KC)
            kc = key_scr[pl.ds(ks, KC), :]
            eqf = jnp.where(kc == thr, thr_valid, 0.0)
            pref = jnp.dot(ltri, eqf.astype(BF16), preferred_element_type=F32)
            keep_tie = eqf * jnp.where((tie_cnt + pref) <= n_tie, 1.0, 0.0)
            sel = jnp.where(kc > thr, 1.0, keep_tie) > 0.5
            tie_cnt = tie_cnt + pref[KC - 1:KC, :]
            kzc = kz_ref[0, pl.ds(ks, KC), :]
            for hh in range(A_HEADS):
                lt = jnp.dot(kzc, qpad[hh], preferred_element_type=F32)
                lt = jnp.where(sel, lt, neg_inf)
                lg_scr[hh, pl.ds(ks, KC), :] = lt
                ms[hh] = jnp.maximum(ms[hh], jnp.max(lt, axis=0, keepdims=True))
        return tie_cnt, tuple(ms)

    init_m = tuple(jnp.full((1, QB), neg_inf, F32) for _ in range(A_HEADS))
    _, ms = lax.fori_loop(0, npair, pass_a, (jnp.zeros((1, QB), F32), init_m))

    acc_scr[...] = jnp.zeros_like(acc_scr)

    def pass_b(p, ls):
        ls = list(ls)
        ks = pl.multiple_of(p * pair, pair)
        vt2 = jnp.concatenate([vt_ref[0, 2 * p], vt_ref[0, 2 * p + 1]], axis=1)
        for hh in range(A_HEADS):
            pr = jnp.exp(lg_scr[hh, pl.ds(ks, pair), :] - ms[hh])
            ls[hh] = ls[hh] + jnp.sum(pr, axis=0, keepdims=True)
            acc_scr[hh * HEAD_DIM:(hh + 1) * HEAD_DIM, :] += jnp.dot(
                vt2, pr.astype(BF16), preferred_element_type=F32)
        return tuple(ls)

    ls = lax.fori_loop(0, npair, pass_b, tuple(jnp.zeros((1, QB), F32) for _ in range(A_HEADS)))
    o_t = jnp.concatenate(
        [acc_scr[hh * HEAD_DIM:(hh + 1) * HEAD_DIM, :] / ls[hh] for hh in range(A_HEADS)], axis=0)
    out_ref[0] = o_t.T.astype(BF16)


def _dsa(ik3, kz, vt4, iq3t, iwt, qt):
    bn, s, _ = kz.shape
    assert s % QB == 0
    n_top = min(TOPK_MAX, s // 4)
    nq = s // QB
    return pl.pallas_call(
        functools.partial(_dsa_kernel, n_top),
        grid=(bn, nq),
        in_specs=[pl.BlockSpec((1, s, 2 * LANES), lambda b, q: (b, 0, 0)),
                  pl.BlockSpec((1, s, LANES), lambda b, q: (b, 0, 0)),
                  pl.BlockSpec((1, s // KC, HEAD_DIM, KC), lambda b, q: (b, 0, 0, 0)),
                  pl.BlockSpec((1, IDX_HEADS, 4 * IDX_DIM, QB), lambda b, q: (b, 0, 0, q)),
                  pl.BlockSpec((1, 8, QB), lambda b, q: (b, 0, q)),
                  pl.BlockSpec((1, A_DIM, QB), lambda b, q: (b, 0, q))],
        out_specs=pl.BlockSpec((1, QB, A_DIM), lambda b, q: (b, q, 0)),
        out_shape=jax.ShapeDtypeStruct((bn, s, A_DIM), BF16),
        scratch_shapes=[pltpu.VMEM((s, QB), I32), pltpu.VMEM((s, QB), I16), pltpu.VMEM((s, QB), I16),
                        pltpu.VMEM((A_HEADS, s, QB), F32), pltpu.VMEM((A_DIM, QB), F32)],
        compiler_params=_cparams("parallel", "arbitrary"),
    )(ik3, kz, vt4, iq3t, iwt, qt)


def _hgrn_kernel(q_ref, f_ref, i_ref, g_ref, lb_ref, gn_ref, o_ref, st_scr, oi_scr):
    s = q_ref.shape[1]
    ch = HGRN_CHUNK
    sb = HGRN_SUB
    nsb = ch // sb
    rows = HGRN_GROUP * ch
    lane = lax.broadcasted_iota(I32, (ch, LANES), 1)
    head0 = lane < B_DK
    lr = lax.broadcasted_iota(I32, (2 * rows, rows), 0)
    lc = lax.broadcasted_iota(I32, (2 * rows, rows), 1)
    lt = jnp.where(lr >= rows, lr - rows, lr)
    same_chunk = (lt // ch) == (lc // ch)
    upper = jnp.where(lr >= rows, (lt // sb) * sb - 1, lt)
    lcat = jnp.where(same_chunk & (lc <= upper), 1.0, 0.0).astype(BF16)
    ar = lax.broadcasted_iota(I32, (2 * ch, nsb * ch), 0) % ch
    ac = lax.broadcasted_iota(I32, (2 * ch, nsb * ch), 1)
    att_mask = ((ar // sb) == (ac // ch)) & ((ac % ch) <= ar)
    bd_r = lax.broadcasted_iota(I32, (LANES, LANES), 0)
    bd_c = lax.broadcasted_iota(I32, (LANES, LANES), 1)
    same_head = (bd_r < B_DK) == (bd_c < B_DK)
    ones_bd = jnp.where(same_head, 1.0, 0.0).astype(BF16)
    s_iota = lax.broadcasted_iota(I32, (ch, LANES), 0)

    lb = lb_ref[...]
    log_lb = jnp.log(lb)
    log_1mlb = jnp.log1p(-lb)
    gn = gn_ref[...]
    st_scr[...] = jnp.zeros_like(st_scr)

    def group_body(it, carry):
        t0 = pl.multiple_of(it * rows, rows)
        z = f_ref[0, pl.ds(t0, rows), :]
        q = q_ref[0, pl.ds(t0, rows), :].astype(F32)
        v = i_ref[0, pl.ds(t0, rows), :]
        softplus_tail = jnp.log1p(jnp.exp(-jnp.abs(z)))
        log_sig = -(jnp.maximum(-z, 0.0) + softplus_tail)
        x2 = log_1mlb + log_sig
        amax = jnp.maximum(log_lb, x2)
        log_f = amax + jnp.log1p(jnp.exp(-jnp.abs(log_lb - x2)))
        kk = (1.0 - lb) * jnp.exp(-(jnp.maximum(z, 0.0) + softplus_tail))
        f_hi, f_mid, f_lo = _split3(log_f)
        br = (jnp.dot(lcat, f_hi, preferred_element_type=F32)
              + jnp.dot(lcat, f_mid, preferred_element_type=F32)
              + jnp.dot(lcat, f_lo, preferred_element_type=F32))
        b = br[:rows]
        ref = br[rows:]
        q_loc = q * jnp.exp(b - ref)
        q_chk = (q * jnp.exp(b)).astype(BF16)
        span = ref - b

        st = st_scr[...]
        o_inter = []
        for c in range(HGRN_GROUP):
            lo_r = c * ch
            b_c = b[lo_r:lo_r + ch]
            ref_c = ref[lo_r:lo_r + ch]
            kk_c = kk[lo_r:lo_r + ch]
            v_c = v[lo_r:lo_r + ch]
            b_last = b_c[ch - 1:ch]
            kcat = jnp.concatenate(
                [kk_c * jnp.exp(jnp.minimum(ref_c[i * sb:i * sb + 1] - b_c, HGRN_FAST_SPAN))
                 for i in range(nsb)], axis=0).astype(BF16)
            q_c = q_loc[lo_r:lo_r + ch]
            qm = jnp.concatenate([jnp.where(head0, q_c, 0.0), jnp.where(head0, 0.0, q_c)],
                                 axis=0).astype(BF16)
            att = lax.dot_general(qm, kcat, NT_DIMS, preferred_element_type=F32)
            att = jnp.where(att_mask, att, 0.0).astype(BF16)
            o2 = jnp.dot(att, jnp.concatenate([v_c] * nsb, axis=0), preferred_element_type=F32)
            oi_scr[lo_r:lo_r + ch, :] = jnp.where(head0, o2[:ch], o2[ch:])
            o_inter.append(lax.dot_general(q_chk[lo_r:lo_r + ch], st.astype(BF16), NT_DIMS,
                                           preferred_element_type=F32))
            kh = (kk_c * jnp.exp(b_last - b_c)).astype(BF16)
            upd = lax.dot_general(v_c, kh, TN_DIMS, preferred_element_type=F32)
            st = st * jnp.exp(b_last) + jnp.where(same_head, upd, 0.0)
        st_scr[...] = st

        for c in range(HGRN_GROUP):
            lo_r = c * ch

            @pl.when(jnp.max(span[lo_r:lo_r + ch]) > HGRN_FAST_SPAN)
            def _():
                b_c = b[lo_r:lo_r + ch]
                q_c = q[lo_r:lo_r + ch]
                kk_c = kk[lo_r:lo_r + ch]
                vf = v[lo_r:lo_r + ch].astype(F32)

                def t_body(t, carry2):
                    onehot = jnp.where(s_iota == t, 1.0, 0.0)
                    b_t = jnp.sum(onehot * b_c, axis=0, keepdims=True)
                    q_t = jnp.sum(onehot * q_c, axis=0, keepdims=True)
                    dec = jnp.exp(jnp.where(s_iota <= t, b_t - b_c, -jnp.inf))
                    w = q_t * kk_c * dec
                    w0 = jnp.sum(jnp.where(head0, w, 0.0), axis=1, keepdims=True)
                    w1 = jnp.sum(jnp.where(head0, 0.0, w), axis=1, keepdims=True)
                    a_col = jnp.where(head0, w0, w1)
                    oi_scr[pl.ds(lo_r + t, 1), :] = jnp.sum(a_col * vf, axis=0, keepdims=True)
                    return carry2

                lax.fori_loop(0, ch, t_body, 0)

        o = oi_scr[...] + jnp.concatenate(o_inter, axis=0)
        o2_hi, o2_lo = _split2(o * o)
        ss = (jnp.dot(o2_hi, ones_bd, preferred_element_type=F32)
              + jnp.dot(o2_lo, ones_bd, preferred_element_type=F32))
        y = (o * lax.rsqrt(ss * (1.0 / B_DV) + EPS)) * gn
        g = g_ref[0, pl.ds(t0, rows), :].astype(F32)
        o_ref[0, pl.ds(t0, rows), :] = (y * _silu(g)).astype(o_ref.dtype)
        return carry

    lax.fori_loop(0, s // rows, group_body, 0)


def _hgrn(bq, bf, bi, bg, lb, gn):
    bn, s, _ = bq.shape
    assert s % (HGRN_GROUP * HGRN_CHUNK) == 0
    npair = B_DIM // LANES
    tok = pl.BlockSpec((1, s, LANES), lambda b, p: (b, 0, p))
    return pl.pallas_call(
        _hgrn_kernel,
        grid=(bn, npair),
        in_specs=[tok, tok, tok, tok,
                  pl.BlockSpec((1, LANES), lambda b, p: (0, p)),
                  pl.BlockSpec((1, LANES), lambda b, p: (0, 0))],
        out_specs=tok,
        out_shape=jax.ShapeDtypeStruct((bn, s, B_DIM), BF16),
        scratch_shapes=[pltpu.VMEM((LANES, LANES), F32),
                        pltpu.VMEM((HGRN_GROUP * HGRN_CHUNK, LANES), F32)],
        compiler_params=_cparams("parallel", "parallel"),
    )(bq, bf, bi, bg, lb, gn)


def _gmlp_kernel(uv_ref, vg_ref, vb_ref, ws_ref, bias_ref, o_ref):
    uv = uv_ref[0].astype(F32)
    uv = 0.5 * uv * (1.0 + lax.erf(uv * (2.0 ** -0.5)))
    u = uv[:, :C_DIM]
    v = uv[:, C_DIM:]
    mu = jnp.mean(v, axis=-1, keepdims=True)
    var = jnp.mean(jnp.square(v - mu), axis=-1, keepdims=True)
    vn = ((v - mu) * lax.rsqrt(var + EPS)) * vg_ref[...] + vb_ref[...]
    vn_b = vn.astype(BF16)
    r_t = lax.broadcasted_iota(I32, (C_CHUNK, C_CHUNK), 0)
    r_s = lax.broadcasted_iota(I32, (C_CHUNK, C_CHUNK), 1)
    lane = lax.broadcasted_iota(I32, (C_CHUNK, C_DIM), 1)
    mixed = jnp.zeros((C_CHUNK, C_DIM), F32)
    for gi in range(C_GROUPS):
        w = jnp.where(r_t >= r_s, ws_ref[gi], 0.0).astype(BF16)
        m = jnp.dot(w, vn_b, preferred_element_type=F32)
        in_group = (lane >= gi * C_GROUP_DIM) & (lane < (gi + 1) * C_GROUP_DIM)
        mixed = jnp.where(in_group, m, mixed)
    o_ref[0] = (u * (mixed + bias_ref[...])).astype(o_ref.dtype)


def _gmlp(cuv, vg, vb, ws, bs):
    bn, s, _ = cuv.shape
    bias = jnp.repeat(bs.T, C_GROUP_DIM, axis=1)
    full = lambda a: pl.BlockSpec(a.shape, lambda b, t: (0,) * a.ndim)
    return pl.pallas_call(
        _gmlp_kernel,
        grid=(bn, s // C_CHUNK),
        in_specs=[pl.BlockSpec((1, C_CHUNK, 2 * C_DIM), lambda b, t: (b, t, 0)),
                  full(vg), full(vb), full(ws), full(bias)],
        out_specs=pl.BlockSpec((1, C_CHUNK, C_DIM), lambda b, t: (b, t, 0)),
        out_shape=jax.ShapeDtypeStruct((bn, s, C_DIM), BF16),
        compiler_params=_cparams("parallel", "parallel"),
    )(cuv, vg, vb, ws, bias)


def _outproj_kernel(x_ref, a_ref, b_ref, c_ref, wa_ref, wb_ref, wc_ref, g1_ref, o_ref):
    mix = (jnp.dot(a_ref[0], wa_ref[...], preferred_element_type=F32)
           + jnp.dot(b_ref[0], wb_ref[...], preferred_element_type=F32)
           + jnp.dot(c_ref[0], wc_ref[...], preferred_element_type=F32))
    o_ref[0] = x_ref[0] + g1_ref[0] * mix


def _outproj(x, a, b, c, w_out_l, g1, tm):
    bn, s, d = x.shape
    wa = w_out_l[:A_DIM].astype(BF16)
    wb = w_out_l[A_DIM:A_DIM + B_DIM].astype(BF16)
    wc = w_out_l[A_DIM + B_DIM:].astype(BF16)
    tok = lambda w: pl.BlockSpec((1, tm, w), lambda bb, t: (bb, t, 0))
    full = lambda arr: pl.BlockSpec(arr.shape, lambda bb, t: (0, 0))
    return pl.pallas_call(
        _outproj_kernel,
        grid=(bn, s // tm),
        in_specs=[tok(d), tok(A_DIM), tok(B_DIM), tok(C_DIM), full(wa), full(wb), full(wc),
                  pl.BlockSpec((1, 1, d), lambda bb, t: (bb, 0, 0))],
        out_specs=tok(d),
        out_shape=jax.ShapeDtypeStruct((bn, s, d), F32),
        compiler_params=_cparams("parallel", "parallel"),
    )(x, a, b, c, wa, wb, wc, g1)


def _ffn_kernel(x_ref, sh_ref, sc_ref, gate_ref, g_ref, wg_ref, wu_ref, wd_ref, o_ref, h_scr, acc_scr):
    f = pl.program_id(2)

    @pl.when(f == 0)
    def _():
        h_scr[...] = _modulated_rmsnorm(x_ref[0], g_ref[...], sc_ref[0], sh_ref[0]).astype(BF16)
        acc_scr[...] = jnp.zeros_like(acc_scr)

    h = h_scr[...]
    a = jnp.dot(h, wg_ref[...], preferred_element_type=F32)
    u = jnp.dot(h, wu_ref[...], preferred_element_type=F32)
    act = (_silu(a) * u).astype(BF16)
    acc_scr[...] += jnp.dot(act, wd_ref[...], preferred_element_type=F32)

    @pl.when(f == pl.num_programs(2) - 1)
    def _():
        o_ref[0] = x_ref[0] + gate_ref[0] * acc_scr[...]


def _ffn(x, sh, sc, gate, g, wg, wu, wd, tm, tf):
    bn, s, d = x.shape
    fdim = wg.shape[1]
    tok = pl.BlockSpec((1, tm, d), lambda b, t, f: (b, t, 0))
    vec = pl.BlockSpec((1, 1, d), lambda b, t, f: (b, 0, 0))
    return pl.pallas_call(
        _ffn_kernel,
        grid=(bn, s // tm, fdim // tf),
        in_specs=[tok, vec, vec, vec,
                  pl.BlockSpec((1, d), lambda b, t, f: (0, 0)),
                  pl.BlockSpec((d, tf), lambda b, t, f: (0, f)),
                  pl.BlockSpec((d, tf), lambda b, t, f: (0, f)),
                  pl.BlockSpec((tf, d), lambda b, t, f: (f, 0))],
        out_specs=tok,
        out_shape=jax.ShapeDtypeStruct((bn, s, d), F32),
        scratch_shapes=[pltpu.VMEM((tm, d), BF16), pltpu.VMEM((tm, d), F32)],
        compiler_params=_cparams("parallel", "parallel", "arbitrary"),
    )(x, sh, sc, gate, g, wg, wu, wd)


MOE_RC = 256


def _moe_kernel(x_ref, sh_ref, sc_ref, gate_ref, g_ref, wrt_ref, wg_ref, wu_ref, wd_ref, o_ref,
                h_scr, xe_scr, ye_scr, gcol_scr, gates_scr, rsel_scr, rcol_scr, cnt_smem):
    e = pl.program_id(2)
    f = pl.program_id(3)
    tm = x_ref.shape[1]
    rc_rows = MOE_RC
    neg_inf = jnp.float32(-jnp.inf)

    @pl.when((e == 0) & (f == 0))
    def _route():
        h = _modulated_rmsnorm(x_ref[0], g_ref[...], sc_ref[0], sh_ref[0])
        h_scr[...] = h.astype(BF16)
        o_ref[0] = jnp.zeros((tm, o_ref.shape[2]), F32)
        logits = lax.dot_general(wrt_ref[...], h, NT_DIMS, precision=lax.Precision.HIGHEST,
                                 preferred_element_type=F32)
        ridx = lax.broadcasted_iota(I32, (N_EXPERTS, tm), 0)
        m1 = jnp.max(logits, axis=0, keepdims=True)
        i1 = jnp.min(jnp.where(logits == m1, ridx, N_EXPERTS), axis=0, keepdims=True)
        rest = jnp.where(ridx == i1, neg_inf, logits)
        m2 = jnp.max(rest, axis=0, keepdims=True)
        i2 = jnp.min(jnp.where(rest == m2, ridx, N_EXPERTS), axis=0, keepdims=True)
        e2 = jnp.exp(m2 - m1)
        den = 1.0 + e2
        gates_scr[...] = jnp.where(ridx == i1, 1.0 / den, jnp.where(ridx == i2, e2 / den, 0.0))
        sel = jnp.where(ridx == i1, 1.0, jnp.where(ridx == i2, 1.0, 0.0))
        ur = lax.broadcasted_iota(I32, (LANES, LANES), 0)
        uc = lax.broadcasted_iota(I32, (LANES, LANES), 1)
        utri = jnp.where(ur <= uc, 1.0, 0.0).astype(BF16)
        carry = jnp.zeros((N_EXPERTS, 1), F32)
        for kb in range(tm // LANES):
            sb = sel[:, kb * LANES:(kb + 1) * LANES]
            pref = jnp.dot(sb.astype(BF16), utri, preferred_element_type=F32) + carry
            rsel_scr[:, kb * LANES:(kb + 1) * LANES] = jnp.where(sb > 0.0, pref, -1.0)
            carry = pref[:, LANES - 1:LANES]
        r8 = lax.broadcasted_iota(I32, (N_EXPERTS, 1), 0)
        for ee in range(N_EXPERTS):
            cnt_smem[ee] = jnp.sum(jnp.where(r8 == ee, carry, 0.0)).astype(I32)

    n_rc = (cnt_smem[e] + (rc_rows - 1)) // rc_rows

    @pl.when(f == 0)
    def _gather():
        rsel_row = rsel_scr[pl.ds(e, 1), :]
        gate_row = gates_scr[pl.ds(e, 1), :]
        r8 = lax.broadcasted_iota(I32, (N_EXPERTS, LANES), 0)
        pick = jnp.where(r8 == e, 1.0, 0.0)
        rcol_scr[...] = lax.dot_general(rsel_scr[...], pick, TN_DIMS, precision=lax.Precision.HIGHEST,
                                        preferred_element_type=F32)
        h = h_scr[...]

        def body(rc, carry):
            r0 = pl.multiple_of(rc * rc_rows, rc_rows)
            want = (r0 + 1 + lax.broadcasted_iota(I32, (rc_rows, tm), 0)).astype(F32)
            pm = rsel_row == want
            pb = jnp.where(pm, 1.0, 0.0).astype(BF16)
            xe_scr[pl.ds(r0, rc_rows), :] = jnp.dot(pb, h, preferred_element_type=F32).astype(BF16)
            gcol = jnp.sum(jnp.where(pm, gate_row, 0.0), axis=1, keepdims=True)
            gcol_scr[pl.ds(r0, rc_rows), :] = jnp.broadcast_to(gcol, (rc_rows, LANES))
            ye_scr[pl.ds(r0, rc_rows), :] = jnp.zeros((rc_rows, ye_scr.shape[1]), F32)
            return carry

        lax.fori_loop(0, n_rc, body, 0)

    def ffn_body(rc, carry):
        r0 = pl.multiple_of(rc * rc_rows, rc_rows)
        xe = xe_scr[pl.ds(r0, rc_rows), :]
        a = jnp.dot(xe, wg_ref[0], preferred_element_type=F32)
        u = jnp.dot(xe, wu_ref[0], preferred_element_type=F32)
        act = (_silu(a) * u * gcol_scr[pl.ds(r0, rc_rows), 0:1]).astype(BF16)
        ye_scr[pl.ds(r0, rc_rows), :] += jnp.dot(act, wd_ref[0], preferred_element_type=F32)
        return carry

    lax.fori_loop(0, n_rc, ffn_body, 0)

    @pl.when(f == pl.num_programs(3) - 1)
    def _scatter():
        rcol = rcol_scr[:, 0:1]

        def body(rc, carry):
            r0 = pl.multiple_of(rc * rc_rows, rc_rows)
            want = (r0 + 1 + lax.broadcasted_iota(I32, (tm, rc_rows), 1)).astype(F32)
            pt = jnp.where(rcol == want, 1.0, 0.0).astype(BF16)
            ye = ye_scr[pl.ds(r0, rc_rows), :].astype(BF16)
            o_ref[0] += jnp.dot(pt, ye, preferred_element_type=F32)
            return carry

        lax.fori_loop(0, n_rc, body, 0)

    @pl.when((e == pl.num_programs(2) - 1) & (f == pl.num_programs(3) - 1))
    def _residual():
        o_ref[0] = x_ref[0] + gate_ref[0] * o_ref[0]


def _moe(x, sh, sc, gate, g, w_router, wg, wu, wd, tm, tf):
    bn, s, d = x.shape
    ne, _, fdim = wg.shape
    assert ne == N_EXPERTS and tm % MOE_RC == 0
    tok_in = pl.BlockSpec((1, tm, d), lambda b, t, e, f: (b, t, 0), pipeline_mode=pl.Buffered(1))
    tok_out = pl.BlockSpec((1, tm, d), lambda b, t, e, f: (b, t, 0))
    vec = pl.BlockSpec((1, 1, d), lambda b, t, e, f: (b, 0, 0))
    return pl.pallas_call(
        _moe_kernel,
        grid=(bn, s // tm, ne, fdim // tf),
        in_specs=[tok_in, vec, vec, vec,
                  pl.BlockSpec((1, d), lambda b, t, e, f: (0, 0)),
                  pl.BlockSpec((ne, d), lambda b, t, e, f: (0, 0)),
                  pl.BlockSpec((1, d, tf), lambda b, t, e, f: (e, 0, f)),
                  pl.BlockSpec((1, d, tf), lambda b, t, e, f: (e, 0, f)),
                  pl.BlockSpec((1, tf, d), lambda b, t, e, f: (e, f, 0))],
        out_specs=tok_out,
        out_shape=jax.ShapeDtypeStruct((bn, s, d), F32),
        scratch_shapes=[pltpu.VMEM((tm, d), BF16),
                        pltpu.VMEM((tm, d), BF16),
                        pltpu.VMEM((tm, d), F32),
                        pltpu.VMEM((tm, LANES), F32),
                        pltpu.VMEM((ne, tm), F32),
                        pltpu.VMEM((ne, tm), F32),
                        pltpu.VMEM((tm, LANES), F32),
                        pltpu.SMEM((ne,), I32)],
        compiler_params=pltpu.CompilerParams(
            dimension_semantics=("parallel", "parallel", "arbitrary", "arbitrary"),
            vmem_limit_bytes=56 * 1024 * 1024),
    )(x, sh, sc, gate, g, w_router.T, wg, wu, wd)


def _final_norm_kernel(x_ref, g_ref, o_ref):
    x = x_ref[0]
    var = jnp.mean(x * x, axis=-1, keepdims=True)
    o_ref[0] = (x * lax.rsqrt(var + EPS)) * g_ref[...]


def _final_norm(x, g, tm):
    bn, s, d = x.shape
    tok = pl.BlockSpec((1, tm, d), lambda b, t: (b, t, 0))
    return pl.pallas_call(
        _final_norm_kernel,
        grid=(bn, s // tm),
        in_specs=[tok, pl.BlockSpec((1, d), lambda b, t: (0, 0))],
        out_specs=tok,
        out_shape=jax.ShapeDtypeStruct((bn, s, d), F32),
        compiler_params=_cparams("parallel", "parallel"),
    )(x, g)


def _token_tile(s, want):
    tm = min(want, s)
    assert s % tm == 0 and tm % QB == 0
    return tm


def kernel(x, c, positions, w_ada, b_ada, g_norm_mix, g_norm_ffn, w_in, w_out, hgrn_lb_logits, hgrn_out_norm, gmlp_vnorm_g, gmlp_vnorm_b, gmlp_w_s, gmlp_b_s, ffn_w_gate, ffn_w_up, ffn_w_down, moe_w_router, moe_w_gate, moe_w_up, moe_w_down, g_final):
    bn, s, d = x.shape
    depth = w_in.shape[0]
    assert s % QB == 0 and d == sum(IN_WIDTHS[:1]) + B_DIM + C_DIM
    tm_proj = _token_tile(s, 512)
    tm_ffn = _token_tile(s, 1024)

    p_lb = jax.nn.softmax(hgrn_lb_logits.astype(F32), axis=0)
    cum = jnp.cumsum(p_lb, axis=0)
    lower_bounds = cum - cum[0:1]

    mod = _ada_mod(c, w_ada, b_ada)
    cosn, sinn, cost, sint = _rope_tables(positions)

    for l in range(depth):
        sh1, sc1, g1, sh2, sc2, g2 = [mod[l, :, i * d:(i + 1) * d].reshape(bn, 1, d) for i in range(6)]
        w_n, w_t = _prep_in_weights(w_in[l])
        (kz, ik3, bq, bf, bi, bg, cuv, qt, iq3t, vt4, iwt) = _inproj(
            x, sh1, sc1, g_norm_mix[l].reshape(1, d), w_n, w_t, cosn, sinn, cost, sint, tm_proj)
        a_out = _dsa(ik3, kz, vt4, iq3t, iwt, qt)
        b_out = _hgrn(bq, bf, bi, bg, lower_bounds[l].reshape(1, B_DIM),
                      jnp.tile(hgrn_out_norm[l], LANES // B_DV).reshape(1, LANES))
        c_out = _gmlp(cuv, gmlp_vnorm_g[l].reshape(1, C_DIM), gmlp_vnorm_b[l].reshape(1, C_DIM),
                      gmlp_w_s[l], gmlp_b_s[l])
        x = _outproj(x, a_out, b_out, c_out, w_out[l], g1, tm_proj)
        gf = g_norm_ffn[l].reshape(1, d)
        if l % 2 == 0:
            i = l // 2
            x = _ffn(x, sh2, sc2, g2, gf, ffn_w_gate[i].astype(BF16), ffn_w_up[i].astype(BF16),
                     ffn_w_down[i].astype(BF16), tm_ffn, 256)
        else:
            i = l // 2
            x = _moe(x, sh2, sc2, g2, gf, moe_w_router[i], moe_w_gate[i].astype(BF16),
                     moe_w_up[i].astype(BF16), moe_w_down[i].astype(BF16), _token_tile(s, 2048), 512)
    return _final_norm(x, g_final.reshape(1, d), tm_proj)
```

```python
import functools

import numpy as np
import jax
import jax.numpy as jnp
from jax import lax
from jax.experimental import pallas as pl
from jax.experimental.pallas import tpu as pltpu

F32 = jnp.float32
BF16 = jnp.bfloat16
I32 = jnp.int32
I16 = jnp.int16

HEAD_DIM = 64
A_HEADS = 6
A_DIM = A_HEADS * HEAD_DIM
IDX_HEADS = 4
IDX_DIM = 64
TOPK_MAX = 256
B_HEADS = 6
B_DK = 64
B_DV = 64
B_DIM = B_HEADS * B_DV
HGRN_CHUNK = 64
C_GROUPS = 4
C_GROUP_DIM = 64
C_DIM = C_GROUPS * C_GROUP_DIM
C_CHUNK = 128
ROPE_THETA = 10000.0
N_EXPERTS = 8
EPS = 1e-6
IN_WIDTHS = (A_DIM, HEAD_DIM, HEAD_DIM, IDX_HEADS * IDX_DIM, IDX_DIM, IDX_HEADS,
             B_DIM, B_DIM, B_DIM, B_DIM, 2 * C_DIM)

LANES = 128
KC = 128
QB = 2 * KC
VMEM_LIMIT = 48 * 1024 * 1024
HGRN_FAST_SPAN = 80.0
HGRN_SUB = 16
HGRN_GROUP = 4
INT_MIN = -2 ** 31

NT_DIMS = (((1,), (1,)), ((), ()))
TN_DIMS = (((0,), (0,)), ((), ()))


def _cparams(*sem):
    return pltpu.CompilerParams(dimension_semantics=sem, vmem_limit_bytes=VMEM_LIMIT)


def _split2(x):
    hi = x.astype(BF16)
    lo = (x - hi.astype(F32)).astype(BF16)
    return hi, lo


def _split3(x):
    hi = x.astype(BF16)
    r1 = x - hi.astype(F32)
    mid = r1.astype(BF16)
    lo = (r1 - mid.astype(F32)).astype(BF16)
    return hi, mid, lo


def _silu(x):
    return x * (1.0 / (1.0 + jnp.exp(-x)))


def _modulated_rmsnorm(x, g, sc, sh):
    var = jnp.mean(x * x, axis=-1, keepdims=True)
    y = x * lax.rsqrt(var + EPS)
    return (y * g) * (1.0 + sc) + sh


def _ada_kernel(c_ref, w_ref, b_ref, o_ref):
    cond = _silu(c_ref[...])
    o_ref[0] = jnp.dot(cond, w_ref[0], precision=lax.Precision.HIGHEST,
                       preferred_element_type=F32) + b_ref[0]


def _ada_mod(c, w_ada, b_ada):
    depth, d, d6 = w_ada.shape
    bn = c.shape[0]
    tn = 1536
    return pl.pallas_call(
        _ada_kernel,
        grid=(depth, d6 // tn),
        in_specs=[pl.BlockSpec((bn, d), lambda l, n: (0, 0)),
                  pl.BlockSpec((1, d, tn), lambda l, n: (l, 0, n)),
                  pl.BlockSpec((1, 1, tn), lambda l, n: (l, 0, n))],
        out_specs=pl.BlockSpec((1, bn, tn), lambda l, n: (l, 0, n)),
        out_shape=jax.ShapeDtypeStruct((depth, bn, d6), F32),
        compiler_params=_cparams("parallel", "parallel"),
    )(c, w_ada, b_ada.reshape(depth, 1, d6))


def _rope_kernel(posn_ref, post_ref, invn_ref, signn_ref, invt_ref,
                 cosn_ref, sinn_ref, cost_ref, sint_ref):
    ang_n = posn_ref[0].astype(F32) * invn_ref[...]
    cosn_ref[0] = jnp.cos(ang_n)
    sinn_ref[0] = jnp.sin(ang_n) * signn_ref[...]
    ang_t = invt_ref[...] * post_ref[0].astype(F32)
    cost_ref[0] = jnp.cos(ang_t)
    sint_ref[0] = jnp.sin(ang_t)


def _rope_tables(positions):
    bn, s = positions.shape
    half = HEAD_DIM // 2
    inv = ROPE_THETA ** (-jnp.arange(0, HEAD_DIM, 2, dtype=F32) / HEAD_DIM)
    inv_n = jnp.tile(inv, LANES // half).reshape(1, LANES)
    sign_n = jnp.tile(jnp.concatenate([-jnp.ones((half,), F32), jnp.ones((half,), F32)]),
                      LANES // HEAD_DIM).reshape(1, LANES)
    inv_t = inv.reshape(half, 1)
    full = lambda shape: pl.BlockSpec(shape, lambda b: (0,) * len(shape))
    return pl.pallas_call(
        _rope_kernel,
        grid=(bn,),
        in_specs=[pl.BlockSpec((1, s, 1), lambda b: (b, 0, 0)),
                  pl.BlockSpec((1, 1, s), lambda b: (b, 0, 0)),
                  full((1, LANES)), full((1, LANES)), full((half, 1))],
        out_specs=[pl.BlockSpec((1, s, LANES), lambda b: (b, 0, 0)),
                   pl.BlockSpec((1, s, LANES), lambda b: (b, 0, 0)),
                   pl.BlockSpec((1, half, s), lambda b: (b, 0, 0)),
                   pl.BlockSpec((1, half, s), lambda b: (b, 0, 0))],
        out_shape=[jax.ShapeDtypeStruct((bn, s, LANES), F32),
                   jax.ShapeDtypeStruct((bn, s, LANES), F32),
                   jax.ShapeDtypeStruct((bn, half, s), F32),
                   jax.ShapeDtypeStruct((bn, half, s), F32)],
        compiler_params=_cparams("parallel"),
    )(positions.reshape(bn, s, 1), positions.reshape(bn, 1, s), inv_n, sign_n, inv_t)


N_KZ = 0
N_IK = 2 * LANES
N_BQ = 4 * LANES
N_BF = N_BQ + B_DIM
N_BI = N_BF + B_DIM
N_BG = N_BI + B_DIM
N_CUV = N_BG + B_DIM
N_COLS = N_CUV + 2 * C_DIM
T_Q = 0
T_IQ = A_DIM
T_V = T_IQ + IDX_HEADS * IDX_DIM
T_IW = T_V + HEAD_DIM
T_BI = T_IW + 8
T_ROWS = T_BI + B_DIM


def _prep_in_weights(w_in_l):
    offs = np.concatenate([[0], np.cumsum(IN_WIDTHS)])
    sl = lambda i: w_in_l[:, int(offs[i]):int(offs[i + 1])]
    aq, ak, av, iq, ik, iw, bq, bf, bi, bg, cuv = [sl(i) for i in range(11)]
    half = HEAD_DIM // 2
    rot = lambda w: jnp.concatenate([w[:, half:], w[:, :half]], axis=1)
    z = jnp.zeros_like(ak)
    w_n = jnp.concatenate([ak, z, rot(ak), z, ik, ik, rot(ik), rot(ik), bq, bf, bi, bg, cuv], axis=1)
    w_t = jnp.concatenate([aq, iq, av, iw, jnp.zeros((w_in_l.shape[0], 4), w_in_l.dtype), bi], axis=1).T
    return w_n.astype(BF16), w_t.astype(BF16)


def _inproj_kernel(x_ref, sh_ref, sc_ref, g_ref, wn_ref, wt_ref, cosn_ref, sinn_ref, cost_ref, sint_ref,
                   kz_ref, ik3_ref, bq_ref, bf_ref, bi_ref, bg_ref, cuv_ref,
                   qt_ref, iq3t_ref, vt_ref, iwt_ref, bit_ref):
    tm = x_ref.shape[1]
    half = HEAD_DIM // 2
    h = _modulated_rmsnorm(x_ref[0], g_ref[...], sc_ref[0], sh_ref[0]).astype(BF16)

    cosn = cosn_ref[0]
    sinn = sinn_ref[0]
    zk = jnp.dot(h, wn_ref[:, N_KZ:N_KZ + 2 * LANES], preferred_element_type=F32)
    kz_ref[0] = (zk[:, :LANES] * cosn + zk[:, LANES:] * sinn).astype(BF16)
    zi = jnp.dot(h, wn_ref[:, N_IK:N_IK + 2 * LANES], preferred_element_type=F32)
    ik2 = zi[:, :LANES] * cosn + zi[:, LANES:] * sinn
    hi, lo = _split2(ik2)
    lane = lax.broadcasted_iota(I32, (tm, LANES), 1)
    ik3_ref[0, :, :LANES] = jnp.where(lane < IDX_DIM, hi, lo)
    ik3_ref[0, :, LANES:] = hi
    bq_ref[0] = jnp.dot(h, wn_ref[:, N_BQ:N_BF], preferred_element_type=F32).astype(BF16)
    bf_ref[0] = jnp.dot(h, wn_ref[:, N_BF:N_BI], preferred_element_type=F32)
    bi_ref[0] = jnp.dot(h, wn_ref[:, N_BI:N_BG], preferred_element_type=F32).astype(BF16)
    bg_ref[0] = jnp.dot(h, wn_ref[:, N_BG:N_CUV], preferred_element_type=F32).astype(BF16)
    cuv_ref[0] = jnp.dot(h, wn_ref[:, N_CUV:N_COLS], preferred_element_type=F32).astype(BF16)

    zt = lax.dot_general(wt_ref[...], h, NT_DIMS, preferred_element_type=F32)
    cost = cost_ref[0]
    sint = sint_ref[0]
    qscale = HEAD_DIM ** -0.5
    for hh in range(A_HEADS):
        r0 = T_Q + hh * HEAD_DIM
        x1 = zt[r0:r0 + half]
        x2 = zt[r0 + half:r0 + HEAD_DIM]
        qt_ref[0, hh * HEAD_DIM:hh * HEAD_DIM + half] = ((x1 * cost - x2 * sint) * qscale).astype(BF16)
        qt_ref[0, hh * HEAD_DIM + half:(hh + 1) * HEAD_DIM] = ((x2 * cost + x1 * sint) * qscale).astype(BF16)
    iscale = IDX_DIM ** -0.5
    zero = jnp.zeros((IDX_DIM, tm), BF16)
    for hh in range(IDX_HEADS):
        r0 = T_IQ + hh * IDX_DIM
        x1 = zt[r0:r0 + half]
        x2 = zt[r0 + half:r0 + IDX_DIM]
        y = jnp.concatenate([(x1 * cost - x2 * sint) * iscale, (x2 * cost + x1 * sint) * iscale], axis=0)
        hi, lo = _split2(y)
        iq3t_ref[0, hh, 0 * IDX_DIM:1 * IDX_DIM] = hi
        iq3t_ref[0, hh, 1 * IDX_DIM:2 * IDX_DIM] = hi
        iq3t_ref[0, hh, 2 * IDX_DIM:3 * IDX_DIM] = lo
        iq3t_ref[0, hh, 3 * IDX_DIM:4 * IDX_DIM] = zero
    vt = zt[T_V:T_V + HEAD_DIM].astype(BF16)
    for i in range(tm // KC):
        vt_ref[0, i] = vt[:, i * KC:(i + 1) * KC]
    iwt_ref[0] = zt[T_IW:T_IW + 8] * (IDX_HEADS ** -0.5)
    bit = zt[T_BI:T_BI + B_DIM].astype(BF16)
    for i in range(tm // LANES):
        bit_ref[0, i] = bit[:, i * LANES:(i + 1) * LANES]


def _inproj(x, sh, sc, g, w_n, w_t, cosn, sinn, cost, sint, tm):
    bn, s, d = x.shape
    nt = s // tm
    half = HEAD_DIM // 2
    tok = lambda w: pl.BlockSpec((1, tm, w), lambda b, t: (b, t, 0))
    vec = pl.BlockSpec((1, 1, d), lambda b, t: (b, 0, 0))
    full2 = lambda a: pl.BlockSpec(a.shape, lambda b, t: (0, 0))
    out_shapes = [
        jax.ShapeDtypeStruct((bn, s, LANES), BF16),
        jax.ShapeDtypeStruct((bn, s, 2 * LANES), BF16),
        jax.ShapeDtypeStruct((bn, s, B_DIM), BF16),
        jax.ShapeDtypeStruct((bn, s, B_DIM), F32),
        jax.ShapeDtypeStruct((bn, s, B_DIM), BF16),
        jax.ShapeDtypeStruct((bn, s, B_DIM), BF16),
        jax.ShapeDtypeStruct((bn, s, 2 * C_DIM), BF16),
        jax.ShapeDtypeStruct((bn, A_DIM, s), BF16),
        jax.ShapeDtypeStruct((bn, IDX_HEADS, 4 * IDX_DIM, s), BF16),
        jax.ShapeDtypeStruct((bn, s // KC, HEAD_DIM, KC), BF16),
        jax.ShapeDtypeStruct((bn, 8, s), F32),
        jax.ShapeDtypeStruct((bn, s // LANES, B_DIM, LANES), BF16),
    ]
    out_specs = [
        tok(LANES), tok(2 * LANES), tok(B_DIM), tok(B_DIM), tok(B_DIM), tok(B_DIM), tok(2 * C_DIM),
        pl.BlockSpec((1, A_DIM, tm), lambda b, t: (b, 0, t)),
        pl.BlockSpec((1, IDX_HEADS, 4 * IDX_DIM, tm), lambda b, t: (b, 0, 0, t)),
        pl.BlockSpec((1, tm // KC, HEAD_DIM, KC), lambda b, t: (b, t, 0, 0)),
        pl.BlockSpec((1, 8, tm), lambda b, t: (b, 0, t)),
        pl.BlockSpec((1, tm // LANES, B_DIM, LANES), lambda b, t: (b, t, 0, 0)),
    ]
    return pl.pallas_call(
        _inproj_kernel,
        grid=(bn, nt),
        in_specs=[tok(d), vec, vec, full2(g), full2(w_n), full2(w_t),
                  tok(LANES), tok(LANES),
                  pl.BlockSpec((1, half, tm), lambda b, t: (b, 0, t)),
                  pl.BlockSpec((1, half, tm), lambda b, t: (b, 0, t))],
        out_specs=out_specs,
        out_shape=out_shapes,
        compiler_params=_cparams("parallel", "parallel"),
    )(x, sh, sc, g, w_n, w_t, cosn, sinn, cost, sint)


def _dsa_kernel(n_top, ik3_ref, kz_ref, vt_ref, iq3t_ref, iwt_ref, qt_ref, out_ref,
                key_scr, hi_scr, lo_scr, lg_scr, acc_scr):
    j = pl.program_id(1)
    npair = j + 1
    pair = 2 * KC
    row = lax.broadcasted_iota(I32, (KC, QB), 0)
    col = lax.broadcasted_iota(I32, (KC, QB), 1)
    int_min = jnp.int32(INT_MIN)
    i16_min = jnp.int16(-2 ** 15)

    def score_pair(p, diagonal):
        for u in range(2):
            ks = pl.multiple_of(p * pair + u * KC, KC)
            ikc = ik3_ref[0, pl.ds(ks, KC), :]
            sc = jnp.zeros((KC, QB), F32)
            for hh in range(IDX_HEADS):
                rel = jnp.dot(ikc, iq3t_ref[0, hh], preferred_element_type=F32)
                sc = sc + jnp.maximum(rel, 0.0) * iwt_ref[0, hh:hh + 1, :]
            sc = jnp.where(sc == 0.0, 0.0, sc)
            bits = pltpu.bitcast(sc, I32)
            key = bits ^ ((bits >> 31) & jnp.int32(0x7FFFFFFF))
            if diagonal:
                key = jnp.where((u * KC + row) <= col, key, int_min)
            key_scr[pl.ds(ks, KC), :] = key
            hi_scr[pl.ds(ks, KC), :] = (key >> 16).astype(I16)
            lo_scr[pl.ds(ks, KC), :] = ((key & jnp.int32(0xFFFF)) - 32768).astype(I16)

    def score_body(p, carry):
        score_pair(p, False)
        return carry

    lax.fori_loop(0, j, score_body, 0)
    score_pair(j, True)

    def count16(ref, pred_fn):
        def body(p, acc):
            ks = pl.multiple_of(p * pair, pair)
            m = jnp.where(pred_fn(ref[pl.ds(ks, pair), :]), jnp.int16(1), jnp.int16(0))
            parts = [m[16 * i:16 * (i + 1)] for i in range(pair // 16)]
            while len(parts) > 1:
                parts = [parts[i] + parts[i + 1] for i in range(0, len(parts), 2)]
            return acc + parts[0]
        acc = lax.fori_loop(0, npair, body, jnp.zeros((16, QB), I16))
        return jnp.sum(acc.astype(I32), axis=0, keepdims=True)

    def bisect16(ref, k_needed):
        def bit_body(i, t_u):
            cand = t_u | jnp.left_shift(jnp.int32(1), 15 - i)
            cand16 = (cand - 32768).astype(I16)
            cnt = count16(ref, lambda x: x >= cand16)
            return jnp.where(cnt >= k_needed, cand, t_u)
        return lax.fori_loop(0, 16, bit_body, jnp.zeros((1, QB), I32))

    t_hi = bisect16(hi_scr, n_top)
    t_hi16 = (t_hi - 32768).astype(I16)
    n_hi_gt = count16(hi_scr, lambda x: x > t_hi16)

    def bucket_body(p, carry):
        ks = pl.multiple_of(p * pair, pair)
        in_bucket = hi_scr[pl.ds(ks, pair), :] == t_hi16
        lo_scr[pl.ds(ks, pair), :] = jnp.where(in_bucket, lo_scr[pl.ds(ks, pair), :], i16_min)
        return carry

    lax.fori_loop(0, npair, bucket_body, 0)
    t_lo = bisect16(lo_scr, n_top - n_hi_gt)
    t_lo16 = (t_lo - 32768).astype(I16)
    n_gt = n_hi_gt + count16(lo_scr, lambda x: x > t_lo16)
    thr = jnp.left_shift(t_hi - 32768, 16) | t_lo
    n_tie = (n_top - n_gt).astype(F32)
    thr_valid = jnp.where(thr > int_min, 1.0, 0.0)

    tr = lax.broadcasted_iota(I32, (KC, KC), 0)
    tc = lax.broadcasted_iota(I32, (KC, KC), 1)
    ltri = jnp.where(tr >= tc, 1.0, 0.0).astype(BF16)
    zpad = jnp.zeros((HEAD_DIM, QB), BF16)
    qpad = [jnp.concatenate([qt_ref[0, hh * HEAD_DIM:(hh + 1) * HEAD_DIM, :], zpad], axis=0)
            for hh in range(A_HEADS)]
    neg_inf = jnp.float32(-jnp.inf)

    def pass_a(p, carry):
        tie_cnt, ms = carry
        ms = list(ms)
        for u in range(2):
            ks = pl.multiple_of(p * pair + u * KC, KC)
            kc = key_scr[pl.ds(ks, KC), :]
            eqf = jnp.where(kc == thr, thr_valid, 0.0)
            pref = jnp.dot(ltri, eqf.astype(BF16), preferred_element_type=F32)
            keep_tie = eqf * jnp.where((tie_cnt + pref) <= n_tie, 1.0, 0.0)
            sel = jnp.where(kc > thr, 1.0, keep_tie) > 0.5
            tie_cnt = tie_cnt + pref[KC - 1:KC, :]
            kzc = kz_ref[0, pl.ds(ks, KC), :]
            for hh in range(A_HEADS):
                lt = jnp.dot(kzc, qpad[hh], preferred_element_type=F32)
                lt = jnp.where(sel, lt, neg_inf)
                lg_scr[hh, pl.ds(ks, KC), :] = lt
                ms[hh] = jnp.maximum(ms[hh], jnp.max(lt, axis=0, keepdims=True))
        return tie_cnt, tuple(ms)

    init_m = tuple(jnp.full((1, QB), neg_inf, F32) for _ in range(A_HEADS))
    _, ms = lax.fori_loop(0, npair, pass_a, (jnp.zeros((1, QB), F32), init_m))

    acc_scr[...] = jnp.zeros_like(acc_scr)

    def pass_b(p, ls):
        ls = list(ls)
        ks = pl.multiple_of(p * pair, pair)
        vt2 = jnp.concatenate([vt_ref[0, 2 * p], vt_ref[0, 2 * p + 1]], axis=1)
        for hh in range(A_HEADS):
            pr = jnp.exp(lg_scr[hh, pl.ds(ks, pair), :] - ms[hh])
            ls[hh] = ls[hh] + jnp.sum(pr, axis=0, keepdims=True)
            acc_scr[hh * HEAD_DIM:(hh + 1) * HEAD_DIM, :] += jnp.dot(
                vt2, pr.astype(BF16), preferred_element_type=F32)
        return tuple(ls)

    ls = lax.fori_loop(0, npair, pass_b, tuple(jnp.zeros((1, QB), F32) for _ in range(A_HEADS)))
    o_t = jnp.concatenate(
        [acc_scr[hh * HEAD_DIM:(hh + 1) * HEAD_DIM, :] / ls[hh] for hh in range(A_HEADS)], axis=0)
    out_ref[0] = o_t.T.astype(BF16)


def _dsa(ik3, kz, vt4, iq3t, iwt, qt):
    bn, s, _ = kz.shape
    assert s % QB == 0
    n_top = min(TOPK_MAX, s // 4)
    nq = s // QB
    return pl.pallas_call(
        functools.partial(_dsa_kernel, n_top),
        grid=(bn, nq),
        in_specs=[pl.BlockSpec((1, s, 2 * LANES), lambda b, q: (b, 0, 0)),
                  pl.BlockSpec((1, s, LANES), lambda b, q: (b, 0, 0)),
                  pl.BlockSpec((1, s // KC, HEAD_DIM, KC), lambda b, q: (b, 0, 0, 0)),
                  pl.BlockSpec((1, IDX_HEADS, 4 * IDX_DIM, QB), lambda b, q: (b, 0, 0, q)),
                  pl.BlockSpec((1, 8, QB), lambda b, q: (b, 0, q)),
                  pl.BlockSpec((1, A_DIM, QB), lambda b, q: (b, 0, q))],
        out_specs=pl.BlockSpec((1, QB, A_DIM), lambda b, q: (b, q, 0)),
        out_shape=jax.ShapeDtypeStruct((bn, s, A_DIM), BF16),
        scratch_shapes=[pltpu.VMEM((s, QB), I32), pltpu.VMEM((s, QB), I16), pltpu.VMEM((s, QB), I16),
                        pltpu.VMEM((A_HEADS, s, QB), F32), pltpu.VMEM((A_DIM, QB), F32)],
        compiler_params=_cparams("parallel", "arbitrary"),
    )(ik3, kz, vt4, iq3t, iwt, qt)


def _hgrn_kernel(q_ref, f_ref, i_ref, it_ref, g_ref, lb_ref, gn_ref, o_ref, st_scr, oi_scr):
    s = q_ref.shape[1]
    ch = HGRN_CHUNK
    sb = HGRN_SUB
    nsb = ch // sb
    rows = HGRN_GROUP * ch
    wins_per_group = rows // LANES
    zeros_half = jnp.zeros((ch, LANES), BF16)
    lane = lax.broadcasted_iota(I32, (ch, LANES), 1)
    head0 = lane < B_DK
    lr = lax.broadcasted_iota(I32, (ch, 3 * ch), 0)
    lc = lax.broadcasted_iota(I32, (ch, 3 * ch), 1) % ch
    ltri3 = jnp.where(lc <= lr, 1.0, 0.0).astype(BF16)
    ar = lax.broadcasted_iota(I32, (2 * ch, nsb * ch), 0) % ch
    ac = lax.broadcasted_iota(I32, (2 * ch, nsb * ch), 1)
    att_mask = ((ar // sb) == (ac // ch)) & ((ac % ch) <= ar)
    bd_r = lax.broadcasted_iota(I32, (LANES, LANES), 0)
    bd_c = lax.broadcasted_iota(I32, (LANES, LANES), 1)
    same_head = (bd_r < B_DK) == (bd_c < B_DK)
    ones_bd = jnp.where(same_head, 1.0, 0.0).astype(BF16)
    s_iota = lax.broadcasted_iota(I32, (ch, LANES), 0)

    lb = lb_ref[...]
    log_lb = jnp.log(lb)
    log_1mlb = jnp.log1p(-lb)
    gn = gn_ref[...]
    st_scr[...] = jnp.zeros_like(st_scr)

    def group_body(it, carry):
        t0 = pl.multiple_of(it * rows, rows)
        z = f_ref[0, pl.ds(t0, rows), :]
        q = q_ref[0, pl.ds(t0, rows), :].astype(F32)
        v = i_ref[0, pl.ds(t0, rows), :]
        softplus_tail = jnp.log1p(jnp.exp(-jnp.abs(z)))
        log_sig = -(jnp.maximum(-z, 0.0) + softplus_tail)
        x2 = log_1mlb + log_sig
        amax = jnp.maximum(log_lb, x2)
        log_f = amax + jnp.log1p(jnp.exp(-jnp.abs(log_lb - x2)))
        kk = (1.0 - lb) * jnp.exp(-(jnp.maximum(z, 0.0) + softplus_tail))
        f_hi, f_mid, f_lo = _split3(log_f)
        b_parts, ref_parts = [], []
        for c in range(HGRN_GROUP):
            sl = slice(c * ch, (c + 1) * ch)
            b_c = jnp.dot(ltri3, jnp.concatenate([f_hi[sl], f_mid[sl], f_lo[sl]], axis=0),
                          preferred_element_type=F32)
            b_parts.append(b_c)
            ref_parts.append(jnp.zeros((sb, LANES), F32))
            for i in range(1, nsb):
                ref_parts.append(jnp.broadcast_to(b_c[i * sb - 1:i * sb], (sb, LANES)))
        b = jnp.concatenate(b_parts, axis=0)
        ref = jnp.concatenate(ref_parts, axis=0)
        q_loc = q * jnp.exp(b - ref)
        q_chk = (q * jnp.exp(b)).astype(BF16)
        span = ref - b
        span_max = jnp.max(span)

        st = st_scr[...]
        o_inter = []
        for c in range(HGRN_GROUP):
            lo_r = c * ch
            b_c = b[lo_r:lo_r + ch]
            ref_c = ref[lo_r:lo_r + ch]
            kk_c = kk[lo_r:lo_r + ch]
            v_c = v[lo_r:lo_r + ch]
            b_last = b_c[ch - 1:ch]
            kcat = jnp.concatenate(
                [kk_c * jnp.exp(jnp.minimum(ref_c[i * sb:i * sb + 1] - b_c, HGRN_FAST_SPAN))
                 for i in range(nsb)], axis=0).astype(BF16)
            q_c = q_loc[lo_r:lo_r + ch]
            qm = jnp.concatenate([jnp.where(head0, q_c, 0.0), jnp.where(head0, 0.0, q_c)],
                                 axis=0).astype(BF16)
            att = lax.dot_general(qm, kcat, NT_DIMS, preferred_element_type=F32)
            att = jnp.where(att_mask, att, 0.0).astype(BF16)
            o2 = jnp.dot(att, jnp.concatenate([v_c] * nsb, axis=0), preferred_element_type=F32)
            oi_scr[lo_r:lo_r + ch, :] = jnp.where(head0, o2[:ch], o2[ch:])
            o_inter.append(lax.dot_general(q_chk[lo_r:lo_r + ch], st.astype(BF16), NT_DIMS,
                                           preferred_element_type=F32))
            kh = (kk_c * jnp.exp(b_last - b_c)).astype(BF16)
            kh_win = jnp.concatenate([kh, zeros_half] if c % 2 == 0 else [zeros_half, kh], axis=0)
            upd = jnp.dot(it_ref[0, wins_per_group * it + c // 2], kh_win, preferred_element_type=F32)
            st = st * jnp.exp(b_last) + jnp.where(same_head, upd, 0.0)
        st_scr[...] = st

        @pl.when(span_max > HGRN_FAST_SPAN)
        def _():
            for c in range(HGRN_GROUP):
                lo_r = c * ch

                @pl.when(jnp.max(span[lo_r:lo_r + ch]) > HGRN_FAST_SPAN)
                def _():
                    b_c = b[lo_r:lo_r + ch]
                    q_c = q[lo_r:lo_r + ch]
                    kk_c = kk[lo_r:lo_r + ch]
                    vf = v[lo_r:lo_r + ch].astype(F32)

                    def t_body(t, carry2):
                        onehot = jnp.where(s_iota == t, 1.0, 0.0)
                        b_t = jnp.sum(onehot * b_c, axis=0, keepdims=True)
                        q_t = jnp.sum(onehot * q_c, axis=0, keepdims=True)
                        dec = jnp.exp(jnp.where(s_iota <= t, b_t - b_c, -jnp.inf))
                        w = q_t * kk_c * dec
                        w0 = jnp.sum(jnp.where(head0, w, 0.0), axis=1, keepdims=True)
                        w1 = jnp.sum(jnp.where(head0, 0.0, w), axis=1, keepdims=True)
                        a_col = jnp.where(head0, w0, w1)
                        oi_scr[pl.ds(lo_r + t, 1), :] = jnp.sum(a_col * vf, axis=0, keepdims=True)
                        return carry2

                    lax.fori_loop(0, ch, t_body, 0)

        o = oi_scr[...] + jnp.concatenate(o_inter, axis=0)
        o2_hi, o2_lo = _split2(o * o)
        ss = (jnp.dot(o2_hi, ones_bd, preferred_element_type=F32)
              + jnp.dot(o2_lo, ones_bd, preferred_element_type=F32))
        y = (o * lax.rsqrt(ss * (1.0 / B_DV) + EPS)) * gn
        g = g_ref[0, pl.ds(t0, rows), :].astype(F32)
        o_ref[0, pl.ds(t0, rows), :] = (y * _silu(g)).astype(o_ref.dtype)
        return carry

    lax.fori_loop(0, s // rows, group_body, 0)


def _hgrn(bq, bf, bi, bit, bg, lb, gn):
    bn, s, _ = bq.shape
    assert s % (HGRN_GROUP * HGRN_CHUNK) == 0 and 2 * HGRN_CHUNK == LANES and HGRN_GROUP % 2 == 0
    npair = B_DIM // LANES
    tok = pl.BlockSpec((1, s, LANES), lambda b, p: (b, 0, p))
    return pl.pallas_call(
        _hgrn_kernel,
        grid=(bn, npair),
        in_specs=[tok, tok, tok,
                  pl.BlockSpec((1, s // LANES, LANES, LANES), lambda b, p: (b, 0, p, 0)),
                  tok,
                  pl.BlockSpec((1, LANES), lambda b, p: (0, p)),
                  pl.BlockSpec((1, LANES), lambda b, p: (0, 0))],
        out_specs=tok,
        out_shape=jax.ShapeDtypeStruct((bn, s, B_DIM), BF16),
        scratch_shapes=[pltpu.VMEM((LANES, LANES), F32),
                        pltpu.VMEM((HGRN_GROUP * HGRN_CHUNK, LANES), F32)],
        compiler_params=_cparams("parallel", "parallel"),
    )(bq, bf, bi, bit, bg, lb, gn)


def _gmlp_kernel(uv_ref, vg_ref, vb_ref, ws_ref, bias_ref, o_ref):
    uv = uv_ref[0].astype(F32)
    uv = 0.5 * uv * (1.0 + lax.erf(uv * (2.0 ** -0.5)))
    u = uv[:, :C_DIM]
    v = uv[:, C_DIM:]
    mu = jnp.mean(v, axis=-1, keepdims=True)
    var = jnp.mean(jnp.square(v - mu), axis=-1, keepdims=True)
    vn = ((v - mu) * lax.rsqrt(var + EPS)) * vg_ref[...] + vb_ref[...]
    vn_b = vn.astype(BF16)
    r_t = lax.broadcasted_iota(I32, (C_CHUNK, C_CHUNK), 0)
    r_s = lax.broadcasted_iota(I32, (C_CHUNK, C_CHUNK), 1)
    group = lax.broadcasted_iota(I32, (C_CHUNK, C_DIM), 1) // C_GROUP_DIM
    ws = [jnp.where(r_t >= r_s, ws_ref[gi], 0.0).astype(BF16) for gi in range(C_GROUPS)]
    bias = bias_ref[...]
    for c in range(uv_ref.shape[1] // C_CHUNK):
        sl = slice(c * C_CHUNK, (c + 1) * C_CHUNK)
        mixed = jnp.zeros((C_CHUNK, C_DIM), F32)
        for gi in range(C_GROUPS):
            m = jnp.dot(ws[gi], vn_b[sl], preferred_element_type=F32)
            mixed = jnp.where(group == gi, m, mixed)
        o_ref[0, sl, :] = (u[sl] * (mixed + bias)).astype(o_ref.dtype)


def _gmlp(cuv, vg, vb, ws, bs):
    bn, s, _ = cuv.shape
    tm = _token_tile(s, 4 * C_CHUNK)
    bias = jnp.repeat(bs.T, C_GROUP_DIM, axis=1)
    full = lambda a: pl.BlockSpec(a.shape, lambda b, t: (0,) * a.ndim)
    return pl.pallas_call(
        _gmlp_kernel,
        grid=(bn, s // tm),
        in_specs=[pl.BlockSpec((1, tm, 2 * C_DIM), lambda b, t: (b, t, 0)),
                  full(vg), full(vb), full(ws), full(bias)],
        out_specs=pl.BlockSpec((1, tm, C_DIM), lambda b, t: (b, t, 0)),
        out_shape=jax.ShapeDtypeStruct((bn, s, C_DIM), BF16),
        compiler_params=_cparams("parallel", "parallel"),
    )(cuv, vg, vb, ws, bias)


def _outproj_kernel(x_ref, a_ref, b_ref, c_ref, wa_ref, wb_ref, wc_ref, g1_ref, o_ref):
    mix = (jnp.dot(a_ref[0], wa_ref[...], preferred_element_type=F32)
           + jnp.dot(b_ref[0], wb_ref[...], preferred_element_type=F32)
           + jnp.dot(c_ref[0], wc_ref[...], preferred_element_type=F32))
    o_ref[0] = x_ref[0] + g1_ref[0] * mix


def _outproj(x, a, b, c, w_out_l, g1, tm):
    bn, s, d = x.shape
    wa = w_out_l[:A_DIM].astype(BF16)
    wb = w_out_l[A_DIM:A_DIM + B_DIM].astype(BF16)
    wc = w_out_l[A_DIM + B_DIM:].astype(BF16)
    tok = lambda w: pl.BlockSpec((1, tm, w), lambda bb, t: (bb, t, 0))
    full = lambda arr: pl.BlockSpec(arr.shape, lambda bb, t: (0, 0))
    return pl.pallas_call(
        _outproj_kernel,
        grid=(bn, s // tm),
        in_specs=[tok(d), tok(A_DIM), tok(B_DIM), tok(C_DIM), full(wa), full(wb), full(wc),
                  pl.BlockSpec((1, 1, d), lambda bb, t: (bb, 0, 0))],
        out_specs=tok(d),
        out_shape=jax.ShapeDtypeStruct((bn, s, d), F32),
        compiler_params=_cparams("parallel", "parallel"),
    )(x, a, b, c, wa, wb, wc, g1)


def _ffn_kernel(x_ref, sh_ref, sc_ref, gate_ref, g_ref, wg_ref, wu_ref, wd_ref, o_ref, h_scr, acc_scr):
    f = pl.program_id(2)

    @pl.when(f == 0)
    def _():
        h_scr[...] = _modulated_rmsnorm(x_ref[0], g_ref[...], sc_ref[0], sh_ref[0]).astype(BF16)
        acc_scr[...] = jnp.zeros_like(acc_scr)

    h = h_scr[...]
    a = jnp.dot(h, wg_ref[...], preferred_element_type=F32)
    u = jnp.dot(h, wu_ref[...], preferred_element_type=F32)
    act = (_silu(a) * u).astype(BF16)
    acc_scr[...] += jnp.dot(act, wd_ref[...], preferred_element_type=F32)

    @pl.when(f == pl.num_programs(2) - 1)
    def _():
        o_ref[0] = x_ref[0] + gate_ref[0] * acc_scr[...]


def _ffn(x, sh, sc, gate, g, wg, wu, wd, tm, tf):
    bn, s, d = x.shape
    fdim = wg.shape[1]
    tok = pl.BlockSpec((1, tm, d), lambda b, t, f: (b, t, 0))
    vec = pl.BlockSpec((1, 1, d), lambda b, t, f: (b, 0, 0))
    return pl.pallas_call(
        _ffn_kernel,
        grid=(bn, s // tm, fdim // tf),
        in_specs=[tok, vec, vec, vec,
                  pl.BlockSpec((1, d), lambda b, t, f: (0, 0)),
                  pl.BlockSpec((d, tf), lambda b, t, f: (0, f)),
                  pl.BlockSpec((d, tf), lambda b, t, f: (0, f)),
                  pl.BlockSpec((tf, d), lambda b, t, f: (f, 0))],
        out_specs=tok,
        out_shape=jax.ShapeDtypeStruct((bn, s, d), F32),
        scratch_shapes=[pltpu.VMEM((tm, d), BF16), pltpu.VMEM((tm, d), F32)],
        compiler_params=_cparams("parallel", "parallel", "arbitrary"),
    )(x, sh, sc, gate, g, wg, wu, wd)


MOE_RC = 256
MOE_TB = 512


def _moe_kernel(x_ref, sh_ref, sc_ref, gate_ref, g_ref, wr_ref, wg_ref, wu_ref, wd_ref, o_ref,
                h_scr, xe_scr, ye_scr, gcol_scr, gates_scr, rsel_scr, rn_scr, rcol_scr, cnt_smem):
    e = pl.program_id(2)
    f = pl.program_id(3)
    tm = x_ref.shape[1]
    rc_rows = MOE_RC
    tb = MOE_TB
    ntb = tm // tb
    neg_inf = jnp.float32(-jnp.inf)

    @pl.when((e == 0) & (f == 0))
    def _route():
        for jb in range(ntb):
            rows = slice(jb * tb, (jb + 1) * tb)
            h = _modulated_rmsnorm(x_ref[0, rows, :], g_ref[...], sc_ref[0], sh_ref[0])
            h_scr[rows, :] = h.astype(BF16)
            rcol_scr[rows, :] = jnp.dot(h, wr_ref[...], precision=lax.Precision.HIGHEST,
                                        preferred_element_type=F32)
            o_ref[0, rows, :] = jnp.zeros((tb, o_ref.shape[2]), F32)
        logits = rcol_scr[...].T[:N_EXPERTS]
        ridx = lax.broadcasted_iota(I32, (N_EXPERTS, tm), 0)
        m1 = jnp.max(logits, axis=0, keepdims=True)
        i1 = jnp.min(jnp.where(logits == m1, ridx, N_EXPERTS), axis=0, keepdims=True)
        rest = jnp.where(ridx == i1, neg_inf, logits)
        m2 = jnp.max(rest, axis=0, keepdims=True)
        i2 = jnp.min(jnp.where(rest == m2, ridx, N_EXPERTS), axis=0, keepdims=True)
        e2 = jnp.exp(m2 - m1)
        den = 1.0 + e2
        gates_scr[...] = jnp.where(ridx == i1, 1.0 / den, jnp.where(ridx == i2, e2 / den, 0.0))
        sel = jnp.where(ridx == i1, 1.0, jnp.where(ridx == i2, 1.0, 0.0))
        ur = lax.broadcasted_iota(I32, (LANES, LANES), 0)
        uc = lax.broadcasted_iota(I32, (LANES, LANES), 1)
        utri = jnp.where(ur <= uc, 1.0, 0.0).astype(BF16)
        carry = jnp.zeros((N_EXPERTS, 1), F32)
        for kb in range(tm // LANES):
            sb = sel[:, kb * LANES:(kb + 1) * LANES]
            pref = jnp.dot(sb.astype(BF16), utri, preferred_element_type=F32) + carry
            rsel_scr[:, kb * LANES:(kb + 1) * LANES] = jnp.where(sb > 0.0, pref, -1.0)
            carry = pref[:, LANES - 1:LANES]
        r8 = lax.broadcasted_iota(I32, (N_EXPERTS, 1), 0)
        for ee in range(N_EXPERTS):
            cnt_smem[ee] = jnp.sum(jnp.where(r8 == ee, carry, 0.0)).astype(I32)
        rn_scr[...] = jnp.concatenate(
            [rsel_scr[...], jnp.zeros((LANES - N_EXPERTS, tm), F32)], axis=0).T

    n_rc = (cnt_smem[e] + (rc_rows - 1)) // rc_rows

    @pl.when(f == 0)
    def _gather():
        rsel_row = rsel_scr[pl.ds(e, 1), :]
        gate_row = gates_scr[pl.ds(e, 1), :]
        lane = lax.broadcasted_iota(I32, (tm, LANES), 1)
        rcol = jnp.sum(jnp.where(lane == e, rn_scr[...], 0.0), axis=1, keepdims=True)
        rcol_scr[...] = jnp.broadcast_to(rcol, (tm, LANES))
        h = h_scr[...]

        def body(rc, carry):
            r0 = pl.multiple_of(rc * rc_rows, rc_rows)
            want = (r0 + 1 + lax.broadcasted_iota(I32, (rc_rows, tm), 0)).astype(F32)
            pm = rsel_row == want
            pb = jnp.where(pm, 1.0, 0.0).astype(BF16)
            xe_scr[pl.ds(r0, rc_rows), :] = jnp.dot(pb, h, preferred_element_type=F32).astype(BF16)
            gcol = jnp.sum(jnp.where(pm, gate_row, 0.0), axis=1, keepdims=True)
            gcol_scr[pl.ds(r0, rc_rows), :] = jnp.broadcast_to(gcol, (rc_rows, LANES))
            ye_scr[pl.ds(r0, rc_rows), :] = jnp.zeros((rc_rows, ye_scr.shape[1]), F32)
            return carry

        lax.fori_loop(0, n_rc, body, 0)

    def ffn_body(rc, carry):
        r0 = pl.multiple_of(rc * rc_rows, rc_rows)
        xe = xe_scr[pl.ds(r0, rc_rows), :]
        a = jnp.dot(xe, wg_ref[0], preferred_element_type=F32)
        u = jnp.dot(xe, wu_ref[0], preferred_element_type=F32)
        act = (_silu(a) * u * gcol_scr[pl.ds(r0, rc_rows), 0:1]).astype(BF16)
        ye_scr[pl.ds(r0, rc_rows), :] += jnp.dot(act, wd_ref[0], preferred_element_type=F32)
        return carry

    lax.fori_loop(0, n_rc, ffn_body, 0)

    @pl.when(f == pl.num_programs(3) - 1)
    def _scatter():
        rcol = rcol_scr[:, 0:1]

        def body(rc, carry):
            r0 = pl.multiple_of(rc * rc_rows, rc_rows)
            want = (r0 + 1 + lax.broadcasted_iota(I32, (tm, rc_rows), 1)).astype(F32)
            pt = jnp.where(rcol == want, 1.0, 0.0).astype(BF16)
            ye = ye_scr[pl.ds(r0, rc_rows), :].astype(BF16)
            o_ref[0] += jnp.dot(pt, ye, preferred_element_type=F32)
            return carry

        lax.fori_loop(0, n_rc, body, 0)

    @pl.when((e == pl.num_programs(2) - 1) & (f == pl.num_programs(3) - 1))
    def _residual():
        o_ref[0] = x_ref[0] + gate_ref[0] * o_ref[0]


def _moe(x, sh, sc, gate, g, w_router, wg, wu, wd, tm, tf):
    bn, s, d = x.shape
    ne, _, fdim = wg.shape
    assert ne == N_EXPERTS and tm % MOE_RC == 0 and tm % MOE_TB == 0
    wr = jnp.zeros((d, LANES), F32).at[:, :ne].set(w_router)
    tok_in = pl.BlockSpec((1, tm, d), lambda b, t, e, f: (b, t, 0), pipeline_mode=pl.Buffered(1))
    tok_out = pl.BlockSpec((1, tm, d), lambda b, t, e, f: (b, t, 0))
    vec = pl.BlockSpec((1, 1, d), lambda b, t, e, f: (b, 0, 0))
    return pl.pallas_call(
        _moe_kernel,
        grid=(bn, s // tm, ne, fdim // tf),
        in_specs=[tok_in, vec, vec, vec,
                  pl.BlockSpec((1, d), lambda b, t, e, f: (0, 0)),
                  pl.BlockSpec((d, LANES), lambda b, t, e, f: (0, 0)),
                  pl.BlockSpec((1, d, tf), lambda b, t, e, f: (e, 0, f)),
                  pl.BlockSpec((1, d, tf), lambda b, t, e, f: (e, 0, f)),
                  pl.BlockSpec((1, tf, d), lambda b, t, e, f: (e, f, 0))],
        out_specs=tok_out,
        out_shape=jax.ShapeDtypeStruct((bn, s, d), F32),
        scratch_shapes=[pltpu.VMEM((tm, d), BF16),
                        pltpu.VMEM((tm, d), BF16),
                        pltpu.VMEM((tm, d), F32),
                        pltpu.VMEM((tm, LANES), F32),
                        pltpu.VMEM((ne, tm), F32),
                        pltpu.VMEM((ne, tm), F32),
                        pltpu.VMEM((tm, LANES), F32),
                        pltpu.VMEM((tm, LANES), F32),
                        pltpu.SMEM((ne,), I32)],
        compiler_params=pltpu.CompilerParams(
            dimension_semantics=("parallel", "parallel", "arbitrary", "arbitrary"),
            vmem_limit_bytes=56 * 1024 * 1024),
    )(x, sh, sc, gate, g, wr, wg, wu, wd)


def _final_norm_kernel(x_ref, g_ref, o_ref):
    x = x_ref[0]
    var = jnp.mean(x * x, axis=-1, keepdims=True)
    o_ref[0] = (x * lax.rsqrt(var + EPS)) * g_ref[...]


def _final_norm(x, g, tm):
    bn, s, d = x.shape
    tok = pl.BlockSpec((1, tm, d), lambda b, t: (b, t, 0))
    return pl.pallas_call(
        _final_norm_kernel,
        grid=(bn, s // tm),
        in_specs=[tok, pl.BlockSpec((1, d), lambda b, t: (0, 0))],
        out_specs=tok,
        out_shape=jax.ShapeDtypeStruct((bn, s, d), F32),
        compiler_params=_cparams("parallel", "parallel"),
    )(x, g)


def _token_tile(s, want):
    tm = min(want, s)
    assert s % tm == 0 and tm % QB == 0
    return tm


def kernel(x, c, positions, w_ada, b_ada, g_norm_mix, g_norm_ffn, w_in, w_out, hgrn_lb_logits, hgrn_out_norm, gmlp_vnorm_g, gmlp_vnorm_b, gmlp_w_s, gmlp_b_s, ffn_w_gate, ffn_w_up, ffn_w_down, moe_w_router, moe_w_gate, moe_w_up, moe_w_down, g_final):
    bn, s, d = x.shape
    depth = w_in.shape[0]
    assert s % QB == 0 and d == sum(IN_WIDTHS[:1]) + B_DIM + C_DIM
    tm_proj = _token_tile(s, 512)
    tm_ffn = _token_tile(s, 1024)

    p_lb = jax.nn.softmax(hgrn_lb_logits.astype(F32), axis=0)
    cum = jnp.cumsum(p_lb, axis=0)
    lower_bounds = cum - cum[0:1]

    mod = _ada_mod(c, w_ada, b_ada)
    cosn, sinn, cost, sint = _rope_tables(positions)

    for l in range(depth):
        sh1, sc1, g1, sh2, sc2, g2 = [mod[l, :, i * d:(i + 1) * d].reshape(bn, 1, d) for i in range(6)]
        w_n, w_t = _prep_in_weights(w_in[l])
        (kz, ik3, bq, bf, bi, bg, cuv, qt, iq3t, vt4, iwt, bit) = _inproj(
            x, sh1, sc1, g_norm_mix[l].reshape(1, d), w_n, w_t, cosn, sinn, cost, sint, tm_proj)
        a_out = _dsa(ik3, kz, vt4, iq3t, iwt, qt)
        b_out = _hgrn(bq, bf, bi, bit, bg, lower_bounds[l].reshape(1, B_DIM),
                      jnp.tile(hgrn_out_norm[l], LANES // B_DV).reshape(1, LANES))
        c_out = _gmlp(cuv, gmlp_vnorm_g[l].reshape(1, C_DIM), gmlp_vnorm_b[l].reshape(1, C_DIM),
                      gmlp_w_s[l], gmlp_b_s[l])
        x = _outproj(x, a_out, b_out, c_out, w_out[l], g1, tm_proj)
        gf = g_norm_ffn[l].reshape(1, d)
        if l % 2 == 0:
            i = l // 2
            x = _ffn(x, sh2, sc2, g2, gf, ffn_w_gate[i].astype(BF16), ffn_w_up[i].astype(BF16),
                     ffn_w_down[i].astype(BF16), tm_ffn, 256)
        else:
            i = l // 2
            x = _moe(x, sh2, sc2, g2, gf, moe_w_router[i], moe_w_gate[i].astype(BF16),
                     moe_w_up[i].astype(BF16), moe_w_down[i].astype(BF16), _token_tile(s, 2048), 512)
    return _final_norm(x, g_final.reshape(1, d), tm_proj)
```

```python
import functools

import numpy as np
import jax
import jax.numpy as jnp
from jax import lax
from jax.experimental import pallas as pl
from jax.experimental.pallas import tpu as pltpu

F32 = jnp.float32
BF16 = jnp.bfloat16
I32 = jnp.int32
I16 = jnp.int16

HEAD_DIM = 64
A_HEADS = 6
A_DIM = A_HEADS * HEAD_DIM
IDX_HEADS = 4
IDX_DIM = 64
TOPK_MAX = 256
B_HEADS = 6
B_DK = 64
B_DV = 64
B_DIM = B_HEADS * B_DV
HGRN_CHUNK = 64
C_GROUPS = 4
C_GROUP_DIM = 64
C_DIM = C_GROUPS * C_GROUP_DIM
C_CHUNK = 128
ROPE_THETA = 10000.0
N_EXPERTS = 8
EPS = 1e-6
IN_WIDTHS = (A_DIM, HEAD_DIM, HEAD_DIM, IDX_HEADS * IDX_DIM, IDX_DIM, IDX_HEADS,
             B_DIM, B_DIM, B_DIM, B_DIM, 2 * C_DIM)

LANES = 128
KC = 128
QB = 2 * KC
VMEM_LIMIT = 48 * 1024 * 1024
HGRN_FAST_SPAN = 80.0
HGRN_SUB = 16
HGRN_GROUP = 4
INT_MIN = -2 ** 31

NT_DIMS = (((1,), (1,)), ((), ()))
TN_DIMS = (((0,), (0,)), ((), ()))


def _cparams(*sem):
    return pltpu.CompilerParams(dimension_semantics=sem, vmem_limit_bytes=VMEM_LIMIT)


def _split2(x):
    hi = x.astype(BF16)
    lo = (x - hi.astype(F32)).astype(BF16)
    return hi, lo


def _split3(x):
    hi = x.astype(BF16)
    r1 = x - hi.astype(F32)
    mid = r1.astype(BF16)
    lo = (r1 - mid.astype(F32)).astype(BF16)
    return hi, mid, lo


def _silu(x):
    return x * (1.0 / (1.0 + jnp.exp(-x)))


def _modulated_rmsnorm(x, g, sc, sh):
    var = jnp.mean(x * x, axis=-1, keepdims=True)
    y = x * lax.rsqrt(var + EPS)
    return (y * g) * (1.0 + sc) + sh


def _ada_kernel(c_ref, w_ref, b_ref, o_ref):
    cond = _silu(c_ref[...])
    o_ref[0] = jnp.dot(cond, w_ref[0], precision=lax.Precision.HIGHEST,
                       preferred_element_type=F32) + b_ref[0]


def _ada_mod(c, w_ada, b_ada):
    depth, d, d6 = w_ada.shape
    bn = c.shape[0]
    tn = 1536
    return pl.pallas_call(
        _ada_kernel,
        grid=(depth, d6 // tn),
        in_specs=[pl.BlockSpec((bn, d), lambda l, n: (0, 0)),
                  pl.BlockSpec((1, d, tn), lambda l, n: (l, 0, n)),
                  pl.BlockSpec((1, 1, tn), lambda l, n: (l, 0, n))],
        out_specs=pl.BlockSpec((1, bn, tn), lambda l, n: (l, 0, n)),
        out_shape=jax.ShapeDtypeStruct((depth, bn, d6), F32),
        compiler_params=_cparams("parallel", "parallel"),
    )(c, w_ada, b_ada.reshape(depth, 1, d6))


def _rope_kernel(posn_ref, post_ref, invn_ref, signn_ref, invt_ref,
                 cosn_ref, sinn_ref, cost_ref, sint_ref):
    ang_n = posn_ref[0].astype(F32) * invn_ref[...]
    cosn_ref[0] = jnp.cos(ang_n)
    sinn_ref[0] = jnp.sin(ang_n) * signn_ref[...]
    ang_t = invt_ref[...] * post_ref[0].astype(F32)
    cost_ref[0] = jnp.cos(ang_t)
    sint_ref[0] = jnp.sin(ang_t)


def _rope_tables(positions):
    bn, s = positions.shape
    half = HEAD_DIM // 2
    inv = ROPE_THETA ** (-jnp.arange(0, HEAD_DIM, 2, dtype=F32) / HEAD_DIM)
    inv_n = jnp.tile(inv, LANES // half).reshape(1, LANES)
    sign_n = jnp.tile(jnp.concatenate([-jnp.ones((half,), F32), jnp.ones((half,), F32)]),
                      LANES // HEAD_DIM).reshape(1, LANES)
    inv_t = inv.reshape(half, 1)
    full = lambda shape: pl.BlockSpec(shape, lambda b: (0,) * len(shape))
    return pl.pallas_call(
        _rope_kernel,
        grid=(bn,),
        in_specs=[pl.BlockSpec((1, s, 1), lambda b: (b, 0, 0)),
                  pl.BlockSpec((1, 1, s), lambda b: (b, 0, 0)),
                  full((1, LANES)), full((1, LANES)), full((half, 1))],
        out_specs=[pl.BlockSpec((1, s, LANES), lambda b: (b, 0, 0)),
                   pl.BlockSpec((1, s, LANES), lambda b: (b, 0, 0)),
                   pl.BlockSpec((1, half, s), lambda b: (b, 0, 0)),
                   pl.BlockSpec((1, half, s), lambda b: (b, 0, 0))],
        out_shape=[jax.ShapeDtypeStruct((bn, s, LANES), F32),
                   jax.ShapeDtypeStruct((bn, s, LANES), F32),
                   jax.ShapeDtypeStruct((bn, half, s), F32),
                   jax.ShapeDtypeStruct((bn, half, s), F32)],
        compiler_params=_cparams("parallel"),
    )(positions.reshape(bn, s, 1), positions.reshape(bn, 1, s), inv_n, sign_n, inv_t)


N_KZ = 0
N_IK = 2 * LANES
N_BQ = 4 * LANES
N_BF = N_BQ + B_DIM
N_BI = N_BF + B_DIM
N_BG = N_BI + B_DIM
N_CUV = N_BG + B_DIM
N_COLS = N_CUV + 2 * C_DIM
T_Q = 0
T_IQ = A_DIM
T_V = T_IQ + IDX_HEADS * IDX_DIM
T_IW = T_V + HEAD_DIM
T_BI = T_IW + 8
T_ROWS = T_BI + B_DIM


def _prep_in_weights(w_in_l):
    offs = np.concatenate([[0], np.cumsum(IN_WIDTHS)])
    sl = lambda i: w_in_l[:, int(offs[i]):int(offs[i + 1])]
    aq, ak, av, iq, ik, iw, bq, bf, bi, bg, cuv = [sl(i) for i in range(11)]
    half = HEAD_DIM // 2
    rot = lambda w: jnp.concatenate([w[:, half:], w[:, :half]], axis=1)
    z = jnp.zeros_like(ak)
    w_n = jnp.concatenate([ak, z, rot(ak), z, ik, ik, rot(ik), rot(ik), bq, bf, bi, bg, cuv], axis=1)
    w_t = jnp.concatenate([aq, iq, av, iw, jnp.zeros((w_in_l.shape[0], 4), w_in_l.dtype), bi], axis=1).T
    return w_n.astype(BF16), w_t.astype(BF16)


def _inproj_kernel(x_ref, sh_ref, sc_ref, g_ref, wn_ref, wt_ref, cosn_ref, sinn_ref, cost_ref, sint_ref,
                   kz_ref, ik3_ref, bq_ref, bf_ref, bi_ref, bg_ref, cuv_ref,
                   qt_ref, iq3t_ref, vt_ref, iwt_ref, bit_ref):
    tm = x_ref.shape[1]
    half = HEAD_DIM // 2
    h = _modulated_rmsnorm(x_ref[0], g_ref[...], sc_ref[0], sh_ref[0]).astype(BF16)

    cosn = cosn_ref[0]
    sinn = sinn_ref[0]
    zk = jnp.dot(h, wn_ref[:, N_KZ:N_KZ + 2 * LANES], preferred_element_type=F32)
    kz_ref[0] = (zk[:, :LANES] * cosn + zk[:, LANES:] * sinn).astype(BF16)
    zi = jnp.dot(h, wn_ref[:, N_IK:N_IK + 2 * LANES], preferred_element_type=F32)
    ik2 = zi[:, :LANES] * cosn + zi[:, LANES:] * sinn
    hi, lo = _split2(ik2)
    lane = lax.broadcasted_iota(I32, (tm, LANES), 1)
    ik3_ref[0, :, :LANES] = jnp.where(lane < IDX_DIM, hi, lo)
    ik3_ref[0, :, LANES:] = hi
    bq_ref[0] = jnp.dot(h, wn_ref[:, N_BQ:N_BF], preferred_element_type=F32).astype(BF16)
    bf_ref[0] = jnp.dot(h, wn_ref[:, N_BF:N_BI], preferred_element_type=F32)
    bi_ref[0] = jnp.dot(h, wn_ref[:, N_BI:N_BG], preferred_element_type=F32).astype(BF16)
    bg_ref[0] = jnp.dot(h, wn_ref[:, N_BG:N_CUV], preferred_element_type=F32).astype(BF16)
    cuv_ref[0] = jnp.dot(h, wn_ref[:, N_CUV:N_COLS], preferred_element_type=F32).astype(BF16)

    zt = lax.dot_general(wt_ref[...], h, NT_DIMS, preferred_element_type=F32)
    cost = cost_ref[0]
    sint = sint_ref[0]
    qscale = HEAD_DIM ** -0.5
    for hh in range(A_HEADS):
        r0 = T_Q + hh * HEAD_DIM
        x1 = zt[r0:r0 + half]
        x2 = zt[r0 + half:r0 + HEAD_DIM]
        qt_ref[0, hh * HEAD_DIM:hh * HEAD_DIM + half] = ((x1 * cost - x2 * sint) * qscale).astype(BF16)
        qt_ref[0, hh * HEAD_DIM + half:(hh + 1) * HEAD_DIM] = ((x2 * cost + x1 * sint) * qscale).astype(BF16)
    iscale = IDX_DIM ** -0.5
    zero = jnp.zeros((IDX_DIM, tm), BF16)
    for hh in range(IDX_HEADS):
        r0 = T_IQ + hh * IDX_DIM
        x1 = zt[r0:r0 + half]
        x2 = zt[r0 + half:r0 + IDX_DIM]
        y = jnp.concatenate([(x1 * cost - x2 * sint) * iscale, (x2 * cost + x1 * sint) * iscale], axis=0)
        hi, lo = _split2(y)
        iq3t_ref[0, hh, 0 * IDX_DIM:1 * IDX_DIM] = hi
        iq3t_ref[0, hh, 1 * IDX_DIM:2 * IDX_DIM] = hi
        iq3t_ref[0, hh, 2 * IDX_DIM:3 * IDX_DIM] = lo
        iq3t_ref[0, hh, 3 * IDX_DIM:4 * IDX_DIM] = zero
    vt = zt[T_V:T_V + HEAD_DIM].astype(BF16)
    for i in range(tm // KC):
        vt_ref[0, i] = vt[:, i * KC:(i + 1) * KC]
    iwt_ref[0] = zt[T_IW:T_IW + 8] * (IDX_HEADS ** -0.5)
    bit = zt[T_BI:T_BI + B_DIM].astype(BF16)
    for i in range(tm // LANES):
        bit_ref[0, i] = bit[:, i * LANES:(i + 1) * LANES]


def _inproj(x, sh, sc, g, w_n, w_t, cosn, sinn, cost, sint, tm):
    bn, s, d = x.shape
    nt = s // tm
    half = HEAD_DIM // 2
    tok = lambda w: pl.BlockSpec((1, tm, w), lambda b, t: (b, t, 0))
    vec = pl.BlockSpec((1, 1, d), lambda b, t: (b, 0, 0))
    full2 = lambda a: pl.BlockSpec(a.shape, lambda b, t: (0, 0))
    out_shapes = [
        jax.ShapeDtypeStruct((bn, s, LANES), BF16),
        jax.ShapeDtypeStruct((bn, s, 2 * LANES), BF16),
        jax.ShapeDtypeStruct((bn, s, B_DIM), BF16),
        jax.ShapeDtypeStruct((bn, s, B_DIM), F32),
        jax.ShapeDtypeStruct((bn, s, B_DIM), BF16),
        jax.ShapeDtypeStruct((bn, s, B_DIM), BF16),
        jax.ShapeDtypeStruct((bn, s, 2 * C_DIM), BF16),
        jax.ShapeDtypeStruct((bn, A_DIM, s), BF16),
        jax.ShapeDtypeStruct((bn, IDX_HEADS, 4 * IDX_DIM, s), BF16),
        jax.ShapeDtypeStruct((bn, s // KC, HEAD_DIM, KC), BF16),
        jax.ShapeDtypeStruct((bn, 8, s), F32),
        jax.ShapeDtypeStruct((bn, s // LANES, B_DIM, LANES), BF16),
    ]
    out_specs = [
        tok(LANES), tok(2 * LANES), tok(B_DIM), tok(B_DIM), tok(B_DIM), tok(B_DIM), tok(2 * C_DIM),
        pl.BlockSpec((1, A_DIM, tm), lambda b, t: (b, 0, t)),
        pl.BlockSpec((1, IDX_HEADS, 4 * IDX_DIM, tm), lambda b, t: (b, 0, 0, t)),
        pl.BlockSpec((1, tm // KC, HEAD_DIM, KC), lambda b, t: (b, t, 0, 0)),
        pl.BlockSpec((1, 8, tm), lambda b, t: (b, 0, t)),
        pl.BlockSpec((1, tm // LANES, B_DIM, LANES), lambda b, t: (b, t, 0, 0)),
    ]
    return pl.pallas_call(
        _inproj_kernel,
        grid=(bn, nt),
        in_specs=[tok(d), vec, vec, full2(g), full2(w_n), full2(w_t),
                  tok(LANES), tok(LANES),
                  pl.BlockSpec((1, half, tm), lambda b, t: (b, 0, t)),
                  pl.BlockSpec((1, half, tm), lambda b, t: (b, 0, t))],
        out_specs=out_specs,
        out_shape=out_shapes,
        compiler_params=_cparams("parallel", "parallel"),
    )(x, sh, sc, g, w_n, w_t, cosn, sinn, cost, sint)


def _dsa_kernel(n_top, ik3_ref, kz_ref, vt_ref, iq3t_ref, iwt_ref, qt_ref, out_ref,
                key_scr, hi_scr, lo_scr, lg_scr, acc_scr):
    j = pl.program_id(1)
    npair = j + 1
    pair = 2 * KC
    row = lax.broadcasted_iota(I32, (KC, QB), 0)
    col = lax.broadcasted_iota(I32, (KC, QB), 1)
    int_min = jnp.int32(INT_MIN)
    i16_min = jnp.int16(-2 ** 15)

    def score_pair(p, diagonal):
        for u in range(2):
            ks = pl.multiple_of(p * pair + u * KC, KC)
            ikc = ik3_ref[0, pl.ds(ks, KC), :]
            sc = jnp.zeros((KC, QB), F32)
            for hh in range(IDX_HEADS):
                rel = jnp.dot(ikc, iq3t_ref[0, hh], preferred_element_type=F32)
                sc = sc + jnp.maximum(rel, 0.0) * iwt_ref[0, hh:hh + 1, :]
            sc = jnp.where(sc == 0.0, 0.0, sc)
            bits = pltpu.bitcast(sc, I32)
            key = bits ^ ((bits >> 31) & jnp.int32(0x7FFFFFFF))
            if diagonal:
                key = jnp.where((u * KC + row) <= col, key, int_min)
            key_scr[pl.ds(ks, KC), :] = key
            hi_scr[pl.ds(ks, KC), :] = (key >> 16).astype(I16)
            lo_scr[pl.ds(ks, KC), :] = ((key & jnp.int32(0xFFFF)) - 32768).astype(I16)

    def score_body(p, carry):
        score_pair(p, False)
        return carry

    lax.fori_loop(0, j, score_body, 0)
    score_pair(j, True)

    def count16(ref, pred_fn):
        def body(p, acc):
            ks = pl.multiple_of(p * pair, pair)
            m = jnp.where(pred_fn(ref[pl.ds(ks, pair), :]), jnp.int16(1), jnp.int16(0))
            parts = [m[16 * i:16 * (i + 1)] for i in range(pair // 16)]
            while len(parts) > 1:
                parts = [parts[i] + parts[i + 1] for i in range(0, len(parts), 2)]
            return acc + parts[0]
        acc = lax.fori_loop(0, npair, body, jnp.zeros((16, QB), I16))
        return jnp.sum(acc.astype(I32), axis=0, keepdims=True)

    def bisect16(ref, k_needed):
        def bit_body(i, t_u):
            cand = t_u | jnp.left_shift(jnp.int32(1), 15 - i)
            cand16 = (cand - 32768).astype(I16)
            cnt = count16(ref, lambda x: x >= cand16)
            return jnp.where(cnt >= k_needed, cand, t_u)
        return lax.fori_loop(0, 16, bit_body, jnp.zeros((1, QB), I32))

    t_hi = bisect16(hi_scr, n_top)
    t_hi16 = (t_hi - 32768).astype(I16)
    n_hi_gt = count16(hi_scr, lambda x: x > t_hi16)

    def bucket_body(p, carry):
        ks = pl.multiple_of(p * pair, pair)
        in_bucket = hi_scr[pl.ds(ks, pair), :] == t_hi16
        lo_scr[pl.ds(ks, pair), :] = jnp.where(in_bucket, lo_scr[pl.ds(ks, pair), :], i16_min)
        return carry

    lax.fori_loop(0, npair, bucket_body, 0)
    t_lo = bisect16(lo_scr, n_top - n_hi_gt)
    t_lo16 = (t_lo - 32768).astype(I16)
    n_gt = n_hi_gt + count16(lo_scr, lambda x: x > t_lo16)
    thr = jnp.left_shift(t_hi - 32768, 16) | t_lo
    n_tie = (n_top - n_gt).astype(F32)
    thr_valid = jnp.where(thr > int_min, 1.0, 0.0)

    tr = lax.broadcasted_iota(I32, (KC, KC), 0)
    tc = lax.broadcasted_iota(I32, (KC, KC), 1)
    ltri = jnp.where(tr >= tc, 1.0, 0.0).astype(BF16)
    zpad = jnp.zeros((HEAD_DIM, QB), BF16)
    qpad = [jnp.concatenate([qt_ref[0, hh * HEAD_DIM:(hh + 1) * HEAD_DIM, :], zpad], axis=0)
            for hh in range(A_HEADS)]
    neg_inf = jnp.float32(-jnp.inf)

    def pass_a(p, carry):
        tie_cnt, ms = carry
        ms = list(ms)
        for u in range(2):
            ks = pl.multiple_of(p * pair + u * KC, KC)
            kc = key_scr[pl.ds(ks, KC), :]
            eqf = jnp.where(kc == thr, thr_valid, 0.0)
            pref = jnp.dot(ltri, eqf.astype(BF16), preferred_element_type=F32)
            keep_tie = eqf * jnp.where((tie_cnt + pref) <= n_tie, 1.0, 0.0)
            sel = jnp.where(kc > thr, 1.0, keep_tie) > 0.5
            tie_cnt = tie_cnt + pref[KC - 1:KC, :]
            kzc = kz_ref[0, pl.ds(ks, KC), :]
            for hh in range(A_HEADS):
                lt = jnp.dot(kzc, qpad[hh], preferred_element_type=F32)
                lt = jnp.where(sel, lt, neg_inf)
                lg_scr[hh, pl.ds(ks, KC), :] = lt
                ms[hh] = jnp.maximum(ms[hh], jnp.max(lt, axis=0, keepdims=True))
        return tie_cnt, tuple(ms)

    init_m = tuple(jnp.full((1, QB), neg_inf, F32) for _ in range(A_HEADS))
    _, ms = lax.fori_loop(0, npair, pass_a, (jnp.zeros((1, QB), F32), init_m))

    acc_scr[...] = jnp.zeros_like(acc_scr)

    def pass_b(p, ls):
        ls = list(ls)
        ks = pl.multiple_of(p * pair, pair)
        vt2 = jnp.concatenate([vt_ref[0, 2 * p], vt_ref[0, 2 * p + 1]], axis=1)
        for hh in range(A_HEADS):
            pr = jnp.exp(lg_scr[hh, pl.ds(ks, pair), :] - ms[hh])
            ls[hh] = ls[hh] + jnp.sum(pr, axis=0, keepdims=True)
            acc_scr[hh * HEAD_DIM:(hh + 1) * HEAD_DIM, :] += jnp.dot(
                vt2, pr.astype(BF16), preferred_element_type=F32)
        return tuple(ls)

    ls = lax.fori_loop(0, npair, pass_b, tuple(jnp.zeros((1, QB), F32) for _ in range(A_HEADS)))
    o_t = jnp.concatenate(
        [acc_scr[hh * HEAD_DIM:(hh + 1) * HEAD_DIM, :] / ls[hh] for hh in range(A_HEADS)], axis=0)
    out_ref[0] = o_t.T.astype(BF16)


def _dsa(ik3, kz, vt4, iq3t, iwt, qt):
    bn, s, _ = kz.shape
    assert s % QB == 0
    n_top = min(TOPK_MAX, s // 4)
    nq = s // QB
    return pl.pallas_call(
        functools.partial(_dsa_kernel, n_top),
        grid=(bn, nq),
        in_specs=[pl.BlockSpec((1, s, 2 * LANES), lambda b, q: (b, 0, 0)),
                  pl.BlockSpec((1, s, LANES), lambda b, q: (b, 0, 0)),
                  pl.BlockSpec((1, s // KC, HEAD_DIM, KC), lambda b, q: (b, 0, 0, 0)),
                  pl.BlockSpec((1, IDX_HEADS, 4 * IDX_DIM, QB), lambda b, q: (b, 0, 0, q)),
                  pl.BlockSpec((1, 8, QB), lambda b, q: (b, 0, q)),
                  pl.BlockSpec((1, A_DIM, QB), lambda b, q: (b, 0, q))],
        out_specs=pl.BlockSpec((1, QB, A_DIM), lambda b, q: (b, q, 0)),
        out_shape=jax.ShapeDtypeStruct((bn, s, A_DIM), BF16),
        scratch_shapes=[pltpu.VMEM((s, QB), I32), pltpu.VMEM((s, QB), I16), pltpu.VMEM((s, QB), I16),
                        pltpu.VMEM((A_HEADS, s, QB), F32), pltpu.VMEM((A_DIM, QB), F32)],
        compiler_params=_cparams("parallel", "arbitrary"),
    )(ik3, kz, vt4, iq3t, iwt, qt)


def _hgrn_kernel(q_ref, f_ref, i_ref, it_ref, g_ref, lb_ref, gn_ref, o_ref, st_scr, oi_scr):
    s = q_ref.shape[1]
    ch = HGRN_CHUNK
    sb = HGRN_SUB
    nsb = ch // sb
    rows = HGRN_GROUP * ch
    wins_per_group = rows // LANES
    zeros_half = jnp.zeros((ch, LANES), BF16)
    lane = lax.broadcasted_iota(I32, (ch, LANES), 1)
    head0 = lane < B_DK
    lr = lax.broadcasted_iota(I32, (ch, 3 * ch), 0)
    lc = lax.broadcasted_iota(I32, (ch, 3 * ch), 1) % ch
    ltri3 = jnp.where(lc <= lr, 1.0, 0.0).astype(BF16)
    ar = lax.broadcasted_iota(I32, (2 * ch, nsb * ch), 0) % ch
    ac = lax.broadcasted_iota(I32, (2 * ch, nsb * ch), 1)
    att_mask = ((ar // sb) == (ac // ch)) & ((ac % ch) <= ar)
    bd_r = lax.broadcasted_iota(I32, (LANES, LANES), 0)
    bd_c = lax.broadcasted_iota(I32, (LANES, LANES), 1)
    same_head = (bd_r < B_DK) == (bd_c < B_DK)
    ones_bd = jnp.where(same_head, 1.0, 0.0).astype(BF16)
    s_iota = lax.broadcasted_iota(I32, (ch, LANES), 0)

    lb = lb_ref[...]
    log_lb = jnp.log(lb)
    log_1mlb = jnp.log1p(-lb)
    gn = gn_ref[...]
    st_scr[...] = jnp.zeros_like(st_scr)

    def group_body(it, carry):
        t0 = pl.multiple_of(it * rows, rows)
        z = f_ref[0, pl.ds(t0, rows), :]
        q = q_ref[0, pl.ds(t0, rows), :].astype(F32)
        v = i_ref[0, pl.ds(t0, rows), :]
        softplus_tail = jnp.log1p(jnp.exp(-jnp.abs(z)))
        log_sig = -(jnp.maximum(-z, 0.0) + softplus_tail)
        x2 = log_1mlb + log_sig
        amax = jnp.maximum(log_lb, x2)
        log_f = amax + jnp.log1p(jnp.exp(-jnp.abs(log_lb - x2)))
        kk = (1.0 - lb) * jnp.exp(-(jnp.maximum(z, 0.0) + softplus_tail))
        f_hi, f_mid, f_lo = _split3(log_f)
        b_parts, ref_parts = [], []
        for c in range(HGRN_GROUP):
            sl = slice(c * ch, (c + 1) * ch)
            b_c = jnp.dot(ltri3, jnp.concatenate([f_hi[sl], f_mid[sl], f_lo[sl]], axis=0),
                          preferred_element_type=F32)
            b_parts.append(b_c)
            ref_parts.append(jnp.zeros((sb, LANES), F32))
            for i in range(1, nsb):
                ref_parts.append(jnp.broadcast_to(b_c[i * sb - 1:i * sb], (sb, LANES)))
        b = jnp.concatenate(b_parts, axis=0)
        ref = jnp.concatenate(ref_parts, axis=0)
        q_loc = q * jnp.exp(b - ref)
        q_chk = (q * jnp.exp(b)).astype(BF16)
        span = ref - b
        span_max = jnp.max(span)

        st = st_scr[...]
        o_inter = []
        for c in range(HGRN_GROUP):
            lo_r = c * ch
            b_c = b[lo_r:lo_r + ch]
            ref_c = ref[lo_r:lo_r + ch]
            kk_c = kk[lo_r:lo_r + ch]
            v_c = v[lo_r:lo_r + ch]
            b_last = b_c[ch - 1:ch]
            kcat = jnp.concatenate(
                [kk_c * jnp.exp(jnp.minimum(ref_c[i * sb:i * sb + 1] - b_c, HGRN_FAST_SPAN))
                 for i in range(nsb)], axis=0).astype(BF16)
            q_c = q_loc[lo_r:lo_r + ch]
            qm = jnp.concatenate([jnp.where(head0, q_c, 0.0), jnp.where(head0, 0.0, q_c)],
                                 axis=0).astype(BF16)
            att = lax.dot_general(qm, kcat, NT_DIMS, preferred_element_type=F32)
            att = jnp.where(att_mask, att, 0.0).astype(BF16)
            o2 = jnp.dot(att, jnp.concatenate([v_c] * nsb, axis=0), preferred_element_type=F32)
            oi_scr[lo_r:lo_r + ch, :] = jnp.where(head0, o2[:ch], o2[ch:])
            o_inter.append(lax.dot_general(q_chk[lo_r:lo_r + ch], st.astype(BF16), NT_DIMS,
                                           preferred_element_type=F32))
            kh = (kk_c * jnp.exp(b_last - b_c)).astype(BF16)
            kh_win = jnp.concatenate([kh, zeros_half] if c % 2 == 0 else [zeros_half, kh], axis=0)
            upd = jnp.dot(it_ref[0, wins_per_group * it + c // 2], kh_win, preferred_element_type=F32)
            st = st * jnp.exp(b_last) + jnp.where(same_head, upd, 0.0)
        st_scr[...] = st

        @pl.when(span_max > HGRN_FAST_SPAN)
        def _():
            for c in range(HGRN_GROUP):
                lo_r = c * ch

                @pl.when(jnp.max(span[lo_r:lo_r + ch]) > HGRN_FAST_SPAN)
                def _():
                    b_c = b[lo_r:lo_r + ch]
                    q_c = q[lo_r:lo_r + ch]
                    kk_c = kk[lo_r:lo_r + ch]
                    vf = v[lo_r:lo_r + ch].astype(F32)

                    def t_body(t, carry2):
                        onehot = jnp.where(s_iota == t, 1.0, 0.0)
                        b_t = jnp.sum(onehot * b_c, axis=0, keepdims=True)
                        q_t = jnp.sum(onehot * q_c, axis=0, keepdims=True)
                        dec = jnp.exp(jnp.where(s_iota <= t, b_t - b_c, -jnp.inf))
                        w = q_t * kk_c * dec
                        w0 = jnp.sum(jnp.where(head0, w, 0.0), axis=1, keepdims=True)
                        w1 = jnp.sum(jnp.where(head0, 0.0, w), axis=1, keepdims=True)
                        a_col = jnp.where(head0, w0, w1)
                        oi_scr[pl.ds(lo_r + t, 1), :] = jnp.sum(a_col * vf, axis=0, keepdims=True)
                        return carry2

                    lax.fori_loop(0, ch, t_body, 0)

        o = oi_scr[...] + jnp.concatenate(o_inter, axis=0)
        o2_hi, o2_lo = _split2(o * o)
        ss = (jnp.dot(o2_hi, ones_bd, preferred_element_type=F32)
              + jnp.dot(o2_lo, ones_bd, preferred_element_type=F32))
        y = (o * lax.rsqrt(ss * (1.0 / B_DV) + EPS)) * gn
        g = g_ref[0, pl.ds(t0, rows), :].astype(F32)
        o_ref[0, pl.ds(t0, rows), :] = (y * _silu(g)).astype(o_ref.dtype)
        return carry

    lax.fori_loop(0, s // rows, group_body, 0)


def _hgrn(bq, bf, bi, bit, bg, lb, gn):
    bn, s, _ = bq.shape
    assert s % (HGRN_GROUP * HGRN_CHUNK) == 0 and 2 * HGRN_CHUNK == LANES and HGRN_GROUP % 2 == 0
    npair = B_DIM // LANES
    tok = pl.BlockSpec((1, s, LANES), lambda b, p: (b, 0, p))
    return pl.pallas_call(
        _hgrn_kernel,
        grid=(bn, npair),
        in_specs=[tok, tok, tok,
                  pl.BlockSpec((1, s // LANES, LANES, LANES), lambda b, p: (b, 0, p, 0)),
                  tok,
                  pl.BlockSpec((1, LANES), lambda b, p: (0, p)),
                  pl.BlockSpec((1, LANES), lambda b, p: (0, 0))],
        out_specs=tok,
        out_shape=jax.ShapeDtypeStruct((bn, s, B_DIM), BF16),
        scratch_shapes=[pltpu.VMEM((LANES, LANES), F32),
                        pltpu.VMEM((HGRN_GROUP * HGRN_CHUNK, LANES), F32)],
        compiler_params=_cparams("parallel", "parallel"),
    )(bq, bf, bi, bit, bg, lb, gn)


def _gmlp_kernel(uv_ref, vg_ref, vb_ref, ws_ref, bias_ref, o_ref):
    uv = uv_ref[0].astype(F32)
    uv = 0.5 * uv * (1.0 + lax.erf(uv * (2.0 ** -0.5)))
    u = uv[:, :C_DIM]
    v = uv[:, C_DIM:]
    mu = jnp.mean(v, axis=-1, keepdims=True)
    var = jnp.mean(jnp.square(v - mu), axis=-1, keepdims=True)
    vn = ((v - mu) * lax.rsqrt(var + EPS)) * vg_ref[...] + vb_ref[...]
    vn_b = vn.astype(BF16)
    r_t = lax.broadcasted_iota(I32, (C_CHUNK, C_CHUNK), 0)
    r_s = lax.broadcasted_iota(I32, (C_CHUNK, C_CHUNK), 1)
    group = lax.broadcasted_iota(I32, (C_CHUNK, C_DIM), 1) // C_GROUP_DIM
    ws = [jnp.where(r_t >= r_s, ws_ref[gi], 0.0).astype(BF16) for gi in range(C_GROUPS)]
    bias = bias_ref[...]
    for c in range(uv_ref.shape[1] // C_CHUNK):
        sl = slice(c * C_CHUNK, (c + 1) * C_CHUNK)
        mixed = jnp.zeros((C_CHUNK, C_DIM), F32)
        for gi in range(C_GROUPS):
            m = jnp.dot(ws[gi], vn_b[sl], preferred_element_type=F32)
            mixed = jnp.where(group == gi, m, mixed)
        o_ref[0, sl, :] = (u[sl] * (mixed + bias)).astype(o_ref.dtype)


def _gmlp(cuv, vg, vb, ws, bs):
    bn, s, _ = cuv.shape
    tm = _token_tile(s, 4 * C_CHUNK)
    bias = jnp.repeat(bs.T, C_GROUP_DIM, axis=1)
    full = lambda a: pl.BlockSpec(a.shape, lambda b, t: (0,) * a.ndim)
    return pl.pallas_call(
        _gmlp_kernel,
        grid=(bn, s // tm),
        in_specs=[pl.BlockSpec((1, tm, 2 * C_DIM), lambda b, t: (b, t, 0)),
                  full(vg), full(vb), full(ws), full(bias)],
        out_specs=pl.BlockSpec((1, tm, C_DIM), lambda b, t: (b, t, 0)),
        out_shape=jax.ShapeDtypeStruct((bn, s, C_DIM), BF16),
        compiler_params=_cparams("parallel", "parallel"),
    )(cuv, vg, vb, ws, bias)


def _outproj_kernel(x_ref, a_ref, b_ref, c_ref, wa_ref, wb_ref, wc_ref, g1_ref, o_ref):
    mix = (jnp.dot(a_ref[0], wa_ref[...], preferred_element_type=F32)
           + jnp.dot(b_ref[0], wb_ref[...], preferred_element_type=F32)
           + jnp.dot(c_ref[0], wc_ref[...], preferred_element_type=F32))
    o_ref[0] = x_ref[0] + g1_ref[0] * mix


def _outproj(x, a, b, c, w_out_l, g1, tm):
    bn, s, d = x.shape
    wa = w_out_l[:A_DIM].astype(BF16)
    wb = w_out_l[A_DIM:A_DIM + B_DIM].astype(BF16)
    wc = w_out_l[A_DIM + B_DIM:].astype(BF16)
    tok = lambda w: pl.BlockSpec((1, tm, w), lambda bb, t: (bb, t, 0))
    full = lambda arr: pl.BlockSpec(arr.shape, lambda bb, t: (0, 0))
    return pl.pallas_call(
        _outproj_kernel,
        grid=(bn, s // tm),
        in_specs=[tok(d), tok(A_DIM), tok(B_DIM), tok(C_DIM), full(wa), full(wb), full(wc),
                  pl.BlockSpec((1, 1, d), lambda bb, t: (bb, 0, 0))],
        out_specs=tok(d),
        out_shape=jax.ShapeDtypeStruct((bn, s, d), F32),
        compiler_params=_cparams("parallel", "parallel"),
    )(x, a, b, c, wa, wb, wc, g1)


def _ffn_kernel(x_ref, sh_ref, sc_ref, gate_ref, g_ref, wg_ref, wu_ref, wd_ref, o_ref, h_scr, acc_scr):
    f = pl.program_id(2)

    @pl.when(f == 0)
    def _():
        h_scr[...] = _modulated_rmsnorm(x_ref[0], g_ref[...], sc_ref[0], sh_ref[0]).astype(BF16)
        acc_scr[...] = jnp.zeros_like(acc_scr)

    h = h_scr[...]
    a = jnp.dot(h, wg_ref[...], preferred_element_type=F32)
    u = jnp.dot(h, wu_ref[...], preferred_element_type=F32)
    act = (_silu(a) * u).astype(BF16)
    acc_scr[...] += jnp.dot(act, wd_ref[...], preferred_element_type=F32)

    @pl.when(f == pl.num_programs(2) - 1)
    def _():
        o_ref[0] = x_ref[0] + gate_ref[0] * acc_scr[...]


def _ffn(x, sh, sc, gate, g, wg, wu, wd, tm, tf):
    bn, s, d = x.shape
    fdim = wg.shape[1]
    tok = pl.BlockSpec((1, tm, d), lambda b, t, f: (b, t, 0))
    vec = pl.BlockSpec((1, 1, d), lambda b, t, f: (b, 0, 0))
    return pl.pallas_call(
        _ffn_kernel,
        grid=(bn, s // tm, fdim // tf),
        in_specs=[tok, vec, vec, vec,
                  pl.BlockSpec((1, d), lambda b, t, f: (0, 0)),
                  pl.BlockSpec((d, tf), lambda b, t, f: (0, f)),
                  pl.BlockSpec((d, tf), lambda b, t, f: (0, f)),
                  pl.BlockSpec((tf, d), lambda b, t, f: (f, 0))],
        out_specs=tok,
        out_shape=jax.ShapeDtypeStruct((bn, s, d), F32),
        scratch_shapes=[pltpu.VMEM((tm, d), BF16), pltpu.VMEM((tm, d), F32)],
        compiler_params=_cparams("parallel", "parallel", "arbitrary"),
    )(x, sh, sc, gate, g, wg, wu, wd)


MOE_RC = 256
MOE_TB = 512


def _moe_kernel(final_norm, x_ref, sh_ref, sc_ref, gate_ref, g_ref, gfin_ref, wr_ref, wg_ref, wu_ref, wd_ref, o_ref,
                h_scr, xe_scr, ye_scr, gcol_scr, gates_scr, rsel_scr, rn_scr, rcol_scr, cnt_smem):
    e = pl.program_id(2)
    f = pl.program_id(3)
    tm = x_ref.shape[1]
    rc_rows = MOE_RC
    tb = MOE_TB
    ntb = tm // tb
    neg_inf = jnp.float32(-jnp.inf)

    @pl.when((e == 0) & (f == 0))
    def _route():
        for jb in range(ntb):
            rows = slice(jb * tb, (jb + 1) * tb)
            h = _modulated_rmsnorm(x_ref[0, rows, :], g_ref[...], sc_ref[0], sh_ref[0])
            h_scr[rows, :] = h.astype(BF16)
            rcol_scr[rows, :] = jnp.dot(h, wr_ref[...], precision=lax.Precision.HIGHEST,
                                        preferred_element_type=F32)
            o_ref[0, rows, :] = jnp.zeros((tb, o_ref.shape[2]), F32)
        logits = rcol_scr[...].T[:N_EXPERTS]
        ridx = lax.broadcasted_iota(I32, (N_EXPERTS, tm), 0)
        m1 = jnp.max(logits, axis=0, keepdims=True)
        i1 = jnp.min(jnp.where(logits == m1, ridx, N_EXPERTS), axis=0, keepdims=True)
        rest = jnp.where(ridx == i1, neg_inf, logits)
        m2 = jnp.max(rest, axis=0, keepdims=True)
        i2 = jnp.min(jnp.where(rest == m2, ridx, N_EXPERTS), axis=0, keepdims=True)
        e2 = jnp.exp(m2 - m1)
        den = 1.0 + e2
        gates_scr[...] = jnp.where(ridx == i1, 1.0 / den, jnp.where(ridx == i2, e2 / den, 0.0))
        sel = jnp.where(ridx == i1, 1.0, jnp.where(ridx == i2, 1.0, 0.0))
        ur = lax.broadcasted_iota(I32, (LANES, LANES), 0)
        uc = lax.broadcasted_iota(I32, (LANES, LANES), 1)
        utri = jnp.where(ur <= uc, 1.0, 0.0).astype(BF16)
        carry = jnp.zeros((N_EXPERTS, 1), F32)
        for kb in range(tm // LANES):
            sb = sel[:, kb * LANES:(kb + 1) * LANES]
            pref = jnp.dot(sb.astype(BF16), utri, preferred_element_type=F32) + carry
            rsel_scr[:, kb * LANES:(kb + 1) * LANES] = jnp.where(sb > 0.0, pref, -1.0)
            carry = pref[:, LANES - 1:LANES]
            if (kb + 1) * LANES == tm // 2:
                carry_half = carry
        r8 = lax.broadcasted_iota(I32, (N_EXPERTS, 1), 0)
        for ee in range(N_EXPERTS):
            cnt_smem[ee] = jnp.sum(jnp.where(r8 == ee, carry, 0.0)).astype(I32)
            cnt_smem[N_EXPERTS + ee] = jnp.sum(jnp.where(r8 == ee, carry_half, 0.0)).astype(I32)
        rn_scr[...] = jnp.concatenate(
            [rsel_scr[...], jnp.zeros((LANES - N_EXPERTS, tm), F32)], axis=0).T

    n_rc = (cnt_smem[e] + (rc_rows - 1)) // rc_rows
    cnt_first = cnt_smem[N_EXPERTS + e]
    token_regions = ((0, tm), (0, tm // 2), (tm // 2, tm))

    def chunk_region(r0):
        return jnp.where(r0 + rc_rows <= cnt_first, 1, jnp.where(r0 >= cnt_first, 2, 0))

    @pl.when(f == 0)
    def _gather():
        rsel_row = rsel_scr[pl.ds(e, 1), :]
        gate_row = gates_scr[pl.ds(e, 1), :]
        lane = lax.broadcasted_iota(I32, (tm, LANES), 1)
        rcol = jnp.sum(jnp.where(lane == e, rn_scr[...], 0.0), axis=1, keepdims=True)
        rcol_scr[...] = jnp.broadcast_to(rcol, (tm, LANES))

        def compact(r0, lo, hi):
            want = (r0 + 1 + lax.broadcasted_iota(I32, (rc_rows, hi - lo), 0)).astype(F32)
            pm = rsel_row[:, lo:hi] == want
            pb = jnp.where(pm, 1.0, 0.0).astype(BF16)
            xe_scr[pl.ds(r0, rc_rows), :] = jnp.dot(pb, h_scr[lo:hi, :],
                                                    preferred_element_type=F32).astype(BF16)
            gcol = jnp.sum(jnp.where(pm, gate_row[:, lo:hi], 0.0), axis=1, keepdims=True)
            gcol_scr[pl.ds(r0, rc_rows), :] = jnp.broadcast_to(gcol, (rc_rows, LANES))

        def body(rc, carry):
            r0 = pl.multiple_of(rc * rc_rows, rc_rows)
            for region, (lo, hi) in enumerate(token_regions):
                @pl.when(chunk_region(r0) == region)
                def _():
                    compact(r0, lo, hi)
            ye_scr[pl.ds(r0, rc_rows), :] = jnp.zeros((rc_rows, ye_scr.shape[1]), F32)
            return carry

        lax.fori_loop(0, n_rc, body, 0)

    def ffn_body(rc, carry):
        r0 = pl.multiple_of(rc * rc_rows, rc_rows)
        xe = xe_scr[pl.ds(r0, rc_rows), :]
        a = jnp.dot(xe, wg_ref[0], preferred_element_type=F32)
        u = jnp.dot(xe, wu_ref[0], preferred_element_type=F32)
        act = (_silu(a) * u * gcol_scr[pl.ds(r0, rc_rows), 0:1]).astype(BF16)
        ye_scr[pl.ds(r0, rc_rows), :] += jnp.dot(act, wd_ref[0], preferred_element_type=F32)
        return carry

    lax.fori_loop(0, n_rc, ffn_body, 0)

    @pl.when(f == pl.num_programs(3) - 1)
    def _scatter():
        def body(rc, carry):
            r0 = pl.multiple_of(rc * rc_rows, rc_rows)
            ye = ye_scr[pl.ds(r0, rc_rows), :].astype(BF16)
            for region, (lo, hi) in enumerate(token_regions):
                @pl.when(chunk_region(r0) == region)
                def _():
                    want = (r0 + 1 + lax.broadcasted_iota(I32, (hi - lo, rc_rows), 1)).astype(F32)
                    pt = jnp.where(rcol_scr[lo:hi, 0:1] == want, 1.0, 0.0).astype(BF16)
                    o_ref[0, lo:hi, :] += jnp.dot(pt, ye, preferred_element_type=F32)
            return carry

        lax.fori_loop(0, n_rc, body, 0)

    @pl.when((e == pl.num_programs(2) - 1) & (f == pl.num_programs(3) - 1))
    def _residual():
        for jb in range(ntb):
            rows = slice(jb * tb, (jb + 1) * tb)
            y = x_ref[0, rows, :] + gate_ref[0] * o_ref[0, rows, :]
            if final_norm:
                var = jnp.mean(y * y, axis=-1, keepdims=True)
                y = (y * lax.rsqrt(var + EPS)) * gfin_ref[...]
            o_ref[0, rows, :] = y


def _moe(x, sh, sc, gate, g, w_router, wg, wu, wd, tm, tf, g_final=None):
    bn, s, d = x.shape
    ne, _, fdim = wg.shape
    assert ne == N_EXPERTS and tm % MOE_RC == 0 and tm % MOE_TB == 0 and (tm // 2) % LANES == 0
    wr = jnp.zeros((d, LANES), F32).at[:, :ne].set(w_router)
    final_norm = g_final is not None
    gfin = g_final if final_norm else jnp.ones((1, d), F32)
    tok_in = pl.BlockSpec((1, tm, d), lambda b, t, e, f: (b, t, 0), pipeline_mode=pl.Buffered(1))
    tok_out = pl.BlockSpec((1, tm, d), lambda b, t, e, f: (b, t, 0))
    vec = pl.BlockSpec((1, 1, d), lambda b, t, e, f: (b, 0, 0))
    return pl.pallas_call(
        functools.partial(_moe_kernel, final_norm),
        grid=(bn, s // tm, ne, fdim // tf),
        in_specs=[tok_in, vec, vec, vec,
                  pl.BlockSpec((1, d), lambda b, t, e, f: (0, 0)),
                  pl.BlockSpec((1, d), lambda b, t, e, f: (0, 0)),
                  pl.BlockSpec((d, LANES), lambda b, t, e, f: (0, 0)),
                  pl.BlockSpec((1, d, tf), lambda b, t, e, f: (e, 0, f)),
                  pl.BlockSpec((1, d, tf), lambda b, t, e, f: (e, 0, f)),
                  pl.BlockSpec((1, tf, d), lambda b, t, e, f: (e, f, 0))],
        out_specs=tok_out,
        out_shape=jax.ShapeDtypeStruct((bn, s, d), F32),
        scratch_shapes=[pltpu.VMEM((tm, d), BF16),
                        pltpu.VMEM((tm, d), BF16),
                        pltpu.VMEM((tm, d), F32),
                        pltpu.VMEM((tm, LANES), F32),
                        pltpu.VMEM((ne, tm), F32),
                        pltpu.VMEM((ne, tm), F32),
                        pltpu.VMEM((tm, LANES), F32),
                        pltpu.VMEM((tm, LANES), F32),
                        pltpu.SMEM((2 * ne,), I32)],
        compiler_params=pltpu.CompilerParams(
            dimension_semantics=("parallel", "parallel", "arbitrary", "arbitrary"),
            vmem_limit_bytes=56 * 1024 * 1024),
    )(x, sh, sc, gate, g, gfin, wr, wg, wu, wd)


def _final_norm_kernel(x_ref, g_ref, o_ref):
    x = x_ref[0]
    var = jnp.mean(x * x, axis=-1, keepdims=True)
    o_ref[0] = (x * lax.rsqrt(var + EPS)) * g_ref[...]


def _final_norm(x, g, tm):
    bn, s, d = x.shape
    tok = pl.BlockSpec((1, tm, d), lambda b, t: (b, t, 0))
    return pl.pallas_call(
        _final_norm_kernel,
        grid=(bn, s // tm),
        in_specs=[tok, pl.BlockSpec((1, d), lambda b, t: (0, 0))],
        out_specs=tok,
        out_shape=jax.ShapeDtypeStruct((bn, s, d), F32),
        compiler_params=_cparams("parallel", "parallel"),
    )(x, g)


def _token_tile(s, want):
    tm = min(want, s)
    assert s % tm == 0 and tm % QB == 0
    return tm


def kernel(x, c, positions, w_ada, b_ada, g_norm_mix, g_norm_ffn, w_in, w_out, hgrn_lb_logits, hgrn_out_norm, gmlp_vnorm_g, gmlp_vnorm_b, gmlp_w_s, gmlp_b_s, ffn_w_gate, ffn_w_up, ffn_w_down, moe_w_router, moe_w_gate, moe_w_up, moe_w_down, g_final):
    bn, s, d = x.shape
    depth = w_in.shape[0]
    assert s % QB == 0 and d == sum(IN_WIDTHS[:1]) + B_DIM + C_DIM
    tm_proj = _token_tile(s, 512)
    tm_ffn = _token_tile(s, 1024)

    p_lb = jax.nn.softmax(hgrn_lb_logits.astype(F32), axis=0)
    cum = jnp.cumsum(p_lb, axis=0)
    lower_bounds = cum - cum[0:1]

    mod = _ada_mod(c, w_ada, b_ada)
    cosn, sinn, cost, sint = _rope_tables(positions)

    for l in range(depth):
        sh1, sc1, g1, sh2, sc2, g2 = [mod[l, :, i * d:(i + 1) * d].reshape(bn, 1, d) for i in range(6)]
        w_n, w_t = _prep_in_weights(w_in[l])
        (kz, ik3, bq, bf, bi, bg, cuv, qt, iq3t, vt4, iwt, bit) = _inproj(
            x, sh1, sc1, g_norm_mix[l].reshape(1, d), w_n, w_t, cosn, sinn, cost, sint, tm_proj)
        a_out = _dsa(ik3, kz, vt4, iq3t, iwt, qt)
        b_out = _hgrn(bq, bf, bi, bit, bg, lower_bounds[l].reshape(1, B_DIM),
                      jnp.tile(hgrn_out_norm[l], LANES // B_DV).reshape(1, LANES))
        c_out = _gmlp(cuv, gmlp_vnorm_g[l].reshape(1, C_DIM), gmlp_vnorm_b[l].reshape(1, C_DIM),
                      gmlp_w_s[l], gmlp_b_s[l])
        x = _outproj(x, a_out, b_out, c_out, w_out[l], g1, tm_proj)
        gf = g_norm_ffn[l].reshape(1, d)
        if l % 2 == 0:
            i = l // 2
            x = _ffn(x, sh2, sc2, g2, gf, ffn_w_gate[i].astype(BF16), ffn_w_up[i].astype(BF16),
                     ffn_w_down[i].astype(BF16), tm_ffn, 256)
        else:
            i = l // 2
            fused_final = l == depth - 1
            x = _moe(x, sh2, sc2, g2, gf, moe_w_router[i], moe_w_gate[i].astype(BF16),
                     moe_w_up[i].astype(BF16), moe_w_down[i].astype(BF16), _token_tile(s, 2048), 512,
                     g_final.reshape(1, d) if fused_final else None)
    if depth % 2 == 0:
        return x
    return _final_norm(x, g_final.reshape(1, d), tm_proj)
```

```python
import functools

import numpy as np
import jax
import jax.numpy as jnp
from jax import lax
from jax.experimental import pallas as pl
from jax.experimental.pallas import tpu as pltpu

F32 = jnp.float32
BF16 = jnp.bfloat16
I32 = jnp.int32
I16 = jnp.int16

HEAD_DIM = 64
A_HEADS = 6
A_DIM = A_HEADS * HEAD_DIM
IDX_HEADS = 4
IDX_DIM = 64
TOPK_MAX = 256
B_HEADS = 6
B_DK = 64
B_DV = 64
B_DIM = B_HEADS * B_DV
HGRN_CHUNK = 64
C_GROUPS = 4
C_GROUP_DIM = 64
C_DIM = C_GROUPS * C_GROUP_DIM
C_CHUNK = 128
ROPE_THETA = 10000.0
N_EXPERTS = 8
EPS = 1e-6
IN_WIDTHS = (A_DIM, HEAD_DIM, HEAD_DIM, IDX_HEADS * IDX_DIM, IDX_DIM, IDX_HEADS,
             B_DIM, B_DIM, B_DIM, B_DIM, 2 * C_DIM)

LANES = 128
KC = 128
QB = 2 * KC
PV_ROWS = HEAD_DIM + 16
VMEM_LIMIT = 48 * 1024 * 1024
HGRN_FAST_SPAN = 80.0
HGRN_SUB = 16
HGRN_GROUP = 4
INT_MIN = -2 ** 31

NT_DIMS = (((1,), (1,)), ((), ()))
TN_DIMS = (((0,), (0,)), ((), ()))


def _cparams(*sem):
    return pltpu.CompilerParams(dimension_semantics=sem, vmem_limit_bytes=VMEM_LIMIT)


def _split2(x):
    hi = x.astype(BF16)
    lo = (x - hi.astype(F32)).astype(BF16)
    return hi, lo


def _split3(x):
    hi = x.astype(BF16)
    r1 = x - hi.astype(F32)
    mid = r1.astype(BF16)
    lo = (r1 - mid.astype(F32)).astype(BF16)
    return hi, mid, lo


def _silu(x):
    return x * (1.0 / (1.0 + jnp.exp(-x)))


def _modulated_rmsnorm(x, g, sc, sh):
    var = jnp.mean(x * x, axis=-1, keepdims=True)
    y = x * lax.rsqrt(var + EPS)
    return (y * g) * (1.0 + sc) + sh


def _ada_kernel(c_ref, w_ref, b_ref, o_ref):
    cond = _silu(c_ref[...])
    o_ref[0] = jnp.dot(cond, w_ref[0], precision=lax.Precision.HIGHEST,
                       preferred_element_type=F32) + b_ref[0]


def _ada_mod(c, w_ada, b_ada):
    depth, d, d6 = w_ada.shape
    bn = c.shape[0]
    tn = 1536
    return pl.pallas_call(
        _ada_kernel,
        grid=(depth, d6 // tn),
        in_specs=[pl.BlockSpec((bn, d), lambda l, n: (0, 0)),
                  pl.BlockSpec((1, d, tn), lambda l, n: (l, 0, n)),
                  pl.BlockSpec((1, 1, tn), lambda l, n: (l, 0, n))],
        out_specs=pl.BlockSpec((1, bn, tn), lambda l, n: (l, 0, n)),
        out_shape=jax.ShapeDtypeStruct((depth, bn, d6), F32),
        compiler_params=_cparams("parallel", "parallel"),
    )(c, w_ada, b_ada.reshape(depth, 1, d6))


def _rope_kernel(posn_ref, post_ref, invn_ref, signn_ref, invt_ref,
                 cosn_ref, sinn_ref, cost_ref, sint_ref):
    ang_n = posn_ref[0].astype(F32) * invn_ref[...]
    cosn_ref[0] = jnp.cos(ang_n)
    sinn_ref[0] = jnp.sin(ang_n) * signn_ref[...]
    ang_t = invt_ref[...] * post_ref[0].astype(F32)
    cost_ref[0] = jnp.cos(ang_t)
    sint_ref[0] = jnp.sin(ang_t)


def _rope_tables(positions):
    bn, s = positions.shape
    half = HEAD_DIM // 2
    inv = ROPE_THETA ** (-jnp.arange(0, HEAD_DIM, 2, dtype=F32) / HEAD_DIM)
    inv_n = jnp.tile(inv, LANES // half).reshape(1, LANES)
    sign_n = jnp.tile(jnp.concatenate([-jnp.ones((half,), F32), jnp.ones((half,), F32)]),
                      LANES // HEAD_DIM).reshape(1, LANES)
    inv_t = inv.reshape(half, 1)
    full = lambda shape: pl.BlockSpec(shape, lambda b: (0,) * len(shape))
    return pl.pallas_call(
        _rope_kernel,
        grid=(bn,),
        in_specs=[pl.BlockSpec((1, s, 1), lambda b: (b, 0, 0)),
                  pl.BlockSpec((1, 1, s), lambda b: (b, 0, 0)),
                  full((1, LANES)), full((1, LANES)), full((half, 1))],
        out_specs=[pl.BlockSpec((1, s, LANES), lambda b: (b, 0, 0)),
                   pl.BlockSpec((1, s, LANES), lambda b: (b, 0, 0)),
                   pl.BlockSpec((1, half, s), lambda b: (b, 0, 0)),
                   pl.BlockSpec((1, half, s), lambda b: (b, 0, 0))],
        out_shape=[jax.ShapeDtypeStruct((bn, s, LANES), F32),
                   jax.ShapeDtypeStruct((bn, s, LANES), F32),
                   jax.ShapeDtypeStruct((bn, half, s), F32),
                   jax.ShapeDtypeStruct((bn, half, s), F32)],
        compiler_params=_cparams("parallel"),
    )(positions.reshape(bn, s, 1), positions.reshape(bn, 1, s), inv_n, sign_n, inv_t)


N_KZ = 0
N_IK = 2 * LANES
N_BQ = 4 * LANES
N_BF = N_BQ + B_DIM
N_BI = N_BF + B_DIM
N_BG = N_BI + B_DIM
N_CUV = N_BG + B_DIM
N_COLS = N_CUV + 2 * C_DIM
T_Q = 0
T_IQ = A_DIM
T_V = T_IQ + IDX_HEADS * IDX_DIM
T_IW = T_V + HEAD_DIM
T_BI = T_IW + 8
T_ROWS = T_BI + B_DIM


def _prep_in_weights(w_in_l):
    offs = np.concatenate([[0], np.cumsum(IN_WIDTHS)])
    sl = lambda i: w_in_l[:, int(offs[i]):int(offs[i + 1])]
    aq, ak, av, iq, ik, iw, bq, bf, bi, bg, cuv = [sl(i) for i in range(11)]
    half = HEAD_DIM // 2
    rot = lambda w: jnp.concatenate([w[:, half:], w[:, :half]], axis=1)
    z = jnp.zeros_like(ak)
    w_n = jnp.concatenate([ak, z, rot(ak), z, ik, ik, rot(ik), rot(ik), bq, bf, bi, bg, cuv], axis=1)
    w_t = jnp.concatenate([aq, iq, av, iw, jnp.zeros((w_in_l.shape[0], 4), w_in_l.dtype), bi], axis=1).T
    return w_n.astype(BF16), w_t.astype(BF16)


def _inproj_kernel(x_ref, sh_ref, sc_ref, g_ref, wn_ref, wt_ref, cosn_ref, sinn_ref, cost_ref, sint_ref,
                   vg_ref, vb_ref, ws_ref, cbias_ref,
                   kz_ref, ik3_ref, bq_ref, bf_ref, bi_ref, bg_ref, c_ref,
                   qt_ref, iq3t_ref, vt_ref, iwt_ref, bit_ref):
    tm = x_ref.shape[1]
    half = HEAD_DIM // 2
    h = _modulated_rmsnorm(x_ref[0], g_ref[...], sc_ref[0], sh_ref[0]).astype(BF16)

    cosn = cosn_ref[0]
    sinn = sinn_ref[0]
    zk = jnp.dot(h, wn_ref[:, N_KZ:N_KZ + 2 * LANES], preferred_element_type=F32)
    kz_ref[0] = (zk[:, :LANES] * cosn + zk[:, LANES:] * sinn).astype(BF16)
    zi = jnp.dot(h, wn_ref[:, N_IK:N_IK + 2 * LANES], preferred_element_type=F32)
    ik2 = zi[:, :LANES] * cosn + zi[:, LANES:] * sinn
    hi, lo = _split2(ik2)
    lane = lax.broadcasted_iota(I32, (tm, LANES), 1)
    ik3_ref[0, :, :LANES] = jnp.where(lane < IDX_DIM, hi, lo)
    ik3_ref[0, :, LANES:] = hi
    bq_ref[0] = jnp.dot(h, wn_ref[:, N_BQ:N_BF], preferred_element_type=F32).astype(BF16)
    bf_ref[0] = jnp.dot(h, wn_ref[:, N_BF:N_BI], preferred_element_type=F32)
    bi_ref[0] = jnp.dot(h, wn_ref[:, N_BI:N_BG], preferred_element_type=F32).astype(BF16)
    bg_ref[0] = jnp.dot(h, wn_ref[:, N_BG:N_CUV], preferred_element_type=F32).astype(BF16)
    cuv = jnp.dot(h, wn_ref[:, N_CUV:N_COLS], preferred_element_type=F32)
    c_ref[0] = _gmlp_mix(cuv, vg_ref[...], vb_ref[...], ws_ref, cbias_ref[...]).astype(BF16)

    zt = lax.dot_general(wt_ref[...], h, NT_DIMS, preferred_element_type=F32)
    cost = cost_ref[0]
    sint = sint_ref[0]
    qscale = HEAD_DIM ** -0.5
    for hh in range(A_HEADS):
        r0 = T_Q + hh * HEAD_DIM
        x1 = zt[r0:r0 + half]
        x2 = zt[r0 + half:r0 + HEAD_DIM]
        qt_ref[0, hh * HEAD_DIM:hh * HEAD_DIM + half] = ((x1 * cost - x2 * sint) * qscale).astype(BF16)
        qt_ref[0, hh * HEAD_DIM + half:(hh + 1) * HEAD_DIM] = ((x2 * cost + x1 * sint) * qscale).astype(BF16)
    iscale = IDX_DIM ** -0.5
    zero = jnp.zeros((IDX_DIM, tm), BF16)
    for hh in range(IDX_HEADS):
        r0 = T_IQ + hh * IDX_DIM
        x1 = zt[r0:r0 + half]
        x2 = zt[r0 + half:r0 + IDX_DIM]
        y = jnp.concatenate([(x1 * cost - x2 * sint) * iscale, (x2 * cost + x1 * sint) * iscale], axis=0)
        hi, lo = _split2(y)
        iq3t_ref[0, hh, 0 * IDX_DIM:1 * IDX_DIM] = hi
        iq3t_ref[0, hh, 1 * IDX_DIM:2 * IDX_DIM] = hi
        iq3t_ref[0, hh, 2 * IDX_DIM:3 * IDX_DIM] = lo
        iq3t_ref[0, hh, 3 * IDX_DIM:4 * IDX_DIM] = zero
    vt = zt[T_V:T_V + HEAD_DIM].astype(BF16)
    for i in range(tm // KC):
        vt_ref[0, i] = vt[:, i * KC:(i + 1) * KC]
    iwt_ref[0] = zt[T_IW:T_IW + 8] * (IDX_HEADS ** -0.5)
    bit = zt[T_BI:T_BI + B_DIM].astype(BF16)
    for i in range(tm // LANES):
        bit_ref[0, i] = bit[:, i * LANES:(i + 1) * LANES]


def _inproj(x, sh, sc, g, w_n, w_t, cosn, sinn, cost, sint, vg, vb, ws, bs, tm):
    bn, s, d = x.shape
    assert tm % C_CHUNK == 0
    nt = s // tm
    half = HEAD_DIM // 2
    cbias = jnp.repeat(bs.T, C_GROUP_DIM, axis=1)
    tok = lambda w: pl.BlockSpec((1, tm, w), lambda b, t: (b, t, 0))
    vec = pl.BlockSpec((1, 1, d), lambda b, t: (b, 0, 0))
    full2 = lambda a: pl.BlockSpec(a.shape, lambda b, t: (0, 0))
    out_shapes = [
        jax.ShapeDtypeStruct((bn, s, LANES), BF16),
        jax.ShapeDtypeStruct((bn, s, 2 * LANES), BF16),
        jax.ShapeDtypeStruct((bn, s, B_DIM), BF16),
        jax.ShapeDtypeStruct((bn, s, B_DIM), F32),
        jax.ShapeDtypeStruct((bn, s, B_DIM), BF16),
        jax.ShapeDtypeStruct((bn, s, B_DIM), BF16),
        jax.ShapeDtypeStruct((bn, s, C_DIM), BF16),
        jax.ShapeDtypeStruct((bn, A_DIM, s), BF16),
        jax.ShapeDtypeStruct((bn, IDX_HEADS, 4 * IDX_DIM, s), BF16),
        jax.ShapeDtypeStruct((bn, s // KC, HEAD_DIM, KC), BF16),
        jax.ShapeDtypeStruct((bn, 8, s), F32),
        jax.ShapeDtypeStruct((bn, s // LANES, B_DIM, LANES), BF16),
    ]
    out_specs = [
        tok(LANES), tok(2 * LANES), tok(B_DIM), tok(B_DIM), tok(B_DIM), tok(B_DIM), tok(C_DIM),
        pl.BlockSpec((1, A_DIM, tm), lambda b, t: (b, 0, t)),
        pl.BlockSpec((1, IDX_HEADS, 4 * IDX_DIM, tm), lambda b, t: (b, 0, 0, t)),
        pl.BlockSpec((1, tm // KC, HEAD_DIM, KC), lambda b, t: (b, t, 0, 0)),
        pl.BlockSpec((1, 8, tm), lambda b, t: (b, 0, t)),
        pl.BlockSpec((1, tm // LANES, B_DIM, LANES), lambda b, t: (b, t, 0, 0)),
    ]
    return pl.pallas_call(
        _inproj_kernel,
        grid=(bn, nt),
        in_specs=[tok(d), vec, vec, full2(g), full2(w_n), full2(w_t),
                  tok(LANES), tok(LANES),
                  pl.BlockSpec((1, half, tm), lambda b, t: (b, 0, t)),
                  pl.BlockSpec((1, half, tm), lambda b, t: (b, 0, t)),
                  full2(vg), full2(vb), pl.BlockSpec(ws.shape, lambda b, t: (0, 0, 0)), full2(cbias)],
        out_specs=out_specs,
        out_shape=out_shapes,
        compiler_params=_cparams("parallel", "parallel"),
    )(x, sh, sc, g, w_n, w_t, cosn, sinn, cost, sint, vg, vb, ws, cbias)


def _dsa_kernel(n_top, ik3_ref, kz_ref, vt_ref, iq3t_ref, iwt_ref, qt_ref, out_ref,
                key_scr, hi_scr, lo_scr, lg_scr, acc_scr):
    j = pl.program_id(1)
    npair = j + 1
    pair = 2 * KC
    row = lax.broadcasted_iota(I32, (KC, QB), 0)
    col = lax.broadcasted_iota(I32, (KC, QB), 1)
    int_min = jnp.int32(INT_MIN)
    i16_min = jnp.int16(-2 ** 15)

    def score_pair(p, diagonal):
        for u in range(2):
            ks = pl.multiple_of(p * pair + u * KC, KC)
            ikc = ik3_ref[0, pl.ds(ks, KC), :]
            sc = jnp.zeros((KC, QB), F32)
            for hh in range(IDX_HEADS):
                rel = jnp.dot(ikc, iq3t_ref[0, hh], preferred_element_type=F32)
                sc = sc + jnp.maximum(rel, 0.0) * iwt_ref[0, hh:hh + 1, :]
            sc = jnp.where(sc == 0.0, 0.0, sc)
            bits = pltpu.bitcast(sc, I32)
            key = bits ^ ((bits >> 31) & jnp.int32(0x7FFFFFFF))
            if diagonal:
                key = jnp.where((u * KC + row) <= col, key, int_min)
            key_scr[pl.ds(ks, KC), :] = key
            hi_scr[pl.ds(ks, KC), :] = (key >> 16).astype(I16)
            lo_scr[pl.ds(ks, KC), :] = ((key & jnp.int32(0xFFFF)) - 32768).astype(I16)

    def score_body(p, carry):
        score_pair(p, False)
        return carry

    lax.fori_loop(0, j, score_body, 0)
    score_pair(j, True)

    def count16(ref, pred_fn):
        def body(p, acc):
            ks = pl.multiple_of(p * pair, pair)
            m = jnp.where(pred_fn(ref[pl.ds(ks, pair), :]), jnp.int16(1), jnp.int16(0))
            parts = [m[16 * i:16 * (i + 1)] for i in range(pair // 16)]
            while len(parts) > 1:
                parts = [parts[i] + parts[i + 1] for i in range(0, len(parts), 2)]
            return acc + parts[0]
        acc = lax.fori_loop(0, npair, body, jnp.zeros((16, QB), I16))
        return jnp.sum(acc.astype(I32), axis=0, keepdims=True)

    def bisect16(ref, k_needed):
        def bit_body(i, t_u):
            cand = t_u | jnp.left_shift(jnp.int32(1), 15 - i)
            cand16 = (cand - 32768).astype(I16)
            cnt = count16(ref, lambda x: x >= cand16)
            return jnp.where(cnt >= k_needed, cand, t_u)
        return lax.fori_loop(0, 16, bit_body, jnp.zeros((1, QB), I32))

    t_hi = bisect16(hi_scr, n_top)
    t_hi16 = (t_hi - 32768).astype(I16)
    n_hi_gt = count16(hi_scr, lambda x: x > t_hi16)

    def bucket_body(p, carry):
        ks = pl.multiple_of(p * pair, pair)
        in_bucket = hi_scr[pl.ds(ks, pair), :] == t_hi16
        lo_scr[pl.ds(ks, pair), :] = jnp.where(in_bucket, lo_scr[pl.ds(ks, pair), :], i16_min)
        return carry

    lax.fori_loop(0, npair, bucket_body, 0)
    t_lo = bisect16(lo_scr, n_top - n_hi_gt)
    t_lo16 = (t_lo - 32768).astype(I16)
    n_gt = n_hi_gt + count16(lo_scr, lambda x: x > t_lo16)
    thr = jnp.left_shift(t_hi - 32768, 16) | t_lo
    n_tie = (n_top - n_gt).astype(F32)
    thr_valid = jnp.where(thr > int_min, 1.0, 0.0)

    tr = lax.broadcasted_iota(I32, (KC, KC), 0)
    tc = lax.broadcasted_iota(I32, (KC, KC), 1)
    ltri = jnp.where(tr >= tc, 1.0, 0.0).astype(BF16)
    zpad = jnp.zeros((HEAD_DIM, QB), BF16)
    qpad = [jnp.concatenate([qt_ref[0, hh * HEAD_DIM:(hh + 1) * HEAD_DIM, :], zpad], axis=0)
            for hh in range(A_HEADS)]
    neg_inf = jnp.float32(-jnp.inf)

    def pass_a(p, carry):
        tie_cnt, ms = carry
        ms = list(ms)
        for u in range(2):
            ks = pl.multiple_of(p * pair + u * KC, KC)
            kc = key_scr[pl.ds(ks, KC), :]
            eqf = jnp.where(kc == thr, thr_valid, 0.0)
            pref = jnp.dot(ltri, eqf.astype(BF16), preferred_element_type=F32)
            keep_tie = eqf * jnp.where((tie_cnt + pref) <= n_tie, 1.0, 0.0)
            sel = jnp.where(kc > thr, 1.0, keep_tie) > 0.5
            tie_cnt = tie_cnt + pref[KC - 1:KC, :]
            kzc = kz_ref[0, pl.ds(ks, KC), :]
            for hh in range(A_HEADS):
                lt = jnp.dot(kzc, qpad[hh], preferred_element_type=F32)
                lt = jnp.where(sel, lt, neg_inf)
                lg_scr[hh, pl.ds(ks, KC), :] = lt
                ms[hh] = jnp.maximum(ms[hh], jnp.max(lt, axis=0, keepdims=True))
        return tie_cnt, tuple(ms)

    init_m = tuple(jnp.full((1, QB), neg_inf, F32) for _ in range(A_HEADS))
    _, ms = lax.fori_loop(0, npair, pass_a, (jnp.zeros((1, QB), F32), init_m))

    acc_scr[...] = jnp.zeros_like(acc_scr)
    ones_rows = jnp.ones((PV_ROWS - HEAD_DIM, pair), BF16)

    def pass_b(p, carry):
        ks = pl.multiple_of(p * pair, pair)
        vt2 = jnp.concatenate([vt_ref[0, 2 * p], vt_ref[0, 2 * p + 1]], axis=1)
        vt2 = jnp.concatenate([vt2, ones_rows], axis=0)
        for hh in range(A_HEADS):
            pr = jnp.exp(lg_scr[hh, pl.ds(ks, pair), :] - ms[hh])
            acc_scr[hh * PV_ROWS:(hh + 1) * PV_ROWS, :] += jnp.dot(
                vt2, pr.astype(BF16), preferred_element_type=F32)
        return carry

    lax.fori_loop(0, npair, pass_b, 0)
    o_t = jnp.concatenate(
        [acc_scr[hh * PV_ROWS:hh * PV_ROWS + HEAD_DIM, :]
         / acc_scr[hh * PV_ROWS + HEAD_DIM:hh * PV_ROWS + HEAD_DIM + 1, :] for hh in range(A_HEADS)], axis=0)
    out_ref[0] = o_t.T.astype(BF16)


def _dsa(ik3, kz, vt4, iq3t, iwt, qt):
    bn, s, _ = kz.shape
    assert s % QB == 0
    n_top = min(TOPK_MAX, s // 4)
    nq = s // QB
    return pl.pallas_call(
        functools.partial(_dsa_kernel, n_top),
        grid=(bn, nq),
        in_specs=[pl.BlockSpec((1, s, 2 * LANES), lambda b, q: (b, 0, 0)),
                  pl.BlockSpec((1, s, LANES), lambda b, q: (b, 0, 0)),
                  pl.BlockSpec((1, s // KC, HEAD_DIM, KC), lambda b, q: (b, 0, 0, 0)),
                  pl.BlockSpec((1, IDX_HEADS, 4 * IDX_DIM, QB), lambda b, q: (b, 0, 0, q)),
                  pl.BlockSpec((1, 8, QB), lambda b, q: (b, 0, q)),
                  pl.BlockSpec((1, A_DIM, QB), lambda b, q: (b, 0, q))],
        out_specs=pl.BlockSpec((1, QB, A_DIM), lambda b, q: (b, q, 0)),
        out_shape=jax.ShapeDtypeStruct((bn, s, A_DIM), BF16),
        scratch_shapes=[pltpu.VMEM((s, QB), I32), pltpu.VMEM((s, QB), I16), pltpu.VMEM((s, QB), I16),
                        pltpu.VMEM((A_HEADS, s, QB), F32), pltpu.VMEM((A_HEADS * PV_ROWS, QB), F32)],
        compiler_params=_cparams("parallel", "arbitrary"),
    )(ik3, kz, vt4, iq3t, iwt, qt)


def _hgrn_kernel(q_ref, f_ref, i_ref, it_ref, g_ref, lb_ref, gn_ref, o_ref, st_scr, oi_scr):
    s = q_ref.shape[1]
    ch = HGRN_CHUNK
    sb = HGRN_SUB
    nsb = ch // sb
    rows = HGRN_GROUP * ch
    wins_per_group = rows // LANES
    zeros_half = jnp.zeros((ch, LANES), BF16)
    lane = lax.broadcasted_iota(I32, (ch, LANES), 1)
    head0 = lane < B_DK
    lr = lax.broadcasted_iota(I32, (ch, 3 * ch), 0)
    lc = lax.broadcasted_iota(I32, (ch, 3 * ch), 1) % ch
    ltri3 = jnp.where(lc <= lr, 1.0, 0.0).astype(BF16)
    ar = lax.broadcasted_iota(I32, (2 * ch, nsb * ch), 0) % ch
    ac = lax.broadcasted_iota(I32, (2 * ch, nsb * ch), 1)
    att_mask = ((ar // sb) == (ac // ch)) & ((ac % ch) <= ar)
    bd_r = lax.broadcasted_iota(I32, (LANES, LANES), 0)
    bd_c = lax.broadcasted_iota(I32, (LANES, LANES), 1)
    same_head = (bd_r < B_DK) == (bd_c < B_DK)
    ones_bd = jnp.where(same_head, 1.0, 0.0).astype(BF16)
    s_iota = lax.broadcasted_iota(I32, (ch, LANES), 0)

    lb = lb_ref[...]
    log_lb = jnp.log(lb)
    log_1mlb = jnp.log1p(-lb)
    gn = gn_ref[...]
    st_scr[...] = jnp.zeros_like(st_scr)

    def group_body(it, carry):
        t0 = pl.multiple_of(it * rows, rows)
        z = f_ref[0, pl.ds(t0, rows), :]
        q = q_ref[0, pl.ds(t0, rows), :].astype(F32)
        v = i_ref[0, pl.ds(t0, rows), :]
        softplus_tail = jnp.log1p(jnp.exp(-jnp.abs(z)))
        log_sig = -(jnp.maximum(-z, 0.0) + softplus_tail)
        x2 = log_1mlb + log_sig
        amax = jnp.maximum(log_lb, x2)
        log_f = amax + jnp.log1p(jnp.exp(-jnp.abs(log_lb - x2)))
        kk = (1.0 - lb) * jnp.exp(-(jnp.maximum(z, 0.0) + softplus_tail))
        f_hi, f_mid, f_lo = _split3(log_f)
        b_parts, ref_parts = [], []
        for c in range(HGRN_GROUP):
            sl = slice(c * ch, (c + 1) * ch)
            b_c = jnp.dot(ltri3, jnp.concatenate([f_hi[sl], f_mid[sl], f_lo[sl]], axis=0),
                          preferred_element_type=F32)
            b_parts.append(b_c)
            ref_parts.append(jnp.zeros((sb, LANES), F32))
            for i in range(1, nsb):
                ref_parts.append(jnp.broadcast_to(b_c[i * sb - 1:i * sb], (sb, LANES)))
        b = jnp.concatenate(b_parts, axis=0)
        ref = jnp.concatenate(ref_parts, axis=0)
        q_loc = q * jnp.exp(b - ref)
        q_chk = (q * jnp.exp(b)).astype(BF16)
        span = ref - b
        span_max = jnp.max(span)

        st = st_scr[...]
        o_inter = []
        for c in range(HGRN_GROUP):
            lo_r = c * ch
            b_c = b[lo_r:lo_r + ch]
            ref_c = ref[lo_r:lo_r + ch]
            kk_c = kk[lo_r:lo_r + ch]
            v_c = v[lo_r:lo_r + ch]
            b_last = b_c[ch - 1:ch]
            kcat = jnp.concatenate(
                [kk_c * jnp.exp(jnp.minimum(ref_c[i * sb:i * sb + 1] - b_c, HGRN_FAST_SPAN))
                 for i in range(nsb)], axis=0).astype(BF16)
            q_c = q_loc[lo_r:lo_r + ch]
            qm = jnp.concatenate([jnp.where(head0, q_c, 0.0), jnp.where(head0, 0.0, q_c)],
                                 axis=0).astype(BF16)
            att = lax.dot_general(qm, kcat, NT_DIMS, preferred_element_type=F32)
            att = jnp.where(att_mask, att, 0.0).astype(BF16)
            o2 = jnp.dot(att, jnp.concatenate([v_c] * nsb, axis=0), preferred_element_type=F32)
            oi_scr[lo_r:lo_r + ch, :] = jnp.where(head0, o2[:ch], o2[ch:])
            o_inter.append(lax.dot_general(q_chk[lo_r:lo_r + ch], st.astype(BF16), NT_DIMS,
                                           preferred_element_type=F32))
            kh = (kk_c * jnp.exp(b_last - b_c)).astype(BF16)
            kh_win = jnp.concatenate([kh, zeros_half] if c % 2 == 0 else [zeros_half, kh], axis=0)
            upd = jnp.dot(it_ref[0, wins_per_group * it + c // 2], kh_win, preferred_element_type=F32)
            st = st * jnp.exp(b_last) + jnp.where(same_head, upd, 0.0)
        st_scr[...] = st

        @pl.when(span_max > HGRN_FAST_SPAN)
        def _():
            for c in range(HGRN_GROUP):
                lo_r = c * ch

                @pl.when(jnp.max(span[lo_r:lo_r + ch]) > HGRN_FAST_SPAN)
                def _():
                    b_c = b[lo_r:lo_r + ch]
                    q_c = q[lo_r:lo_r + ch]
                    kk_c = kk[lo_r:lo_r + ch]
                    vf = v[lo_r:lo_r + ch].astype(F32)

                    def t_body(t, carry2):
                        onehot = jnp.where(s_iota == t, 1.0, 0.0)
                        b_t = jnp.sum(onehot * b_c, axis=0, keepdims=True)
                        q_t = jnp.sum(onehot * q_c, axis=0, keepdims=True)
                        dec = jnp.exp(jnp.where(s_iota <= t, b_t - b_c, -jnp.inf))
                        w = q_t * kk_c * dec
                        w0 = jnp.sum(jnp.where(head0, w, 0.0), axis=1, keepdims=True)
                        w1 = jnp.sum(jnp.where(head0, 0.0, w), axis=1, keepdims=True)
                        a_col = jnp.where(head0, w0, w1)
                        oi_scr[pl.ds(lo_r + t, 1), :] = jnp.sum(a_col * vf, axis=0, keepdims=True)
                        return carry2

                    lax.fori_loop(0, ch, t_body, 0)

        o = oi_scr[...] + jnp.concatenate(o_inter, axis=0)
        o2_hi, o2_lo = _split2(o * o)
        ss = (jnp.dot(o2_hi, ones_bd, preferred_element_type=F32)
              + jnp.dot(o2_lo, ones_bd, preferred_element_type=F32))
        y = (o * lax.rsqrt(ss * (1.0 / B_DV) + EPS)) * gn
        g = g_ref[0, pl.ds(t0, rows), :].astype(F32)
        o_ref[0, pl.ds(t0, rows), :] = (y * _silu(g)).astype(o_ref.dtype)
        return carry

    lax.fori_loop(0, s // rows, group_body, 0)


def _hgrn(bq, bf, bi, bit, bg, lb, gn):
    bn, s, _ = bq.shape
    assert s % (HGRN_GROUP * HGRN_CHUNK) == 0 and 2 * HGRN_CHUNK == LANES and HGRN_GROUP % 2 == 0
    npair = B_DIM // LANES
    tok = pl.BlockSpec((1, s, LANES), lambda b, p: (b, 0, p))
    return pl.pallas_call(
        _hgrn_kernel,
        grid=(bn, npair),
        in_specs=[tok, tok, tok,
                  pl.BlockSpec((1, s // LANES, LANES, LANES), lambda b, p: (b, 0, p, 0)),
                  tok,
                  pl.BlockSpec((1, LANES), lambda b, p: (0, p)),
                  pl.BlockSpec((1, LANES), lambda b, p: (0, 0))],
        out_specs=tok,
        out_shape=jax.ShapeDtypeStruct((bn, s, B_DIM), BF16),
        scratch_shapes=[pltpu.VMEM((LANES, LANES), F32),
                        pltpu.VMEM((HGRN_GROUP * HGRN_CHUNK, LANES), F32)],
        compiler_params=_cparams("parallel", "parallel"),
    )(bq, bf, bi, bit, bg, lb, gn)


def _gmlp_mix(uv, vg, vb, ws_ref, bias):
    uv = 0.5 * uv * (1.0 + lax.erf(uv * (2.0 ** -0.5)))
    u = uv[:, :C_DIM]
    v = uv[:, C_DIM:]
    mu = jnp.mean(v, axis=-1, keepdims=True)
    var = jnp.mean(jnp.square(v - mu), axis=-1, keepdims=True)
    vn_b = (((v - mu) * lax.rsqrt(var + EPS)) * vg + vb).astype(BF16)
    r_t = lax.broadcasted_iota(I32, (C_CHUNK, C_CHUNK), 0)
    r_s = lax.broadcasted_iota(I32, (C_CHUNK, C_CHUNK), 1)
    group = lax.broadcasted_iota(I32, (C_CHUNK, C_DIM), 1) // C_GROUP_DIM
    ws = [jnp.where(r_t >= r_s, ws_ref[gi], 0.0).astype(BF16) for gi in range(C_GROUPS)]
    outs = []
    for c in range(uv.shape[0] // C_CHUNK):
        sl = slice(c * C_CHUNK, (c + 1) * C_CHUNK)
        mixed = jnp.zeros((C_CHUNK, C_DIM), F32)
        for gi in range(C_GROUPS):
            m = jnp.dot(ws[gi], vn_b[sl], preferred_element_type=F32)
            mixed = jnp.where(group == gi, m, mixed)
        outs.append(u[sl] * (mixed + bias))
    return jnp.concatenate(outs, axis=0)


def _outproj_kernel(x_ref, a_ref, b_ref, c_ref, wa_ref, wb_ref, wc_ref, g1_ref, o_ref):
    mix = (jnp.dot(a_ref[0], wa_ref[...], preferred_element_type=F32)
           + jnp.dot(b_ref[0], wb_ref[...], preferred_element_type=F32)
           + jnp.dot(c_ref[0], wc_ref[...], preferred_element_type=F32))
    o_ref[0] = x_ref[0] + g1_ref[0] * mix


def _outproj(x, a, b, c, w_out_l, g1, tm):
    bn, s, d = x.shape
    wa = w_out_l[:A_DIM].astype(BF16)
    wb = w_out_l[A_DIM:A_DIM + B_DIM].astype(BF16)
    wc = w_out_l[A_DIM + B_DIM:].astype(BF16)
    tok = lambda w: pl.BlockSpec((1, tm, w), lambda bb, t: (bb, t, 0))
    full = lambda arr: pl.BlockSpec(arr.shape, lambda bb, t: (0, 0))
    return pl.pallas_call(
        _outproj_kernel,
        grid=(bn, s // tm),
        in_specs=[tok(d), tok(A_DIM), tok(B_DIM), tok(C_DIM), full(wa), full(wb), full(wc),
                  pl.BlockSpec((1, 1, d), lambda bb, t: (bb, 0, 0))],
        out_specs=tok(d),
        out_shape=jax.ShapeDtypeStruct((bn, s, d), F32),
        compiler_params=_cparams("parallel", "parallel"),
    )(x, a, b, c, wa, wb, wc, g1)


def _ffn_kernel(x_ref, a_ref, b_ref, c_ref, wa_ref, wb_ref, wc_ref, g1_ref, sh_ref, sc_ref, gate_ref, g_ref,
                wg_ref, wu_ref, wd_ref, o_ref, xn_scr, h_scr, acc_scr):
    f = pl.program_id(2)

    @pl.when(f == 0)
    def _():
        mix = (jnp.dot(a_ref[0], wa_ref[...], preferred_element_type=F32)
               + jnp.dot(b_ref[0], wb_ref[...], preferred_element_type=F32)
               + jnp.dot(c_ref[0], wc_ref[...], preferred_element_type=F32))
        xn = x_ref[0] + g1_ref[0] * mix
        xn_scr[...] = xn
        h_scr[...] = _modulated_rmsnorm(xn, g_ref[...], sc_ref[0], sh_ref[0]).astype(BF16)
        acc_scr[...] = jnp.zeros_like(acc_scr)

    h = h_scr[...]
    a = jnp.dot(h, wg_ref[...], preferred_element_type=F32)
    u = jnp.dot(h, wu_ref[...], preferred_element_type=F32)
    act = (_silu(a) * u).astype(BF16)
    acc_scr[...] += jnp.dot(act, wd_ref[...], preferred_element_type=F32)

    @pl.when(f == pl.num_programs(2) - 1)
    def _():
        o_ref[0] = xn_scr[...] + gate_ref[0] * acc_scr[...]


def _ffn(x, a, b, c, w_out_l, g1, sh, sc, gate, g, wg, wu, wd, tm, tf):
    bn, s, d = x.shape
    fdim = wg.shape[1]
    wa = w_out_l[:A_DIM].astype(BF16)
    wb = w_out_l[A_DIM:A_DIM + B_DIM].astype(BF16)
    wc = w_out_l[A_DIM + B_DIM:].astype(BF16)
    tok = pl.BlockSpec((1, tm, d), lambda b, t, f: (b, t, 0))
    tokw = lambda w: pl.BlockSpec((1, tm, w), lambda b, t, f: (b, t, 0))
    full = lambda arr: pl.BlockSpec(arr.shape, lambda b, t, f: (0, 0))
    vec = pl.BlockSpec((1, 1, d), lambda b, t, f: (b, 0, 0))
    return pl.pallas_call(
        _ffn_kernel,
        grid=(bn, s // tm, fdim // tf),
        in_specs=[tok, tokw(A_DIM), tokw(B_DIM), tokw(C_DIM), full(wa), full(wb), full(wc), vec,
                  vec, vec, vec,
                  pl.BlockSpec((1, d), lambda b, t, f: (0, 0)),
                  pl.BlockSpec((d, tf), lambda b, t, f: (0, f)),
                  pl.BlockSpec((d, tf), lambda b, t, f: (0, f)),
                  pl.BlockSpec((tf, d), lambda b, t, f: (f, 0))],
        out_specs=tok,
        out_shape=jax.ShapeDtypeStruct((bn, s, d), F32),
        scratch_shapes=[pltpu.VMEM((tm, d), F32), pltpu.VMEM((tm, d), BF16), pltpu.VMEM((tm, d), F32)],
        compiler_params=_cparams("parallel", "parallel", "arbitrary"),
    )(x, a, b, c, wa, wb, wc, g1, sh, sc, gate, g, wg, wu, wd)


MOE_RC = 256
MOE_TB = 512


def _moe_kernel(final_norm, x_ref, sh_ref, sc_ref, gate_ref, g_ref, gfin_ref, wr_ref, wg_ref, wu_ref, wd_ref, o_ref,
                h_scr, xe_scr, ye_scr, gcol_scr, gates_scr, rsel_scr, rn_scr, rcol_scr, cnt_smem):
    e = pl.program_id(2)
    f = pl.program_id(3)
    tm = x_ref.shape[1]
    rc_rows = MOE_RC
    tb = MOE_TB
    ntb = tm // tb
    neg_inf = jnp.float32(-jnp.inf)

    @pl.when((e == 0) & (f == 0))
    def _route():
        for jb in range(ntb):
            rows = slice(jb * tb, (jb + 1) * tb)
            h = _modulated_rmsnorm(x_ref[0, rows, :], g_ref[...], sc_ref[0], sh_ref[0])
            h_scr[rows, :] = h.astype(BF16)
            rcol_scr[rows, :] = jnp.dot(h, wr_ref[...], precision=lax.Precision.HIGHEST,
                                        preferred_element_type=F32)
            o_ref[0, rows, :] = jnp.zeros((tb, o_ref.shape[2]), F32)
        logits = rcol_scr[...].T[:N_EXPERTS]
        ridx = lax.broadcasted_iota(I32, (N_EXPERTS, tm), 0)
        m1 = jnp.max(logits, axis=0, keepdims=True)
        i1 = jnp.min(jnp.where(logits == m1, ridx, N_EXPERTS), axis=0, keepdims=True)
        rest = jnp.where(ridx == i1, neg_inf, logits)
        m2 = jnp.max(rest, axis=0, keepdims=True)
        i2 = jnp.min(jnp.where(rest == m2, ridx, N_EXPERTS), axis=0, keepdims=True)
        e2 = jnp.exp(m2 - m1)
        den = 1.0 + e2
        gates_scr[...] = jnp.where(ridx == i1, 1.0 / den, jnp.where(ridx == i2, e2 / den, 0.0))
        sel = jnp.where(ridx == i1, 1.0, jnp.where(ridx == i2, 1.0, 0.0))
        ur = lax.broadcasted_iota(I32, (LANES, LANES), 0)
        uc = lax.broadcasted_iota(I32, (LANES, LANES), 1)
        utri = jnp.where(ur <= uc, 1.0, 0.0).astype(BF16)
        carry = jnp.zeros((N_EXPERTS, 1), F32)
        for kb in range(tm // LANES):
            sb = sel[:, kb * LANES:(kb + 1) * LANES]
            pref = jnp.dot(sb.astype(BF16), utri, preferred_element_type=F32) + carry
            rsel_scr[:, kb * LANES:(kb + 1) * LANES] = jnp.where(sb > 0.0, pref, -1.0)
            carry = pref[:, LANES - 1:LANES]
            if (kb + 1) * LANES == tm // 2:
                carry_half = carry
        r8 = lax.broadcasted_iota(I32, (N_EXPERTS, 1), 0)
        for ee in range(N_EXPERTS):
            cnt_smem[ee] = jnp.sum(jnp.where(r8 == ee, carry, 0.0)).astype(I32)
            cnt_smem[N_EXPERTS + ee] = jnp.sum(jnp.where(r8 == ee, carry_half, 0.0)).astype(I32)
        rn_scr[...] = jnp.concatenate(
            [rsel_scr[...], jnp.zeros((LANES - N_EXPERTS, tm), F32)], axis=0).T

    n_rc = (cnt_smem[e] + (rc_rows - 1)) // rc_rows
    cnt_first = cnt_smem[N_EXPERTS + e]
    token_regions = ((0, tm), (0, tm // 2), (tm // 2, tm))

    def chunk_region(r0):
        return jnp.where(r0 + rc_rows <= cnt_first, 1, jnp.where(r0 >= cnt_first, 2, 0))

    @pl.when(f == 0)
    def _gather():
        rsel_row = rsel_scr[pl.ds(e, 1), :]
        gate_row = gates_scr[pl.ds(e, 1), :]
        lane = lax.broadcasted_iota(I32, (tm, LANES), 1)
        rcol = jnp.sum(jnp.where(lane == e, rn_scr[...], 0.0), axis=1, keepdims=True)
        rcol_scr[...] = jnp.broadcast_to(rcol, (tm, LANES))

        def compact(r0, lo, hi):
            want = (r0 + 1 + lax.broadcasted_iota(I32, (rc_rows, hi - lo), 0)).astype(F32)
            pm = rsel_row[:, lo:hi] == want
            pb = jnp.where(pm, 1.0, 0.0).astype(BF16)
            xe_scr[pl.ds(r0, rc_rows), :] = jnp.dot(pb, h_scr[lo:hi, :],
                                                    preferred_element_type=F32).astype(BF16)
            gcol = jnp.sum(jnp.where(pm, gate_row[:, lo:hi], 0.0), axis=1, keepdims=True)
            gcol_scr[pl.ds(r0, rc_rows), :] = jnp.broadcast_to(gcol, (rc_rows, LANES))

        def body(rc, carry):
            r0 = pl.multiple_of(rc * rc_rows, rc_rows)
            for region, (lo, hi) in enumerate(token_regions):
                @pl.when(chunk_region(r0) == region)
                def _():
                    compact(r0, lo, hi)
            ye_scr[pl.ds(r0, rc_rows), :] = jnp.zeros((rc_rows, ye_scr.shape[1]), F32)
            return carry

        lax.fori_loop(0, n_rc, body, 0)

    def ffn_body(rc, carry):
        r0 = pl.multiple_of(rc * rc_rows, rc_rows)
        xe = xe_scr[pl.ds(r0, rc_rows), :]
        a = jnp.dot(xe, wg_ref[0], preferred_element_type=F32)
        u = jnp.dot(xe, wu_ref[0], preferred_element_type=F32)
        act = (_silu(a) * u * gcol_scr[pl.ds(r0, rc_rows), 0:1]).astype(BF16)
        ye_scr[pl.ds(r0, rc_rows), :] += jnp.dot(act, wd_ref[0], preferred_element_type=F32)
        return carry

    lax.fori_loop(0, n_rc, ffn_body, 0)

    @pl.when(f == pl.num_programs(3) - 1)
    def _scatter():
        def body(rc, carry):
            r0 = pl.multiple_of(rc * rc_rows, rc_rows)
            ye = ye_scr[pl.ds(r0, rc_rows), :].astype(BF16)
            for region, (lo, hi) in enumerate(token_regions):
                @pl.when(chunk_region(r0) == region)
                def _():
                    want = (r0 + 1 + lax.broadcasted_iota(I32, (hi - lo, rc_rows), 1)).astype(F32)
                    pt = jnp.where(rcol_scr[lo:hi, 0:1] == want, 1.0, 0.0).astype(BF16)
                    o_ref[0, lo:hi, :] += jnp.dot(pt, ye, preferred_element_type=F32)
            return carry

        lax.fori_loop(0, n_rc, body, 0)

    @pl.when((e == pl.num_programs(2) - 1) & (f == pl.num_programs(3) - 1))
    def _residual():
        for jb in range(ntb):
            rows = slice(jb * tb, (jb + 1) * tb)
            y = x_ref[0, rows, :] + gate_ref[0] * o_ref[0, rows, :]
            if final_norm:
                var = jnp.mean(y * y, axis=-1, keepdims=True)
                y = (y * lax.rsqrt(var + EPS)) * gfin_ref[...]
            o_ref[0, rows, :] = y


def _moe(x, sh, sc, gate, g, w_router, wg, wu, wd, tm, tf, g_final=None):
    bn, s, d = x.shape
    ne, _, fdim = wg.shape
    assert ne == N_EXPERTS and tm % MOE_RC == 0 and tm % MOE_TB == 0 and (tm // 2) % LANES == 0
    wr = jnp.zeros((d, LANES), F32).at[:, :ne].set(w_router)
    final_norm = g_final is not None
    gfin = g_final if final_norm else jnp.ones((1, d), F32)
    tok_in = pl.BlockSpec((1, tm, d), lambda b, t, e, f: (b, t, 0), pipeline_mode=pl.Buffered(1))
    tok_out = pl.BlockSpec((1, tm, d), lambda b, t, e, f: (b, t, 0))
    vec = pl.BlockSpec((1, 1, d), lambda b, t, e, f: (b, 0, 0))
    return pl.pallas_call(
        functools.partial(_moe_kernel, final_norm),
        grid=(bn, s // tm, ne, fdim // tf),
        in_specs=[tok_in, vec, vec, vec,
                  pl.BlockSpec((1, d), lambda b, t, e, f: (0, 0)),
                  pl.BlockSpec((1, d), lambda b, t, e, f: (0, 0)),
                  pl.BlockSpec((d, LANES), lambda b, t, e, f: (0, 0)),
                  pl.BlockSpec((1, d, tf), lambda b, t, e, f: (e, 0, f)),
                  pl.BlockSpec((1, d, tf), lambda b, t, e, f: (e, 0, f)),
                  pl.BlockSpec((1, tf, d), lambda b, t, e, f: (e, f, 0))],
        out_specs=tok_out,
        out_shape=jax.ShapeDtypeStruct((bn, s, d), F32),
        scratch_shapes=[pltpu.VMEM((tm, d), BF16),
                        pltpu.VMEM((tm, d), BF16),
                        pltpu.VMEM((tm, d), F32),
                        pltpu.VMEM((tm, LANES), F32),
                        pltpu.VMEM((ne, tm), F32),
                        pltpu.VMEM((ne, tm), F32),
                        pltpu.VMEM((tm, LANES), F32),
                        pltpu.VMEM((tm, LANES), F32),
                        pltpu.SMEM((2 * ne,), I32)],
        compiler_params=pltpu.CompilerParams(
            dimension_semantics=("parallel", "parallel", "arbitrary", "arbitrary"),
            vmem_limit_bytes=56 * 1024 * 1024),
    )(x, sh, sc, gate, g, gfin, wr, wg, wu, wd)


def _final_norm_kernel(x_ref, g_ref, o_ref):
    x = x_ref[0]
    var = jnp.mean(x * x, axis=-1, keepdims=True)
    o_ref[0] = (x * lax.rsqrt(var + EPS)) * g_ref[...]


def _final_norm(x, g, tm):
    bn, s, d = x.shape
    tok = pl.BlockSpec((1, tm, d), lambda b, t: (b, t, 0))
    return pl.pallas_call(
        _final_norm_kernel,
        grid=(bn, s // tm),
        in_specs=[tok, pl.BlockSpec((1, d), lambda b, t: (0, 0))],
        out_specs=tok,
        out_shape=jax.ShapeDtypeStruct((bn, s, d), F32),
        compiler_params=_cparams("parallel", "parallel"),
    )(x, g)


def _token_tile(s, want):
    tm = min(want, s)
    assert s % tm == 0 and tm % QB == 0
    return tm


def kernel(x, c, positions, w_ada, b_ada, g_norm_mix, g_norm_ffn, w_in, w_out, hgrn_lb_logits, hgrn_out_norm, gmlp_vnorm_g, gmlp_vnorm_b, gmlp_w_s, gmlp_b_s, ffn_w_gate, ffn_w_up, ffn_w_down, moe_w_router, moe_w_gate, moe_w_up, moe_w_down, g_final):
    bn, s, d = x.shape
    depth = w_in.shape[0]
    assert s % QB == 0 and d == sum(IN_WIDTHS[:1]) + B_DIM + C_DIM
    tm_proj = _token_tile(s, 512)
    tm_ffn = _token_tile(s, 1024)

    p_lb = jax.nn.softmax(hgrn_lb_logits.astype(F32), axis=0)
    cum = jnp.cumsum(p_lb, axis=0)
    lower_bounds = cum - cum[0:1]

    mod = _ada_mod(c, w_ada, b_ada)
    cosn, sinn, cost, sint = _rope_tables(positions)

    for l in range(depth):
        sh1, sc1, g1, sh2, sc2, g2 = [mod[l, :, i * d:(i + 1) * d].reshape(bn, 1, d) for i in range(6)]
        w_n, w_t = _prep_in_weights(w_in[l])
        (kz, ik3, bq, bf, bi, bg, c_out, qt, iq3t, vt4, iwt, bit) = _inproj(
            x, sh1, sc1, g_norm_mix[l].reshape(1, d), w_n, w_t, cosn, sinn, cost, sint,
            gmlp_vnorm_g[l].reshape(1, C_DIM), gmlp_vnorm_b[l].reshape(1, C_DIM), gmlp_w_s[l], gmlp_b_s[l],
            tm_proj)
        a_out = _dsa(ik3, kz, vt4, iq3t, iwt, qt)
        b_out = _hgrn(bq, bf, bi, bit, bg, lower_bounds[l].reshape(1, B_DIM),
                      jnp.tile(hgrn_out_norm[l], LANES // B_DV).reshape(1, LANES))
        gf = g_norm_ffn[l].reshape(1, d)
        if l % 2 == 0:
            i = l // 2
            x = _ffn(x, a_out, b_out, c_out, w_out[l], g1, sh2, sc2, g2, gf, ffn_w_gate[i].astype(BF16),
                     ffn_w_up[i].astype(BF16), ffn_w_down[i].astype(BF16), tm_ffn, 256)
        else:
            i = l // 2
            x = _outproj(x, a_out, b_out, c_out, w_out[l], g1, tm_proj)
            fused_final = l == depth - 1
            x = _moe(x, sh2, sc2, g2, gf, moe_w_router[i], moe_w_gate[i].astype(BF16),
                     moe_w_up[i].astype(BF16), moe_w_down[i].astype(BF16), _token_tile(s, 2048), 512,
                     g_final.reshape(1, d) if fused_final else None)
    if depth % 2 == 0:
        return x
    return _final_norm(x, g_final.reshape(1, d), tm_proj)
```

```python
import functools

import numpy as np
import jax
import jax.numpy as jnp
from jax import lax
from jax.experimental import pallas as pl
from jax.experimental.pallas import tpu as pltpu

F32 = jnp.float32
BF16 = jnp.bfloat16
I32 = jnp.int32
I16 = jnp.int16

HEAD_DIM = 64
A_HEADS = 6
A_DIM = A_HEADS * HEAD_DIM
IDX_HEADS = 4
IDX_DIM = 64
TOPK_MAX = 256
B_HEADS = 6
B_DK = 64
B_DV = 64
B_DIM = B_HEADS * B_DV
HGRN_CHUNK = 64
C_GROUPS = 4
C_GROUP_DIM = 64
C_DIM = C_GROUPS * C_GROUP_DIM
C_CHUNK = 128
ROPE_THETA = 10000.0
N_EXPERTS = 8
EPS = 1e-6
IN_WIDTHS = (A_DIM, HEAD_DIM, HEAD_DIM, IDX_HEADS * IDX_DIM, IDX_DIM, IDX_HEADS,
             B_DIM, B_DIM, B_DIM, B_DIM, 2 * C_DIM)

LANES = 128
KC = 128
QB = 2 * KC
PV_ROWS = HEAD_DIM + 16
VMEM_LIMIT = 48 * 1024 * 1024
HGRN_FAST_SPAN = 80.0
HGRN_SUB = 16
HGRN_GROUP = 4
INT_MIN = -2 ** 31

NT_DIMS = (((1,), (1,)), ((), ()))
TN_DIMS = (((0,), (0,)), ((), ()))


def _cparams(*sem):
    return pltpu.CompilerParams(dimension_semantics=sem, vmem_limit_bytes=VMEM_LIMIT)


def _split2(x):
    hi = x.astype(BF16)
    lo = (x - hi.astype(F32)).astype(BF16)
    return hi, lo


def _split3(x):
    hi = x.astype(BF16)
    r1 = x - hi.astype(F32)
    mid = r1.astype(BF16)
    lo = (r1 - mid.astype(F32)).astype(BF16)
    return hi, mid, lo


def _silu(x):
    return x * (1.0 / (1.0 + jnp.exp(-x)))


def _modulated_rmsnorm(x, g, sc, sh):
    var = jnp.mean(x * x, axis=-1, keepdims=True)
    y = x * lax.rsqrt(var + EPS)
    return (y * g) * (1.0 + sc) + sh


def _ada_kernel(c_ref, w_ref, b_ref, o_ref):
    cond = _silu(c_ref[...])
    o_ref[0] = jnp.dot(cond, w_ref[0], precision=lax.Precision.HIGHEST,
                       preferred_element_type=F32) + b_ref[0]


def _ada_mod(c, w_ada, b_ada):
    depth, d, d6 = w_ada.shape
    bn = c.shape[0]
    tn = 1536
    return pl.pallas_call(
        _ada_kernel,
        grid=(depth, d6 // tn),
        in_specs=[pl.BlockSpec((bn, d), lambda l, n: (0, 0)),
                  pl.BlockSpec((1, d, tn), lambda l, n: (l, 0, n)),
                  pl.BlockSpec((1, 1, tn), lambda l, n: (l, 0, n))],
        out_specs=pl.BlockSpec((1, bn, tn), lambda l, n: (l, 0, n)),
        out_shape=jax.ShapeDtypeStruct((depth, bn, d6), F32),
        compiler_params=_cparams("parallel", "parallel"),
    )(c, w_ada, b_ada.reshape(depth, 1, d6))


def _rope_kernel(posn_ref, post_ref, invn_ref, signn_ref, invt_ref,
                 cosn_ref, sinn_ref, cost_ref, sint_ref):
    ang_n = posn_ref[0].astype(F32) * invn_ref[...]
    cosn_ref[0] = jnp.cos(ang_n)
    sinn_ref[0] = jnp.sin(ang_n) * signn_ref[...]
    ang_t = invt_ref[...] * post_ref[0].astype(F32)
    cost_ref[0] = jnp.cos(ang_t)
    sint_ref[0] = jnp.sin(ang_t)


def _rope_tables(positions):
    bn, s = positions.shape
    half = HEAD_DIM // 2
    inv = ROPE_THETA ** (-jnp.arange(0, HEAD_DIM, 2, dtype=F32) / HEAD_DIM)
    inv_n = jnp.tile(inv, LANES // half).reshape(1, LANES)
    sign_n = jnp.tile(jnp.concatenate([-jnp.ones((half,), F32), jnp.ones((half,), F32)]),
                      LANES // HEAD_DIM).reshape(1, LANES)
    inv_t = inv.reshape(half, 1)
    full = lambda shape: pl.BlockSpec(shape, lambda b: (0,) * len(shape))
    return pl.pallas_call(
        _rope_kernel,
        grid=(bn,),
        in_specs=[pl.BlockSpec((1, s, 1), lambda b: (b, 0, 0)),
                  pl.BlockSpec((1, 1, s), lambda b: (b, 0, 0)),
                  full((1, LANES)), full((1, LANES)), full((half, 1))],
        out_specs=[pl.BlockSpec((1, s, LANES), lambda b: (b, 0, 0)),
                   pl.BlockSpec((1, s, LANES), lambda b: (b, 0, 0)),
                   pl.BlockSpec((1, half, s), lambda b: (b, 0, 0)),
                   pl.BlockSpec((1, half, s), lambda b: (b, 0, 0))],
        out_shape=[jax.ShapeDtypeStruct((bn, s, LANES), F32),
                   jax.ShapeDtypeStruct((bn, s, LANES), F32),
                   jax.ShapeDtypeStruct((bn, half, s), F32),
                   jax.ShapeDtypeStruct((bn, half, s), F32)],
        compiler_params=_cparams("parallel"),
    )(positions.reshape(bn, s, 1), positions.reshape(bn, 1, s), inv_n, sign_n, inv_t)


N_KZ = 0
N_IK = 2 * LANES
N_BQ = 4 * LANES
N_BF = N_BQ + B_DIM
N_BI = N_BF + B_DIM
N_BG = N_BI + B_DIM
N_CUV = N_BG + B_DIM
N_COLS = N_CUV + 2 * C_DIM
T_Q = 0
T_IQ = A_DIM
T_V = T_IQ + IDX_HEADS * IDX_DIM
T_IW = T_V + HEAD_DIM
T_BI = T_IW + 8
T_ROWS = T_BI + B_DIM


def _prep_in_weights(w_in_l):
    offs = np.concatenate([[0], np.cumsum(IN_WIDTHS)])
    sl = lambda i: w_in_l[:, int(offs[i]):int(offs[i + 1])]
    aq, ak, av, iq, ik, iw, bq, bf, bi, bg, cuv = [sl(i) for i in range(11)]
    half = HEAD_DIM // 2
    rot = lambda w: jnp.concatenate([w[:, half:], w[:, :half]], axis=1)
    z = jnp.zeros_like(ak)
    w_n = jnp.concatenate([ak, z, rot(ak), z, ik, ik, rot(ik), rot(ik), bq, bf, bi, bg, cuv], axis=1)
    w_t = jnp.concatenate([aq, iq, av, iw, jnp.zeros((w_in_l.shape[0], 4), w_in_l.dtype), bi], axis=1).T
    return w_n.astype(BF16), w_t.astype(BF16)


def _inproj_kernel(x_ref, sh_ref, sc_ref, g_ref, wn_ref, wt_ref, cosn_ref, sinn_ref, cost_ref, sint_ref,
                   vg_ref, vb_ref, ws_ref, cbias_ref,
                   kz_ref, ik3_ref, bq_ref, bf_ref, bi_ref, bg_ref, c_ref,
                   qt_ref, iq3t_ref, vt_ref, iwt_ref, bit_ref):
    tm = x_ref.shape[1]
    half = HEAD_DIM // 2
    h = _modulated_rmsnorm(x_ref[0], g_ref[...], sc_ref[0], sh_ref[0]).astype(BF16)

    cosn = cosn_ref[0]
    sinn = sinn_ref[0]
    zk = jnp.dot(h, wn_ref[:, N_KZ:N_KZ + 2 * LANES], preferred_element_type=F32)
    kz_ref[0] = (zk[:, :LANES] * cosn + zk[:, LANES:] * sinn).astype(BF16)
    zi = jnp.dot(h, wn_ref[:, N_IK:N_IK + 2 * LANES], preferred_element_type=F32)
    ik2 = zi[:, :LANES] * cosn + zi[:, LANES:] * sinn
    hi, lo = _split2(ik2)
    lane = lax.broadcasted_iota(I32, (tm, LANES), 1)
    ik3_ref[0, :, :LANES] = jnp.where(lane < IDX_DIM, hi, lo)
    ik3_ref[0, :, LANES:] = hi
    bq_ref[0] = jnp.dot(h, wn_ref[:, N_BQ:N_BF], preferred_element_type=F32).astype(BF16)
    bf_ref[0] = jnp.dot(h, wn_ref[:, N_BF:N_BI], preferred_element_type=F32)
    bi_ref[0] = jnp.dot(h, wn_ref[:, N_BI:N_BG], preferred_element_type=F32).astype(BF16)
    bg_ref[0] = jnp.dot(h, wn_ref[:, N_BG:N_CUV], preferred_element_type=F32).astype(BF16)
    cuv = jnp.dot(h, wn_ref[:, N_CUV:N_COLS], preferred_element_type=F32)
    c_ref[0] = _gmlp_mix(cuv, vg_ref[...], vb_ref[...], ws_ref, cbias_ref[...]).astype(BF16)

    zt = lax.dot_general(wt_ref[...], h, NT_DIMS, preferred_element_type=F32)
    cost = cost_ref[0]
    sint = sint_ref[0]
    qscale = HEAD_DIM ** -0.5
    for hh in range(A_HEADS):
        r0 = T_Q + hh * HEAD_DIM
        x1 = zt[r0:r0 + half]
        x2 = zt[r0 + half:r0 + HEAD_DIM]
        qt_ref[0, hh * HEAD_DIM:hh * HEAD_DIM + half] = ((x1 * cost - x2 * sint) * qscale).astype(BF16)
        qt_ref[0, hh * HEAD_DIM + half:(hh + 1) * HEAD_DIM] = ((x2 * cost + x1 * sint) * qscale).astype(BF16)
    iscale = IDX_DIM ** -0.5
    zero = jnp.zeros((IDX_DIM, tm), BF16)
    for hh in range(IDX_HEADS):
        r0 = T_IQ + hh * IDX_DIM
        x1 = zt[r0:r0 + half]
        x2 = zt[r0 + half:r0 + IDX_DIM]
        y = jnp.concatenate([(x1 * cost - x2 * sint) * iscale, (x2 * cost + x1 * sint) * iscale], axis=0)
        hi, lo = _split2(y)
        iq3t_ref[0, hh, 0 * IDX_DIM:1 * IDX_DIM] = hi
        iq3t_ref[0, hh, 1 * IDX_DIM:2 * IDX_DIM] = hi
        iq3t_ref[0, hh, 2 * IDX_DIM:3 * IDX_DIM] = lo
        iq3t_ref[0, hh, 3 * IDX_DIM:4 * IDX_DIM] = zero
    vt = zt[T_V:T_V + HEAD_DIM].astype(BF16)
    for i in range(tm // KC):
        vt_ref[0, i] = vt[:, i * KC:(i + 1) * KC]
    iwt_ref[0] = zt[T_IW:T_IW + 8] * (IDX_HEADS ** -0.5)
    bit = zt[T_BI:T_BI + B_DIM].astype(BF16)
    for i in range(tm // LANES):
        bit_ref[0, i] = bit[:, i * LANES:(i + 1) * LANES]


def _inproj(x, sh, sc, g, w_n, w_t, cosn, sinn, cost, sint, vg, vb, ws, bs, tm):
    bn, s, d = x.shape
    assert tm % C_CHUNK == 0
    nt = s // tm
    half = HEAD_DIM // 2
    cbias = jnp.repeat(bs.T, C_GROUP_DIM, axis=1)
    tok = lambda w: pl.BlockSpec((1, tm, w), lambda b, t: (b, t, 0))
    vec = pl.BlockSpec((1, 1, d), lambda b, t: (b, 0, 0))
    full2 = lambda a: pl.BlockSpec(a.shape, lambda b, t: (0, 0))
    out_shapes = [
        jax.ShapeDtypeStruct((bn, s, LANES), BF16),
        jax.ShapeDtypeStruct((bn, s, 2 * LANES), BF16),
        jax.ShapeDtypeStruct((bn, s, B_DIM), BF16),
        jax.ShapeDtypeStruct((bn, s, B_DIM), F32),
        jax.ShapeDtypeStruct((bn, s, B_DIM), BF16),
        jax.ShapeDtypeStruct((bn, s, B_DIM), BF16),
        jax.ShapeDtypeStruct((bn, s, C_DIM), BF16),
        jax.ShapeDtypeStruct((bn, A_DIM, s), BF16),
        jax.ShapeDtypeStruct((bn, IDX_HEADS, 4 * IDX_DIM, s), BF16),
        jax.ShapeDtypeStruct((bn, s // KC, HEAD_DIM, KC), BF16),
        jax.ShapeDtypeStruct((bn, 8, s), F32),
        jax.ShapeDtypeStruct((bn, s // LANES, B_DIM, LANES), BF16),
    ]
    out_specs = [
        tok(LANES), tok(2 * LANES), tok(B_DIM), tok(B_DIM), tok(B_DIM), tok(B_DIM), tok(C_DIM),
        pl.BlockSpec((1, A_DIM, tm), lambda b, t: (b, 0, t)),
        pl.BlockSpec((1, IDX_HEADS, 4 * IDX_DIM, tm), lambda b, t: (b, 0, 0, t)),
        pl.BlockSpec((1, tm // KC, HEAD_DIM, KC), lambda b, t: (b, t, 0, 0)),
        pl.BlockSpec((1, 8, tm), lambda b, t: (b, 0, t)),
        pl.BlockSpec((1, tm // LANES, B_DIM, LANES), lambda b, t: (b, t, 0, 0)),
    ]
    return pl.pallas_call(
        _inproj_kernel,
        grid=(bn, nt),
        in_specs=[tok(d), vec, vec, full2(g), full2(w_n), full2(w_t),
                  tok(LANES), tok(LANES),
                  pl.BlockSpec((1, half, tm), lambda b, t: (b, 0, t)),
                  pl.BlockSpec((1, half, tm), lambda b, t: (b, 0, t)),
                  full2(vg), full2(vb), pl.BlockSpec(ws.shape, lambda b, t: (0, 0, 0)), full2(cbias)],
        out_specs=out_specs,
        out_shape=out_shapes,
        compiler_params=_cparams("parallel", "parallel"),
    )(x, sh, sc, g, w_n, w_t, cosn, sinn, cost, sint, vg, vb, ws, cbias)


def _dsa_kernel(n_top, ik3_ref, kz_ref, vt_ref, iq3t_ref, iwt_ref, qt_ref, out_ref,
                key_scr, hi_scr, lo_scr, lg_scr, acc_scr):
    j = pl.program_id(1)
    npair = j + 1
    pair = 2 * KC
    row = lax.broadcasted_iota(I32, (KC, QB), 0)
    col = lax.broadcasted_iota(I32, (KC, QB), 1)
    int_min = jnp.int32(INT_MIN)
    i16_min = jnp.int16(-2 ** 15)

    def score_pair(p, diagonal):
        for u in range(2):
            ks = pl.multiple_of(p * pair + u * KC, KC)
            ikc = ik3_ref[0, pl.ds(ks, KC), :]
            sc = jnp.zeros((KC, QB), F32)
            for hh in range(IDX_HEADS):
                rel = jnp.dot(ikc, iq3t_ref[0, hh], preferred_element_type=F32)
                sc = sc + jnp.maximum(rel, 0.0) * iwt_ref[0, hh:hh + 1, :]
            sc = jnp.where(sc == 0.0, 0.0, sc)
            bits = pltpu.bitcast(sc, I32)
            key = bits ^ ((bits >> 31) & jnp.int32(0x7FFFFFFF))
            if diagonal:
                key = jnp.where((u * KC + row) <= col, key, int_min)
            key_scr[pl.ds(ks, KC), :] = key
            hi_scr[pl.ds(ks, KC), :] = (key >> 16).astype(I16)
            lo_scr[pl.ds(ks, KC), :] = ((key & jnp.int32(0xFFFF)) - 32768).astype(I16)

    def score_body(p, carry):
        score_pair(p, False)
        return carry

    lax.fori_loop(0, j, score_body, 0)
    score_pair(j, True)

    def count16(ref, pred_fn):
        def body(p, acc):
            ks = pl.multiple_of(p * pair, pair)
            m = jnp.where(pred_fn(ref[pl.ds(ks, pair), :]), jnp.int16(1), jnp.int16(0))
            parts = [m[16 * i:16 * (i + 1)] for i in range(pair // 16)]
            while len(parts) > 1:
                parts = [parts[i] + parts[i + 1] for i in range(0, len(parts), 2)]
            return acc + parts[0]
        acc = lax.fori_loop(0, npair, body, jnp.zeros((16, QB), I16))
        return jnp.sum(acc.astype(I32), axis=0, keepdims=True)

    def bisect16(ref, k_needed):
        def bit_body(i, t_u):
            cand = t_u | jnp.left_shift(jnp.int32(1), 15 - i)
            cand16 = (cand - 32768).astype(I16)
            cnt = count16(ref, lambda x: x >= cand16)
            return jnp.where(cnt >= k_needed, cand, t_u)
        return lax.fori_loop(0, 16, bit_body, jnp.zeros((1, QB), I32))

    t_hi = bisect16(hi_scr, n_top)
    t_hi16 = (t_hi - 32768).astype(I16)
    n_hi_gt = count16(hi_scr, lambda x: x > t_hi16)

    def bucket_body(p, carry):
        ks = pl.multiple_of(p * pair, pair)
        in_bucket = hi_scr[pl.ds(ks, pair), :] == t_hi16
        lo_scr[pl.ds(ks, pair), :] = jnp.where(in_bucket, lo_scr[pl.ds(ks, pair), :], i16_min)
        return carry

    lax.fori_loop(0, npair, bucket_body, 0)
    t_lo = bisect16(lo_scr, n_top - n_hi_gt)
    t_lo16 = (t_lo - 32768).astype(I16)
    n_gt = n_hi_gt + count16(lo_scr, lambda x: x > t_lo16)
    thr = jnp.left_shift(t_hi - 32768, 16) | t_lo
    n_tie = (n_top - n_gt).astype(F32)
    thr_valid = jnp.where(thr > int_min, 1.0, 0.0)

    tr = lax.broadcasted_iota(I32, (KC, KC), 0)
    tc = lax.broadcasted_iota(I32, (KC, KC), 1)
    ltri = jnp.where(tr >= tc, 1.0, 0.0).astype(BF16)
    zpad = jnp.zeros((HEAD_DIM, QB), BF16)
    qpad = [jnp.concatenate([qt_ref[0, hh * HEAD_DIM:(hh + 1) * HEAD_DIM, :], zpad], axis=0)
            for hh in range(A_HEADS)]
    neg_inf = jnp.float32(-jnp.inf)

    def pass_a(p, carry):
        tie_cnt, ms = carry
        ms = list(ms)
        for u in range(2):
            ks = pl.multiple_of(p * pair + u * KC, KC)
            kc = key_scr[pl.ds(ks, KC), :]
            eqf = jnp.where(kc == thr, thr_valid, 0.0)
            pref = jnp.dot(ltri, eqf.astype(BF16), preferred_element_type=F32)
            keep_tie = eqf * jnp.where((tie_cnt + pref) <= n_tie, 1.0, 0.0)
            sel = jnp.where(kc > thr, 1.0, keep_tie) > 0.5
            tie_cnt = tie_cnt + pref[KC - 1:KC, :]
            kzc = kz_ref[0, pl.ds(ks, KC), :]
            for hh in range(A_HEADS):
                lt = jnp.dot(kzc, qpad[hh], preferred_element_type=F32)
                lt = jnp.where(sel, lt, neg_inf)
                lg_scr[hh, pl.ds(ks, KC), :] = lt
                ms[hh] = jnp.maximum(ms[hh], jnp.max(lt, axis=0, keepdims=True))
        return tie_cnt, tuple(ms)

    init_m = tuple(jnp.full((1, QB), neg_inf, F32) for _ in range(A_HEADS))
    _, ms = lax.fori_loop(0, npair, pass_a, (jnp.zeros((1, QB), F32), init_m))

    acc_scr[...] = jnp.zeros_like(acc_scr)
    ones_rows = jnp.ones((PV_ROWS - HEAD_DIM, pair), BF16)

    def pass_b(p, carry):
        ks = pl.multiple_of(p * pair, pair)
        vt2 = jnp.concatenate([vt_ref[0, 2 * p], vt_ref[0, 2 * p + 1]], axis=1)
        vt2 = jnp.concatenate([vt2, ones_rows], axis=0)
        for hh in range(A_HEADS):
            pr = jnp.exp(lg_scr[hh, pl.ds(ks, pair), :] - ms[hh])
            acc_scr[hh * PV_ROWS:(hh + 1) * PV_ROWS, :] += jnp.dot(
                vt2, pr.astype(BF16), preferred_element_type=F32)
        return carry

    lax.fori_loop(0, npair, pass_b, 0)
    o_t = jnp.concatenate(
        [acc_scr[hh * PV_ROWS:hh * PV_ROWS + HEAD_DIM, :]
         / acc_scr[hh * PV_ROWS + HEAD_DIM:hh * PV_ROWS + HEAD_DIM + 1, :] for hh in range(A_HEADS)], axis=0)
    out_ref[0] = o_t.T.astype(BF16)


def _dsa(ik3, kz, vt4, iq3t, iwt, qt):
    bn, s, _ = kz.shape
    assert s % QB == 0
    n_top = min(TOPK_MAX, s // 4)
    nq = s // QB
    return pl.pallas_call(
        functools.partial(_dsa_kernel, n_top),
        grid=(bn, nq),
        in_specs=[pl.BlockSpec((1, s, 2 * LANES), lambda b, q: (b, 0, 0)),
                  pl.BlockSpec((1, s, LANES), lambda b, q: (b, 0, 0)),
                  pl.BlockSpec((1, s // KC, HEAD_DIM, KC), lambda b, q: (b, 0, 0, 0)),
                  pl.BlockSpec((1, IDX_HEADS, 4 * IDX_DIM, QB), lambda b, q: (b, 0, 0, q)),
                  pl.BlockSpec((1, 8, QB), lambda b, q: (b, 0, q)),
                  pl.BlockSpec((1, A_DIM, QB), lambda b, q: (b, 0, q))],
        out_specs=pl.BlockSpec((1, QB, A_DIM), lambda b, q: (b, q, 0)),
        out_shape=jax.ShapeDtypeStruct((bn, s, A_DIM), BF16),
        scratch_shapes=[pltpu.VMEM((s, QB), I32), pltpu.VMEM((s, QB), I16), pltpu.VMEM((s, QB), I16),
                        pltpu.VMEM((A_HEADS, s, QB), F32), pltpu.VMEM((A_HEADS * PV_ROWS, QB), F32)],
        compiler_params=_cparams("parallel", "arbitrary"),
    )(ik3, kz, vt4, iq3t, iwt, qt)


def _hgrn_kernel(q_ref, f_ref, i_ref, it_ref, g_ref, lb_ref, gn_ref, o_ref, st_scr, oi_scr):
    s = q_ref.shape[1]
    ch = HGRN_CHUNK
    sb = HGRN_SUB
    nsb = ch // sb
    rows = HGRN_GROUP * ch
    wins_per_group = rows // LANES
    zeros_half = jnp.zeros((ch, LANES), BF16)
    lane = lax.broadcasted_iota(I32, (ch, LANES), 1)
    head0 = lane < B_DK
    lr = lax.broadcasted_iota(I32, (ch, 3 * ch), 0)
    lc = lax.broadcasted_iota(I32, (ch, 3 * ch), 1) % ch
    ltri3 = jnp.where(lc <= lr, 1.0, 0.0).astype(BF16)
    ar = lax.broadcasted_iota(I32, (2 * ch, nsb * ch), 0) % ch
    ac = lax.broadcasted_iota(I32, (2 * ch, nsb * ch), 1)
    att_mask = ((ar // sb) == (ac // ch)) & ((ac % ch) <= ar)
    bd_r = lax.broadcasted_iota(I32, (LANES, LANES), 0)
    bd_c = lax.broadcasted_iota(I32, (LANES, LANES), 1)
    same_head = (bd_r < B_DK) == (bd_c < B_DK)
    ones_bd = jnp.where(same_head, 1.0, 0.0).astype(BF16)
    s_iota = lax.broadcasted_iota(I32, (ch, LANES), 0)

    lb = lb_ref[...]
    log_lb = jnp.log(lb)
    log_1mlb = jnp.log1p(-lb)
    gn = gn_ref[...]
    st_scr[...] = jnp.zeros_like(st_scr)

    def group_body(it, carry):
        t0 = pl.multiple_of(it * rows, rows)
        z = f_ref[0, pl.ds(t0, rows), :]
        q = q_ref[0, pl.ds(t0, rows), :].astype(F32)
        v = i_ref[0, pl.ds(t0, rows), :]
        softplus_tail = jnp.log1p(jnp.exp(-jnp.abs(z)))
        log_sig = -(jnp.maximum(-z, 0.0) + softplus_tail)
        x2 = log_1mlb + log_sig
        amax = jnp.maximum(log_lb, x2)
        log_f = amax + jnp.log1p(jnp.exp(-jnp.abs(log_lb - x2)))
        kk = (1.0 - lb) * jnp.exp(-(jnp.maximum(z, 0.0) + softplus_tail))
        f_hi, f_mid, f_lo = _split3(log_f)
        b_parts, ref_parts = [], []
        for c in range(HGRN_GROUP):
            sl = slice(c * ch, (c + 1) * ch)
            b_c = jnp.dot(ltri3, jnp.concatenate([f_hi[sl], f_mid[sl], f_lo[sl]], axis=0),
                          preferred_element_type=F32)
            b_parts.append(b_c)
            ref_parts.append(jnp.zeros((sb, LANES), F32))
            for i in range(1, nsb):
                ref_parts.append(jnp.broadcast_to(b_c[i * sb - 1:i * sb], (sb, LANES)))
        b = jnp.concatenate(b_parts, axis=0)
        ref = jnp.concatenate(ref_parts, axis=0)
        q_loc = q * jnp.exp(b - ref)
        q_chk = (q * jnp.exp(b)).astype(BF16)
        span = ref - b
        span_max = jnp.max(span)

        st = st_scr[...]
        o_inter = []
        for c in range(HGRN_GROUP):
            lo_r = c * ch
            b_c = b[lo_r:lo_r + ch]
            ref_c = ref[lo_r:lo_r + ch]
            kk_c = kk[lo_r:lo_r + ch]
            v_c = v[lo_r:lo_r + ch]
            b_last = b_c[ch - 1:ch]
            kcat = jnp.concatenate(
                [kk_c * jnp.exp(jnp.minimum(ref_c[i * sb:i * sb + 1] - b_c, HGRN_FAST_SPAN))
                 for i in range(nsb)], axis=0).astype(BF16)
            q_c = q_loc[lo_r:lo_r + ch]
            qm = jnp.concatenate([jnp.where(head0, q_c, 0.0), jnp.where(head0, 0.0, q_c)],
                                 axis=0).astype(BF16)
            att = lax.dot_general(qm, kcat, NT_DIMS, preferred_element_type=F32)
            att = jnp.where(att_mask, att, 0.0).astype(BF16)
            o2 = jnp.dot(att, jnp.concatenate([v_c] * nsb, axis=0), preferred_element_type=F32)
            oi_scr[lo_r:lo_r + ch, :] = jnp.where(head0, o2[:ch], o2[ch:])
            o_inter.append(lax.dot_general(q_chk[lo_r:lo_r + ch], st.astype(BF16), NT_DIMS,
                                           preferred_element_type=F32))
            kh = (kk_c * jnp.exp(b_last - b_c)).astype(BF16)
            kh_win = jnp.concatenate([kh, zeros_half] if c % 2 == 0 else [zeros_half, kh], axis=0)
            upd = jnp.dot(it_ref[0, wins_per_group * it + c // 2], kh_win, preferred_element_type=F32)
            st = st * jnp.exp(b_last) + jnp.where(same_head, upd, 0.0)
        st_scr[...] = st

        @pl.when(span_max > HGRN_FAST_SPAN)
        def _():
            for c in range(HGRN_GROUP):
                lo_r = c * ch

                @pl.when(jnp.max(span[lo_r:lo_r + ch]) > HGRN_FAST_SPAN)
                def _():
                    b_c = b[lo_r:lo_r + ch]
                    q_c = q[lo_r:lo_r + ch]
                    kk_c = kk[lo_r:lo_r + ch]
                    vf = v[lo_r:lo_r + ch].astype(F32)

                    def t_body(t, carry2):
                        onehot = jnp.where(s_iota == t, 1.0, 0.0)
                        b_t = jnp.sum(onehot * b_c, axis=0, keepdims=True)
                        q_t = jnp.sum(onehot * q_c, axis=0, keepdims=True)
                        dec = jnp.exp(jnp.where(s_iota <= t, b_t - b_c, -jnp.inf))
                        w = q_t * kk_c * dec
                        w0 = jnp.sum(jnp.where(head0, w, 0.0), axis=1, keepdims=True)
                        w1 = jnp.sum(jnp.where(head0, 0.0, w), axis=1, keepdims=True)
                        a_col = jnp.where(head0, w0, w1)
                        oi_scr[pl.ds(lo_r + t, 1), :] = jnp.sum(a_col * vf, axis=0, keepdims=True)
                        return carry2

                    lax.fori_loop(0, ch, t_body, 0)

        o = oi_scr[...] + jnp.concatenate(o_inter, axis=0)
        o2_hi, o2_lo = _split2(o * o)
        ss = (jnp.dot(o2_hi, ones_bd, preferred_element_type=F32)
              + jnp.dot(o2_lo, ones_bd, preferred_element_type=F32))
        y = (o * lax.rsqrt(ss * (1.0 / B_DV) + EPS)) * gn
        g = g_ref[0, pl.ds(t0, rows), :].astype(F32)
        o_ref[0, pl.ds(t0, rows), :] = (y * _silu(g)).astype(o_ref.dtype)
        return carry

    lax.fori_loop(0, s // rows, group_body, 0)


def _hgrn(bq, bf, bi, bit, bg, lb, gn):
    bn, s, _ = bq.shape
    assert s % (HGRN_GROUP * HGRN_CHUNK) == 0 and 2 * HGRN_CHUNK == LANES and HGRN_GROUP % 2 == 0
    npair = B_DIM // LANES
    tok = pl.BlockSpec((1, s, LANES), lambda b, p: (b, 0, p))
    return pl.pallas_call(
        _hgrn_kernel,
        grid=(bn, npair),
        in_specs=[tok, tok, tok,
                  pl.BlockSpec((1, s // LANES, LANES, LANES), lambda b, p: (b, 0, p, 0)),
                  tok,
                  pl.BlockSpec((1, LANES), lambda b, p: (0, p)),
                  pl.BlockSpec((1, LANES), lambda b, p: (0, 0))],
        out_specs=tok,
        out_shape=jax.ShapeDtypeStruct((bn, s, B_DIM), BF16),
        scratch_shapes=[pltpu.VMEM((LANES, LANES), F32),
                        pltpu.VMEM((HGRN_GROUP * HGRN_CHUNK, LANES), F32)],
        compiler_params=_cparams("parallel", "parallel"),
    )(bq, bf, bi, bit, bg, lb, gn)


def _gmlp_mix(uv, vg, vb, ws_ref, bias):
    uv = 0.5 * uv * (1.0 + lax.erf(uv * (2.0 ** -0.5)))
    u = uv[:, :C_DIM]
    v = uv[:, C_DIM:]
    mu = jnp.mean(v, axis=-1, keepdims=True)
    var = jnp.mean(jnp.square(v - mu), axis=-1, keepdims=True)
    vn_b = (((v - mu) * lax.rsqrt(var + EPS)) * vg + vb).astype(BF16)
    r_t = lax.broadcasted_iota(I32, (C_CHUNK, C_CHUNK), 0)
    r_s = lax.broadcasted_iota(I32, (C_CHUNK, C_CHUNK), 1)
    group = lax.broadcasted_iota(I32, (C_CHUNK, C_DIM), 1) // C_GROUP_DIM
    ws = [jnp.where(r_t >= r_s, ws_ref[gi], 0.0).astype(BF16) for gi in range(C_GROUPS)]
    outs = []
    for c in range(uv.shape[0] // C_CHUNK):
        sl = slice(c * C_CHUNK, (c + 1) * C_CHUNK)
        mixed = jnp.zeros((C_CHUNK, C_DIM), F32)
        for gi in range(C_GROUPS):
            m = jnp.dot(ws[gi], vn_b[sl], preferred_element_type=F32)
            mixed = jnp.where(group == gi, m, mixed)
        outs.append(u[sl] * (mixed + bias))
    return jnp.concatenate(outs, axis=0)


def _outproj_kernel(x_ref, a_ref, b_ref, c_ref, wa_ref, wb_ref, wc_ref, g1_ref, o_ref):
    mix = (jnp.dot(a_ref[0], wa_ref[...], preferred_element_type=F32)
           + jnp.dot(b_ref[0], wb_ref[...], preferred_element_type=F32)
           + jnp.dot(c_ref[0], wc_ref[...], preferred_element_type=F32))
    o_ref[0] = x_ref[0] + g1_ref[0] * mix


def _outproj(x, a, b, c, w_out_l, g1, tm):
    bn, s, d = x.shape
    wa = w_out_l[:A_DIM].astype(BF16)
    wb = w_out_l[A_DIM:A_DIM + B_DIM].astype(BF16)
    wc = w_out_l[A_DIM + B_DIM:].astype(BF16)
    tok = lambda w: pl.BlockSpec((1, tm, w), lambda bb, t: (bb, t, 0))
    full = lambda arr: pl.BlockSpec(arr.shape, lambda bb, t: (0, 0))
    return pl.pallas_call(
        _outproj_kernel,
        grid=(bn, s // tm),
        in_specs=[tok(d), tok(A_DIM), tok(B_DIM), tok(C_DIM), full(wa), full(wb), full(wc),
                  pl.BlockSpec((1, 1, d), lambda bb, t: (bb, 0, 0))],
        out_specs=tok(d),
        out_shape=jax.ShapeDtypeStruct((bn, s, d), F32),
        compiler_params=_cparams("parallel", "parallel"),
    )(x, a, b, c, wa, wb, wc, g1)


def _ffn_kernel(x_ref, a_ref, b_ref, c_ref, wa_ref, wb_ref, wc_ref, g1_ref, sh_ref, sc_ref, gate_ref, g_ref,
                wg_ref, wu_ref, wd_ref, o_ref, xn_scr, h_scr, acc_scr):
    f = pl.program_id(2)

    @pl.when(f == 0)
    def _():
        mix = (jnp.dot(a_ref[0], wa_ref[...], preferred_element_type=F32)
               + jnp.dot(b_ref[0], wb_ref[...], preferred_element_type=F32)
               + jnp.dot(c_ref[0], wc_ref[...], preferred_element_type=F32))
        xn = x_ref[0] + g1_ref[0] * mix
        xn_scr[...] = xn
        h_scr[...] = _modulated_rmsnorm(xn, g_ref[...], sc_ref[0], sh_ref[0]).astype(BF16)
        acc_scr[...] = jnp.zeros_like(acc_scr)

    h = h_scr[...]
    a = jnp.dot(h, wg_ref[...], preferred_element_type=F32)
    u = jnp.dot(h, wu_ref[...], preferred_element_type=F32)
    act = (_silu(a) * u).astype(BF16)
    acc_scr[...] += jnp.dot(act, wd_ref[...], preferred_element_type=F32)

    @pl.when(f == pl.num_programs(2) - 1)
    def _():
        o_ref[0] = xn_scr[...] + gate_ref[0] * acc_scr[...]


def _ffn(x, a, b, c, w_out_l, g1, sh, sc, gate, g, wg, wu, wd, tm, tf):
    bn, s, d = x.shape
    fdim = wg.shape[1]
    wa = w_out_l[:A_DIM].astype(BF16)
    wb = w_out_l[A_DIM:A_DIM + B_DIM].astype(BF16)
    wc = w_out_l[A_DIM + B_DIM:].astype(BF16)
    tok = pl.BlockSpec((1, tm, d), lambda b, t, f: (b, t, 0))
    tokw = lambda w: pl.BlockSpec((1, tm, w), lambda b, t, f: (b, t, 0))
    full = lambda arr: pl.BlockSpec(arr.shape, lambda b, t, f: (0, 0))
    vec = pl.BlockSpec((1, 1, d), lambda b, t, f: (b, 0, 0))
    return pl.pallas_call(
        _ffn_kernel,
        grid=(bn, s // tm, fdim // tf),
        in_specs=[tok, tokw(A_DIM), tokw(B_DIM), tokw(C_DIM), full(wa), full(wb), full(wc), vec,
                  vec, vec, vec,
                  pl.BlockSpec((1, d), lambda b, t, f: (0, 0)),
                  pl.BlockSpec((d, tf), lambda b, t, f: (0, f)),
                  pl.BlockSpec((d, tf), lambda b, t, f: (0, f)),
                  pl.BlockSpec((tf, d), lambda b, t, f: (f, 0))],
        out_specs=tok,
        out_shape=jax.ShapeDtypeStruct((bn, s, d), F32),
        scratch_shapes=[pltpu.VMEM((tm, d), F32), pltpu.VMEM((tm, d), BF16), pltpu.VMEM((tm, d), F32)],
        compiler_params=_cparams("parallel", "parallel", "arbitrary"),
    )(x, a, b, c, wa, wb, wc, g1, sh, sc, gate, g, wg, wu, wd)


MOE_RC = 256
MOE_TB = 512


def _moe_kernel(final_norm, x_ref, sh_ref, sc_ref, gate_ref, g_ref, gfin_ref, wr_ref, wg_ref, wu_ref, wd_ref, o_ref,
                h_scr, xe_scr, ye_scr, gcol_scr, gates_scr, rsel_scr, rn_scr, rcol_scr, cnt_smem):
    e = pl.program_id(2)
    f = pl.program_id(3)
    tm = x_ref.shape[1]
    rc_rows = MOE_RC
    tb = MOE_TB
    ntb = tm // tb
    neg_inf = jnp.float32(-jnp.inf)

    @pl.when((e == 0) & (f == 0))
    def _route():
        for jb in range(ntb):
            rows = slice(jb * tb, (jb + 1) * tb)
            h = _modulated_rmsnorm(x_ref[0, rows, :], g_ref[...], sc_ref[0], sh_ref[0])
            h_scr[rows, :] = h.astype(BF16)
            rcol_scr[rows, :] = jnp.dot(h, wr_ref[...], precision=lax.Precision.HIGHEST,
                                        preferred_element_type=F32)
            o_ref[0, rows, :] = jnp.zeros((tb, o_ref.shape[2]), F32)
        logits = rcol_scr[...].T[:N_EXPERTS]
        ridx = lax.broadcasted_iota(I32, (N_EXPERTS, tm), 0)
        m1 = jnp.max(logits, axis=0, keepdims=True)
        i1 = jnp.min(jnp.where(logits == m1, ridx, N_EXPERTS), axis=0, keepdims=True)
        rest = jnp.where(ridx == i1, neg_inf, logits)
        m2 = jnp.max(rest, axis=0, keepdims=True)
        i2 = jnp.min(jnp.where(rest == m2, ridx, N_EXPERTS), axis=0, keepdims=True)
        e2 = jnp.exp(m2 - m1)
        den = 1.0 + e2
        gates_scr[...] = jnp.where(ridx == i1, 1.0 / den, jnp.where(ridx == i2, e2 / den, 0.0))
        sel = jnp.where(ridx == i1, 1.0, jnp.where(ridx == i2, 1.0, 0.0))
        ur = lax.broadcasted_iota(I32, (LANES, LANES), 0)
        uc = lax.broadcasted_iota(I32, (LANES, LANES), 1)
        utri = jnp.where(ur <= uc, 1.0, 0.0).astype(BF16)
        carry = jnp.zeros((N_EXPERTS, 1), F32)
        for kb in range(tm // LANES):
            sb = sel[:, kb * LANES:(kb + 1) * LANES]
            pref = jnp.dot(sb.astype(BF16), utri, preferred_element_type=F32) + carry
            rsel_scr[:, kb * LANES:(kb + 1) * LANES] = jnp.where(sb > 0.0, pref, -1.0)
            carry = pref[:, LANES - 1:LANES]
            if (kb + 1) * LANES == tm // 2:
                carry_half = carry
        r8 = lax.broadcasted_iota(I32, (N_EXPERTS, 1), 0)
        for ee in range(N_EXPERTS):
            cnt_smem[ee] = jnp.sum(jnp.where(r8 == ee, carry, 0.0)).astype(I32)
            cnt_smem[N_EXPERTS + ee] = jnp.sum(jnp.where(r8 == ee, carry_half, 0.0)).astype(I32)
        rn_scr[...] = jnp.concatenate(
            [rsel_scr[...], jnp.zeros((LANES - N_EXPERTS, tm), F32)], axis=0).T

    n_rc = (cnt_smem[e] + (rc_rows - 1)) // rc_rows
    cnt_first = cnt_smem[N_EXPERTS + e]
    token_regions = ((0, tm), (0, tm // 2), (tm // 2, tm))

    def chunk_region(r0):
        return jnp.where(r0 + rc_rows <= cnt_first, 1, jnp.where(r0 >= cnt_first, 2, 0))

    @pl.when(f == 0)
    def _gather():
        rsel_row = rsel_scr[pl.ds(e, 1), :]
        gate_row = gates_scr[pl.ds(e, 1), :]
        lane = lax.broadcasted_iota(I32, (tm, LANES), 1)
        rcol = jnp.sum(jnp.where(lane == e, rn_scr[...], 0.0), axis=1, keepdims=True)
        rcol_scr[...] = jnp.broadcast_to(rcol, (tm, LANES))

        def compact(r0, lo, hi):
            want = (r0 + 1 + lax.broadcasted_iota(I32, (rc_rows, hi - lo), 0)).astype(F32)
            pm = rsel_row[:, lo:hi] == want
            pb = jnp.where(pm, 1.0, 0.0).astype(BF16)
            xe_scr[pl.ds(r0, rc_rows), :] = jnp.dot(pb, h_scr[lo:hi, :],
                                                    preferred_element_type=F32).astype(BF16)
            gcol = jnp.sum(jnp.where(pm, gate_row[:, lo:hi], 0.0), axis=1, keepdims=True)
            gcol_scr[pl.ds(r0, rc_rows), :] = jnp.broadcast_to(gcol, (rc_rows, LANES))

        def body(rc, carry):
            r0 = pl.multiple_of(rc * rc_rows, rc_rows)
            for region, (lo, hi) in enumerate(token_regions):
                @pl.when(chunk_region(r0) == region)
                def _():
                    compact(r0, lo, hi)
            ye_scr[pl.ds(r0, rc_rows), :] = jnp.zeros((rc_rows, ye_scr.shape[1]), F32)
            return carry

        lax.fori_loop(0, n_rc, body, 0)

    def ffn_body(rc, carry):
        r0 = pl.multiple_of(rc * rc_rows, rc_rows)
        xe = xe_scr[pl.ds(r0, rc_rows), :]
        a = jnp.dot(xe, wg_ref[0], preferred_element_type=F32)
        u = jnp.dot(xe, wu_ref[0], preferred_element_type=F32)
        act = (_silu(a) * u * gcol_scr[pl.ds(r0, rc_rows), 0:1]).astype(BF16)
        ye_scr[pl.ds(r0, rc_rows), :] += jnp.dot(act, wd_ref[0], preferred_element_type=F32)
        return carry

    lax.fori_loop(0, n_rc, ffn_body, 0)

    @pl.when(f == pl.num_programs(3) - 1)
    def _scatter():
        def body(rc, carry):
            r0 = pl.multiple_of(rc * rc_rows, rc_rows)
            ye = ye_scr[pl.ds(r0, rc_rows), :].astype(BF16)
            for region, (lo, hi) in enumerate(token_regions):
                @pl.when(chunk_region(r0) == region)
                def _():
                    want = (r0 + 1 + lax.broadcasted_iota(I32, (hi - lo, rc_rows), 1)).astype(F32)
                    pt = jnp.where(rcol_scr[lo:hi, 0:1] == want, 1.0, 0.0).astype(BF16)
                    o_ref[0, lo:hi, :] += jnp.dot(pt, ye, preferred_element_type=F32)
            return carry

        lax.fori_loop(0, n_rc, body, 0)

    @pl.when((e == pl.num_programs(2) - 1) & (f == pl.num_programs(3) - 1))
    def _residual():
        for jb in range(ntb):
            rows = slice(jb * tb, (jb + 1) * tb)
            y = x_ref[0, rows, :] + gate_ref[0] * o_ref[0, rows, :]
            if final_norm:
                var = jnp.mean(y * y, axis=-1, keepdims=True)
                y = (y * lax.rsqrt(var + EPS)) * gfin_ref[...]
            o_ref[0, rows, :] = y


def _moe(x, sh, sc, gate, g, w_router, wg, wu, wd, tm, tf, g_final=None):
    bn, s, d = x.shape
    ne, _, fdim = wg.shape
    assert ne == N_EXPERTS and tm % MOE_RC == 0 and tm % MOE_TB == 0 and (tm // 2) % LANES == 0
    wr = jnp.zeros((d, LANES), F32).at[:, :ne].set(w_router)
    final_norm = g_final is not None
    gfin = g_final if final_norm else jnp.ones((1, d), F32)
    tok_in = pl.BlockSpec((1, tm, d), lambda b, t, e, f: (b, t, 0), pipeline_mode=pl.Buffered(1))
    tok_out = pl.BlockSpec((1, tm, d), lambda b, t, e, f: (b, t, 0), pipeline_mode=pl.Buffered(1))
    vec = pl.BlockSpec((1, 1, d), lambda b, t, e, f: (b, 0, 0))
    return pl.pallas_call(
        functools.partial(_moe_kernel, final_norm),
        grid=(bn, s // tm, ne, fdim // tf),
        in_specs=[tok_in, vec, vec, vec,
                  pl.BlockSpec((1, d), lambda b, t, e, f: (0, 0)),
                  pl.BlockSpec((1, d), lambda b, t, e, f: (0, 0)),
                  pl.BlockSpec((d, LANES), lambda b, t, e, f: (0, 0)),
                  pl.BlockSpec((1, d, tf), lambda b, t, e, f: (e, 0, f)),
                  pl.BlockSpec((1, d, tf), lambda b, t, e, f: (e, 0, f)),
                  pl.BlockSpec((1, tf, d), lambda b, t, e, f: (e, f, 0))],
        out_specs=tok_out,
        out_shape=jax.ShapeDtypeStruct((bn, s, d), F32),
        scratch_shapes=[pltpu.VMEM((tm, d), BF16),
                        pltpu.VMEM((tm, d), BF16),
                        pltpu.VMEM((tm, d), F32),
                        pltpu.VMEM((tm, LANES), F32),
                        pltpu.VMEM((ne, tm), F32),
                        pltpu.VMEM((ne, tm), F32),
                        pltpu.VMEM((tm, LANES), F32),
                        pltpu.VMEM((tm, LANES), F32),
                        pltpu.SMEM((2 * ne,), I32)],
        compiler_params=pltpu.CompilerParams(
            dimension_semantics=("parallel", "parallel", "arbitrary", "arbitrary"),
            vmem_limit_bytes=56 * 1024 * 1024),
    )(x, sh, sc, gate, g, gfin, wr, wg, wu, wd)


def _final_norm_kernel(x_ref, g_ref, o_ref):
    x = x_ref[0]
    var = jnp.mean(x * x, axis=-1, keepdims=True)
    o_ref[0] = (x * lax.rsqrt(var + EPS)) * g_ref[...]


def _final_norm(x, g, tm):
    bn, s, d = x.shape
    tok = pl.BlockSpec((1, tm, d), lambda b, t: (b, t, 0))
    return pl.pallas_call(
        _final_norm_kernel,
        grid=(bn, s // tm),
        in_specs=[tok, pl.BlockSpec((1, d), lambda b, t: (0, 0))],
        out_specs=tok,
        out_shape=jax.ShapeDtypeStruct((bn, s, d), F32),
        compiler_params=_cparams("parallel", "parallel"),
    )(x, g)


def _token_tile(s, want):
    tm = min(want, s)
    assert s % tm == 0 and tm % QB == 0
    return tm


def kernel(x, c, positions, w_ada, b_ada, g_norm_mix, g_norm_ffn, w_in, w_out, hgrn_lb_logits, hgrn_out_norm, gmlp_vnorm_g, gmlp_vnorm_b, gmlp_w_s, gmlp_b_s, ffn_w_gate, ffn_w_up, ffn_w_down, moe_w_router, moe_w_gate, moe_w_up, moe_w_down, g_final):
    bn, s, d = x.shape
    depth = w_in.shape[0]
    assert s % QB == 0 and d == sum(IN_WIDTHS[:1]) + B_DIM + C_DIM
    tm_proj = _token_tile(s, 512)
    tm_ffn = _token_tile(s, 1024)

    p_lb = jax.nn.softmax(hgrn_lb_logits.astype(F32), axis=0)
    cum = jnp.cumsum(p_lb, axis=0)
    lower_bounds = cum - cum[0:1]

    mod = _ada_mod(c, w_ada, b_ada)
    cosn, sinn, cost, sint = _rope_tables(positions)

    for l in range(depth):
        sh1, sc1, g1, sh2, sc2, g2 = [mod[l, :, i * d:(i + 1) * d].reshape(bn, 1, d) for i in range(6)]
        w_n, w_t = _prep_in_weights(w_in[l])
        (kz, ik3, bq, bf, bi, bg, c_out, qt, iq3t, vt4, iwt, bit) = _inproj(
            x, sh1, sc1, g_norm_mix[l].reshape(1, d), w_n, w_t, cosn, sinn, cost, sint,
            gmlp_vnorm_g[l].reshape(1, C_DIM), gmlp_vnorm_b[l].reshape(1, C_DIM), gmlp_w_s[l], gmlp_b_s[l],
            tm_proj)
        a_out = _dsa(ik3, kz, vt4, iq3t, iwt, qt)
        b_out = _hgrn(bq, bf, bi, bit, bg, lower_bounds[l].reshape(1, B_DIM),
                      jnp.tile(hgrn_out_norm[l], LANES // B_DV).reshape(1, LANES))
        gf = g_norm_ffn[l].reshape(1, d)
        if l % 2 == 0:
            i = l // 2
            x = _ffn(x, a_out, b_out, c_out, w_out[l], g1, sh2, sc2, g2, gf, ffn_w_gate[i].astype(BF16),
                     ffn_w_up[i].astype(BF16), ffn_w_down[i].astype(BF16), tm_ffn, 256)
        else:
            i = l // 2
            x = _outproj(x, a_out, b_out, c_out, w_out[l], g1, tm_proj)
            fused_final = l == depth - 1
            x = _moe(x, sh2, sc2, g2, gf, moe_w_router[i], moe_w_gate[i].astype(BF16),
                     moe_w_up[i].astype(BF16), moe_w_down[i].astype(BF16), _token_tile(s, 2048), 896,
                     g_final.reshape(1, d) if fused_final else None)
    if depth % 2 == 0:
        return x
    return _final_norm(x, g_final.reshape(1, d), tm_proj)
```

```python
import functools

import numpy as np
import jax
import jax.numpy as jnp
from jax import lax
from jax.experimental import pallas as pl
from jax.experimental.pallas import tpu as pltpu

F32 = jnp.float32
BF16 = jnp.bfloat16
I32 = jnp.int32
I16 = jnp.int16

HEAD_DIM = 64
A_HEADS = 6
A_DIM = A_HEADS * HEAD_DIM
IDX_HEADS = 4
IDX_DIM = 64
TOPK_MAX = 256
B_HEADS = 6
B_DK = 64
B_DV = 64
B_DIM = B_HEADS * B_DV
HGRN_CHUNK = 64
C_GROUPS = 4
C_GROUP_DIM = 64
C_DIM = C_GROUPS * C_GROUP_DIM
C_CHUNK = 128
ROPE_THETA = 10000.0
N_EXPERTS = 8
EPS = 1e-6
IN_WIDTHS = (A_DIM, HEAD_DIM, HEAD_DIM, IDX_HEADS * IDX_DIM, IDX_DIM, IDX_HEADS,
             B_DIM, B_DIM, B_DIM, B_DIM, 2 * C_DIM)

LANES = 128
KC = 128
QB = 2 * KC
PV_ROWS = HEAD_DIM + 16
VMEM_LIMIT = 48 * 1024 * 1024
HGRN_FAST_SPAN = 80.0
HGRN_SUB = 32
HGRN_GROUP = 4
INT_MIN = -2 ** 31

NT_DIMS = (((1,), (1,)), ((), ()))
TN_DIMS = (((0,), (0,)), ((), ()))


def _cparams(*sem):
    return pltpu.CompilerParams(dimension_semantics=sem, vmem_limit_bytes=VMEM_LIMIT)


def _split2(x):
    hi = x.astype(BF16)
    lo = (x - hi.astype(F32)).astype(BF16)
    return hi, lo


def _split3(x):
    hi = x.astype(BF16)
    r1 = x - hi.astype(F32)
    mid = r1.astype(BF16)
    lo = (r1 - mid.astype(F32)).astype(BF16)
    return hi, mid, lo


def _silu(x):
    return x * (1.0 / (1.0 + jnp.exp(-x)))


def _modulated_rmsnorm(x, g, sc, sh):
    var = jnp.mean(x * x, axis=-1, keepdims=True)
    y = x * lax.rsqrt(var + EPS)
    return (y * g) * (1.0 + sc) + sh


def _ada_kernel(c_ref, w_ref, b_ref, o_ref):
    cond = _silu(c_ref[...])
    o_ref[0] = jnp.dot(cond, w_ref[0], precision=lax.Precision.HIGHEST,
                       preferred_element_type=F32) + b_ref[0]


def _ada_mod(c, w_ada, b_ada):
    depth, d, d6 = w_ada.shape
    bn = c.shape[0]
    tn = 1536
    return pl.pallas_call(
        _ada_kernel,
        grid=(depth, d6 // tn),
        in_specs=[pl.BlockSpec((bn, d), lambda l, n: (0, 0)),
                  pl.BlockSpec((1, d, tn), lambda l, n: (l, 0, n)),
                  pl.BlockSpec((1, 1, tn), lambda l, n: (l, 0, n))],
        out_specs=pl.BlockSpec((1, bn, tn), lambda l, n: (l, 0, n)),
        out_shape=jax.ShapeDtypeStruct((depth, bn, d6), F32),
        compiler_params=_cparams("parallel", "parallel"),
    )(c, w_ada, b_ada.reshape(depth, 1, d6))


def _rope_kernel(posn_ref, post_ref, invn_ref, signn_ref, invt_ref,
                 cosn_ref, sinn_ref, cost_ref, sint_ref):
    ang_n = posn_ref[0].astype(F32) * invn_ref[...]
    cosn_ref[0] = jnp.cos(ang_n)
    sinn_ref[0] = jnp.sin(ang_n) * signn_ref[...]
    ang_t = invt_ref[...] * post_ref[0].astype(F32)
    cost_ref[0] = jnp.cos(ang_t)
    sint_ref[0] = jnp.sin(ang_t)


def _rope_tables(positions):
    bn, s = positions.shape
    half = HEAD_DIM // 2
    inv = ROPE_THETA ** (-jnp.arange(0, HEAD_DIM, 2, dtype=F32) / HEAD_DIM)
    inv_n = jnp.tile(inv, LANES // half).reshape(1, LANES)
    sign_n = jnp.tile(jnp.concatenate([-jnp.ones((half,), F32), jnp.ones((half,), F32)]),
                      LANES // HEAD_DIM).reshape(1, LANES)
    inv_t = inv.reshape(half, 1)
    full = lambda shape: pl.BlockSpec(shape, lambda b: (0,) * len(shape))
    return pl.pallas_call(
        _rope_kernel,
        grid=(bn,),
        in_specs=[pl.BlockSpec((1, s, 1), lambda b: (b, 0, 0)),
                  pl.BlockSpec((1, 1, s), lambda b: (b, 0, 0)),
                  full((1, LANES)), full((1, LANES)), full((half, 1))],
        out_specs=[pl.BlockSpec((1, s, LANES), lambda b: (b, 0, 0)),
                   pl.BlockSpec((1, s, LANES), lambda b: (b, 0, 0)),
                   pl.BlockSpec((1, half, s), lambda b: (b, 0, 0)),
                   pl.BlockSpec((1, half, s), lambda b: (b, 0, 0))],
        out_shape=[jax.ShapeDtypeStruct((bn, s, LANES), F32),
                   jax.ShapeDtypeStruct((bn, s, LANES), F32),
                   jax.ShapeDtypeStruct((bn, half, s), F32),
                   jax.ShapeDtypeStruct((bn, half, s), F32)],
        compiler_params=_cparams("parallel"),
    )(positions.reshape(bn, s, 1), positions.reshape(bn, 1, s), inv_n, sign_n, inv_t)


N_KI = 0
N_BQ = 2 * LANES
N_BF = N_BQ + B_DIM
N_BI = N_BF + B_DIM
N_BG = N_BI + B_DIM
N_CUV = N_BG + B_DIM
N_COLS = N_CUV + 2 * C_DIM
T_Q = 0
T_IQ = A_DIM
T_V = T_IQ + IDX_HEADS * IDX_DIM
T_IW = T_V + HEAD_DIM
T_BI = T_IW + 8
T_ROWS = T_BI + B_DIM


def _prep_in_weights(w_in_l):
    offs = np.concatenate([[0], np.cumsum(IN_WIDTHS)])
    sl = lambda i: w_in_l[:, int(offs[i]):int(offs[i + 1])]
    aq, ak, av, iq, ik, iw, bq, bf, bi, bg, cuv = [sl(i) for i in range(11)]
    half = HEAD_DIM // 2
    rot = lambda w: jnp.concatenate([w[:, half:], w[:, :half]], axis=1)
    w_n = jnp.concatenate([ak, ik, rot(ak), rot(ik), bq, bf, bi, bg, cuv], axis=1)
    w_t = jnp.concatenate([aq, iq, av, iw, jnp.zeros((w_in_l.shape[0], 4), w_in_l.dtype), bi], axis=1).T
    return w_n.astype(BF16), w_t.astype(BF16)


def _inproj_kernel(x_ref, sh_ref, sc_ref, g_ref, wn_ref, wt_ref, cosn_ref, sinn_ref, cost_ref, sint_ref,
                   vg_ref, vb_ref, ws_ref, cbias_ref,
                   kz_ref, ik3_ref, bq_ref, bf_ref, bi_ref, bg_ref, c_ref,
                   qt_ref, iq3t_ref, vt_ref, iwt_ref, bit_ref):
    tm = x_ref.shape[1]
    half = HEAD_DIM // 2
    h = _modulated_rmsnorm(x_ref[0], g_ref[...], sc_ref[0], sh_ref[0]).astype(BF16)

    cosn = cosn_ref[0]
    sinn = sinn_ref[0]
    zk = jnp.dot(h, wn_ref[:, N_KI:N_KI + 2 * LANES], preferred_element_type=F32)
    ki = zk[:, :LANES] * cosn + zk[:, LANES:] * sinn
    lane = lax.broadcasted_iota(I32, (tm, LANES), 1)
    kz_ref[0] = jnp.where(lane < HEAD_DIM, ki, 0.0).astype(BF16)
    ik2 = jnp.where(lane < IDX_DIM, pltpu.roll(ki, HEAD_DIM, axis=1), ki)
    hi, lo = _split2(ik2)
    ik3_ref[0, :, :LANES] = jnp.where(lane < IDX_DIM, hi, lo)
    ik3_ref[0, :, LANES:] = hi
    bq_ref[0] = jnp.dot(h, wn_ref[:, N_BQ:N_BF], preferred_element_type=F32).astype(BF16)
    bf_ref[0] = jnp.dot(h, wn_ref[:, N_BF:N_BI], preferred_element_type=F32)
    bi_ref[0] = jnp.dot(h, wn_ref[:, N_BI:N_BG], preferred_element_type=F32).astype(BF16)
    bg_ref[0] = jnp.dot(h, wn_ref[:, N_BG:N_CUV], preferred_element_type=F32).astype(BF16)
    cuv = jnp.dot(h, wn_ref[:, N_CUV:N_COLS], preferred_element_type=F32)
    c_ref[0] = _gmlp_mix(cuv, vg_ref[...], vb_ref[...], ws_ref, cbias_ref[...]).astype(BF16)

    zt = lax.dot_general(wt_ref[...], h, NT_DIMS, preferred_element_type=F32)
    cost = cost_ref[0]
    sint = sint_ref[0]
    qscale = HEAD_DIM ** -0.5
    for hh in range(A_HEADS):
        r0 = T_Q + hh * HEAD_DIM
        x1 = zt[r0:r0 + half]
        x2 = zt[r0 + half:r0 + HEAD_DIM]
        qt_ref[0, hh * HEAD_DIM:hh * HEAD_DIM + half] = ((x1 * cost - x2 * sint) * qscale).astype(BF16)
        qt_ref[0, hh * HEAD_DIM + half:(hh + 1) * HEAD_DIM] = ((x2 * cost + x1 * sint) * qscale).astype(BF16)
    iscale = IDX_DIM ** -0.5
    zero = jnp.zeros((IDX_DIM, tm), BF16)
    for hh in range(IDX_HEADS):
        r0 = T_IQ + hh * IDX_DIM
        x1 = zt[r0:r0 + half]
        x2 = zt[r0 + half:r0 + IDX_DIM]
        y = jnp.concatenate([(x1 * cost - x2 * sint) * iscale, (x2 * cost + x1 * sint) * iscale], axis=0)
        hi, lo = _split2(y)
        iq3t_ref[0, hh, 0 * IDX_DIM:1 * IDX_DIM] = hi
        iq3t_ref[0, hh, 1 * IDX_DIM:2 * IDX_DIM] = hi
        iq3t_ref[0, hh, 2 * IDX_DIM:3 * IDX_DIM] = lo
        iq3t_ref[0, hh, 3 * IDX_DIM:4 * IDX_DIM] = zero
    vt = zt[T_V:T_V + HEAD_DIM].astype(BF16)
    for i in range(tm // KC):
        vt_ref[0, i] = vt[:, i * KC:(i + 1) * KC]
    iwt_ref[0] = zt[T_IW:T_IW + 8] * (IDX_HEADS ** -0.5)
    bit = zt[T_BI:T_BI + B_DIM].astype(BF16)
    for i in range(tm // LANES):
        bit_ref[0, i] = bit[:, i * LANES:(i + 1) * LANES]


def _inproj(x, sh, sc, g, w_n, w_t, cosn, sinn, cost, sint, vg, vb, ws, bs, tm):
    bn, s, d = x.shape
    assert tm % C_CHUNK == 0
    nt = s // tm
    half = HEAD_DIM // 2
    cbias = jnp.repeat(bs.T, C_GROUP_DIM, axis=1)
    tok = lambda w: pl.BlockSpec((1, tm, w), lambda b, t: (b, t, 0))
    vec = pl.BlockSpec((1, 1, d), lambda b, t: (b, 0, 0))
    full2 = lambda a: pl.BlockSpec(a.shape, lambda b, t: (0, 0))
    out_shapes = [
        jax.ShapeDtypeStruct((bn, s, LANES), BF16),
        jax.ShapeDtypeStruct((bn, s, 2 * LANES), BF16),
        jax.ShapeDtypeStruct((bn, s, B_DIM), BF16),
        jax.ShapeDtypeStruct((bn, s, B_DIM), F32),
        jax.ShapeDtypeStruct((bn, s, B_DIM), BF16),
        jax.ShapeDtypeStruct((bn, s, B_DIM), BF16),
        jax.ShapeDtypeStruct((bn, s, C_DIM), BF16),
        jax.ShapeDtypeStruct((bn, A_DIM, s), BF16),
        jax.ShapeDtypeStruct((bn, IDX_HEADS, 4 * IDX_DIM, s), BF16),
        jax.ShapeDtypeStruct((bn, s // KC, HEAD_DIM, KC), BF16),
        jax.ShapeDtypeStruct((bn, 8, s), F32),
        jax.ShapeDtypeStruct((bn, s // LANES, B_DIM, LANES), BF16),
    ]
    out_specs = [
        tok(LANES), tok(2 * LANES), tok(B_DIM), tok(B_DIM), tok(B_DIM), tok(B_DIM), tok(C_DIM),
        pl.BlockSpec((1, A_DIM, tm), lambda b, t: (b, 0, t)),
        pl.BlockSpec((1, IDX_HEADS, 4 * IDX_DIM, tm), lambda b, t: (b, 0, 0, t)),
        pl.BlockSpec((1, tm // KC, HEAD_DIM, KC), lambda b, t: (b, t, 0, 0)),
        pl.BlockSpec((1, 8, tm), lambda b, t: (b, 0, t)),
        pl.BlockSpec((1, tm // LANES, B_DIM, LANES), lambda b, t: (b, t, 0, 0)),
    ]
    return pl.pallas_call(
        _inproj_kernel,
        grid=(bn, nt),
        in_specs=[tok(d), vec, vec, full2(g), full2(w_n), full2(w_t),
                  tok(LANES), tok(LANES),
                  pl.BlockSpec((1, half, tm), lambda b, t: (b, 0, t)),
                  pl.BlockSpec((1, half, tm), lambda b, t: (b, 0, t)),
                  full2(vg), full2(vb), pl.BlockSpec(ws.shape, lambda b, t: (0, 0, 0)), full2(cbias)],
        out_specs=out_specs,
        out_shape=out_shapes,
        compiler_params=_cparams("parallel", "parallel"),
    )(x, sh, sc, g, w_n, w_t, cosn, sinn, cost, sint, vg, vb, ws, cbias)


def _dsa_kernel(n_top, ik3_ref, kz_ref, vt_ref, iq3t_ref, iwt_ref, qt_ref, out_ref,
                key_scr, hi_scr, lo_scr, lg_scr, acc_scr):
    j = pl.program_id(1)
    npair = j + 1
    pair = 2 * KC
    row = lax.broadcasted_iota(I32, (KC, QB), 0)
    col = lax.broadcasted_iota(I32, (KC, QB), 1)
    int_min = jnp.int32(INT_MIN)
    i16_min = jnp.int16(-2 ** 15)

    def score_pair(p, diagonal):
        for u in range(2):
            ks = pl.multiple_of(p * pair + u * KC, KC)
            ikc = ik3_ref[0, pl.ds(ks, KC), :]
            sc = jnp.zeros((KC, QB), F32)
            for hh in range(IDX_HEADS):
                rel = jnp.dot(ikc, iq3t_ref[0, hh], preferred_element_type=F32)
                sc = sc + jnp.maximum(rel, 0.0) * iwt_ref[0, hh:hh + 1, :]
            sc = jnp.where(sc == 0.0, 0.0, sc)
            bits = pltpu.bitcast(sc, I32)
            key = bits ^ ((bits >> 31) & jnp.int32(0x7FFFFFFF))
            if diagonal:
                key = jnp.where((u * KC + row) <= col, key, int_min)
            key_scr[pl.ds(ks, KC), :] = key
            hi_scr[pl.ds(ks, KC), :] = (key >> 16).astype(I16)
            lo_scr[pl.ds(ks, KC), :] = ((key & jnp.int32(0xFFFF)) - 32768).astype(I16)

    def score_body(p, carry):
        score_pair(p, False)
        return carry

    lax.fori_loop(0, j, score_body, 0)
    score_pair(j, True)

    def count16(ref, pred_fn):
        def body(p, acc):
            ks = pl.multiple_of(p * pair, pair)
            m = jnp.where(pred_fn(ref[pl.ds(ks, pair), :]), jnp.int16(1), jnp.int16(0))
            parts = [m[16 * i:16 * (i + 1)] for i in range(pair // 16)]
            while len(parts) > 1:
                parts = [parts[i] + parts[i + 1] for i in range(0, len(parts), 2)]
            return acc + parts[0]
        acc = lax.fori_loop(0, npair, body, jnp.zeros((16, QB), I16))
        return jnp.sum(acc.astype(I32), axis=0, keepdims=True)

    def bisect16(ref, k_needed):
        def bit_body(i, t_u):
            cand = t_u | jnp.left_shift(jnp.int32(1), 15 - i)
            cand16 = (cand - 32768).astype(I16)
            cnt = count16(ref, lambda x: x >= cand16)
            return jnp.where(cnt >= k_needed, cand, t_u)
        return lax.fori_loop(0, 16, bit_body, jnp.zeros((1, QB), I32))

    t_hi = bisect16(hi_scr, n_top)
    t_hi16 = (t_hi - 32768).astype(I16)
    n_hi_gt = count16(hi_scr, lambda x: x > t_hi16)

    def bucket_body(p, carry):
        ks = pl.multiple_of(p * pair, pair)
        in_bucket = hi_scr[pl.ds(ks, pair), :] == t_hi16
        lo_scr[pl.ds(ks, pair), :] = jnp.where(in_bucket, lo_scr[pl.ds(ks, pair), :], i16_min)
        return carry

    lax.fori_loop(0, npair, bucket_body, 0)
    t_lo = bisect16(lo_scr, n_top - n_hi_gt)
    t_lo16 = (t_lo - 32768).astype(I16)
    n_gt = n_hi_gt + count16(lo_scr, lambda x: x > t_lo16)
    thr = jnp.left_shift(t_hi - 32768, 16) | t_lo
    n_tie = (n_top - n_gt).astype(F32)
    thr_valid = jnp.where(thr > int_min, 1.0, 0.0)

    tr = lax.broadcasted_iota(I32, (KC, KC), 0)
    tc = lax.broadcasted_iota(I32, (KC, KC), 1)
    ltri = jnp.where(tr >= tc, 1.0, 0.0).astype(BF16)
    zpad = jnp.zeros((HEAD_DIM, QB), BF16)
    qpad = [jnp.concatenate([qt_ref[0, hh * HEAD_DIM:(hh + 1) * HEAD_DIM, :], zpad], axis=0)
            for hh in range(A_HEADS)]
    neg_inf = jnp.float32(-jnp.inf)

    def pass_a(p, carry):
        tie_cnt, ms = carry
        ms = list(ms)
        for u in range(2):
            ks = pl.multiple_of(p * pair + u * KC, KC)
            kc = key_scr[pl.ds(ks, KC), :]
            eqf = jnp.where(kc == thr, thr_valid, 0.0)
            pref = jnp.dot(ltri, eqf.astype(BF16), preferred_element_type=F32)
            keep_tie = eqf * jnp.where((tie_cnt + pref) <= n_tie, 1.0, 0.0)
            sel = jnp.where(kc > thr, 1.0, keep_tie) > 0.5
            tie_cnt = tie_cnt + pref[KC - 1:KC, :]
            kzc = kz_ref[0, pl.ds(ks, KC), :]
            for hh in range(A_HEADS):
                lt = jnp.dot(kzc, qpad[hh], preferred_element_type=F32)
                lt = jnp.where(sel, lt, neg_inf)
                lg_scr[hh, pl.ds(ks, KC), :] = lt
                ms[hh] = jnp.maximum(ms[hh], jnp.max(lt, axis=0, keepdims=True))
        return tie_cnt, tuple(ms)

    init_m = tuple(jnp.full((1, QB), neg_inf, F32) for _ in range(A_HEADS))
    _, ms = lax.fori_loop(0, npair, pass_a, (jnp.zeros((1, QB), F32), init_m))

    acc_scr[...] = jnp.zeros_like(acc_scr)
    ones_rows = jnp.ones((PV_ROWS - HEAD_DIM, pair), BF16)

    def pass_b(p, carry):
        ks = pl.multiple_of(p * pair, pair)
        vt2 = jnp.concatenate([vt_ref[0, 2 * p], vt_ref[0, 2 * p + 1]], axis=1)
        vt2 = jnp.concatenate([vt2, ones_rows], axis=0)
        for hh in range(A_HEADS):
            pr = jnp.exp(lg_scr[hh, pl.ds(ks, pair), :] - ms[hh])
            acc_scr[hh * PV_ROWS:(hh + 1) * PV_ROWS, :] += jnp.dot(
                vt2, pr.astype(BF16), preferred_element_type=F32)
        return carry

    lax.fori_loop(0, npair, pass_b, 0)
    o_t = jnp.concatenate(
        [acc_scr[hh * PV_ROWS:hh * PV_ROWS + HEAD_DIM, :]
         / acc_scr[hh * PV_ROWS + HEAD_DIM:hh * PV_ROWS + HEAD_DIM + 1, :] for hh in range(A_HEADS)], axis=0)
    out_ref[0] = o_t.T.astype(BF16)


def _dsa(ik3, kz, vt4, iq3t, iwt, qt):
    bn, s, _ = kz.shape
    assert s % QB == 0
    n_top = min(TOPK_MAX, s // 4)
    nq = s // QB
    return pl.pallas_call(
        functools.partial(_dsa_kernel, n_top),
        grid=(bn, nq),
        in_specs=[pl.BlockSpec((1, s, 2 * LANES), lambda b, q: (b, 0, 0)),
                  pl.BlockSpec((1, s, LANES), lambda b, q: (b, 0, 0)),
                  pl.BlockSpec((1, s // KC, HEAD_DIM, KC), lambda b, q: (b, 0, 0, 0)),
                  pl.BlockSpec((1, IDX_HEADS, 4 * IDX_DIM, QB), lambda b, q: (b, 0, 0, q)),
                  pl.BlockSpec((1, 8, QB), lambda b, q: (b, 0, q)),
                  pl.BlockSpec((1, A_DIM, QB), lambda b, q: (b, 0, q))],
        out_specs=pl.BlockSpec((1, QB, A_DIM), lambda b, q: (b, q, 0)),
        out_shape=jax.ShapeDtypeStruct((bn, s, A_DIM), BF16),
        scratch_shapes=[pltpu.VMEM((s, QB), I32), pltpu.VMEM((s, QB), I16), pltpu.VMEM((s, QB), I16),
                        pltpu.VMEM((A_HEADS, s, QB), F32), pltpu.VMEM((A_HEADS * PV_ROWS, QB), F32)],
        compiler_params=_cparams("parallel", "arbitrary"),
    )(ik3, kz, vt4, iq3t, iwt, qt)


def _hgrn_kernel(q_ref, f_ref, i_ref, it_ref, g_ref, lb_ref, gn_ref, o_ref, st_scr, oi_scr):
    s = q_ref.shape[1]
    ch = HGRN_CHUNK
    sb = HGRN_SUB
    nsb = ch // sb
    rows = HGRN_GROUP * ch
    wins_per_group = rows // LANES
    lane = lax.broadcasted_iota(I32, (ch, LANES), 1)
    head0 = lane < B_DK
    lr = lax.broadcasted_iota(I32, (rows, 3 * rows), 0)
    lc = lax.broadcasted_iota(I32, (rows, 3 * rows), 1) % rows
    lcum = jnp.where(((lr // ch) == (lc // ch)) & (lc <= lr), 1.0, 0.0).astype(BF16)
    att_rows = HGRN_GROUP * 2 * ch
    att_cols = HGRN_GROUP * nsb * ch
    ar = lax.broadcasted_iota(I32, (att_rows, att_cols), 0)
    ac = lax.broadcasted_iota(I32, (att_rows, att_cols), 1)
    at = ar % ch
    att_mask = (((ar // (2 * ch)) == (ac // (nsb * ch))) & ((at // sb) == ((ac // ch) % nsb))
                & ((ac % ch) <= at))
    bd_r = lax.broadcasted_iota(I32, (LANES, LANES), 0)
    bd_c = lax.broadcasted_iota(I32, (LANES, LANES), 1)
    same_head = (bd_r < B_DK) == (bd_c < B_DK)
    ones_bd = jnp.where(same_head, 1.0, 0.0).astype(BF16)
    ones_bd2 = jnp.concatenate([ones_bd, ones_bd], axis=0)
    s_iota = lax.broadcasted_iota(I32, (ch, LANES), 0)

    lb = lb_ref[...]
    log_lb = jnp.log(lb)
    log_1mlb = jnp.log1p(-lb)
    gn = gn_ref[...]
    st_scr[...] = jnp.zeros_like(st_scr)

    def group_body(it, carry):
        t0 = pl.multiple_of(it * rows, rows)
        z = f_ref[0, pl.ds(t0, rows), :]
        q = q_ref[0, pl.ds(t0, rows), :].astype(F32)
        v = i_ref[0, pl.ds(t0, rows), :]
        softplus_tail = jnp.log1p(jnp.exp(-jnp.abs(z)))
        log_sig = -(jnp.maximum(-z, 0.0) + softplus_tail)
        x2 = log_1mlb + log_sig
        amax = jnp.maximum(log_lb, x2)
        log_f = amax + jnp.log1p(jnp.exp(-jnp.abs(log_lb - x2)))
        kk = (1.0 - lb) * jnp.exp(-(jnp.maximum(z, 0.0) + softplus_tail))
        f_hi, f_mid, f_lo = _split3(log_f)
        b = jnp.dot(lcum, jnp.concatenate([f_hi, f_mid, f_lo], axis=0), preferred_element_type=F32)
        ref_parts = []
        for c in range(HGRN_GROUP):
            ref_parts.append(jnp.zeros((sb, LANES), F32))
            for i in range(1, nsb):
                r = c * ch + i * sb
                ref_parts.append(jnp.broadcast_to(b[r - 1:r], (sb, LANES)))
        ref = jnp.concatenate(ref_parts, axis=0)
        q_loc = q * jnp.exp(b - ref)
        q_chk = (q * jnp.exp(b)).astype(BF16)
        span = ref - b
        span_max = jnp.max(span)

        k_parts, q_parts, v_parts, kh_cols, decay = [], [], [], [], []
        for c in range(HGRN_GROUP):
            lo_r = c * ch
            b_c = b[lo_r:lo_r + ch]
            kk_c = kk[lo_r:lo_r + ch]
            b_last = b_c[ch - 1:ch]
            for i in range(nsb):
                r = lo_r + i * sb
                k_parts.append(kk_c * jnp.exp(jnp.minimum(ref[r:r + 1] - b_c, HGRN_FAST_SPAN)))
            q_c = q_loc[lo_r:lo_r + ch]
            q_parts += [jnp.where(head0, q_c, 0.0), jnp.where(head0, 0.0, q_c)]
            v_parts += [v[lo_r:lo_r + ch]] * nsb
            kh = (kk_c * jnp.exp(b_last - b_c)).astype(BF16)
            kh_cols.append(jnp.concatenate(
                ([jnp.zeros((lo_r, LANES), BF16)] if lo_r else []) + [kh]
                + ([jnp.zeros((rows - lo_r - ch, LANES), BF16)] if rows - lo_r - ch else []), axis=0))
            decay.append(jnp.exp(b_last))
        att = lax.dot_general(jnp.concatenate(q_parts, axis=0).astype(BF16),
                              jnp.concatenate(k_parts, axis=0).astype(BF16), NT_DIMS,
                              preferred_element_type=F32)
        att = jnp.where(att_mask, att, 0.0).astype(BF16)
        o2 = jnp.dot(att, jnp.concatenate(v_parts, axis=0), preferred_element_type=F32)
        for c in range(HGRN_GROUP):
            r = c * 2 * ch
            oi_scr[c * ch:(c + 1) * ch, :] = jnp.where(head0, o2[r:r + ch], o2[r + ch:r + 2 * ch])

        it_win = jnp.concatenate([it_ref[0, wins_per_group * it + w] for w in range(wins_per_group)], axis=1)
        upd = jnp.dot(it_win, jnp.concatenate(kh_cols, axis=1), preferred_element_type=F32)
        st = st_scr[...]
        states = []
        for c in range(HGRN_GROUP):
            states.append(st.astype(BF16))
            st = st * decay[c] + jnp.where(same_head, upd[:, c * LANES:(c + 1) * LANES], 0.0)
        st_scr[...] = st
        oi_all = lax.dot_general(q_chk, jnp.concatenate(states, axis=0), NT_DIMS,
                                 preferred_element_type=F32)
        o_inter = [oi_all[c * ch:(c + 1) * ch, c * LANES:(c + 1) * LANES] for c in range(HGRN_GROUP)]

        @pl.when(span_max > HGRN_FAST_SPAN)
        def _():
            for c in range(HGRN_GROUP):
                lo_r = c * ch

                @pl.when(jnp.max(span[lo_r:lo_r + ch]) > HGRN_FAST_SPAN)
                def _():
                    b_c = b[lo_r:lo_r + ch]
                    q_c = q[lo_r:lo_r + ch]
                    kk_c = kk[lo_r:lo_r + ch]
                    vf = v[lo_r:lo_r + ch].astype(F32)

                    def t_body(t, carry2):
                        onehot = jnp.where(s_iota == t, 1.0, 0.0)
                        b_t = jnp.sum(onehot * b_c, axis=0, keepdims=True)
                        q_t = jnp.sum(onehot * q_c, axis=0, keepdims=True)
                        dec = jnp.exp(jnp.where(s_iota <= t, b_t - b_c, -jnp.inf))
                        w = q_t * kk_c * dec
                        w0 = jnp.sum(jnp.where(head0, w, 0.0), axis=1, keepdims=True)
                        w1 = jnp.sum(jnp.where(head0, 0.0, w), axis=1, keepdims=True)
                        a_col = jnp.where(head0, w0, w1)
                        oi_scr[pl.ds(lo_r + t, 1), :] = jnp.sum(a_col * vf, axis=0, keepdims=True)
                        return carry2

                    lax.fori_loop(0, ch, t_body, 0)

        o = oi_scr[...] + jnp.concatenate(o_inter, axis=0)
        sq_hi, sq_lo = _split2(o * o)
        ss = jnp.dot(jnp.concatenate([sq_hi, sq_lo], axis=1), ones_bd2, preferred_element_type=F32)
        y = (o * lax.rsqrt(ss * (1.0 / B_DV) + EPS)) * gn
        g = g_ref[0, pl.ds(t0, rows), :].astype(F32)
        o_ref[0, pl.ds(t0, rows), :] = (y * _silu(g)).astype(o_ref.dtype)
        return carry

    lax.fori_loop(0, s // rows, group_body, 0)


def _hgrn(bq, bf, bi, bit, bg, lb, gn):
    bn, s, _ = bq.shape
    assert s % (HGRN_GROUP * HGRN_CHUNK) == 0 and 2 * HGRN_CHUNK == LANES and HGRN_GROUP % 2 == 0
    npair = B_DIM // LANES
    tok = pl.BlockSpec((1, s, LANES), lambda b, p: (b, 0, p))
    return pl.pallas_call(
        _hgrn_kernel,
        grid=(bn, npair),
        in_specs=[tok, tok, tok,
                  pl.BlockSpec((1, s // LANES, LANES, LANES), lambda b, p: (b, 0, p, 0)),
                  tok,
                  pl.BlockSpec((1, LANES), lambda b, p: (0, p)),
                  pl.BlockSpec((1, LANES), lambda b, p: (0, 0))],
        out_specs=tok,
        out_shape=jax.ShapeDtypeStruct((bn, s, B_DIM), BF16),
        scratch_shapes=[pltpu.VMEM((LANES, LANES), F32),
                        pltpu.VMEM((HGRN_GROUP * HGRN_CHUNK, LANES), F32)],
        compiler_params=_cparams("parallel", "parallel"),
    )(bq, bf, bi, bit, bg, lb, gn)


def _gmlp_mix(uv, vg, vb, ws_ref, bias):
    uv = 0.5 * uv * (1.0 + lax.erf(uv * (2.0 ** -0.5)))
    u = uv[:, :C_DIM]
    v = uv[:, C_DIM:]
    mu = jnp.mean(v, axis=-1, keepdims=True)
    var = jnp.mean(jnp.square(v - mu), axis=-1, keepdims=True)
    vn_b = (((v - mu) * lax.rsqrt(var + EPS)) * vg + vb).astype(BF16)
    r_t = lax.broadcasted_iota(I32, (C_CHUNK, C_CHUNK), 0)
    r_s = lax.broadcasted_iota(I32, (C_CHUNK, C_CHUNK), 1)
    group = lax.broadcasted_iota(I32, (C_CHUNK, C_DIM), 1) // C_GROUP_DIM
    ws = [jnp.where(r_t >= r_s, ws_ref[gi], 0.0).astype(BF16) for gi in range(C_GROUPS)]
    outs = []
    for c in range(uv.shape[0] // C_CHUNK):
        sl = slice(c * C_CHUNK, (c + 1) * C_CHUNK)
        mixed = jnp.zeros((C_CHUNK, C_DIM), F32)
        for gi in range(C_GROUPS):
            m = jnp.dot(ws[gi], vn_b[sl], preferred_element_type=F32)
            mixed = jnp.where(group == gi, m, mixed)
        outs.append(u[sl] * (mixed + bias))
    return jnp.concatenate(outs, axis=0)


def _outproj_kernel(x_ref, a_ref, b_ref, c_ref, wa_ref, wb_ref, wc_ref, g1_ref, o_ref):
    mix = (jnp.dot(a_ref[0], wa_ref[...], preferred_element_type=F32)
           + jnp.dot(b_ref[0], wb_ref[...], preferred_element_type=F32)
           + jnp.dot(c_ref[0], wc_ref[...], preferred_element_type=F32))
    o_ref[0] = x_ref[0] + g1_ref[0] * mix


def _outproj(x, a, b, c, w_out_l, g1, tm):
    bn, s, d = x.shape
    wa = w_out_l[:A_DIM].astype(BF16)
    wb = w_out_l[A_DIM:A_DIM + B_DIM].astype(BF16)
    wc = w_out_l[A_DIM + B_DIM:].astype(BF16)
    tok = lambda w: pl.BlockSpec((1, tm, w), lambda bb, t: (bb, t, 0))
    full = lambda arr: pl.BlockSpec(arr.shape, lambda bb, t: (0, 0))
    return pl.pallas_call(
        _outproj_kernel,
        grid=(bn, s // tm),
        in_specs=[tok(d), tok(A_DIM), tok(B_DIM), tok(C_DIM), full(wa), full(wb), full(wc),
                  pl.BlockSpec((1, 1, d), lambda bb, t: (bb, 0, 0))],
        out_specs=tok(d),
        out_shape=jax.ShapeDtypeStruct((bn, s, d), F32),
        compiler_params=_cparams("parallel", "parallel"),
    )(x, a, b, c, wa, wb, wc, g1)


def _ffn_kernel(x_ref, a_ref, b_ref, c_ref, wa_ref, wb_ref, wc_ref, g1_ref, sh_ref, sc_ref, gate_ref, g_ref,
                wg_ref, wu_ref, wd_ref, o_ref, xn_scr, h_scr, acc_scr):
    f = pl.program_id(2)

    @pl.when(f == 0)
    def _():
        mix = (jnp.dot(a_ref[0], wa_ref[...], preferred_element_type=F32)
               + jnp.dot(b_ref[0], wb_ref[...], preferred_element_type=F32)
               + jnp.dot(c_ref[0], wc_ref[...], preferred_element_type=F32))
        xn = x_ref[0] + g1_ref[0] * mix
        xn_scr[...] = xn
        h_scr[...] = _modulated_rmsnorm(xn, g_ref[...], sc_ref[0], sh_ref[0]).astype(BF16)
        acc_scr[...] = jnp.zeros_like(acc_scr)

    h = h_scr[...]
    a = jnp.dot(h, wg_ref[...], preferred_element_type=F32)
    u = jnp.dot(h, wu_ref[...], preferred_element_type=F32)
    act = (_silu(a) * u).astype(BF16)
    acc_scr[...] += jnp.dot(act, wd_ref[...], preferred_element_type=F32)

    @pl.when(f == pl.num_programs(2) - 1)
    def _():
        o_ref[0] = xn_scr[...] + gate_ref[0] * acc_scr[...]


def _ffn(x, a, b, c, w_out_l, g1, sh, sc, gate, g, wg, wu, wd, tm, tf):
    bn, s, d = x.shape
    fdim = wg.shape[1]
    wa = w_out_l[:A_DIM].astype(BF16)
    wb = w_out_l[A_DIM:A_DIM + B_DIM].astype(BF16)
    wc = w_out_l[A_DIM + B_DIM:].astype(BF16)
    tok = pl.BlockSpec((1, tm, d), lambda b, t, f: (b, t, 0))
    tokw = lambda w: pl.BlockSpec((1, tm, w), lambda b, t, f: (b, t, 0))
    full = lambda arr: pl.BlockSpec(arr.shape, lambda b, t, f: (0, 0))
    vec = pl.BlockSpec((1, 1, d), lambda b, t, f: (b, 0, 0))
    return pl.pallas_call(
        _ffn_kernel,
        grid=(bn, s // tm, fdim // tf),
        in_specs=[tok, tokw(A_DIM), tokw(B_DIM), tokw(C_DIM), full(wa), full(wb), full(wc), vec,
                  vec, vec, vec,
                  pl.BlockSpec((1, d), lambda b, t, f: (0, 0)),
                  pl.BlockSpec((d, tf), lambda b, t, f: (0, f)),
                  pl.BlockSpec((d, tf), lambda b, t, f: (0, f)),
                  pl.BlockSpec((tf, d), lambda b, t, f: (f, 0))],
        out_specs=tok,
        out_shape=jax.ShapeDtypeStruct((bn, s, d), F32),
        scratch_shapes=[pltpu.VMEM((tm, d), F32), pltpu.VMEM((tm, d), BF16), pltpu.VMEM((tm, d), F32)],
        compiler_params=_cparams("parallel", "parallel", "arbitrary"),
    )(x, a, b, c, wa, wb, wc, g1, sh, sc, gate, g, wg, wu, wd)


MOE_RC = 256
MOE_TB = 512


def _moe_kernel(final_norm, x_ref, sh_ref, sc_ref, gate_ref, g_ref, gfin_ref, wr_ref, wg_ref, wu_ref, wd_ref, o_ref,
                h_scr, xe_scr, ye_scr, gcol_scr, gates_scr, rsel_scr, rn_scr, rcol_scr, cnt_smem):
    e = pl.program_id(2)
    f = pl.program_id(3)
    tm = x_ref.shape[1]
    rc_rows = MOE_RC
    tb = MOE_TB
    ntb = tm // tb
    neg_inf = jnp.float32(-jnp.inf)

    @pl.when((e == 0) & (f == 0))
    def _route():
        for jb in range(ntb):
            rows = slice(jb * tb, (jb + 1) * tb)
            h = _modulated_rmsnorm(x_ref[0, rows, :], g_ref[...], sc_ref[0], sh_ref[0])
            h_scr[rows, :] = h.astype(BF16)
            rcol_scr[rows, :] = jnp.dot(h, wr_ref[...], precision=lax.Precision.HIGHEST,
                                        preferred_element_type=F32)
            o_ref[0, rows, :] = jnp.zeros((tb, o_ref.shape[2]), F32)
        logits = rcol_scr[...].T[:N_EXPERTS]
        ridx = lax.broadcasted_iota(I32, (N_EXPERTS, tm), 0)
        m1 = jnp.max(logits, axis=0, keepdims=True)
        i1 = jnp.min(jnp.where(logits == m1, ridx, N_EXPERTS), axis=0, keepdims=True)
        rest = jnp.where(ridx == i1, neg_inf, logits)
        m2 = jnp.max(rest, axis=0, keepdims=True)
        i2 = jnp.min(jnp.where(rest == m2, ridx, N_EXPERTS), axis=0, keepdims=True)
        e2 = jnp.exp(m2 - m1)
        den = 1.0 + e2
        gates_scr[...] = jnp.where(ridx == i1, 1.0 / den, jnp.where(ridx == i2, e2 / den, 0.0))
        sel = jnp.where(ridx == i1, 1.0, jnp.where(ridx == i2, 1.0, 0.0))
        ur = lax.broadcasted_iota(I32, (LANES, LANES), 0)
        uc = lax.broadcasted_iota(I32, (LANES, LANES), 1)
        utri = jnp.where(ur <= uc, 1.0, 0.0).astype(BF16)
        carry = jnp.zeros((N_EXPERTS, 1), F32)
        for kb in range(tm // LANES):
            sb = sel[:, kb * LANES:(kb + 1) * LANES]
            pref = jnp.dot(sb.astype(BF16), utri, preferred_element_type=F32) + carry
            rsel_scr[:, kb * LANES:(kb + 1) * LANES] = jnp.where(sb > 0.0, pref, -1.0)
            carry = pref[:, LANES - 1:LANES]
            if (kb + 1) * LANES == tm // 2:
                carry_half = carry
        r8 = lax.broadcasted_iota(I32, (N_EXPERTS, 1), 0)
        for ee in range(N_EXPERTS):
            cnt_smem[ee] = jnp.sum(jnp.where(r8 == ee, carry, 0.0)).astype(I32)
            cnt_smem[N_EXPERTS + ee] = jnp.sum(jnp.where(r8 == ee, carry_half, 0.0)).astype(I32)
        rn_scr[...] = jnp.concatenate(
            [rsel_scr[...], jnp.zeros((LANES - N_EXPERTS, tm), F32)], axis=0).T

    n_rc = (cnt_smem[e] + (rc_rows - 1)) // rc_rows
    cnt_first = cnt_smem[N_EXPERTS + e]
    token_regions = ((0, tm), (0, tm // 2), (tm // 2, tm))

    def chunk_region(r0):
        return jnp.where(r0 + rc_rows <= cnt_first, 1, jnp.where(r0 >= cnt_first, 2, 0))

    @pl.when(f == 0)
    def _gather():
        rsel_row = rsel_scr[pl.ds(e, 1), :]
        gate_row = gates_scr[pl.ds(e, 1), :]
        lane = lax.broadcasted_iota(I32, (tm, LANES), 1)
        rcol = jnp.sum(jnp.where(lane == e, rn_scr[...], 0.0), axis=1, keepdims=True)
        rcol_scr[...] = jnp.broadcast_to(rcol, (tm, LANES))

        def compact(r0, lo, hi):
            want = (r0 + 1 + lax.broadcasted_iota(I32, (rc_rows, hi - lo), 0)).astype(F32)
            pm = rsel_row[:, lo:hi] == want
            pb = jnp.where(pm, 1.0, 0.0).astype(BF16)
            xe_scr[pl.ds(r0, rc_rows), :] = jnp.dot(pb, h_scr[lo:hi, :],
                                                    preferred_element_type=F32).astype(BF16)
            gcol = jnp.sum(jnp.where(pm, gate_row[:, lo:hi], 0.0), axis=1, keepdims=True)
            gcol_scr[pl.ds(r0, rc_rows), :] = jnp.broadcast_to(gcol, (rc_rows, LANES))

        def body(rc, carry):
            r0 = pl.multiple_of(rc * rc_rows, rc_rows)
            for region, (lo, hi) in enumerate(token_regions):
                @pl.when(chunk_region(r0) == region)
                def _():
                    compact(r0, lo, hi)
            ye_scr[pl.ds(r0, rc_rows), :] = jnp.zeros((rc_rows, ye_scr.shape[1]), F32)
            return carry

        lax.fori_loop(0, n_rc, body, 0)

    def ffn_body(rc, carry):
        r0 = pl.multiple_of(rc * rc_rows, rc_rows)
        xe = xe_scr[pl.ds(r0, rc_rows), :]
        a = jnp.dot(xe, wg_ref[0], preferred_element_type=F32)
        u = jnp.dot(xe, wu_ref[0], preferred_element_type=F32)
        act = (_silu(a) * u * gcol_scr[pl.ds(r0, rc_rows), 0:1]).astype(BF16)
        ye_scr[pl.ds(r0, rc_rows), :] += jnp.dot(act, wd_ref[0], preferred_element_type=F32)
        return carry

    lax.fori_loop(0, n_rc, ffn_body, 0)

    @pl.when(f == pl.num_programs(3) - 1)
    def _scatter():
        def body(rc, carry):
            r0 = pl.multiple_of(rc * rc_rows, rc_rows)
            ye = ye_scr[pl.ds(r0, rc_rows), :].astype(BF16)
            for region, (lo, hi) in enumerate(token_regions):
                @pl.when(chunk_region(r0) == region)
                def _():
                    want = (r0 + 1 + lax.broadcasted_iota(I32, (hi - lo, rc_rows), 1)).astype(F32)
                    pt = jnp.where(rcol_scr[lo:hi, 0:1] == want, 1.0, 0.0).astype(BF16)
                    o_ref[0, lo:hi, :] += jnp.dot(pt, ye, preferred_element_type=F32)
            return carry

        lax.fori_loop(0, n_rc, body, 0)

    @pl.when((e == pl.num_programs(2) - 1) & (f == pl.num_programs(3) - 1))
    def _residual():
        for jb in range(ntb):
            rows = slice(jb * tb, (jb + 1) * tb)
            y = x_ref[0, rows, :] + gate_ref[0] * o_ref[0, rows, :]
            if final_norm:
                var = jnp.mean(y * y, axis=-1, keepdims=True)
                y = (y * lax.rsqrt(var + EPS)) * gfin_ref[...]
            o_ref[0, rows, :] = y


def _moe(x, sh, sc, gate, g, w_router, wg, wu, wd, tm, tf, g_final=None):
    bn, s, d = x.shape
    ne, _, fdim = wg.shape
    assert ne == N_EXPERTS and tm % MOE_RC == 0 and tm % MOE_TB == 0 and (tm // 2) % LANES == 0
    wr = jnp.zeros((d, LANES), F32).at[:, :ne].set(w_router)
    final_norm = g_final is not None
    gfin = g_final if final_norm else jnp.ones((1, d), F32)
    tok_in = pl.BlockSpec((1, tm, d), lambda b, t, e, f: (b, t, 0), pipeline_mode=pl.Buffered(1))
    tok_out = pl.BlockSpec((1, tm, d), lambda b, t, e, f: (b, t, 0), pipeline_mode=pl.Buffered(1))
    vec = pl.BlockSpec((1, 1, d), lambda b, t, e, f: (b, 0, 0))
    return pl.pallas_call(
        functools.partial(_moe_kernel, final_norm),
        grid=(bn, s // tm, ne, fdim // tf),
        in_specs=[tok_in, vec, vec, vec,
                  pl.BlockSpec((1, d), lambda b, t, e, f: (0, 0)),
                  pl.BlockSpec((1, d), lambda b, t, e, f: (0, 0)),
                  pl.BlockSpec((d, LANES), lambda b, t, e, f: (0, 0)),
                  pl.BlockSpec((1, d, tf), lambda b, t, e, f: (e, 0, f)),
                  pl.BlockSpec((1, d, tf), lambda b, t, e, f: (e, 0, f)),
                  pl.BlockSpec((1, tf, d), lambda b, t, e, f: (e, f, 0))],
        out_specs=tok_out,
        out_shape=jax.ShapeDtypeStruct((bn, s, d), F32),
        scratch_shapes=[pltpu.VMEM((tm, d), BF16),
                        pltpu.VMEM((tm, d), BF16),
                        pltpu.VMEM((tm, d), F32),
                        pltpu.VMEM((tm, LANES), F32),
                        pltpu.VMEM((ne, tm), F32),
                        pltpu.VMEM((ne, tm), F32),
                        pltpu.VMEM((tm, LANES), F32),
                        pltpu.VMEM((tm, LANES), F32),
                        pltpu.SMEM((2 * ne,), I32)],
        compiler_params=pltpu.CompilerParams(
            dimension_semantics=("parallel", "parallel", "arbitrary", "arbitrary"),
            vmem_limit_bytes=56 * 1024 * 1024),
    )(x, sh, sc, gate, g, gfin, wr, wg, wu, wd)


def _final_norm_kernel(x_ref, g_ref, o_ref):
    x = x_ref[0]
    var = jnp.mean(x * x, axis=-1, keepdims=True)
    o_ref[0] = (x * lax.rsqrt(var + EPS)) * g_ref[...]


def _final_norm(x, g, tm):
    bn, s, d = x.shape
    tok = pl.BlockSpec((1, tm, d), lambda b, t: (b, t, 0))
    return pl.pallas_call(
        _final_norm_kernel,
        grid=(bn, s // tm),
        in_specs=[tok, pl.BlockSpec((1, d), lambda b, t: (0, 0))],
        out_specs=tok,
        out_shape=jax.ShapeDtypeStruct((bn, s, d), F32),
        compiler_params=_cparams("parallel", "parallel"),
    )(x, g)


def _token_tile(s, want):
    tm = min(want, s)
    assert s % tm == 0 and tm % QB == 0
    return tm


def kernel(x, c, positions, w_ada, b_ada, g_norm_mix, g_norm_ffn, w_in, w_out, hgrn_lb_logits, hgrn_out_norm, gmlp_vnorm_g, gmlp_vnorm_b, gmlp_w_s, gmlp_b_s, ffn_w_gate, ffn_w_up, ffn_w_down, moe_w_router, moe_w_gate, moe_w_up, moe_w_down, g_final):
    bn, s, d = x.shape
    depth = w_in.shape[0]
    assert s % QB == 0 and d == sum(IN_WIDTHS[:1]) + B_DIM + C_DIM
    tm_proj = _token_tile(s, 512)
    tm_ffn = _token_tile(s, 1024)

    p_lb = jax.nn.softmax(hgrn_lb_logits.astype(F32), axis=0)
    cum = jnp.cumsum(p_lb, axis=0)
    lower_bounds = cum - cum[0:1]

    mod = _ada_mod(c, w_ada, b_ada)
    cosn, sinn, cost, sint = _rope_tables(positions)

    for l in range(depth):
        sh1, sc1, g1, sh2, sc2, g2 = [mod[l, :, i * d:(i + 1) * d].reshape(bn, 1, d) for i in range(6)]
        w_n, w_t = _prep_in_weights(w_in[l])
        (kz, ik3, bq, bf, bi, bg, c_out, qt, iq3t, vt4, iwt, bit) = _inproj(
            x, sh1, sc1, g_norm_mix[l].reshape(1, d), w_n, w_t, cosn, sinn, cost, sint,
            gmlp_vnorm_g[l].reshape(1, C_DIM), gmlp_vnorm_b[l].reshape(1, C_DIM), gmlp_w_s[l], gmlp_b_s[l],
            tm_proj)
        a_out = _dsa(ik3, kz, vt4, iq3t, iwt, qt)
        b_out = _hgrn(bq, bf, bi, bit, bg, lower_bounds[l].reshape(1, B_DIM),
                      jnp.tile(hgrn_out_norm[l], LANES // B_DV).reshape(1, LANES))
        gf = g_norm_ffn[l].reshape(1, d)
        if l % 2 == 0:
            i = l // 2
            x = _ffn(x, a_out, b_out, c_out, w_out[l], g1, sh2, sc2, g2, gf, ffn_w_gate[i].astype(BF16),
                     ffn_w_up[i].astype(BF16), ffn_w_down[i].astype(BF16), tm_ffn, 256)
        else:
            i = l // 2
            x = _outproj(x, a_out, b_out, c_out, w_out[l], g1, tm_proj)
            fused_final = l == depth - 1
            x = _moe(x, sh2, sc2, g2, gf, moe_w_router[i], moe_w_gate[i].astype(BF16),
                     moe_w_up[i].astype(BF16), moe_w_down[i].astype(BF16), _token_tile(s, 2048), 896,
                     g_final.reshape(1, d) if fused_final else None)
    if depth % 2 == 0:
        return x
    return _final_norm(x, g_final.reshape(1, d), tm_proj)
```

```python
import functools

import numpy as np
import jax
import jax.numpy as jnp
from jax import lax
from jax.experimental import pallas as pl
from jax.experimental.pallas import tpu as pltpu

F32 = jnp.float32
BF16 = jnp.bfloat16
I32 = jnp.int32
I16 = jnp.int16

HEAD_DIM = 64
A_HEADS = 6
A_DIM = A_HEADS * HEAD_DIM
IDX_HEADS = 4
IDX_DIM = 64
TOPK_MAX = 256
B_HEADS = 6
B_DK = 64
B_DV = 64
B_DIM = B_HEADS * B_DV
HGRN_CHUNK = 64
C_GROUPS = 4
C_GROUP_DIM = 64
C_DIM = C_GROUPS * C_GROUP_DIM
C_CHUNK = 128
ROPE_THETA = 10000.0
N_EXPERTS = 8
EPS = 1e-6
IN_WIDTHS = (A_DIM, HEAD_DIM, HEAD_DIM, IDX_HEADS * IDX_DIM, IDX_DIM, IDX_HEADS,
             B_DIM, B_DIM, B_DIM, B_DIM, 2 * C_DIM)

LANES = 128
KC = 128
QB = 2 * KC
PV_ROWS = HEAD_DIM + 16
VMEM_LIMIT = 48 * 1024 * 1024
HGRN_FAST_SPAN = 80.0
HGRN_SUB = 32
HGRN_GROUP = 4
INT_MIN = -2 ** 31

NT_DIMS = (((1,), (1,)), ((), ()))
TN_DIMS = (((0,), (0,)), ((), ()))


def _cparams(*sem):
    return pltpu.CompilerParams(dimension_semantics=sem, vmem_limit_bytes=VMEM_LIMIT)


def _split2(x):
    hi = x.astype(BF16)
    lo = (x - hi.astype(F32)).astype(BF16)
    return hi, lo


def _split3(x):
    hi = x.astype(BF16)
    r1 = x - hi.astype(F32)
    mid = r1.astype(BF16)
    lo = (r1 - mid.astype(F32)).astype(BF16)
    return hi, mid, lo


def _silu(x):
    return x * (1.0 / (1.0 + jnp.exp(-x)))


def _modulated_rmsnorm(x, g, sc, sh):
    var = jnp.mean(x * x, axis=-1, keepdims=True)
    y = x * lax.rsqrt(var + EPS)
    return (y * g) * (1.0 + sc) + sh


def _ada_kernel(c_ref, w_ref, b_ref, o_ref):
    cond = _silu(c_ref[...])
    o_ref[0] = jnp.dot(cond, w_ref[0], precision=lax.Precision.HIGHEST,
                       preferred_element_type=F32) + b_ref[0]


def _ada_mod(c, w_ada, b_ada):
    depth, d, d6 = w_ada.shape
    bn = c.shape[0]
    tn = 1536
    return pl.pallas_call(
        _ada_kernel,
        grid=(depth, d6 // tn),
        in_specs=[pl.BlockSpec((bn, d), lambda l, n: (0, 0)),
                  pl.BlockSpec((1, d, tn), lambda l, n: (l, 0, n)),
                  pl.BlockSpec((1, 1, tn), lambda l, n: (l, 0, n))],
        out_specs=pl.BlockSpec((1, bn, tn), lambda l, n: (l, 0, n)),
        out_shape=jax.ShapeDtypeStruct((depth, bn, d6), F32),
        compiler_params=_cparams("parallel", "parallel"),
    )(c, w_ada, b_ada.reshape(depth, 1, d6))


def _rope_kernel(posn_ref, post_ref, invn_ref, signn_ref, invt_ref,
                 cosn_ref, sinn_ref, cost_ref, sint_ref):
    ang_n = posn_ref[0].astype(F32) * invn_ref[...]
    cosn_ref[0] = jnp.cos(ang_n)
    sinn_ref[0] = jnp.sin(ang_n) * signn_ref[...]
    ang_t = invt_ref[...] * post_ref[0].astype(F32)
    cost_ref[0] = jnp.cos(ang_t)
    sint_ref[0] = jnp.sin(ang_t)


def _rope_tables(positions):
    bn, s = positions.shape
    half = HEAD_DIM // 2
    inv = ROPE_THETA ** (-jnp.arange(0, HEAD_DIM, 2, dtype=F32) / HEAD_DIM)
    inv_n = jnp.tile(inv, LANES // half).reshape(1, LANES)
    sign_n = jnp.tile(jnp.concatenate([-jnp.ones((half,), F32), jnp.ones((half,), F32)]),
                      LANES // HEAD_DIM).reshape(1, LANES)
    inv_t = inv.reshape(half, 1)
    full = lambda shape: pl.BlockSpec(shape, lambda b: (0,) * len(shape))
    return pl.pallas_call(
        _rope_kernel,
        grid=(bn,),
        in_specs=[pl.BlockSpec((1, s, 1), lambda b: (b, 0, 0)),
                  pl.BlockSpec((1, 1, s), lambda b: (b, 0, 0)),
                  full((1, LANES)), full((1, LANES)), full((half, 1))],
        out_specs=[pl.BlockSpec((1, s, LANES), lambda b: (b, 0, 0)),
                   pl.BlockSpec((1, s, LANES), lambda b: (b, 0, 0)),
                   pl.BlockSpec((1, half, s), lambda b: (b, 0, 0)),
                   pl.BlockSpec((1, half, s), lambda b: (b, 0, 0))],
        out_shape=[jax.ShapeDtypeStruct((bn, s, LANES), F32),
                   jax.ShapeDtypeStruct((bn, s, LANES), F32),
                   jax.ShapeDtypeStruct((bn, half, s), F32),
                   jax.ShapeDtypeStruct((bn, half, s), F32)],
        compiler_params=_cparams("parallel"),
    )(positions.reshape(bn, s, 1), positions.reshape(bn, 1, s), inv_n, sign_n, inv_t)


N_KI = 0
N_BQ = 2 * LANES
N_BF = N_BQ + B_DIM
N_BI = N_BF + B_DIM
N_BG = N_BI + B_DIM
N_CUV = N_BG + B_DIM
N_COLS = N_CUV + 2 * C_DIM
T_Q = 0
T_IQ = A_DIM
T_V = T_IQ + IDX_HEADS * IDX_DIM
T_IW = T_V + HEAD_DIM
T_BI = T_IW + 8
T_ROWS = T_BI + B_DIM


def _prep_in_weights(w_in_l):
    offs = np.concatenate([[0], np.cumsum(IN_WIDTHS)])
    sl = lambda i: w_in_l[:, int(offs[i]):int(offs[i + 1])]
    aq, ak, av, iq, ik, iw, bq, bf, bi, bg, cuv = [sl(i) for i in range(11)]
    half = HEAD_DIM // 2
    rot = lambda w: jnp.concatenate([w[:, half:], w[:, :half]], axis=1)
    w_n = jnp.concatenate([ak, ik, rot(ak), rot(ik), bq, bf, bi, bg, cuv], axis=1)
    w_t = jnp.concatenate([aq, iq, av, iw, jnp.zeros((w_in_l.shape[0], 4), w_in_l.dtype), bi], axis=1).T
    return w_n.astype(BF16), w_t.astype(BF16)


def _inproj_kernel(x_ref, sh_ref, sc_ref, g_ref, wn_ref, wt_ref, cosn_ref, sinn_ref, cost_ref, sint_ref,
                   vg_ref, vb_ref, ws_ref, cbias_ref,
                   kz_ref, ik3_ref, bq_ref, bf_ref, bi_ref, bg_ref, c_ref,
                   qt_ref, iq3t_ref, vt_ref, iwt_ref, bit_ref):
    tm = x_ref.shape[1]
    half = HEAD_DIM // 2
    h = _modulated_rmsnorm(x_ref[0], g_ref[...], sc_ref[0], sh_ref[0]).astype(BF16)

    cosn = cosn_ref[0]
    sinn = sinn_ref[0]
    zk = jnp.dot(h, wn_ref[:, N_KI:N_KI + 2 * LANES], preferred_element_type=F32)
    ki = zk[:, :LANES] * cosn + zk[:, LANES:] * sinn
    lane = lax.broadcasted_iota(I32, (tm, LANES), 1)
    kz_ref[0] = jnp.where(lane < HEAD_DIM, ki, 0.0).astype(BF16)
    ik2 = jnp.where(lane < IDX_DIM, pltpu.roll(ki, HEAD_DIM, axis=1), ki)
    hi, lo = _split2(ik2)
    ik3_ref[0, :, :LANES] = jnp.where(lane < IDX_DIM, hi, lo)
    ik3_ref[0, :, LANES:] = hi
    bq_ref[0] = jnp.dot(h, wn_ref[:, N_BQ:N_BF], preferred_element_type=F32).astype(BF16)
    bf_ref[0] = jnp.dot(h, wn_ref[:, N_BF:N_BI], preferred_element_type=F32)
    bi_ref[0] = jnp.dot(h, wn_ref[:, N_BI:N_BG], preferred_element_type=F32).astype(BF16)
    bg_ref[0] = jnp.dot(h, wn_ref[:, N_BG:N_CUV], preferred_element_type=F32).astype(BF16)
    cuv = jnp.dot(h, wn_ref[:, N_CUV:N_COLS], preferred_element_type=F32)
    c_ref[0] = _gmlp_mix(cuv, vg_ref[...], vb_ref[...], ws_ref, cbias_ref[...]).astype(BF16)

    zt = lax.dot_general(wt_ref[...], h, NT_DIMS, preferred_element_type=F32)
    cost = cost_ref[0]
    sint = sint_ref[0]
    qscale = HEAD_DIM ** -0.5
    for hh in range(A_HEADS):
        r0 = T_Q + hh * HEAD_DIM
        x1 = zt[r0:r0 + half]
        x2 = zt[r0 + half:r0 + HEAD_DIM]
        qt_ref[0, hh * HEAD_DIM:hh * HEAD_DIM + half] = ((x1 * cost - x2 * sint) * qscale).astype(BF16)
        qt_ref[0, hh * HEAD_DIM + half:(hh + 1) * HEAD_DIM] = ((x2 * cost + x1 * sint) * qscale).astype(BF16)
    iscale = IDX_DIM ** -0.5
    zero = jnp.zeros((IDX_DIM, tm), BF16)
    for hh in range(IDX_HEADS):
        r0 = T_IQ + hh * IDX_DIM
        x1 = zt[r0:r0 + half]
        x2 = zt[r0 + half:r0 + IDX_DIM]
        y = jnp.concatenate([(x1 * cost - x2 * sint) * iscale, (x2 * cost + x1 * sint) * iscale], axis=0)
        hi, lo = _split2(y)
        iq3t_ref[0, hh, 0 * IDX_DIM:1 * IDX_DIM] = hi
        iq3t_ref[0, hh, 1 * IDX_DIM:2 * IDX_DIM] = hi
        iq3t_ref[0, hh, 2 * IDX_DIM:3 * IDX_DIM] = lo
        iq3t_ref[0, hh, 3 * IDX_DIM:4 * IDX_DIM] = zero
    vt = zt[T_V:T_V + HEAD_DIM].astype(BF16)
    for i in range(tm // KC):
        vt_ref[0, i] = vt[:, i * KC:(i + 1) * KC]
    iwt_ref[0] = zt[T_IW:T_IW + 8] * (IDX_HEADS ** -0.5)
    bit = zt[T_BI:T_BI + B_DIM].astype(BF16)
    for i in range(tm // LANES):
        bit_ref[0, i] = bit[:, i * LANES:(i + 1) * LANES]


def _inproj(x, sh, sc, g, w_n, w_t, cosn, sinn, cost, sint, vg, vb, ws, bs, tm):
    bn, s, d = x.shape
    assert tm % C_CHUNK == 0
    nt = s // tm
    half = HEAD_DIM // 2
    cbias = jnp.repeat(bs.T, C_GROUP_DIM, axis=1)
    tok = lambda w: pl.BlockSpec((1, tm, w), lambda b, t: (b, t, 0))
    vec = pl.BlockSpec((1, 1, d), lambda b, t: (b, 0, 0))
    full2 = lambda a: pl.BlockSpec(a.shape, lambda b, t: (0, 0))
    out_shapes = [
        jax.ShapeDtypeStruct((bn, s, LANES), BF16),
        jax.ShapeDtypeStruct((bn, s, 2 * LANES), BF16),
        jax.ShapeDtypeStruct((bn, s, B_DIM), BF16),
        jax.ShapeDtypeStruct((bn, s, B_DIM), F32),
        jax.ShapeDtypeStruct((bn, s, B_DIM), BF16),
        jax.ShapeDtypeStruct((bn, s, B_DIM), BF16),
        jax.ShapeDtypeStruct((bn, s, C_DIM), BF16),
        jax.ShapeDtypeStruct((bn, A_DIM, s), BF16),
        jax.ShapeDtypeStruct((bn, IDX_HEADS, 4 * IDX_DIM, s), BF16),
        jax.ShapeDtypeStruct((bn, s // KC, HEAD_DIM, KC), BF16),
        jax.ShapeDtypeStruct((bn, 8, s), F32),
        jax.ShapeDtypeStruct((bn, s // LANES, B_DIM, LANES), BF16),
    ]
    out_specs = [
        tok(LANES), tok(2 * LANES), tok(B_DIM), tok(B_DIM), tok(B_DIM), tok(B_DIM), tok(C_DIM),
        pl.BlockSpec((1, A_DIM, tm), lambda b, t: (b, 0, t)),
        pl.BlockSpec((1, IDX_HEADS, 4 * IDX_DIM, tm), lambda b, t: (b, 0, 0, t)),
        pl.BlockSpec((1, tm // KC, HEAD_DIM, KC), lambda b, t: (b, t, 0, 0)),
        pl.BlockSpec((1, 8, tm), lambda b, t: (b, 0, t)),
        pl.BlockSpec((1, tm // LANES, B_DIM, LANES), lambda b, t: (b, t, 0, 0)),
    ]
    return pl.pallas_call(
        _inproj_kernel,
        grid=(bn, nt),
        in_specs=[tok(d), vec, vec, full2(g), full2(w_n), full2(w_t),
                  tok(LANES), tok(LANES),
                  pl.BlockSpec((1, half, tm), lambda b, t: (b, 0, t)),
                  pl.BlockSpec((1, half, tm), lambda b, t: (b, 0, t)),
                  full2(vg), full2(vb), pl.BlockSpec(ws.shape, lambda b, t: (0, 0, 0)), full2(cbias)],
        out_specs=out_specs,
        out_shape=out_shapes,
        compiler_params=_cparams("parallel", "parallel"),
    )(x, sh, sc, g, w_n, w_t, cosn, sinn, cost, sint, vg, vb, ws, cbias)


def _dsa_kernel(n_top, ik3_ref, kz_ref, vt_ref, iq3t_ref, iwt_ref, qt_ref, out_ref,
                key_scr, hi_scr, lo_scr, lg_scr, acc_scr):
    j = pl.program_id(1)
    npair = j + 1
    pair = 2 * KC
    row = lax.broadcasted_iota(I32, (KC, QB), 0)
    col = lax.broadcasted_iota(I32, (KC, QB), 1)
    int_min = jnp.int32(INT_MIN)
    i16_min = jnp.int16(-2 ** 15)

    def score_pair(p, diagonal):
        for u in range(2):
            ks = pl.multiple_of(p * pair + u * KC, KC)
            ikc = ik3_ref[0, pl.ds(ks, KC), :]
            sc = jnp.zeros((KC, QB), F32)
            for hh in range(IDX_HEADS):
                rel = jnp.dot(ikc, iq3t_ref[0, hh], preferred_element_type=F32)
                sc = sc + jnp.maximum(rel, 0.0) * iwt_ref[0, hh:hh + 1, :]
            sc = jnp.where(sc == 0.0, 0.0, sc)
            bits = pltpu.bitcast(sc, I32)
            key = bits ^ ((bits >> 31) & jnp.int32(0x7FFFFFFF))
            if diagonal:
                key = jnp.where((u * KC + row) <= col, key, int_min)
            key_scr[pl.ds(ks, KC), :] = key
            hi_scr[pl.ds(ks, KC), :] = (key >> 16).astype(I16)
            lo_scr[pl.ds(ks, KC), :] = ((key & jnp.int32(0xFFFF)) - 32768).astype(I16)

    def score_body(p, carry):
        score_pair(p, False)
        return carry

    lax.fori_loop(0, j, score_body, 0)
    score_pair(j, True)

    def count16(ref, pred_fn):
        def body(p, acc):
            ks = pl.multiple_of(p * pair, pair)
            m = jnp.where(pred_fn(ref[pl.ds(ks, pair), :]), jnp.int16(1), jnp.int16(0))
            parts = [m[16 * i:16 * (i + 1)] for i in range(pair // 16)]
            while len(parts) > 1:
                parts = [parts[i] + parts[i + 1] for i in range(0, len(parts), 2)]
            return acc + parts[0]
        acc = lax.fori_loop(0, npair, body, jnp.zeros((16, QB), I16))
        return jnp.sum(acc.astype(I32), axis=0, keepdims=True)

    def bisect16(ref, k_needed):
        def bit_body(i, t_u):
            cand = t_u | jnp.left_shift(jnp.int32(1), 15 - i)
            cand16 = (cand - 32768).astype(I16)
            cnt = count16(ref, lambda x: x >= cand16)
            return jnp.where(cnt >= k_needed, cand, t_u)
        return lax.fori_loop(0, 16, bit_body, jnp.zeros((1, QB), I32))

    t_hi = bisect16(hi_scr, n_top)
    t_hi16 = (t_hi - 32768).astype(I16)
    n_hi_gt = count16(hi_scr, lambda x: x > t_hi16)

    def bucket_body(p, carry):
        ks = pl.multiple_of(p * pair, pair)
        in_bucket = hi_scr[pl.ds(ks, pair), :] == t_hi16
        lo_scr[pl.ds(ks, pair), :] = jnp.where(in_bucket, lo_scr[pl.ds(ks, pair), :], i16_min)
        return carry

    lax.fori_loop(0, npair, bucket_body, 0)
    t_lo = bisect16(lo_scr, n_top - n_hi_gt)
    t_lo16 = (t_lo - 32768).astype(I16)
    n_gt = n_hi_gt + count16(lo_scr, lambda x: x > t_lo16)
    thr = jnp.left_shift(t_hi - 32768, 16) | t_lo
    n_tie = (n_top - n_gt).astype(F32)
    thr_valid = jnp.where(thr > int_min, 1.0, 0.0)

    tr = lax.broadcasted_iota(I32, (KC, KC), 0)
    tc = lax.broadcasted_iota(I32, (KC, KC), 1)
    ltri = jnp.where(tr >= tc, 1.0, 0.0).astype(BF16)
    zpad = jnp.zeros((HEAD_DIM, QB), BF16)
    qpad = [jnp.concatenate([qt_ref[0, hh * HEAD_DIM:(hh + 1) * HEAD_DIM, :], zpad], axis=0)
            for hh in range(A_HEADS)]
    neg_inf = jnp.float32(-jnp.inf)

    def pass_a(p, carry):
        tie_cnt, ms = carry
        ms = list(ms)
        for u in range(2):
            ks = pl.multiple_of(p * pair + u * KC, KC)
            kc = key_scr[pl.ds(ks, KC), :]
            eqf = jnp.where(kc == thr, thr_valid, 0.0)
            pref = jnp.dot(ltri, eqf.astype(BF16), preferred_element_type=F32)
            keep_tie = eqf * jnp.where((tie_cnt + pref) <= n_tie, 1.0, 0.0)
            sel = jnp.where(kc > thr, 1.0, keep_tie) > 0.5
            tie_cnt = tie_cnt + pref[KC - 1:KC, :]
            kzc = kz_ref[0, pl.ds(ks, KC), :]
            for hh in range(A_HEADS):
                lt = jnp.dot(kzc, qpad[hh], preferred_element_type=F32)
                lt = jnp.where(sel, lt, neg_inf)
                lg_scr[hh, pl.ds(ks, KC), :] = lt
                ms[hh] = jnp.maximum(ms[hh], jnp.max(lt, axis=0, keepdims=True))
        return tie_cnt, tuple(ms)

    init_m = tuple(jnp.full((1, QB), neg_inf, F32) for _ in range(A_HEADS))
    _, ms = lax.fori_loop(0, npair, pass_a, (jnp.zeros((1, QB), F32), init_m))

    acc_scr[...] = jnp.zeros_like(acc_scr)
    ones_rows = jnp.ones((PV_ROWS - HEAD_DIM, pair), BF16)

    def pass_b(p, carry):
        ks = pl.multiple_of(p * pair, pair)
        vt2 = jnp.concatenate([vt_ref[0, 2 * p], vt_ref[0, 2 * p + 1]], axis=1)
        vt2 = jnp.concatenate([vt2, ones_rows], axis=0)
        for hh in range(A_HEADS):
            pr = jnp.exp(lg_scr[hh, pl.ds(ks, pair), :] - ms[hh])
            acc_scr[hh * PV_ROWS:(hh + 1) * PV_ROWS, :] += jnp.dot(
                vt2, pr.astype(BF16), preferred_element_type=F32)
        return carry

    lax.fori_loop(0, npair, pass_b, 0)
    o_t = jnp.concatenate(
        [acc_scr[hh * PV_ROWS:hh * PV_ROWS + HEAD_DIM, :]
         / acc_scr[hh * PV_ROWS + HEAD_DIM:hh * PV_ROWS + HEAD_DIM + 1, :] for hh in range(A_HEADS)], axis=0)
    out_ref[0] = o_t.T.astype(BF16)


def _dsa(ik3, kz, vt4, iq3t, iwt, qt):
    bn, s, _ = kz.shape
    assert s % QB == 0
    n_top = min(TOPK_MAX, s // 4)
    nq = s // QB
    return pl.pallas_call(
        functools.partial(_dsa_kernel, n_top),
        grid=(bn, nq),
        in_specs=[pl.BlockSpec((1, s, 2 * LANES), lambda b, q: (b, 0, 0)),
                  pl.BlockSpec((1, s, LANES), lambda b, q: (b, 0, 0)),
                  pl.BlockSpec((1, s // KC, HEAD_DIM, KC), lambda b, q: (b, 0, 0, 0)),
                  pl.BlockSpec((1, IDX_HEADS, 4 * IDX_DIM, QB), lambda b, q: (b, 0, 0, q)),
                  pl.BlockSpec((1, 8, QB), lambda b, q: (b, 0, q)),
                  pl.BlockSpec((1, A_DIM, QB), lambda b, q: (b, 0, q))],
        out_specs=pl.BlockSpec((1, QB, A_DIM), lambda b, q: (b, q, 0)),
        out_shape=jax.ShapeDtypeStruct((bn, s, A_DIM), BF16),
        scratch_shapes=[pltpu.VMEM((s, QB), I32), pltpu.VMEM((s, QB), I16), pltpu.VMEM((s, QB), I16),
                        pltpu.VMEM((A_HEADS, s, QB), F32), pltpu.VMEM((A_HEADS * PV_ROWS, QB), F32)],
        compiler_params=_cparams("parallel", "arbitrary"),
    )(ik3, kz, vt4, iq3t, iwt, qt)


def _hgrn_kernel(q_ref, f_ref, i_ref, it_ref, g_ref, lb_ref, gn_ref, o_ref, st_scr, oi_scr):
    s = q_ref.shape[1]
    ch = HGRN_CHUNK
    sb = HGRN_SUB
    nsb = ch // sb
    rows = HGRN_GROUP * ch
    wins_per_group = rows // LANES
    lane = lax.broadcasted_iota(I32, (ch, LANES), 1)
    head0 = lane < B_DK
    lr = lax.broadcasted_iota(I32, (rows, 3 * rows), 0)
    lc = lax.broadcasted_iota(I32, (rows, 3 * rows), 1) % rows
    lcum = jnp.where(((lr // ch) == (lc // ch)) & (lc <= lr), 1.0, 0.0).astype(BF16)
    att_rows = HGRN_GROUP * 2 * ch
    att_cols = HGRN_GROUP * nsb * ch
    ar = lax.broadcasted_iota(I32, (att_rows, att_cols), 0)
    ac = lax.broadcasted_iota(I32, (att_rows, att_cols), 1)
    at = ar % ch
    att_mask = (((ar // (2 * ch)) == (ac // (nsb * ch))) & ((at // sb) == ((ac // ch) % nsb))
                & ((ac % ch) <= at))
    bd_r = lax.broadcasted_iota(I32, (LANES, LANES), 0)
    bd_c = lax.broadcasted_iota(I32, (LANES, LANES), 1)
    same_head = (bd_r < B_DK) == (bd_c < B_DK)
    ones_bd = jnp.where(same_head, 1.0, 0.0).astype(BF16)
    ones_bd2 = jnp.concatenate([ones_bd, ones_bd], axis=0)
    s_iota = lax.broadcasted_iota(I32, (ch, LANES), 0)

    lb = lb_ref[...]
    log_lb = jnp.log(lb)
    log_1mlb = jnp.log1p(-lb)
    gn = gn_ref[...]
    st_scr[...] = jnp.zeros_like(st_scr)

    def group_body(it, carry):
        t0 = pl.multiple_of(it * rows, rows)
        z = f_ref[0, pl.ds(t0, rows), :]
        q = q_ref[0, pl.ds(t0, rows), :].astype(F32)
        v = i_ref[0, pl.ds(t0, rows), :]
        softplus_tail = jnp.log1p(jnp.exp(-jnp.abs(z)))
        log_sig = -(jnp.maximum(-z, 0.0) + softplus_tail)
        x2 = log_1mlb + log_sig
        amax = jnp.maximum(log_lb, x2)
        log_f = amax + jnp.log1p(jnp.exp(-jnp.abs(log_lb - x2)))
        kk = (1.0 - lb) * jnp.exp(-(jnp.maximum(z, 0.0) + softplus_tail))
        f_hi, f_mid, f_lo = _split3(log_f)
        b = jnp.dot(lcum, jnp.concatenate([f_hi, f_mid, f_lo], axis=0), preferred_element_type=F32)
        ref_parts = []
        for c in range(HGRN_GROUP):
            ref_parts.append(jnp.zeros((sb, LANES), F32))
            for i in range(1, nsb):
                r = c * ch + i * sb
                ref_parts.append(jnp.broadcast_to(b[r - 1:r], (sb, LANES)))
        ref = jnp.concatenate(ref_parts, axis=0)
        q_loc = q * jnp.exp(b - ref)
        q_chk = (q * jnp.exp(b)).astype(BF16)
        span = ref - b
        span_max = jnp.max(span)

        k_parts, q_parts, v_parts, kh_cols, decay = [], [], [], [], []
        for c in range(HGRN_GROUP):
            lo_r = c * ch
            b_c = b[lo_r:lo_r + ch]
            kk_c = kk[lo_r:lo_r + ch]
            b_last = b_c[ch - 1:ch]
            for i in range(nsb):
                r = lo_r + i * sb
                k_parts.append(kk_c * jnp.exp(jnp.minimum(ref[r:r + 1] - b_c, HGRN_FAST_SPAN)))
            q_c = q_loc[lo_r:lo_r + ch]
            q_parts += [jnp.where(head0, q_c, 0.0), jnp.where(head0, 0.0, q_c)]
            v_parts += [v[lo_r:lo_r + ch]] * nsb
            kh = (kk_c * jnp.exp(b_last - b_c)).astype(BF16)
            kh_cols.append(jnp.concatenate(
                ([jnp.zeros((lo_r, LANES), BF16)] if lo_r else []) + [kh]
                + ([jnp.zeros((rows - lo_r - ch, LANES), BF16)] if rows - lo_r - ch else []), axis=0))
            decay.append(jnp.exp(b_last))
        att = lax.dot_general(jnp.concatenate(q_parts, axis=0).astype(BF16),
                              jnp.concatenate(k_parts, axis=0).astype(BF16), NT_DIMS,
                              preferred_element_type=F32)
        att = jnp.where(att_mask, att, 0.0).astype(BF16)
        o2 = jnp.dot(att, jnp.concatenate(v_parts, axis=0), preferred_element_type=F32)
        for c in range(HGRN_GROUP):
            r = c * 2 * ch
            oi_scr[c * ch:(c + 1) * ch, :] = jnp.where(head0, o2[r:r + ch], o2[r + ch:r + 2 * ch])

        it_win = jnp.concatenate([it_ref[0, wins_per_group * it + w] for w in range(wins_per_group)], axis=1)
        upd = jnp.dot(it_win, jnp.concatenate(kh_cols, axis=1), preferred_element_type=F32)
        st = st_scr[...]
        states = []
        for c in range(HGRN_GROUP):
            states.append(st.astype(BF16))
            st = st * decay[c] + jnp.where(same_head, upd[:, c * LANES:(c + 1) * LANES], 0.0)
        st_scr[...] = st
        oi_all = lax.dot_general(q_chk, jnp.concatenate(states, axis=0), NT_DIMS,
                                 preferred_element_type=F32)
        o_inter = [oi_all[c * ch:(c + 1) * ch, c * LANES:(c + 1) * LANES] for c in range(HGRN_GROUP)]

        @pl.when(span_max > HGRN_FAST_SPAN)
        def _():
            for c in range(HGRN_GROUP):
                lo_r = c * ch

                @pl.when(jnp.max(span[lo_r:lo_r + ch]) > HGRN_FAST_SPAN)
                def _():
                    b_c = b[lo_r:lo_r + ch]
                    q_c = q[lo_r:lo_r + ch]
                    kk_c = kk[lo_r:lo_r + ch]
                    vf = v[lo_r:lo_r + ch].astype(F32)

                    def t_body(t, carry2):
                        onehot = jnp.where(s_iota == t, 1.0, 0.0)
                        b_t = jnp.sum(onehot * b_c, axis=0, keepdims=True)
                        q_t = jnp.sum(onehot * q_c, axis=0, keepdims=True)
                        dec = jnp.exp(jnp.where(s_iota <= t, b_t - b_c, -jnp.inf))
                        w = q_t * kk_c * dec
                        w0 = jnp.sum(jnp.where(head0, w, 0.0), axis=1, keepdims=True)
                        w1 = jnp.sum(jnp.where(head0, 0.0, w), axis=1, keepdims=True)
                        a_col = jnp.where(head0, w0, w1)
                        oi_scr[pl.ds(lo_r + t, 1), :] = jnp.sum(a_col * vf, axis=0, keepdims=True)
                        return carry2

                    lax.fori_loop(0, ch, t_body, 0)

        o = oi_scr[...] + jnp.concatenate(o_inter, axis=0)
        sq_hi, sq_lo = _split2(o * o)
        ss = jnp.dot(jnp.concatenate([sq_hi, sq_lo], axis=1), ones_bd2, preferred_element_type=F32)
        y = (o * lax.rsqrt(ss * (1.0 / B_DV) + EPS)) * gn
        g = g_ref[0, pl.ds(t0, rows), :].astype(F32)
        o_ref[0, pl.ds(t0, rows), :] = (y * _silu(g)).astype(o_ref.dtype)
        return carry

    lax.fori_loop(0, s // rows, group_body, 0)


def _hgrn(bq, bf, bi, bit, bg, lb, gn):
    bn, s, _ = bq.shape
    assert s % (HGRN_GROUP * HGRN_CHUNK) == 0 and 2 * HGRN_CHUNK == LANES and HGRN_GROUP % 2 == 0
    npair = B_DIM // LANES
    tok = pl.BlockSpec((1, s, LANES), lambda b, p: (b, 0, p))
    return pl.pallas_call(
        _hgrn_kernel,
        grid=(bn, npair),
        in_specs=[tok, tok, tok,
                  pl.BlockSpec((1, s // LANES, LANES, LANES), lambda b, p: (b, 0, p, 0)),
                  tok,
                  pl.BlockSpec((1, LANES), lambda b, p: (0, p)),
                  pl.BlockSpec((1, LANES), lambda b, p: (0, 0))],
        out_specs=tok,
        out_shape=jax.ShapeDtypeStruct((bn, s, B_DIM), BF16),
        scratch_shapes=[pltpu.VMEM((LANES, LANES), F32),
                        pltpu.VMEM((HGRN_GROUP * HGRN_CHUNK, LANES), F32)],
        compiler_params=_cparams("parallel", "parallel"),
    )(bq, bf, bi, bit, bg, lb, gn)


def _gmlp_mix(uv, vg, vb, ws_ref, bias):
    uv = 0.5 * uv * (1.0 + lax.erf(uv * (2.0 ** -0.5)))
    u = uv[:, :C_DIM]
    v = uv[:, C_DIM:]
    mu = jnp.mean(v, axis=-1, keepdims=True)
    var = jnp.mean(jnp.square(v - mu), axis=-1, keepdims=True)
    vn_b = (((v - mu) * lax.rsqrt(var + EPS)) * vg + vb).astype(BF16)
    r_t = lax.broadcasted_iota(I32, (C_CHUNK, C_CHUNK), 0)
    r_s = lax.broadcasted_iota(I32, (C_CHUNK, C_CHUNK), 1)
    group = lax.broadcasted_iota(I32, (C_CHUNK, C_DIM), 1) // C_GROUP_DIM
    ws = [jnp.where(r_t >= r_s, ws_ref[gi], 0.0).astype(BF16) for gi in range(C_GROUPS)]
    outs = []
    for c in range(uv.shape[0] // C_CHUNK):
        sl = slice(c * C_CHUNK, (c + 1) * C_CHUNK)
        mixed = jnp.zeros((C_CHUNK, C_DIM), F32)
        for gi in range(C_GROUPS):
            m = jnp.dot(ws[gi], vn_b[sl], preferred_element_type=F32)
            mixed = jnp.where(group == gi, m, mixed)
        outs.append(u[sl] * (mixed + bias))
    return jnp.concatenate(outs, axis=0)


def _outproj_kernel(x_ref, a_ref, b_ref, c_ref, wa_ref, wb_ref, wc_ref, g1_ref, o_ref):
    mix = (jnp.dot(a_ref[0], wa_ref[...], preferred_element_type=F32)
           + jnp.dot(b_ref[0], wb_ref[...], preferred_element_type=F32)
           + jnp.dot(c_ref[0], wc_ref[...], preferred_element_type=F32))
    o_ref[0] = x_ref[0] + g1_ref[0] * mix


def _outproj(x, a, b, c, w_out_l, g1, tm):
    bn, s, d = x.shape
    wa = w_out_l[:A_DIM].astype(BF16)
    wb = w_out_l[A_DIM:A_DIM + B_DIM].astype(BF16)
    wc = w_out_l[A_DIM + B_DIM:].astype(BF16)
    tok = lambda w: pl.BlockSpec((1, tm, w), lambda bb, t: (bb, t, 0))
    full = lambda arr: pl.BlockSpec(arr.shape, lambda bb, t: (0, 0))
    return pl.pallas_call(
        _outproj_kernel,
        grid=(bn, s // tm),
        in_specs=[tok(d), tok(A_DIM), tok(B_DIM), tok(C_DIM), full(wa), full(wb), full(wc),
                  pl.BlockSpec((1, 1, d), lambda bb, t: (bb, 0, 0))],
        out_specs=tok(d),
        out_shape=jax.ShapeDtypeStruct((bn, s, d), F32),
        compiler_params=_cparams("parallel", "parallel"),
    )(x, a, b, c, wa, wb, wc, g1)


def _ffn_kernel(x_ref, a_ref, b_ref, c_ref, wa_ref, wb_ref, wc_ref, g1_ref, sh_ref, sc_ref, gate_ref, g_ref,
                wg_ref, wu_ref, wd_ref, o_ref, xn_scr, h_scr, acc_scr):
    f = pl.program_id(2)

    @pl.when(f == 0)
    def _():
        mix = (jnp.dot(a_ref[0], wa_ref[...], preferred_element_type=F32)
               + jnp.dot(b_ref[0], wb_ref[...], preferred_element_type=F32)
               + jnp.dot(c_ref[0], wc_ref[...], preferred_element_type=F32))
        xn = x_ref[0] + g1_ref[0] * mix
        xn_scr[...] = xn
        h_scr[...] = _modulated_rmsnorm(xn, g_ref[...], sc_ref[0], sh_ref[0]).astype(BF16)
        acc_scr[...] = jnp.zeros_like(acc_scr)

    h = h_scr[...]
    a = jnp.dot(h, wg_ref[...], preferred_element_type=F32)
    u = jnp.dot(h, wu_ref[...], preferred_element_type=F32)
    act = (_silu(a) * u).astype(BF16)
    acc_scr[...] += jnp.dot(act, wd_ref[...], preferred_element_type=F32)

    @pl.when(f == pl.num_programs(2) - 1)
    def _():
        o_ref[0] = xn_scr[...] + gate_ref[0] * acc_scr[...]


def _ffn(x, a, b, c, w_out_l, g1, sh, sc, gate, g, wg, wu, wd, tm, tf):
    bn, s, d = x.shape
    fdim = wg.shape[1]
    wa = w_out_l[:A_DIM].astype(BF16)
    wb = w_out_l[A_DIM:A_DIM + B_DIM].astype(BF16)
    wc = w_out_l[A_DIM + B_DIM:].astype(BF16)
    tok = pl.BlockSpec((1, tm, d), lambda b, t, f: (b, t, 0))
    tokw = lambda w: pl.BlockSpec((1, tm, w), lambda b, t, f: (b, t, 0))
    full = lambda arr: pl.BlockSpec(arr.shape, lambda b, t, f: (0, 0))
    vec = pl.BlockSpec((1, 1, d), lambda b, t, f: (b, 0, 0))
    return pl.pallas_call(
        _ffn_kernel,
        grid=(bn, s // tm, fdim // tf),
        in_specs=[tok, tokw(A_DIM), tokw(B_DIM), tokw(C_DIM), full(wa), full(wb), full(wc), vec,
                  vec, vec, vec,
                  pl.BlockSpec((1, d), lambda b, t, f: (0, 0)),
                  pl.BlockSpec((d, tf), lambda b, t, f: (0, f)),
                  pl.BlockSpec((d, tf), lambda b, t, f: (0, f)),
                  pl.BlockSpec((tf, d), lambda b, t, f: (f, 0))],
        out_specs=tok,
        out_shape=jax.ShapeDtypeStruct((bn, s, d), F32),
        scratch_shapes=[pltpu.VMEM((tm, d), F32), pltpu.VMEM((tm, d), BF16), pltpu.VMEM((tm, d), F32)],
        compiler_params=_cparams("parallel", "parallel", "arbitrary"),
    )(x, a, b, c, wa, wb, wc, g1, sh, sc, gate, g, wg, wu, wd)


MOE_RC = 272
MOE_SC = 256
MOE_TB = 512


def _moe_kernel(final_norm, x_ref, sh_ref, sc_ref, gate_ref, g_ref, gfin_ref, wr_ref, wg_ref, wu_ref, wd_ref, o_ref,
                h_scr, xe_scr, ye_scr, gcol_scr, gates_scr, rsel_scr, rn_scr, rcol_scr, cnt_smem):
    e = pl.program_id(2)
    f = pl.program_id(3)
    tm = x_ref.shape[1]
    rc_rows = MOE_RC
    sc_rows = MOE_SC
    tb = MOE_TB
    ntb = tm // tb
    neg_inf = jnp.float32(-jnp.inf)

    @pl.when((e == 0) & (f == 0))
    def _route():
        for jb in range(ntb):
            rows = slice(jb * tb, (jb + 1) * tb)
            h = _modulated_rmsnorm(x_ref[0, rows, :], g_ref[...], sc_ref[0], sh_ref[0])
            h_scr[rows, :] = h.astype(BF16)
            rcol_scr[rows, :] = jnp.dot(h, wr_ref[...], precision=lax.Precision.HIGHEST,
                                        preferred_element_type=F32)
            o_ref[0, rows, :] = jnp.zeros((tb, o_ref.shape[2]), F32)
        logits = rcol_scr[...].T[:N_EXPERTS]
        ridx = lax.broadcasted_iota(I32, (N_EXPERTS, tm), 0)
        m1 = jnp.max(logits, axis=0, keepdims=True)
        i1 = jnp.min(jnp.where(logits == m1, ridx, N_EXPERTS), axis=0, keepdims=True)
        rest = jnp.where(ridx == i1, neg_inf, logits)
        m2 = jnp.max(rest, axis=0, keepdims=True)
        i2 = jnp.min(jnp.where(rest == m2, ridx, N_EXPERTS), axis=0, keepdims=True)
        e2 = jnp.exp(m2 - m1)
        den = 1.0 + e2
        gates_scr[...] = jnp.where(ridx == i1, 1.0 / den, jnp.where(ridx == i2, e2 / den, 0.0))
        sel = jnp.where(ridx == i1, 1.0, jnp.where(ridx == i2, 1.0, 0.0))
        ur = lax.broadcasted_iota(I32, (LANES, LANES), 0)
        uc = lax.broadcasted_iota(I32, (LANES, LANES), 1)
        utri = jnp.where(ur <= uc, 1.0, 0.0).astype(BF16)
        carry = jnp.zeros((N_EXPERTS, 1), F32)
        for kb in range(tm // LANES):
            sb = sel[:, kb * LANES:(kb + 1) * LANES]
            pref = jnp.dot(sb.astype(BF16), utri, preferred_element_type=F32) + carry
            rsel_scr[:, kb * LANES:(kb + 1) * LANES] = jnp.where(sb > 0.0, pref, -1.0)
            carry = pref[:, LANES - 1:LANES]
            if (kb + 1) * LANES == tm // 2:
                carry_half = carry
        r8 = lax.broadcasted_iota(I32, (N_EXPERTS, 1), 0)
        for ee in range(N_EXPERTS):
            cnt_smem[ee] = jnp.sum(jnp.where(r8 == ee, carry, 0.0)).astype(I32)
            cnt_smem[N_EXPERTS + ee] = jnp.sum(jnp.where(r8 == ee, carry_half, 0.0)).astype(I32)
        rn_scr[...] = jnp.concatenate(
            [rsel_scr[...], jnp.zeros((LANES - N_EXPERTS, tm), F32)], axis=0).T

    n_rc = (cnt_smem[e] + (rc_rows - 1)) // rc_rows
    cnt_first = cnt_smem[N_EXPERTS + e]
    token_regions = ((0, tm), (0, tm // 2), (tm // 2, tm))

    def chunk_region(r0, nrows):
        return jnp.where(r0 + nrows <= cnt_first, 1, jnp.where(r0 >= cnt_first, 2, 0))

    @pl.when(f == 0)
    def _gather():
        rsel_row = rsel_scr[pl.ds(e, 1), :]
        gate_row = gates_scr[pl.ds(e, 1), :]
        lane = lax.broadcasted_iota(I32, (tm, LANES), 1)
        rcol = jnp.sum(jnp.where(lane == e, rn_scr[...], 0.0), axis=1, keepdims=True)
        rcol_scr[...] = jnp.broadcast_to(rcol, (tm, LANES))

        def compact(r0, lo, hi):
            want = (r0 + 1 + lax.broadcasted_iota(I32, (rc_rows, hi - lo), 0)).astype(F32)
            pm = rsel_row[:, lo:hi] == want
            pb = jnp.where(pm, 1.0, 0.0).astype(BF16)
            xe_scr[pl.ds(r0, rc_rows), :] = jnp.dot(pb, h_scr[lo:hi, :],
                                                    preferred_element_type=F32).astype(BF16)
            gcol = jnp.sum(jnp.where(pm, gate_row[:, lo:hi], 0.0), axis=1, keepdims=True)
            gcol_scr[pl.ds(r0, rc_rows), :] = jnp.broadcast_to(gcol, (rc_rows, LANES))

        def body(rc, carry):
            r0 = pl.multiple_of(rc * rc_rows, rc_rows)
            for region, (lo, hi) in enumerate(token_regions):
                @pl.when(chunk_region(r0, rc_rows) == region)
                def _():
                    compact(r0, lo, hi)
            ye_scr[pl.ds(r0, rc_rows), :] = jnp.zeros((rc_rows, ye_scr.shape[1]), F32)
            return carry

        lax.fori_loop(0, n_rc, body, 0)
        tail = pl.multiple_of(n_rc * rc_rows, 16)
        ye_scr[pl.ds(tail, sc_rows), :] = jnp.zeros((sc_rows, ye_scr.shape[1]), F32)

    def ffn_body(rc, carry):
        r0 = pl.multiple_of(rc * rc_rows, rc_rows)
        xe = xe_scr[pl.ds(r0, rc_rows), :]
        a = jnp.dot(xe, wg_ref[0], preferred_element_type=F32)
        u = jnp.dot(xe, wu_ref[0], preferred_element_type=F32)
        act = (_silu(a) * u * gcol_scr[pl.ds(r0, rc_rows), 0:1]).astype(BF16)
        ye_scr[pl.ds(r0, rc_rows), :] += jnp.dot(act, wd_ref[0], preferred_element_type=F32)
        return carry

    lax.fori_loop(0, n_rc, ffn_body, 0)

    @pl.when(f == pl.num_programs(3) - 1)
    def _scatter():
        def body(rc, carry):
            r0 = pl.multiple_of(rc * sc_rows, sc_rows)
            ye = ye_scr[pl.ds(r0, sc_rows), :].astype(BF16)
            for region, (lo, hi) in enumerate(token_regions):
                @pl.when(chunk_region(r0, sc_rows) == region)
                def _():
                    want = (r0 + 1 + lax.broadcasted_iota(I32, (hi - lo, sc_rows), 1)).astype(F32)
                    pt = jnp.where(rcol_scr[lo:hi, 0:1] == want, 1.0, 0.0).astype(BF16)
                    o_ref[0, lo:hi, :] += jnp.dot(pt, ye, preferred_element_type=F32)
            return carry

        lax.fori_loop(0, (cnt_smem[e] + (sc_rows - 1)) // sc_rows, body, 0)

    @pl.when((e == pl.num_programs(2) - 1) & (f == pl.num_programs(3) - 1))
    def _residual():
        for jb in range(ntb):
            rows = slice(jb * tb, (jb + 1) * tb)
            y = x_ref[0, rows, :] + gate_ref[0] * o_ref[0, rows, :]
            if final_norm:
                var = jnp.mean(y * y, axis=-1, keepdims=True)
                y = (y * lax.rsqrt(var + EPS)) * gfin_ref[...]
            o_ref[0, rows, :] = y


def _moe(x, sh, sc, gate, g, w_router, wg, wu, wd, tm, tf, g_final=None):
    bn, s, d = x.shape
    ne, _, fdim = wg.shape
    assert ne == N_EXPERTS and MOE_RC % 16 == 0 and tm % MOE_TB == 0 and (tm // 2) % LANES == 0
    cap = -(-tm // MOE_RC) * MOE_RC + MOE_SC
    wr = jnp.zeros((d, LANES), F32).at[:, :ne].set(w_router)
    final_norm = g_final is not None
    gfin = g_final if final_norm else jnp.ones((1, d), F32)
    tok_in = pl.BlockSpec((1, tm, d), lambda b, t, e, f: (b, t, 0), pipeline_mode=pl.Buffered(1))
    tok_out = pl.BlockSpec((1, tm, d), lambda b, t, e, f: (b, t, 0), pipeline_mode=pl.Buffered(1))
    vec = pl.BlockSpec((1, 1, d), lambda b, t, e, f: (b, 0, 0))
    return pl.pallas_call(
        functools.partial(_moe_kernel, final_norm),
        grid=(bn, s // tm, ne, fdim // tf),
        in_specs=[tok_in, vec, vec, vec,
                  pl.BlockSpec((1, d), lambda b, t, e, f: (0, 0)),
                  pl.BlockSpec((1, d), lambda b, t, e, f: (0, 0)),
                  pl.BlockSpec((d, LANES), lambda b, t, e, f: (0, 0)),
                  pl.BlockSpec((1, d, tf), lambda b, t, e, f: (e, 0, f)),
                  pl.BlockSpec((1, d, tf), lambda b, t, e, f: (e, 0, f)),
                  pl.BlockSpec((1, tf, d), lambda b, t, e, f: (e, f, 0))],
        out_specs=tok_out,
        out_shape=jax.ShapeDtypeStruct((bn, s, d), F32),
        scratch_shapes=[pltpu.VMEM((tm, d), BF16),
                        pltpu.VMEM((cap, d), BF16),
                        pltpu.VMEM((cap, d), F32),
                        pltpu.VMEM((cap, LANES), F32),
                        pltpu.VMEM((ne, tm), F32),
                        pltpu.VMEM((ne, tm), F32),
                        pltpu.VMEM((tm, LANES), F32),
                        pltpu.VMEM((tm, LANES), F32),
                        pltpu.SMEM((2 * ne,), I32)],
        compiler_params=pltpu.CompilerParams(
            dimension_semantics=("parallel", "parallel", "arbitrary", "arbitrary"),
            vmem_limit_bytes=56 * 1024 * 1024),
    )(x, sh, sc, gate, g, gfin, wr, wg, wu, wd)


def _final_norm_kernel(x_ref, g_ref, o_ref):
    x = x_ref[0]
    var = jnp.mean(x * x, axis=-1, keepdims=True)
    o_ref[0] = (x * lax.rsqrt(var + EPS)) * g_ref[...]


def _final_norm(x, g, tm):
    bn, s, d = x.shape
    tok = pl.BlockSpec((1, tm, d), lambda b, t: (b, t, 0))
    return pl.pallas_call(
        _final_norm_kernel,
        grid=(bn, s // tm),
        in_specs=[tok, pl.BlockSpec((1, d), lambda b, t: (0, 0))],
        out_specs=tok,
        out_shape=jax.ShapeDtypeStruct((bn, s, d), F32),
        compiler_params=_cparams("parallel", "parallel"),
    )(x, g)


def _token_tile(s, want):
    tm = min(want, s)
    assert s % tm == 0 and tm % QB == 0
    return tm


def kernel(x, c, positions, w_ada, b_ada, g_norm_mix, g_norm_ffn, w_in, w_out, hgrn_lb_logits, hgrn_out_norm, gmlp_vnorm_g, gmlp_vnorm_b, gmlp_w_s, gmlp_b_s, ffn_w_gate, ffn_w_up, ffn_w_down, moe_w_router, moe_w_gate, moe_w_up, moe_w_down, g_final):
    bn, s, d = x.shape
    depth = w_in.shape[0]
    assert s % QB == 0 and d == sum(IN_WIDTHS[:1]) + B_DIM + C_DIM
    tm_proj = _token_tile(s, 512)
    tm_ffn = _token_tile(s, 1024)

    p_lb = jax.nn.softmax(hgrn_lb_logits.astype(F32), axis=0)
    cum = jnp.cumsum(p_lb, axis=0)
    lower_bounds = cum - cum[0:1]

    mod = _ada_mod(c, w_ada, b_ada)
    cosn, sinn, cost, sint = _rope_tables(positions)

    for l in range(depth):
        sh1, sc1, g1, sh2, sc2, g2 = [mod[l, :, i * d:(i + 1) * d].reshape(bn, 1, d) for i in range(6)]
        w_n, w_t = _prep_in_weights(w_in[l])
        (kz, ik3, bq, bf, bi, bg, c_out, qt, iq3t, vt4, iwt, bit) = _inproj(
            x, sh1, sc1, g_norm_mix[l].reshape(1, d), w_n, w_t, cosn, sinn, cost, sint,
            gmlp_vnorm_g[l].reshape(1, C_DIM), gmlp_vnorm_b[l].reshape(1, C_DIM), gmlp_w_s[l], gmlp_b_s[l],
            tm_proj)
        a_out = _dsa(ik3, kz, vt4, iq3t, iwt, qt)
        b_out = _hgrn(bq, bf, bi, bit, bg, lower_bounds[l].reshape(1, B_DIM),
                      jnp.tile(hgrn_out_norm[l], LANES // B_DV).reshape(1, LANES))
        gf = g_norm_ffn[l].reshape(1, d)
        if l % 2 == 0:
            i = l // 2
            x = _ffn(x, a_out, b_out, c_out, w_out[l], g1, sh2, sc2, g2, gf, ffn_w_gate[i].astype(BF16),
                     ffn_w_up[i].astype(BF16), ffn_w_down[i].astype(BF16), tm_ffn, 256)
        else:
            i = l // 2
            x = _outproj(x, a_out, b_out, c_out, w_out[l], g1, tm_proj)
            fused_final = l == depth - 1
            x = _moe(x, sh2, sc2, g2, gf, moe_w_router[i], moe_w_gate[i].astype(BF16),
                     moe_w_up[i].astype(BF16), moe_w_down[i].astype(BF16), _token_tile(s, 2048), 896,
                     g_final.reshape(1, d) if fused_final else None)
    if depth % 2 == 0:
        return x
    return _final_norm(x, g_final.reshape(1, d), tm_proj)
```

```python
import functools

import numpy as np
import jax
import jax.numpy as jnp
from jax import lax
from jax.experimental import pallas as pl
from jax.experimental.pallas import tpu as pltpu

F32 = jnp.float32
BF16 = jnp.bfloat16
I32 = jnp.int32
I16 = jnp.int16

HEAD_DIM = 64
A_HEADS = 6
A_DIM = A_HEADS * HEAD_DIM
IDX_HEADS = 4
IDX_DIM = 64
TOPK_MAX = 256
B_HEADS = 6
B_DK = 64
B_DV = 64
B_DIM = B_HEADS * B_DV
HGRN_CHUNK = 64
C_GROUPS = 4
C_GROUP_DIM = 64
C_DIM = C_GROUPS * C_GROUP_DIM
C_CHUNK = 128
ROPE_THETA = 10000.0
N_EXPERTS = 8
EPS = 1e-6
IN_WIDTHS = (A_DIM, HEAD_DIM, HEAD_DIM, IDX_HEADS * IDX_DIM, IDX_DIM, IDX_HEADS,
             B_DIM, B_DIM, B_DIM, B_DIM, 2 * C_DIM)

LANES = 128
KC = 128
QB = 2 * KC
PV_ROWS = HEAD_DIM + 16
VMEM_LIMIT = 48 * 1024 * 1024
HGRN_FAST_SPAN = 80.0
HGRN_SUB = 32
HGRN_GROUP = 4
INT_MIN = -2 ** 31

NT_DIMS = (((1,), (1,)), ((), ()))
TN_DIMS = (((0,), (0,)), ((), ()))


def _cparams(*sem):
    return pltpu.CompilerParams(dimension_semantics=sem, vmem_limit_bytes=VMEM_LIMIT)


def _split2(x):
    hi = x.astype(BF16)
    lo = (x - hi.astype(F32)).astype(BF16)
    return hi, lo


def _split3(x):
    hi = x.astype(BF16)
    r1 = x - hi.astype(F32)
    mid = r1.astype(BF16)
    lo = (r1 - mid.astype(F32)).astype(BF16)
    return hi, mid, lo


def _silu(x):
    return x * (1.0 / (1.0 + jnp.exp(-x)))


def _modulated_rmsnorm(x, g, sc, sh):
    var = jnp.mean(x * x, axis=-1, keepdims=True)
    y = x * lax.rsqrt(var + EPS)
    return (y * g) * (1.0 + sc) + sh


def _ada_kernel(c_ref, w_ref, b_ref, o_ref):
    cond = _silu(c_ref[...])
    o_ref[0] = jnp.dot(cond, w_ref[0], precision=lax.Precision.HIGHEST,
                       preferred_element_type=F32) + b_ref[0]


def _ada_mod(c, w_ada, b_ada):
    depth, d, d6 = w_ada.shape
    bn = c.shape[0]
    tn = 1536
    return pl.pallas_call(
        _ada_kernel,
        grid=(depth, d6 // tn),
        in_specs=[pl.BlockSpec((bn, d), lambda l, n: (0, 0)),
                  pl.BlockSpec((1, d, tn), lambda l, n: (l, 0, n)),
                  pl.BlockSpec((1, 1, tn), lambda l, n: (l, 0, n))],
        out_specs=pl.BlockSpec((1, bn, tn), lambda l, n: (l, 0, n)),
        out_shape=jax.ShapeDtypeStruct((depth, bn, d6), F32),
        compiler_params=_cparams("parallel", "parallel"),
    )(c, w_ada, b_ada.reshape(depth, 1, d6))


def _rope_kernel(posn_ref, post_ref, invn_ref, signn_ref, invt_ref,
                 cosn_ref, sinn_ref, cost_ref, sint_ref):
    ang_n = posn_ref[0].astype(F32) * invn_ref[...]
    cosn_ref[0] = jnp.cos(ang_n)
    sinn_ref[0] = jnp.sin(ang_n) * signn_ref[...]
    ang_t = invt_ref[...] * post_ref[0].astype(F32)
    cost_ref[0] = jnp.cos(ang_t)
    sint_ref[0] = jnp.sin(ang_t)


def _rope_tables(positions):
    bn, s = positions.shape
    half = HEAD_DIM // 2
    inv = ROPE_THETA ** (-jnp.arange(0, HEAD_DIM, 2, dtype=F32) / HEAD_DIM)
    inv_n = jnp.tile(inv, LANES // half).reshape(1, LANES)
    sign_n = jnp.tile(jnp.concatenate([-jnp.ones((half,), F32), jnp.ones((half,), F32)]),
                      LANES // HEAD_DIM).reshape(1, LANES)
    inv_t = inv.reshape(half, 1)
    full = lambda shape: pl.BlockSpec(shape, lambda b: (0,) * len(shape))
    return pl.pallas_call(
        _rope_kernel,
        grid=(bn,),
        in_specs=[pl.BlockSpec((1, s, 1), lambda b: (b, 0, 0)),
                  pl.BlockSpec((1, 1, s), lambda b: (b, 0, 0)),
                  full((1, LANES)), full((1, LANES)), full((half, 1))],
        out_specs=[pl.BlockSpec((1, s, LANES), lambda b: (b, 0, 0)),
                   pl.BlockSpec((1, s, LANES), lambda b: (b, 0, 0)),
                   pl.BlockSpec((1, half, s), lambda b: (b, 0, 0)),
                   pl.BlockSpec((1, half, s), lambda b: (b, 0, 0))],
        out_shape=[jax.ShapeDtypeStruct((bn, s, LANES), F32),
                   jax.ShapeDtypeStruct((bn, s, LANES), F32),
                   jax.ShapeDtypeStruct((bn, half, s), F32),
                   jax.ShapeDtypeStruct((bn, half, s), F32)],
        compiler_params=_cparams("parallel"),
    )(positions.reshape(bn, s, 1), positions.reshape(bn, 1, s), inv_n, sign_n, inv_t)


N_KI = 0
N_BQ = 2 * LANES
N_BF = N_BQ + B_DIM
N_BI = N_BF + B_DIM
N_BG = N_BI + B_DIM
N_CUV = N_BG + B_DIM
N_COLS = N_CUV + 2 * C_DIM
T_Q = 0
T_IQ = A_DIM
T_V = T_IQ + IDX_HEADS * IDX_DIM
T_IW = T_V + HEAD_DIM
T_BI = T_IW + 8
T_ROWS = T_BI + B_DIM


def _prep_in_weights(w_in_l):
    offs = np.concatenate([[0], np.cumsum(IN_WIDTHS)])
    sl = lambda i: w_in_l[:, int(offs[i]):int(offs[i + 1])]
    aq, ak, av, iq, ik, iw, bq, bf, bi, bg, cuv = [sl(i) for i in range(11)]
    half = HEAD_DIM // 2
    rot = lambda w: jnp.concatenate([w[:, half:], w[:, :half]], axis=1)
    w_n = jnp.concatenate([ak, ik, rot(ak), rot(ik), bq, bf, bi, bg, cuv], axis=1)
    w_t = jnp.concatenate([aq, iq, av, iw, jnp.zeros((w_in_l.shape[0], 4), w_in_l.dtype), bi], axis=1).T
    return w_n.astype(BF16), w_t.astype(BF16)


def _inproj_kernel(x_ref, sh_ref, sc_ref, g_ref, wn_ref, wt_ref, cosn_ref, sinn_ref, cost_ref, sint_ref,
                   vg_ref, vb_ref, ws_ref, cbias_ref,
                   kz_ref, ik3_ref, bq_ref, bf_ref, bi_ref, bg_ref, c_ref,
                   qt_ref, iq3t_ref, vt_ref, iwt_ref, bit_ref):
    tm = x_ref.shape[1]
    half = HEAD_DIM // 2
    h = _modulated_rmsnorm(x_ref[0], g_ref[...], sc_ref[0], sh_ref[0]).astype(BF16)

    cosn = cosn_ref[0]
    sinn = sinn_ref[0]
    zk = jnp.dot(h, wn_ref[:, N_KI:N_KI + 2 * LANES], preferred_element_type=F32)
    ki = zk[:, :LANES] * cosn + zk[:, LANES:] * sinn
    lane = lax.broadcasted_iota(I32, (tm, LANES), 1)
    kz_ref[0] = jnp.where(lane < HEAD_DIM, ki, 0.0).astype(BF16)
    ik2 = jnp.where(lane < IDX_DIM, pltpu.roll(ki, HEAD_DIM, axis=1), ki)
    hi, lo = _split2(ik2)
    ik3_ref[0, :, :LANES] = jnp.where(lane < IDX_DIM, hi, lo)
    ik3_ref[0, :, LANES:] = hi
    bq_ref[0] = jnp.dot(h, wn_ref[:, N_BQ:N_BF], preferred_element_type=F32).astype(BF16)
    bf_ref[0] = jnp.dot(h, wn_ref[:, N_BF:N_BI], preferred_element_type=F32)
    bi_ref[0] = jnp.dot(h, wn_ref[:, N_BI:N_BG], preferred_element_type=F32).astype(BF16)
    bg_ref[0] = jnp.dot(h, wn_ref[:, N_BG:N_CUV], preferred_element_type=F32).astype(BF16)
    cuv = jnp.dot(h, wn_ref[:, N_CUV:N_COLS], preferred_element_type=F32)
    c_ref[0] = _gmlp_mix(cuv, vg_ref[...], vb_ref[...], ws_ref, cbias_ref[...]).astype(BF16)

    zt = lax.dot_general(wt_ref[...], h, NT_DIMS, preferred_element_type=F32)
    cost = cost_ref[0]
    sint = sint_ref[0]
    qscale = HEAD_DIM ** -0.5
    for hh in range(A_HEADS):
        r0 = T_Q + hh * HEAD_DIM
        x1 = zt[r0:r0 + half]
        x2 = zt[r0 + half:r0 + HEAD_DIM]
        qt_ref[0, hh * HEAD_DIM:hh * HEAD_DIM + half] = ((x1 * cost - x2 * sint) * qscale).astype(BF16)
        qt_ref[0, hh * HEAD_DIM + half:(hh + 1) * HEAD_DIM] = ((x2 * cost + x1 * sint) * qscale).astype(BF16)
    iscale = IDX_DIM ** -0.5
    zero = jnp.zeros((IDX_DIM, tm), BF16)
    for hh in range(IDX_HEADS):
        r0 = T_IQ + hh * IDX_DIM
        x1 = zt[r0:r0 + half]
        x2 = zt[r0 + half:r0 + IDX_DIM]
        y = jnp.concatenate([(x1 * cost - x2 * sint) * iscale, (x2 * cost + x1 * sint) * iscale], axis=0)
        hi, lo = _split2(y)
        iq3t_ref[0, hh, 0 * IDX_DIM:1 * IDX_DIM] = hi
        iq3t_ref[0, hh, 1 * IDX_DIM:2 * IDX_DIM] = hi
        iq3t_ref[0, hh, 2 * IDX_DIM:3 * IDX_DIM] = lo
        iq3t_ref[0, hh, 3 * IDX_DIM:4 * IDX_DIM] = zero
    vt = zt[T_V:T_V + HEAD_DIM].astype(BF16)
    for i in range(tm // KC):
        vt_ref[0, i] = vt[:, i * KC:(i + 1) * KC]
    iwt_ref[0] = zt[T_IW:T_IW + 8] * (IDX_HEADS ** -0.5)
    bit = zt[T_BI:T_BI + B_DIM].astype(BF16)
    for i in range(tm // LANES):
        bit_ref[0, i] = bit[:, i * LANES:(i + 1) * LANES]


def _inproj(x, sh, sc, g, w_n, w_t, cosn, sinn, cost, sint, vg, vb, ws, bs, tm):
    bn, s, d = x.shape
    assert tm % C_CHUNK == 0
    nt = s // tm
    half = HEAD_DIM // 2
    cbias = jnp.repeat(bs.T, C_GROUP_DIM, axis=1)
    tok = lambda w: pl.BlockSpec((1, tm, w), lambda b, t: (b, t, 0))
    vec = pl.BlockSpec((1, 1, d), lambda b, t: (b, 0, 0))
    full2 = lambda a: pl.BlockSpec(a.shape, lambda b, t: (0, 0))
    out_shapes = [
        jax.ShapeDtypeStruct((bn, s, LANES), BF16),
        jax.ShapeDtypeStruct((bn, s, 2 * LANES), BF16),
        jax.ShapeDtypeStruct((bn, s, B_DIM), BF16),
        jax.ShapeDtypeStruct((bn, s, B_DIM), F32),
        jax.ShapeDtypeStruct((bn, s, B_DIM), BF16),
        jax.ShapeDtypeStruct((bn, s, B_DIM), BF16),
        jax.ShapeDtypeStruct((bn, s, C_DIM), BF16),
        jax.ShapeDtypeStruct((bn, A_DIM, s), BF16),
        jax.ShapeDtypeStruct((bn, IDX_HEADS, 4 * IDX_DIM, s), BF16),
        jax.ShapeDtypeStruct((bn, s // KC, HEAD_DIM, KC), BF16),
        jax.ShapeDtypeStruct((bn, 8, s), F32),
        jax.ShapeDtypeStruct((bn, s // LANES, B_DIM, LANES), BF16),
    ]
    out_specs = [
        tok(LANES), tok(2 * LANES), tok(B_DIM), tok(B_DIM), tok(B_DIM), tok(B_DIM), tok(C_DIM),
        pl.BlockSpec((1, A_DIM, tm), lambda b, t: (b, 0, t)),
        pl.BlockSpec((1, IDX_HEADS, 4 * IDX_DIM, tm), lambda b, t: (b, 0, 0, t)),
        pl.BlockSpec((1, tm // KC, HEAD_DIM, KC), lambda b, t: (b, t, 0, 0)),
        pl.BlockSpec((1, 8, tm), lambda b, t: (b, 0, t)),
        pl.BlockSpec((1, tm // LANES, B_DIM, LANES), lambda b, t: (b, t, 0, 0)),
    ]
    return pl.pallas_call(
        _inproj_kernel,
        grid=(bn, nt),
        in_specs=[tok(d), vec, vec, full2(g), full2(w_n), full2(w_t),
                  tok(LANES), tok(LANES),
                  pl.BlockSpec((1, half, tm), lambda b, t: (b, 0, t)),
                  pl.BlockSpec((1, half, tm), lambda b, t: (b, 0, t)),
                  full2(vg), full2(vb), pl.BlockSpec(ws.shape, lambda b, t: (0, 0, 0)), full2(cbias)],
        out_specs=out_specs,
        out_shape=out_shapes,
        compiler_params=_cparams("parallel", "parallel"),
    )(x, sh, sc, g, w_n, w_t, cosn, sinn, cost, sint, vg, vb, ws, cbias)


def _dsa_kernel(n_top, ik3_ref, kz_ref, vt_ref, iq3t_ref, iwt_ref, qt_ref, out_ref,
                key_scr, hi_scr, lo_scr, lg_scr, acc_scr):
    j = pl.program_id(1)
    npair = j + 1
    pair = 2 * KC
    row = lax.broadcasted_iota(I32, (KC, QB), 0)
    col = lax.broadcasted_iota(I32, (KC, QB), 1)
    int_min = jnp.int32(INT_MIN)
    i16_min = jnp.int16(-2 ** 15)

    def score_pair(p, diagonal):
        for u in range(2):
            ks = pl.multiple_of(p * pair + u * KC, KC)
            ikc = ik3_ref[0, pl.ds(ks, KC), :]
            sc = jnp.zeros((KC, QB), F32)
            for hh in range(IDX_HEADS):
                rel = jnp.dot(ikc, iq3t_ref[0, hh], preferred_element_type=F32)
                sc = sc + jnp.maximum(rel, 0.0) * iwt_ref[0, hh:hh + 1, :]
            sc = jnp.where(sc == 0.0, 0.0, sc)
            bits = pltpu.bitcast(sc, I32)
            key = bits ^ ((bits >> 31) & jnp.int32(0x7FFFFFFF))
            if diagonal:
                key = jnp.where((u * KC + row) <= col, key, int_min)
            key_scr[pl.ds(ks, KC), :] = key
            hi_scr[pl.ds(ks, KC), :] = (key >> 16).astype(I16)
            lo_scr[pl.ds(ks, KC), :] = ((key & jnp.int32(0xFFFF)) - 32768).astype(I16)

    def score_body(p, carry):
        score_pair(p, False)
        return carry

    lax.fori_loop(0, j, score_body, 0)
    score_pair(j, True)

    def count16(ref, pred_fn):
        def body(p, acc):
            ks = pl.multiple_of(p * pair, pair)
            m = jnp.where(pred_fn(ref[pl.ds(ks, pair), :]), jnp.int16(1), jnp.int16(0))
            parts = [m[16 * i:16 * (i + 1)] for i in range(pair // 16)]
            while len(parts) > 1:
                parts = [parts[i] + parts[i + 1] for i in range(0, len(parts), 2)]
            return acc + parts[0]
        acc = lax.fori_loop(0, npair, body, jnp.zeros((16, QB), I16))
        return jnp.sum(acc.astype(I32), axis=0, keepdims=True)

    def bisect16(ref, k_needed):
        def bit_body(i, t_u):
            cand = t_u | jnp.left_shift(jnp.int32(1), 15 - i)
            cand16 = (cand - 32768).astype(I16)
            cnt = count16(ref, lambda x: x >= cand16)
            return jnp.where(cnt >= k_needed, cand, t_u)
        return lax.fori_loop(0, 16, bit_body, jnp.zeros((1, QB), I32))

    t_hi = bisect16(hi_scr, n_top)
    t_hi16 = (t_hi - 32768).astype(I16)
    n_hi_gt = count16(hi_scr, lambda x: x > t_hi16)

    def bucket_body(p, carry):
        ks = pl.multiple_of(p * pair, pair)
        in_bucket = hi_scr[pl.ds(ks, pair), :] == t_hi16
        lo_scr[pl.ds(ks, pair), :] = jnp.where(in_bucket, lo_scr[pl.ds(ks, pair), :], i16_min)
        return carry

    lax.fori_loop(0, npair, bucket_body, 0)
    t_lo = bisect16(lo_scr, n_top - n_hi_gt)
    t_lo16 = (t_lo - 32768).astype(I16)
    n_gt = n_hi_gt + count16(lo_scr, lambda x: x > t_lo16)
    thr = jnp.left_shift(t_hi - 32768, 16) | t_lo
    n_tie = (n_top - n_gt).astype(F32)
    thr_valid = jnp.where(thr > int_min, 1.0, 0.0)

    tr = lax.broadcasted_iota(I32, (KC, KC), 0)
    tc = lax.broadcasted_iota(I32, (KC, KC), 1)
    ltri = jnp.where(tr >= tc, 1.0, 0.0).astype(BF16)
    zpad = jnp.zeros((HEAD_DIM, QB), BF16)
    qpad = [jnp.concatenate([qt_ref[0, hh * HEAD_DIM:(hh + 1) * HEAD_DIM, :], zpad], axis=0)
            for hh in range(A_HEADS)]
    neg_inf = jnp.float32(-jnp.inf)

    def pass_a(p, carry):
        tie_cnt, ms = carry
        ms = list(ms)
        for u in range(2):
            ks = pl.multiple_of(p * pair + u * KC, KC)
            kc = key_scr[pl.ds(ks, KC), :]
            eqf = jnp.where(kc == thr, thr_valid, 0.0)
            pref = jnp.dot(ltri, eqf.astype(BF16), preferred_element_type=F32)
            keep_tie = eqf * jnp.where((tie_cnt + pref) <= n_tie, 1.0, 0.0)
            sel = jnp.where(kc > thr, 1.0, keep_tie) > 0.5
            tie_cnt = tie_cnt + pref[KC - 1:KC, :]
            kzc = kz_ref[0, pl.ds(ks, KC), :]
            for hh in range(A_HEADS):
                lt = jnp.dot(kzc, qpad[hh], preferred_element_type=F32)
                lt = jnp.where(sel, lt, neg_inf)
                lg_scr[hh, pl.ds(ks, KC), :] = lt
                ms[hh] = jnp.maximum(ms[hh], jnp.max(lt, axis=0, keepdims=True))
        return tie_cnt, tuple(ms)

    init_m = tuple(jnp.full((1, QB), neg_inf, F32) for _ in range(A_HEADS))
    _, ms = lax.fori_loop(0, npair, pass_a, (jnp.zeros((1, QB), F32), init_m))

    acc_scr[...] = jnp.zeros_like(acc_scr)
    ones_rows = jnp.ones((PV_ROWS - HEAD_DIM, pair), BF16)

    def pass_b(p, carry):
        ks = pl.multiple_of(p * pair, pair)
        vt2 = jnp.concatenate([vt_ref[0, 2 * p], vt_ref[0, 2 * p + 1]], axis=1)
        vt2 = jnp.concatenate([vt2, ones_rows], axis=0)
        for hh in range(A_HEADS):
            pr = jnp.exp(lg_scr[hh, pl.ds(ks, pair), :] - ms[hh])
            acc_scr[hh * PV_ROWS:(hh + 1) * PV_ROWS, :] += jnp.dot(
                vt2, pr.astype(BF16), preferred_element_type=F32)
        return carry

    lax.fori_loop(0, npair, pass_b, 0)
    o_t = jnp.concatenate(
        [acc_scr[hh * PV_ROWS:hh * PV_ROWS + HEAD_DIM, :]
         / acc_scr[hh * PV_ROWS + HEAD_DIM:hh * PV_ROWS + HEAD_DIM + 1, :] for hh in range(A_HEADS)], axis=0)
    out_ref[0] = o_t.T.astype(BF16)


def _dsa(ik3, kz, vt4, iq3t, iwt, qt):
    bn, s, _ = kz.shape
    assert s % QB == 0
    n_top = min(TOPK_MAX, s // 4)
    nq = s // QB
    return pl.pallas_call(
        functools.partial(_dsa_kernel, n_top),
        grid=(bn, nq),
        in_specs=[pl.BlockSpec((1, s, 2 * LANES), lambda b, q: (b, 0, 0)),
                  pl.BlockSpec((1, s, LANES), lambda b, q: (b, 0, 0)),
                  pl.BlockSpec((1, s // KC, HEAD_DIM, KC), lambda b, q: (b, 0, 0, 0)),
                  pl.BlockSpec((1, IDX_HEADS, 4 * IDX_DIM, QB), lambda b, q: (b, 0, 0, q)),
                  pl.BlockSpec((1, 8, QB), lambda b, q: (b, 0, q)),
                  pl.BlockSpec((1, A_DIM, QB), lambda b, q: (b, 0, q))],
        out_specs=pl.BlockSpec((1, QB, A_DIM), lambda b, q: (b, q, 0)),
        out_shape=jax.ShapeDtypeStruct((bn, s, A_DIM), BF16),
        scratch_shapes=[pltpu.VMEM((s, QB), I32), pltpu.VMEM((s, QB), I16), pltpu.VMEM((s, QB), I16),
                        pltpu.VMEM((A_HEADS, s, QB), F32), pltpu.VMEM((A_HEADS * PV_ROWS, QB), F32)],
        compiler_params=_cparams("parallel", "arbitrary"),
    )(ik3, kz, vt4, iq3t, iwt, qt)


def _hgrn_kernel(q_ref, f_ref, i_ref, it_ref, g_ref, lb_ref, gn_ref, o_ref, st_scr, oi_scr):
    s = q_ref.shape[1]
    ch = HGRN_CHUNK
    sb = HGRN_SUB
    nsb = ch // sb
    rows = HGRN_GROUP * ch
    wins_per_group = rows // LANES
    lane = lax.broadcasted_iota(I32, (ch, LANES), 1)
    head0 = lane < B_DK
    lr = lax.broadcasted_iota(I32, (rows, 3 * rows), 0)
    lc = lax.broadcasted_iota(I32, (rows, 3 * rows), 1) % rows
    lcum = jnp.where(((lr // ch) == (lc // ch)) & (lc <= lr), 1.0, 0.0).astype(BF16)
    att_rows = HGRN_GROUP * 2 * ch
    att_cols = HGRN_GROUP * nsb * ch
    ar = lax.broadcasted_iota(I32, (att_rows, att_cols), 0)
    ac = lax.broadcasted_iota(I32, (att_rows, att_cols), 1)
    at = ar % ch
    att_mask = (((ar // (2 * ch)) == (ac // (nsb * ch))) & ((at // sb) == ((ac // ch) % nsb))
                & ((ac % ch) <= at))
    bd_r = lax.broadcasted_iota(I32, (LANES, LANES), 0)
    bd_c = lax.broadcasted_iota(I32, (LANES, LANES), 1)
    same_head = (bd_r < B_DK) == (bd_c < B_DK)
    ones_bd = jnp.where(same_head, 1.0, 0.0).astype(BF16)
    ones_bd2 = jnp.concatenate([ones_bd, ones_bd], axis=0)
    s_iota = lax.broadcasted_iota(I32, (ch, LANES), 0)

    lb = lb_ref[...]
    log_lb = jnp.log(lb)
    log_1mlb = jnp.log1p(-lb)
    gn = gn_ref[...]
    st_scr[...] = jnp.zeros_like(st_scr)

    def group_body(it, carry):
        t0 = pl.multiple_of(it * rows, rows)
        z = f_ref[0, pl.ds(t0, rows), :]
        q = q_ref[0, pl.ds(t0, rows), :].astype(F32)
        v = i_ref[0, pl.ds(t0, rows), :]
        softplus_tail = jnp.log1p(jnp.exp(-jnp.abs(z)))
        log_sig = -(jnp.maximum(-z, 0.0) + softplus_tail)
        x2 = log_1mlb + log_sig
        amax = jnp.maximum(log_lb, x2)
        log_f = amax + jnp.log1p(jnp.exp(-jnp.abs(log_lb - x2)))
        kk = (1.0 - lb) * jnp.exp(-(jnp.maximum(z, 0.0) + softplus_tail))
        f_hi, f_mid, f_lo = _split3(log_f)
        b = jnp.dot(lcum, jnp.concatenate([f_hi, f_mid, f_lo], axis=0), preferred_element_type=F32)
        ref_parts = []
        for c in range(HGRN_GROUP):
            ref_parts.append(jnp.zeros((sb, LANES), F32))
            for i in range(1, nsb):
                r = c * ch + i * sb
                ref_parts.append(jnp.broadcast_to(b[r - 1:r], (sb, LANES)))
        ref = jnp.concatenate(ref_parts, axis=0)
        q_loc = q * jnp.exp(b - ref)
        q_chk = (q * jnp.exp(b)).astype(BF16)
        span = ref - b
        span_max = jnp.max(span)

        k_parts, q_parts, v_parts, kh_cols, decay = [], [], [], [], []
        for c in range(HGRN_GROUP):
            lo_r = c * ch
            b_c = b[lo_r:lo_r + ch]
            kk_c = kk[lo_r:lo_r + ch]
            b_last = b_c[ch - 1:ch]
            for i in range(nsb):
                r = lo_r + i * sb
                k_parts.append(kk_c * jnp.exp(jnp.minimum(ref[r:r + 1] - b_c, HGRN_FAST_SPAN)))
            q_c = q_loc[lo_r:lo_r + ch]
            q_parts += [jnp.where(head0, q_c, 0.0), jnp.where(head0, 0.0, q_c)]
            v_parts += [v[lo_r:lo_r + ch]] * nsb
            kh = (kk_c * jnp.exp(b_last - b_c)).astype(BF16)
            kh_cols.append(jnp.concatenate(
                ([jnp.zeros((lo_r, LANES), BF16)] if lo_r else []) + [kh]
                + ([jnp.zeros((rows - lo_r - ch, LANES), BF16)] if rows - lo_r - ch else []), axis=0))
            decay.append(jnp.exp(b_last))
        att = lax.dot_general(jnp.concatenate(q_parts, axis=0).astype(BF16),
                              jnp.concatenate(k_parts, axis=0).astype(BF16), NT_DIMS,
                              preferred_element_type=F32)
        att = jnp.where(att_mask, att, 0.0).astype(BF16)
        o2 = jnp.dot(att, jnp.concatenate(v_parts, axis=0), preferred_element_type=F32)
        for c in range(HGRN_GROUP):
            r = c * 2 * ch
            oi_scr[c * ch:(c + 1) * ch, :] = jnp.where(head0, o2[r:r + ch], o2[r + ch:r + 2 * ch])

        it_win = jnp.concatenate([it_ref[0, wins_per_group * it + w] for w in range(wins_per_group)], axis=1)
        upd = jnp.dot(it_win, jnp.concatenate(kh_cols, axis=1), preferred_element_type=F32)
        st = st_scr[...]
        states = []
        for c in range(HGRN_GROUP):
            states.append(st.astype(BF16))
            st = st * decay[c] + jnp.where(same_head, upd[:, c * LANES:(c + 1) * LANES], 0.0)
        st_scr[...] = st
        oi_all = lax.dot_general(q_chk, jnp.concatenate(states, axis=0), NT_DIMS,
                                 preferred_element_type=F32)
        o_inter = [oi_all[c * ch:(c + 1) * ch, c * LANES:(c + 1) * LANES] for c in range(HGRN_GROUP)]

        @pl.when(span_max > HGRN_FAST_SPAN)
        def _():
            for c in range(HGRN_GROUP):
                lo_r = c * ch

                @pl.when(jnp.max(span[lo_r:lo_r + ch]) > HGRN_FAST_SPAN)
                def _():
                    b_c = b[lo_r:lo_r + ch]
                    q_c = q[lo_r:lo_r + ch]
                    kk_c = kk[lo_r:lo_r + ch]
                    vf = v[lo_r:lo_r + ch].astype(F32)

                    def t_body(t, carry2):
                        onehot = jnp.where(s_iota == t, 1.0, 0.0)
                        b_t = jnp.sum(onehot * b_c, axis=0, keepdims=True)
                        q_t = jnp.sum(onehot * q_c, axis=0, keepdims=True)
                        dec = jnp.exp(jnp.where(s_iota <= t, b_t - b_c, -jnp.inf))
                        w = q_t * kk_c * dec
                        w0 = jnp.sum(jnp.where(head0, w, 0.0), axis=1, keepdims=True)
                        w1 = jnp.sum(jnp.where(head0, 0.0, w), axis=1, keepdims=True)
                        a_col = jnp.where(head0, w0, w1)
                        oi_scr[pl.ds(lo_r + t, 1), :] = jnp.sum(a_col * vf, axis=0, keepdims=True)
                        return carry2

                    lax.fori_loop(0, ch, t_body, 0)

        o = oi_scr[...] + jnp.concatenate(o_inter, axis=0)
        sq_hi, sq_lo = _split2(o * o)
        ss = jnp.dot(jnp.concatenate([sq_hi, sq_lo], axis=1), ones_bd2, preferred_element_type=F32)
        y = (o * lax.rsqrt(ss * (1.0 / B_DV) + EPS)) * gn
        g = g_ref[0, pl.ds(t0, rows), :].astype(F32)
        o_ref[0, pl.ds(t0, rows), :] = (y * _silu(g)).astype(o_ref.dtype)
        return carry

    lax.fori_loop(0, s // rows, group_body, 0)


def _hgrn(bq, bf, bi, bit, bg, lb, gn):
    bn, s, _ = bq.shape
    assert s % (HGRN_GROUP * HGRN_CHUNK) == 0 and 2 * HGRN_CHUNK == LANES and HGRN_GROUP % 2 == 0
    npair = B_DIM // LANES
    tok = pl.BlockSpec((1, s, LANES), lambda b, p: (b, 0, p))
    return pl.pallas_call(
        _hgrn_kernel,
        grid=(bn, npair),
        in_specs=[tok, tok, tok,
                  pl.BlockSpec((1, s // LANES, LANES, LANES), lambda b, p: (b, 0, p, 0)),
                  tok,
                  pl.BlockSpec((1, LANES), lambda b, p: (0, p)),
                  pl.BlockSpec((1, LANES), lambda b, p: (0, 0))],
        out_specs=tok,
        out_shape=jax.ShapeDtypeStruct((bn, s, B_DIM), BF16),
        scratch_shapes=[pltpu.VMEM((LANES, LANES), F32),
                        pltpu.VMEM((HGRN_GROUP * HGRN_CHUNK, LANES), F32)],
        compiler_params=_cparams("parallel", "parallel"),
    )(bq, bf, bi, bit, bg, lb, gn)


def _gmlp_mix(uv, vg, vb, ws_ref, bias):
    uv = 0.5 * uv * (1.0 + lax.erf(uv * (2.0 ** -0.5)))
    u = uv[:, :C_DIM]
    v = uv[:, C_DIM:]
    mu = jnp.mean(v, axis=-1, keepdims=True)
    var = jnp.mean(jnp.square(v - mu), axis=-1, keepdims=True)
    vn_b = (((v - mu) * lax.rsqrt(var + EPS)) * vg + vb).astype(BF16)
    r_t = lax.broadcasted_iota(I32, (C_CHUNK, C_CHUNK), 0)
    r_s = lax.broadcasted_iota(I32, (C_CHUNK, C_CHUNK), 1)
    group = lax.broadcasted_iota(I32, (C_CHUNK, C_DIM), 1) // C_GROUP_DIM
    ws = [jnp.where(r_t >= r_s, ws_ref[gi], 0.0).astype(BF16) for gi in range(C_GROUPS)]
    outs = []
    for c in range(uv.shape[0] // C_CHUNK):
        sl = slice(c * C_CHUNK, (c + 1) * C_CHUNK)
        mixed = jnp.zeros((C_CHUNK, C_DIM), F32)
        for gi in range(C_GROUPS):
            m = jnp.dot(ws[gi], vn_b[sl], preferred_element_type=F32)
            mixed = jnp.where(group == gi, m, mixed)
        outs.append(u[sl] * (mixed + bias))
    return jnp.concatenate(outs, axis=0)


def _outproj_kernel(x_ref, a_ref, b_ref, c_ref, wa_ref, wb_ref, wc_ref, g1_ref, o_ref):
    mix = (jnp.dot(a_ref[0], wa_ref[...], preferred_element_type=F32)
           + jnp.dot(b_ref[0], wb_ref[...], preferred_element_type=F32)
           + jnp.dot(c_ref[0], wc_ref[...], preferred_element_type=F32))
    o_ref[0] = x_ref[0] + g1_ref[0] * mix


def _outproj(x, a, b, c, w_out_l, g1, tm):
    bn, s, d = x.shape
    wa = w_out_l[:A_DIM].astype(BF16)
    wb = w_out_l[A_DIM:A_DIM + B_DIM].astype(BF16)
    wc = w_out_l[A_DIM + B_DIM:].astype(BF16)
    tok = lambda w: pl.BlockSpec((1, tm, w), lambda bb, t: (bb, t, 0))
    full = lambda arr: pl.BlockSpec(arr.shape, lambda bb, t: (0, 0))
    return pl.pallas_call(
        _outproj_kernel,
        grid=(bn, s // tm),
        in_specs=[tok(d), tok(A_DIM), tok(B_DIM), tok(C_DIM), full(wa), full(wb), full(wc),
                  pl.BlockSpec((1, 1, d), lambda bb, t: (bb, 0, 0))],
        out_specs=tok(d),
        out_shape=jax.ShapeDtypeStruct((bn, s, d), F32),
        compiler_params=_cparams("parallel", "parallel"),
    )(x, a, b, c, wa, wb, wc, g1)


def _column_blocks(w, tf):
    *lead, d, f = w.shape
    return jnp.swapaxes(w.astype(BF16).reshape(*lead, d, f // tf, tf), -3, -2)


def _ffn_kernel(x_ref, a_ref, b_ref, c_ref, wa_ref, wb_ref, wc_ref, g1_ref, sh_ref, sc_ref, gate_ref, g_ref,
                wg_ref, wu_ref, wd_ref, o_ref, xn_scr, h_scr, acc_scr):
    f = pl.program_id(2)

    @pl.when(f == 0)
    def _():
        mix = (jnp.dot(a_ref[0], wa_ref[...], preferred_element_type=F32)
               + jnp.dot(b_ref[0], wb_ref[...], preferred_element_type=F32)
               + jnp.dot(c_ref[0], wc_ref[...], preferred_element_type=F32))
        xn = x_ref[0] + g1_ref[0] * mix
        xn_scr[...] = xn
        h_scr[...] = _modulated_rmsnorm(xn, g_ref[...], sc_ref[0], sh_ref[0]).astype(BF16)
        acc_scr[...] = jnp.zeros_like(acc_scr)

    h = h_scr[...]
    a = jnp.dot(h, wg_ref[0], preferred_element_type=F32)
    u = jnp.dot(h, wu_ref[0], preferred_element_type=F32)
    act = (_silu(a) * u).astype(BF16)
    acc_scr[...] += jnp.dot(act, wd_ref[...], preferred_element_type=F32)

    @pl.when(f == pl.num_programs(2) - 1)
    def _():
        o_ref[0] = xn_scr[...] + gate_ref[0] * acc_scr[...]


def _ffn(x, a, b, c, w_out_l, g1, sh, sc, gate, g, wg, wu, wd, tm, tf):
    bn, s, d = x.shape
    fdim = wg.shape[1]
    wg, wu = _column_blocks(wg, tf), _column_blocks(wu, tf)
    wa = w_out_l[:A_DIM].astype(BF16)
    wb = w_out_l[A_DIM:A_DIM + B_DIM].astype(BF16)
    wc = w_out_l[A_DIM + B_DIM:].astype(BF16)
    tok = pl.BlockSpec((1, tm, d), lambda b, t, f: (b, t, 0))
    tokw = lambda w: pl.BlockSpec((1, tm, w), lambda b, t, f: (b, t, 0))
    full = lambda arr: pl.BlockSpec(arr.shape, lambda b, t, f: (0, 0))
    vec = pl.BlockSpec((1, 1, d), lambda b, t, f: (b, 0, 0))
    return pl.pallas_call(
        _ffn_kernel,
        grid=(bn, s // tm, fdim // tf),
        in_specs=[tok, tokw(A_DIM), tokw(B_DIM), tokw(C_DIM), full(wa), full(wb), full(wc), vec,
                  vec, vec, vec,
                  pl.BlockSpec((1, d), lambda b, t, f: (0, 0)),
                  pl.BlockSpec((1, d, tf), lambda b, t, f: (f, 0, 0)),
                  pl.BlockSpec((1, d, tf), lambda b, t, f: (f, 0, 0)),
                  pl.BlockSpec((tf, d), lambda b, t, f: (f, 0))],
        out_specs=tok,
        out_shape=jax.ShapeDtypeStruct((bn, s, d), F32),
        scratch_shapes=[pltpu.VMEM((tm, d), F32), pltpu.VMEM((tm, d), BF16), pltpu.VMEM((tm, d), F32)],
        compiler_params=_cparams("parallel", "parallel", "arbitrary"),
    )(x, a, b, c, wa, wb, wc, g1, sh, sc, gate, g, wg, wu, wd)


MOE_RC = 272
MOE_SC = 256
MOE_TB = 512


def _moe_kernel(final_norm, x_ref, sh_ref, sc_ref, gate_ref, g_ref, gfin_ref, wr_ref, wg_ref, wu_ref, wd_ref, o_ref,
                h_scr, xe_scr, ye_scr, gcol_scr, gates_scr, rsel_scr, rn_scr, rcol_scr, cnt_smem):
    e = pl.program_id(2)
    f = pl.program_id(3)
    tm = x_ref.shape[1]
    rc_rows = MOE_RC
    sc_rows = MOE_SC
    tb = MOE_TB
    ntb = tm // tb
    neg_inf = jnp.float32(-jnp.inf)

    @pl.when((e == 0) & (f == 0))
    def _route():
        for jb in range(ntb):
            rows = slice(jb * tb, (jb + 1) * tb)
            h = _modulated_rmsnorm(x_ref[0, rows, :], g_ref[...], sc_ref[0], sh_ref[0])
            h_scr[rows, :] = h.astype(BF16)
            rcol_scr[rows, :] = jnp.dot(h, wr_ref[...], precision=lax.Precision.HIGHEST,
                                        preferred_element_type=F32)
            o_ref[0, rows, :] = jnp.zeros((tb, o_ref.shape[2]), F32)
        logits = rcol_scr[...].T[:N_EXPERTS]
        ridx = lax.broadcasted_iota(I32, (N_EXPERTS, tm), 0)
        m1 = jnp.max(logits, axis=0, keepdims=True)
        i1 = jnp.min(jnp.where(logits == m1, ridx, N_EXPERTS), axis=0, keepdims=True)
        rest = jnp.where(ridx == i1, neg_inf, logits)
        m2 = jnp.max(rest, axis=0, keepdims=True)
        i2 = jnp.min(jnp.where(rest == m2, ridx, N_EXPERTS), axis=0, keepdims=True)
        e2 = jnp.exp(m2 - m1)
        den = 1.0 + e2
        gates_scr[...] = jnp.where(ridx == i1, 1.0 / den, jnp.where(ridx == i2, e2 / den, 0.0))
        sel = jnp.where(ridx == i1, 1.0, jnp.where(ridx == i2, 1.0, 0.0))
        ur = lax.broadcasted_iota(I32, (LANES, LANES), 0)
        uc = lax.broadcasted_iota(I32, (LANES, LANES), 1)
        utri = jnp.where(ur <= uc, 1.0, 0.0).astype(BF16)
        carry = jnp.zeros((N_EXPERTS, 1), F32)
        for kb in range(tm // LANES):
            sb = sel[:, kb * LANES:(kb + 1) * LANES]
            pref = jnp.dot(sb.astype(BF16), utri, preferred_element_type=F32) + carry
            rsel_scr[:, kb * LANES:(kb + 1) * LANES] = jnp.where(sb > 0.0, pref, -1.0)
            carry = pref[:, LANES - 1:LANES]
            if (kb + 1) * LANES == tm // 2:
                carry_half = carry
        r8 = lax.broadcasted_iota(I32, (N_EXPERTS, 1), 0)
        for ee in range(N_EXPERTS):
            cnt_smem[ee] = jnp.sum(jnp.where(r8 == ee, carry, 0.0)).astype(I32)
            cnt_smem[N_EXPERTS + ee] = jnp.sum(jnp.where(r8 == ee, carry_half, 0.0)).astype(I32)
        rn_scr[...] = jnp.concatenate(
            [rsel_scr[...], jnp.zeros((LANES - N_EXPERTS, tm), F32)], axis=0).T

    n_rc = (cnt_smem[e] + (rc_rows - 1)) // rc_rows
    cnt_first = cnt_smem[N_EXPERTS + e]
    token_regions = ((0, tm), (0, tm // 2), (tm // 2, tm))

    def chunk_region(r0, nrows):
        return jnp.where(r0 + nrows <= cnt_first, 1, jnp.where(r0 >= cnt_first, 2, 0))

    @pl.when(f == 0)
    def _gather():
        rsel_row = rsel_scr[pl.ds(e, 1), :]
        gate_row = gates_scr[pl.ds(e, 1), :]
        lane = lax.broadcasted_iota(I32, (tm, LANES), 1)
        rcol = jnp.sum(jnp.where(lane == e, rn_scr[...], 0.0), axis=1, keepdims=True)
        rcol_scr[...] = jnp.broadcast_to(rcol, (tm, LANES))

        def compact(r0, lo, hi):
            want = (r0 + 1 + lax.broadcasted_iota(I32, (rc_rows, hi - lo), 0)).astype(F32)
            pm = rsel_row[:, lo:hi] == want
            pb = jnp.where(pm, 1.0, 0.0).astype(BF16)
            xe_scr[pl.ds(r0, rc_rows), :] = jnp.dot(pb, h_scr[lo:hi, :],
                                                    preferred_element_type=F32).astype(BF16)
            gcol = jnp.sum(jnp.where(pm, gate_row[:, lo:hi], 0.0), axis=1, keepdims=True)
            gcol_scr[pl.ds(r0, rc_rows), :] = jnp.broadcast_to(gcol, (rc_rows, LANES))

        def body(rc, carry):
            r0 = pl.multiple_of(rc * rc_rows, rc_rows)
            for region, (lo, hi) in enumerate(token_regions):
                @pl.when(chunk_region(r0, rc_rows) == region)
                def _():
                    compact(r0, lo, hi)
            ye_scr[pl.ds(r0, rc_rows), :] = jnp.zeros((rc_rows, ye_scr.shape[1]), F32)
            return carry

        lax.fori_loop(0, n_rc, body, 0)
        tail = pl.multiple_of(n_rc * rc_rows, 16)
        ye_scr[pl.ds(tail, sc_rows), :] = jnp.zeros((sc_rows, ye_scr.shape[1]), F32)

    def ffn_body(rc, carry):
        r0 = pl.multiple_of(rc * rc_rows, rc_rows)
        xe = xe_scr[pl.ds(r0, rc_rows), :]
        a = jnp.dot(xe, wg_ref[0, 0], preferred_element_type=F32)
        u = jnp.dot(xe, wu_ref[0, 0], preferred_element_type=F32)
        act = (_silu(a) * u * gcol_scr[pl.ds(r0, rc_rows), 0:1]).astype(BF16)
        ye_scr[pl.ds(r0, rc_rows), :] += jnp.dot(act, wd_ref[0], preferred_element_type=F32)
        return carry

    lax.fori_loop(0, n_rc, ffn_body, 0)

    @pl.when(f == pl.num_programs(3) - 1)
    def _scatter():
        def body(rc, carry):
            r0 = pl.multiple_of(rc * sc_rows, sc_rows)
            ye = ye_scr[pl.ds(r0, sc_rows), :].astype(BF16)
            for region, (lo, hi) in enumerate(token_regions):
                @pl.when(chunk_region(r0, sc_rows) == region)
                def _():
                    want = (r0 + 1 + lax.broadcasted_iota(I32, (hi - lo, sc_rows), 1)).astype(F32)
                    pt = jnp.where(rcol_scr[lo:hi, 0:1] == want, 1.0, 0.0).astype(BF16)
                    o_ref[0, lo:hi, :] += jnp.dot(pt, ye, preferred_element_type=F32)
            return carry

        lax.fori_loop(0, (cnt_smem[e] + (sc_rows - 1)) // sc_rows, body, 0)

    @pl.when((e == pl.num_programs(2) - 1) & (f == pl.num_programs(3) - 1))
    def _residual():
        for jb in range(ntb):
            rows = slice(jb * tb, (jb + 1) * tb)
            y = x_ref[0, rows, :] + gate_ref[0] * o_ref[0, rows, :]
            if final_norm:
                var = jnp.mean(y * y, axis=-1, keepdims=True)
                y = (y * lax.rsqrt(var + EPS)) * gfin_ref[...]
            o_ref[0, rows, :] = y


def _moe(x, sh, sc, gate, g, w_router, wg, wu, wd, tm, tf, g_final=None):
    bn, s, d = x.shape
    ne, _, fdim = wg.shape
    wg, wu = _column_blocks(wg, tf), _column_blocks(wu, tf)
    assert ne == N_EXPERTS and MOE_RC % 16 == 0 and tm % MOE_TB == 0 and (tm // 2) % LANES == 0
    cap = -(-tm // MOE_RC) * MOE_RC + MOE_SC
    wr = jnp.zeros((d, LANES), F32).at[:, :ne].set(w_router)
    final_norm = g_final is not None
    gfin = g_final if final_norm else jnp.ones((1, d), F32)
    tok_in = pl.BlockSpec((1, tm, d), lambda b, t, e, f: (b, t, 0), pipeline_mode=pl.Buffered(1))
    tok_out = pl.BlockSpec((1, tm, d), lambda b, t, e, f: (b, t, 0), pipeline_mode=pl.Buffered(1))
    vec = pl.BlockSpec((1, 1, d), lambda b, t, e, f: (b, 0, 0))
    return pl.pallas_call(
        functools.partial(_moe_kernel, final_norm),
        grid=(bn, s // tm, ne, fdim // tf),
        in_specs=[tok_in, vec, vec, vec,
                  pl.BlockSpec((1, d), lambda b, t, e, f: (0, 0)),
                  pl.BlockSpec((1, d), lambda b, t, e, f: (0, 0)),
                  pl.BlockSpec((d, LANES), lambda b, t, e, f: (0, 0)),
                  pl.BlockSpec((1, 1, d, tf), lambda b, t, e, f: (e, f, 0, 0)),
                  pl.BlockSpec((1, 1, d, tf), lambda b, t, e, f: (e, f, 0, 0)),
                  pl.BlockSpec((1, tf, d), lambda b, t, e, f: (e, f, 0))],
        out_specs=tok_out,
        out_shape=jax.ShapeDtypeStruct((bn, s, d), F32),
        scratch_shapes=[pltpu.VMEM((tm, d), BF16),
                        pltpu.VMEM((cap, d), BF16),
                        pltpu.VMEM((cap, d), F32),
                        pltpu.VMEM((cap, LANES), F32),
                        pltpu.VMEM((ne, tm), F32),
                        pltpu.VMEM((ne, tm), F32),
                        pltpu.VMEM((tm, LANES), F32),
                        pltpu.VMEM((tm, LANES), F32),
                        pltpu.SMEM((2 * ne,), I32)],
        compiler_params=pltpu.CompilerParams(
            dimension_semantics=("parallel", "parallel", "arbitrary", "arbitrary"),
            vmem_limit_bytes=56 * 1024 * 1024),
    )(x, sh, sc, gate, g, gfin, wr, wg, wu, wd)


def _final_norm_kernel(x_ref, g_ref, o_ref):
    x = x_ref[0]
    var = jnp.mean(x * x, axis=-1, keepdims=True)
    o_ref[0] = (x * lax.rsqrt(var + EPS)) * g_ref[...]


def _final_norm(x, g, tm):
    bn, s, d = x.shape
    tok = pl.BlockSpec((1, tm, d), lambda b, t: (b, t, 0))
    return pl.pallas_call(
        _final_norm_kernel,
        grid=(bn, s // tm),
        in_specs=[tok, pl.BlockSpec((1, d), lambda b, t: (0, 0))],
        out_specs=tok,
        out_shape=jax.ShapeDtypeStruct((bn, s, d), F32),
        compiler_params=_cparams("parallel", "parallel"),
    )(x, g)


def _token_tile(s, want):
    tm = min(want, s)
    assert s % tm == 0 and tm % QB == 0
    return tm


def kernel(x, c, positions, w_ada, b_ada, g_norm_mix, g_norm_ffn, w_in, w_out, hgrn_lb_logits, hgrn_out_norm, gmlp_vnorm_g, gmlp_vnorm_b, gmlp_w_s, gmlp_b_s, ffn_w_gate, ffn_w_up, ffn_w_down, moe_w_router, moe_w_gate, moe_w_up, moe_w_down, g_final):
    bn, s, d = x.shape
    depth = w_in.shape[0]
    assert s % QB == 0 and d == sum(IN_WIDTHS[:1]) + B_DIM + C_DIM
    tm_proj = _token_tile(s, 512)
    tm_ffn = _token_tile(s, 1024)

    p_lb = jax.nn.softmax(hgrn_lb_logits.astype(F32), axis=0)
    cum = jnp.cumsum(p_lb, axis=0)
    lower_bounds = cum - cum[0:1]

    mod = _ada_mod(c, w_ada, b_ada)
    cosn, sinn, cost, sint = _rope_tables(positions)

    for l in range(depth):
        sh1, sc1, g1, sh2, sc2, g2 = [mod[l, :, i * d:(i + 1) * d].reshape(bn, 1, d) for i in range(6)]
        w_n, w_t = _prep_in_weights(w_in[l])
        (kz, ik3, bq, bf, bi, bg, c_out, qt, iq3t, vt4, iwt, bit) = _inproj(
            x, sh1, sc1, g_norm_mix[l].reshape(1, d), w_n, w_t, cosn, sinn, cost, sint,
            gmlp_vnorm_g[l].reshape(1, C_DIM), gmlp_vnorm_b[l].reshape(1, C_DIM), gmlp_w_s[l], gmlp_b_s[l],
            tm_proj)
        a_out = _dsa(ik3, kz, vt4, iq3t, iwt, qt)
        b_out = _hgrn(bq, bf, bi, bit, bg, lower_bounds[l].reshape(1, B_DIM),
                      jnp.tile(hgrn_out_norm[l], LANES // B_DV).reshape(1, LANES))
        gf = g_norm_ffn[l].reshape(1, d)
        if l % 2 == 0:
            i = l // 2
            x = _ffn(x, a_out, b_out, c_out, w_out[l], g1, sh2, sc2, g2, gf, ffn_w_gate[i],
                     ffn_w_up[i], ffn_w_down[i].astype(BF16), tm_ffn, 256)
        else:
            i = l // 2
            x = _outproj(x, a_out, b_out, c_out, w_out[l], g1, tm_proj)
            fused_final = l == depth - 1
            x = _moe(x, sh2, sc2, g2, gf, moe_w_router[i], moe_w_gate[i],
                     moe_w_up[i], moe_w_down[i].astype(BF16), _token_tile(s, 2048), 896,
                     g_final.reshape(1, d) if fused_final else None)
    if depth % 2 == 0:
        return x
    return _final_norm(x, g_final.reshape(1, d), tm_proj)
```

```python
import functools

import numpy as np
import jax
import jax.numpy as jnp
from jax import lax
from jax.experimental import pallas as pl
from jax.experimental.pallas import tpu as pltpu

F32 = jnp.float32
BF16 = jnp.bfloat16
I32 = jnp.int32
I16 = jnp.int16

HEAD_DIM = 64
A_HEADS = 6
A_DIM = A_HEADS * HEAD_DIM
IDX_HEADS = 4
IDX_DIM = 64
TOPK_MAX = 256
B_HEADS = 6
B_DK = 64
B_DV = 64
B_DIM = B_HEADS * B_DV
HGRN_CHUNK = 64
C_GROUPS = 4
C_GROUP_DIM = 64
C_DIM = C_GROUPS * C_GROUP_DIM
C_CHUNK = 128
ROPE_THETA = 10000.0
N_EXPERTS = 8
EPS = 1e-6
IN_WIDTHS = (A_DIM, HEAD_DIM, HEAD_DIM, IDX_HEADS * IDX_DIM, IDX_DIM, IDX_HEADS,
             B_DIM, B_DIM, B_DIM, B_DIM, 2 * C_DIM)

LANES = 128
KC = 128
QB = 2 * KC
PV_ROWS = HEAD_DIM + 16
VMEM_LIMIT = 48 * 1024 * 1024
HGRN_FAST_SPAN = 80.0
HGRN_SUB = 32
HGRN_GROUP = 4
INT_MIN = -2 ** 31

NT_DIMS = (((1,), (1,)), ((), ()))
TN_DIMS = (((0,), (0,)), ((), ()))


def _cparams(*sem):
    return pltpu.CompilerParams(dimension_semantics=sem, vmem_limit_bytes=VMEM_LIMIT)


def _split2(x):
    hi = x.astype(BF16)
    lo = (x - hi.astype(F32)).astype(BF16)
    return hi, lo


def _split3(x):
    hi = x.astype(BF16)
    r1 = x - hi.astype(F32)
    mid = r1.astype(BF16)
    lo = (r1 - mid.astype(F32)).astype(BF16)
    return hi, mid, lo


def _silu(x):
    return x * (1.0 / (1.0 + jnp.exp(-x)))


def _modulated_rmsnorm(x, g, sc, sh):
    var = jnp.mean(x * x, axis=-1, keepdims=True)
    y = x * lax.rsqrt(var + EPS)
    return (y * g) * (1.0 + sc) + sh


def _ada_kernel(c_ref, w_ref, b_ref, o_ref):
    cond = _silu(c_ref[...])
    o_ref[0] = jnp.dot(cond, w_ref[0], precision=lax.Precision.HIGHEST,
                       preferred_element_type=F32) + b_ref[0]


def _ada_mod(c, w_ada, b_ada):
    depth, d, d6 = w_ada.shape
    bn = c.shape[0]
    tn = 1536
    return pl.pallas_call(
        _ada_kernel,
        grid=(depth, d6 // tn),
        in_specs=[pl.BlockSpec((bn, d), lambda l, n: (0, 0)),
                  pl.BlockSpec((1, d, tn), lambda l, n: (l, 0, n)),
                  pl.BlockSpec((1, 1, tn), lambda l, n: (l, 0, n))],
        out_specs=pl.BlockSpec((1, bn, tn), lambda l, n: (l, 0, n)),
        out_shape=jax.ShapeDtypeStruct((depth, bn, d6), F32),
        compiler_params=_cparams("parallel", "parallel"),
    )(c, w_ada, b_ada.reshape(depth, 1, d6))


def _rope_kernel(posn_ref, post_ref, invn_ref, signn_ref, invt_ref,
                 cosn_ref, sinn_ref, cost_ref, sint_ref):
    ang_n = posn_ref[0].astype(F32) * invn_ref[...]
    cosn_ref[0] = jnp.cos(ang_n)
    sinn_ref[0] = jnp.sin(ang_n) * signn_ref[...]
    ang_t = invt_ref[...] * post_ref[0].astype(F32)
    cost_ref[0] = jnp.cos(ang_t)
    sint_ref[0] = jnp.sin(ang_t)


def _rope_tables(positions):
    bn, s = positions.shape
    half = HEAD_DIM // 2
    inv = ROPE_THETA ** (-jnp.arange(0, HEAD_DIM, 2, dtype=F32) / HEAD_DIM)
    inv_n = jnp.tile(inv, LANES // half).reshape(1, LANES)
    sign_n = jnp.tile(jnp.concatenate([-jnp.ones((half,), F32), jnp.ones((half,), F32)]),
                      LANES // HEAD_DIM).reshape(1, LANES)
    inv_t = inv.reshape(half, 1)
    full = lambda shape: pl.BlockSpec(shape, lambda b: (0,) * len(shape))
    return pl.pallas_call(
        _rope_kernel,
        grid=(bn,),
        in_specs=[pl.BlockSpec((1, s, 1), lambda b: (b, 0, 0)),
                  pl.BlockSpec((1, 1, s), lambda b: (b, 0, 0)),
                  full((1, LANES)), full((1, LANES)), full((half, 1))],
        out_specs=[pl.BlockSpec((1, s, LANES), lambda b: (b, 0, 0)),
                   pl.BlockSpec((1, s, LANES), lambda b: (b, 0, 0)),
                   pl.BlockSpec((1, half, s), lambda b: (b, 0, 0)),
                   pl.BlockSpec((1, half, s), lambda b: (b, 0, 0))],
        out_shape=[jax.ShapeDtypeStruct((bn, s, LANES), F32),
                   jax.ShapeDtypeStruct((bn, s, LANES), F32),
                   jax.ShapeDtypeStruct((bn, half, s), F32),
                   jax.ShapeDtypeStruct((bn, half, s), F32)],
        compiler_params=_cparams("parallel"),
    )(positions.reshape(bn, s, 1), positions.reshape(bn, 1, s), inv_n, sign_n, inv_t)


N_KI = 0
N_BQ = 2 * LANES
N_BF = N_BQ + B_DIM
N_BI = N_BF + B_DIM
N_BG = N_BI + B_DIM
N_CUV = N_BG + B_DIM
N_COLS = N_CUV + 2 * C_DIM
T_Q = 0
T_IQ = A_DIM
T_V = T_IQ + IDX_HEADS * IDX_DIM
T_IW = T_V + HEAD_DIM
T_BI = T_IW + 8
T_ROWS = T_BI + B_DIM


def _prep_in_weights(w_in_l):
    offs = np.concatenate([[0], np.cumsum(IN_WIDTHS)])
    sl = lambda i: w_in_l[:, int(offs[i]):int(offs[i + 1])]
    aq, ak, av, iq, ik, iw, bq, bf, bi, bg, cuv = [sl(i) for i in range(11)]
    half = HEAD_DIM // 2
    rot = lambda w: jnp.concatenate([w[:, half:], w[:, :half]], axis=1)
    w_n = jnp.concatenate([ak, ik, rot(ak), rot(ik), bq, bf, bi, bg, cuv], axis=1)
    w_t = jnp.concatenate([aq, iq, av, iw, jnp.zeros((w_in_l.shape[0], 4), w_in_l.dtype), bi], axis=1).T
    return w_n.astype(BF16), w_t.astype(BF16)


def _inproj_kernel(x_ref, sh_ref, sc_ref, g_ref, wn_ref, wt_ref, cosn_ref, sinn_ref, cost_ref, sint_ref,
                   vg_ref, vb_ref, ws_ref, cbias_ref,
                   kz_ref, ik3_ref, bq_ref, bf_ref, bi_ref, bg_ref, c_ref,
                   qt_ref, iq3t_ref, vt_ref, iwt_ref, bit_ref):
    tm = x_ref.shape[1]
    half = HEAD_DIM // 2
    h = _modulated_rmsnorm(x_ref[0], g_ref[...], sc_ref[0], sh_ref[0]).astype(BF16)

    cosn = cosn_ref[0]
    sinn = sinn_ref[0]
    zk = jnp.dot(h, wn_ref[:, N_KI:N_KI + 2 * LANES], preferred_element_type=F32)
    ki = zk[:, :LANES] * cosn + zk[:, LANES:] * sinn
    lane = lax.broadcasted_iota(I32, (tm, LANES), 1)
    kz_ref[0] = jnp.where(lane < HEAD_DIM, ki, 0.0).astype(BF16)
    ik2 = jnp.where(lane < IDX_DIM, pltpu.roll(ki, HEAD_DIM, axis=1), ki)
    hi, lo = _split2(ik2)
    ik3_ref[0, :, :LANES] = jnp.where(lane < IDX_DIM, hi, lo)
    ik3_ref[0, :, LANES:] = hi
    bq_ref[0] = jnp.dot(h, wn_ref[:, N_BQ:N_BF], preferred_element_type=F32).astype(BF16)
    bf_ref[0] = jnp.dot(h, wn_ref[:, N_BF:N_BI], preferred_element_type=F32)
    bi_ref[0] = jnp.dot(h, wn_ref[:, N_BI:N_BG], preferred_element_type=F32).astype(BF16)
    bg_ref[0] = jnp.dot(h, wn_ref[:, N_BG:N_CUV], preferred_element_type=F32).astype(BF16)
    cuv = jnp.dot(h, wn_ref[:, N_CUV:N_COLS], preferred_element_type=F32)
    c_ref[0] = _gmlp_mix(cuv, vg_ref[...], vb_ref[...], ws_ref, cbias_ref[...]).astype(BF16)

    zt = lax.dot_general(wt_ref[...], h, NT_DIMS, preferred_element_type=F32)
    cost = cost_ref[0]
    sint = sint_ref[0]
    qscale = HEAD_DIM ** -0.5
    for hh in range(A_HEADS):
        r0 = T_Q + hh * HEAD_DIM
        x1 = zt[r0:r0 + half]
        x2 = zt[r0 + half:r0 + HEAD_DIM]
        qt_ref[0, hh * HEAD_DIM:hh * HEAD_DIM + half] = ((x1 * cost - x2 * sint) * qscale).astype(BF16)
        qt_ref[0, hh * HEAD_DIM + half:(hh + 1) * HEAD_DIM] = ((x2 * cost + x1 * sint) * qscale).astype(BF16)
    iscale = IDX_DIM ** -0.5
    zero = jnp.zeros((IDX_DIM, tm), BF16)
    for hh in range(IDX_HEADS):
        r0 = T_IQ + hh * IDX_DIM
        x1 = zt[r0:r0 + half]
        x2 = zt[r0 + half:r0 + IDX_DIM]
        y = jnp.concatenate([(x1 * cost - x2 * sint) * iscale, (x2 * cost + x1 * sint) * iscale], axis=0)
        hi, lo = _split2(y)
        iq3t_ref[0, hh, 0 * IDX_DIM:1 * IDX_DIM] = hi
        iq3t_ref[0, hh, 1 * IDX_DIM:2 * IDX_DIM] = hi
        iq3t_ref[0, hh, 2 * IDX_DIM:3 * IDX_DIM] = lo
        iq3t_ref[0, hh, 3 * IDX_DIM:4 * IDX_DIM] = zero
    vt = zt[T_V:T_V + HEAD_DIM].astype(BF16)
    for i in range(tm // KC):
        vt_ref[0, i] = vt[:, i * KC:(i + 1) * KC]
    iwt_ref[0] = zt[T_IW:T_IW + 8] * (IDX_HEADS ** -0.5)
    bit = zt[T_BI:T_BI + B_DIM].astype(BF16)
    for i in range(tm // LANES):
        bit_ref[0, i] = bit[:, i * LANES:(i + 1) * LANES]


def _inproj(x, sh, sc, g, w_n, w_t, cosn, sinn, cost, sint, vg, vb, ws, bs, tm):
    bn, s, d = x.shape
    assert tm % C_CHUNK == 0
    nt = s // tm
    half = HEAD_DIM // 2
    cbias = jnp.repeat(bs.T, C_GROUP_DIM, axis=1)
    tok = lambda w: pl.BlockSpec((1, tm, w), lambda b, t: (b, t, 0))
    vec = pl.BlockSpec((1, 1, d), lambda b, t: (b, 0, 0))
    full2 = lambda a: pl.BlockSpec(a.shape, lambda b, t: (0, 0))
    out_shapes = [
        jax.ShapeDtypeStruct((bn, s, LANES), BF16),
        jax.ShapeDtypeStruct((bn, s, 2 * LANES), BF16),
        jax.ShapeDtypeStruct((bn, s, B_DIM), BF16),
        jax.ShapeDtypeStruct((bn, s, B_DIM), F32),
        jax.ShapeDtypeStruct((bn, s, B_DIM), BF16),
        jax.ShapeDtypeStruct((bn, s, B_DIM), BF16),
        jax.ShapeDtypeStruct((bn, s, C_DIM), BF16),
        jax.ShapeDtypeStruct((bn, A_DIM, s), BF16),
        jax.ShapeDtypeStruct((bn, IDX_HEADS, 4 * IDX_DIM, s), BF16),
        jax.ShapeDtypeStruct((bn, s // KC, HEAD_DIM, KC), BF16),
        jax.ShapeDtypeStruct((bn, 8, s), F32),
        jax.ShapeDtypeStruct((bn, s // LANES, B_DIM, LANES), BF16),
    ]
    out_specs = [
        tok(LANES), tok(2 * LANES), tok(B_DIM), tok(B_DIM), tok(B_DIM), tok(B_DIM), tok(C_DIM),
        pl.BlockSpec((1, A_DIM, tm), lambda b, t: (b, 0, t)),
        pl.BlockSpec((1, IDX_HEADS, 4 * IDX_DIM, tm), lambda b, t: (b, 0, 0, t)),
        pl.BlockSpec((1, tm // KC, HEAD_DIM, KC), lambda b, t: (b, t, 0, 0)),
        pl.BlockSpec((1, 8, tm), lambda b, t: (b, 0, t)),
        pl.BlockSpec((1, tm // LANES, B_DIM, LANES), lambda b, t: (b, t, 0, 0)),
    ]
    return pl.pallas_call(
        _inproj_kernel,
        grid=(bn, nt),
        in_specs=[tok(d), vec, vec, full2(g), full2(w_n), full2(w_t),
                  tok(LANES), tok(LANES),
                  pl.BlockSpec((1, half, tm), lambda b, t: (b, 0, t)),
                  pl.BlockSpec((1, half, tm), lambda b, t: (b, 0, t)),
                  full2(vg), full2(vb), pl.BlockSpec(ws.shape, lambda b, t: (0, 0, 0)), full2(cbias)],
        out_specs=out_specs,
        out_shape=out_shapes,
        compiler_params=_cparams("parallel", "parallel"),
    )(x, sh, sc, g, w_n, w_t, cosn, sinn, cost, sint, vg, vb, ws, cbias)


def _dsa_kernel(n_top, ik3_ref, kz_ref, vt_ref, iq3t_ref, iwt_ref, qt_ref, out_ref,
                key_scr, hi_scr, lo_scr, lg_scr, acc_scr):
    j = pl.program_id(1)
    npair = j + 1
    pair = 2 * KC
    row = lax.broadcasted_iota(I32, (KC, QB), 0)
    col = lax.broadcasted_iota(I32, (KC, QB), 1)
    int_min = jnp.int32(INT_MIN)
    i16_min = jnp.int16(-2 ** 15)

    def score_pair(p, diagonal):
        for u in range(2):
            ks = pl.multiple_of(p * pair + u * KC, KC)
            ikc = ik3_ref[0, pl.ds(ks, KC), :]
            sc = jnp.zeros((KC, QB), F32)
            for hh in range(IDX_HEADS):
                rel = jnp.dot(ikc, iq3t_ref[0, hh], preferred_element_type=F32)
                sc = sc + jnp.maximum(rel, 0.0) * iwt_ref[0, hh:hh + 1, :]
            sc = jnp.where(sc == 0.0, 0.0, sc)
            bits = pltpu.bitcast(sc, I32)
            key = bits ^ ((bits >> 31) & jnp.int32(0x7FFFFFFF))
            if diagonal:
                key = jnp.where((u * KC + row) <= col, key, int_min)
            key_scr[pl.ds(ks, KC), :] = key
            hi_scr[pl.ds(ks, KC), :] = (key >> 16).astype(I16)
            lo_scr[pl.ds(ks, KC), :] = ((key & jnp.int32(0xFFFF)) - 32768).astype(I16)

    def score_body(p, carry):
        score_pair(p, False)
        return carry

    lax.fori_loop(0, j, score_body, 0)
    score_pair(j, True)

    def count16(ref, pred_fn):
        def body(p, acc):
            ks = pl.multiple_of(p * pair, pair)
            m = jnp.where(pred_fn(ref[pl.ds(ks, pair), :]), jnp.int16(1), jnp.int16(0))
            parts = [m[16 * i:16 * (i + 1)] for i in range(pair // 16)]
            while len(parts) > 1:
                parts = [parts[i] + parts[i + 1] for i in range(0, len(parts), 2)]
            return acc + parts[0]
        acc = lax.fori_loop(0, npair, body, jnp.zeros((16, QB), I16))
        return jnp.sum(acc.astype(I32), axis=0, keepdims=True)

    def bisect16(ref, k_needed):
        def bit_body(i, t_u):
            cand = t_u | jnp.left_shift(jnp.int32(1), 15 - i)
            cand16 = (cand - 32768).astype(I16)
            cnt = count16(ref, lambda x: x >= cand16)
            return jnp.where(cnt >= k_needed, cand, t_u)
        return lax.fori_loop(0, 16, bit_body, jnp.zeros((1, QB), I32))

    t_hi = bisect16(hi_scr, n_top)
    t_hi16 = (t_hi - 32768).astype(I16)
    n_hi_gt = count16(hi_scr, lambda x: x > t_hi16)

    def bucket_body(p, carry):
        ks = pl.multiple_of(p * pair, pair)
        in_bucket = hi_scr[pl.ds(ks, pair), :] == t_hi16
        lo_scr[pl.ds(ks, pair), :] = jnp.where(in_bucket, lo_scr[pl.ds(ks, pair), :], i16_min)
        return carry

    lax.fori_loop(0, npair, bucket_body, 0)
    t_lo = bisect16(lo_scr, n_top - n_hi_gt)
    t_lo16 = (t_lo - 32768).astype(I16)
    n_gt = n_hi_gt + count16(lo_scr, lambda x: x > t_lo16)
    thr = jnp.left_shift(t_hi - 32768, 16) | t_lo
    n_tie = (n_top - n_gt).astype(F32)
    thr_valid = jnp.where(thr > int_min, 1.0, 0.0)

    tr = lax.broadcasted_iota(I32, (KC, KC), 0)
    tc = lax.broadcasted_iota(I32, (KC, KC), 1)
    ltri = jnp.where(tr >= tc, 1.0, 0.0).astype(BF16)
    zpad = jnp.zeros((HEAD_DIM, QB), BF16)
    qpad = [jnp.concatenate([qt_ref[0, hh * HEAD_DIM:(hh + 1) * HEAD_DIM, :], zpad], axis=0)
            for hh in range(A_HEADS)]
    neg_inf = jnp.float32(-jnp.inf)

    def pass_a(p, carry):
        tie_cnt, ms = carry
        ms = list(ms)
        for u in range(2):
            ks = pl.multiple_of(p * pair + u * KC, KC)
            kc = key_scr[pl.ds(ks, KC), :]
            eqf = jnp.where(kc == thr, thr_valid, 0.0)
            pref = jnp.dot(ltri, eqf.astype(BF16), preferred_element_type=F32)
            keep_tie = eqf * jnp.where((tie_cnt + pref) <= n_tie, 1.0, 0.0)
            sel = jnp.where(kc > thr, 1.0, keep_tie) > 0.5
            tie_cnt = tie_cnt + pref[KC - 1:KC, :]
            kzc = kz_ref[0, pl.ds(ks, KC), :]
            for hh in range(A_HEADS):
                lt = jnp.dot(kzc, qpad[hh], preferred_element_type=F32)
                lt = jnp.where(sel, lt, neg_inf)
                lg_scr[hh, pl.ds(ks, KC), :] = lt
                ms[hh] = jnp.maximum(ms[hh], jnp.max(lt, axis=0, keepdims=True))
        return tie_cnt, tuple(ms)

    init_m = tuple(jnp.full((1, QB), neg_inf, F32) for _ in range(A_HEADS))
    _, ms = lax.fori_loop(0, npair, pass_a, (jnp.zeros((1, QB), F32), init_m))

    acc_scr[...] = jnp.zeros_like(acc_scr)
    ones_rows = jnp.ones((PV_ROWS - HEAD_DIM, pair), BF16)

    def pass_b(p, carry):
        ks = pl.multiple_of(p * pair, pair)
        vt2 = jnp.concatenate([vt_ref[0, 2 * p], vt_ref[0, 2 * p + 1]], axis=1)
        vt2 = jnp.concatenate([vt2, ones_rows], axis=0)
        for hh in range(A_HEADS):
            pr = jnp.exp(lg_scr[hh, pl.ds(ks, pair), :] - ms[hh])
            acc_scr[hh * PV_ROWS:(hh + 1) * PV_ROWS, :] += jnp.dot(
                vt2, pr.astype(BF16), preferred_element_type=F32)
        return carry

    lax.fori_loop(0, npair, pass_b, 0)
    o_t = jnp.concatenate(
        [acc_scr[hh * PV_ROWS:hh * PV_ROWS + HEAD_DIM, :]
         / acc_scr[hh * PV_ROWS + HEAD_DIM:hh * PV_ROWS + HEAD_DIM + 1, :] for hh in range(A_HEADS)], axis=0)
    out_ref[0] = o_t.T.astype(BF16)


def _dsa(ik3, kz, vt4, iq3t, iwt, qt):
    bn, s, _ = kz.shape
    assert s % QB == 0
    n_top = min(TOPK_MAX, s // 4)
    nq = s // QB
    return pl.pallas_call(
        functools.partial(_dsa_kernel, n_top),
        grid=(bn, nq),
        in_specs=[pl.BlockSpec((1, s, 2 * LANES), lambda b, q: (b, 0, 0)),
                  pl.BlockSpec((1, s, LANES), lambda b, q: (b, 0, 0)),
                  pl.BlockSpec((1, s // KC, HEAD_DIM, KC), lambda b, q: (b, 0, 0, 0)),
                  pl.BlockSpec((1, IDX_HEADS, 4 * IDX_DIM, QB), lambda b, q: (b, 0, 0, q)),
                  pl.BlockSpec((1, 8, QB), lambda b, q: (b, 0, q)),
                  pl.BlockSpec((1, A_DIM, QB), lambda b, q: (b, 0, q))],
        out_specs=pl.BlockSpec((1, QB, A_DIM), lambda b, q: (b, q, 0)),
        out_shape=jax.ShapeDtypeStruct((bn, s, A_DIM), BF16),
        scratch_shapes=[pltpu.VMEM((s, QB), I32), pltpu.VMEM((s, QB), I16), pltpu.VMEM((s, QB), I16),
                        pltpu.VMEM((A_HEADS, s, QB), F32), pltpu.VMEM((A_HEADS * PV_ROWS, QB), F32)],
        compiler_params=_cparams("parallel", "arbitrary"),
    )(ik3, kz, vt4, iq3t, iwt, qt)


def _hgrn_kernel(q_ref, f_ref, i_ref, it_ref, g_ref, lb_ref, gn_ref, o_ref, st_scr, oi_scr):
    s = q_ref.shape[1]
    ch = HGRN_CHUNK
    sb = HGRN_SUB
    nsb = ch // sb
    rows = HGRN_GROUP * ch
    wins_per_group = rows // LANES
    lane = lax.broadcasted_iota(I32, (ch, LANES), 1)
    head0 = lane < B_DK
    lr = lax.broadcasted_iota(I32, (rows, 3 * rows), 0)
    lc = lax.broadcasted_iota(I32, (rows, 3 * rows), 1) % rows
    lcum = jnp.where(((lr // ch) == (lc // ch)) & (lc <= lr), 1.0, 0.0).astype(BF16)
    att_rows = HGRN_GROUP * 2 * ch
    att_cols = HGRN_GROUP * nsb * ch
    ar = lax.broadcasted_iota(I32, (att_rows, att_cols), 0)
    ac = lax.broadcasted_iota(I32, (att_rows, att_cols), 1)
    at = ar % ch
    att_mask = (((ar // (2 * ch)) == (ac // (nsb * ch))) & ((at // sb) == ((ac // ch) % nsb))
                & ((ac % ch) <= at))
    bd_r = lax.broadcasted_iota(I32, (LANES, LANES), 0)
    bd_c = lax.broadcasted_iota(I32, (LANES, LANES), 1)
    same_head = (bd_r < B_DK) == (bd_c < B_DK)
    ones_bd = jnp.where(same_head, 1.0, 0.0).astype(BF16)
    ones_bd2 = jnp.concatenate([ones_bd, ones_bd], axis=0)
    s_iota = lax.broadcasted_iota(I32, (ch, LANES), 0)

    lb = lb_ref[...]
    log_lb = jnp.log(lb)
    log_1mlb = jnp.log1p(-lb)
    gn = gn_ref[...]
    st_scr[...] = jnp.zeros_like(st_scr)

    def group_body(it, carry):
        t0 = pl.multiple_of(it * rows, rows)
        z = f_ref[0, pl.ds(t0, rows), :]
        q = q_ref[0, pl.ds(t0, rows), :].astype(F32)
        v = i_ref[0, pl.ds(t0, rows), :]
        softplus_tail = jnp.log1p(jnp.exp(-jnp.abs(z)))
        log_sig = -(jnp.maximum(-z, 0.0) + softplus_tail)
        x2 = log_1mlb + log_sig
        amax = jnp.maximum(log_lb, x2)
        log_f = amax + jnp.log1p(jnp.exp(-jnp.abs(log_lb - x2)))
        kk = (1.0 - lb) * jnp.exp(-(jnp.maximum(z, 0.0) + softplus_tail))
        f_hi, f_mid, f_lo = _split3(log_f)
        b = jnp.dot(lcum, jnp.concatenate([f_hi, f_mid, f_lo], axis=0), preferred_element_type=F32)
        ref_parts = []
        for c in range(HGRN_GROUP):
            ref_parts.append(jnp.zeros((sb, LANES), F32))
            for i in range(1, nsb):
                r = c * ch + i * sb
                ref_parts.append(jnp.broadcast_to(b[r - 1:r], (sb, LANES)))
        ref = jnp.concatenate(ref_parts, axis=0)
        q_loc = q * jnp.exp(b - ref)
        q_chk = (q * jnp.exp(b)).astype(BF16)
        span = ref - b
        span_max = jnp.max(span)

        k_parts, q_parts, v_parts, kh_cols, decay = [], [], [], [], []
        for c in range(HGRN_GROUP):
            lo_r = c * ch
            b_c = b[lo_r:lo_r + ch]
            kk_c = kk[lo_r:lo_r + ch]
            b_last = b_c[ch - 1:ch]
            for i in range(nsb):
                r = lo_r + i * sb
                k_parts.append(kk_c * jnp.exp(jnp.minimum(ref[r:r + 1] - b_c, HGRN_FAST_SPAN)))
            q_c = q_loc[lo_r:lo_r + ch]
            q_parts += [jnp.where(head0, q_c, 0.0), jnp.where(head0, 0.0, q_c)]
            v_parts += [v[lo_r:lo_r + ch]] * nsb
            kh = (kk_c * jnp.exp(b_last - b_c)).astype(BF16)
            kh_cols.append(jnp.concatenate(
                ([jnp.zeros((lo_r, LANES), BF16)] if lo_r else []) + [kh]
                + ([jnp.zeros((rows - lo_r - ch, LANES), BF16)] if rows - lo_r - ch else []), axis=0))
            decay.append(jnp.exp(b_last))
        att = lax.dot_general(jnp.concatenate(q_parts, axis=0).astype(BF16),
                              jnp.concatenate(k_parts, axis=0).astype(BF16), NT_DIMS,
                              preferred_element_type=F32)
        att = jnp.where(att_mask, att, 0.0).astype(BF16)
        o2 = jnp.dot(att, jnp.concatenate(v_parts, axis=0), preferred_element_type=F32)
        for c in range(HGRN_GROUP):
            r = c * 2 * ch
            oi_scr[c * ch:(c + 1) * ch, :] = jnp.where(head0, o2[r:r + ch], o2[r + ch:r + 2 * ch])

        it_win = jnp.concatenate([it_ref[0, wins_per_group * it + w] for w in range(wins_per_group)], axis=1)
        upd = jnp.dot(it_win, jnp.concatenate(kh_cols, axis=1), preferred_element_type=F32)
        st = st_scr[...]
        states = []
        for c in range(HGRN_GROUP):
            states.append(st.astype(BF16))
            st = st * decay[c] + jnp.where(same_head, upd[:, c * LANES:(c + 1) * LANES], 0.0)
        st_scr[...] = st
        oi_all = lax.dot_general(q_chk, jnp.concatenate(states, axis=0), NT_DIMS,
                                 preferred_element_type=F32)
        o_inter = [oi_all[c * ch:(c + 1) * ch, c * LANES:(c + 1) * LANES] for c in range(HGRN_GROUP)]

        @pl.when(span_max > HGRN_FAST_SPAN)
        def _():
            for c in range(HGRN_GROUP):
                lo_r = c * ch

                @pl.when(jnp.max(span[lo_r:lo_r + ch]) > HGRN_FAST_SPAN)
                def _():
                    b_c = b[lo_r:lo_r + ch]
                    q_c = q[lo_r:lo_r + ch]
                    kk_c = kk[lo_r:lo_r + ch]
                    vf = v[lo_r:lo_r + ch].astype(F32)

                    def t_body(t, carry2):
                        onehot = jnp.where(s_iota == t, 1.0, 0.0)
                        b_t = jnp.sum(onehot * b_c, axis=0, keepdims=True)
                        q_t = jnp.sum(onehot * q_c, axis=0, keepdims=True)
                        dec = jnp.exp(jnp.where(s_iota <= t, b_t - b_c, -jnp.inf))
                        w = q_t * kk_c * dec
                        w0 = jnp.sum(jnp.where(head0, w, 0.0), axis=1, keepdims=True)
                        w1 = jnp.sum(jnp.where(head0, 0.0, w), axis=1, keepdims=True)
                        a_col = jnp.where(head0, w0, w1)
                        oi_scr[pl.ds(lo_r + t, 1), :] = jnp.sum(a_col * vf, axis=0, keepdims=True)
                        return carry2

                    lax.fori_loop(0, ch, t_body, 0)

        o = oi_scr[...] + jnp.concatenate(o_inter, axis=0)
        sq_hi, sq_lo = _split2(o * o)
        ss = jnp.dot(jnp.concatenate([sq_hi, sq_lo], axis=1), ones_bd2, preferred_element_type=F32)
        y = (o * lax.rsqrt(ss * (1.0 / B_DV) + EPS)) * gn
        g = g_ref[0, pl.ds(t0, rows), :].astype(F32)
        o_ref[0, pl.ds(t0, rows), :] = (y * _silu(g)).astype(o_ref.dtype)
        return carry

    lax.fori_loop(0, s // rows, group_body, 0)


def _hgrn(bq, bf, bi, bit, bg, lb, gn):
    bn, s, _ = bq.shape
    assert s % (HGRN_GROUP * HGRN_CHUNK) == 0 and 2 * HGRN_CHUNK == LANES and HGRN_GROUP % 2 == 0
    npair = B_DIM // LANES
    tok = pl.BlockSpec((1, s, LANES), lambda b, p: (b, 0, p))
    return pl.pallas_call(
        _hgrn_kernel,
        grid=(bn, npair),
        in_specs=[tok, tok, tok,
                  pl.BlockSpec((1, s // LANES, LANES, LANES), lambda b, p: (b, 0, p, 0)),
                  tok,
                  pl.BlockSpec((1, LANES), lambda b, p: (0, p)),
                  pl.BlockSpec((1, LANES), lambda b, p: (0, 0))],
        out_specs=tok,
        out_shape=jax.ShapeDtypeStruct((bn, s, B_DIM), BF16),
        scratch_shapes=[pltpu.VMEM((LANES, LANES), F32),
                        pltpu.VMEM((HGRN_GROUP * HGRN_CHUNK, LANES), F32)],
        compiler_params=_cparams("parallel", "parallel"),
    )(bq, bf, bi, bit, bg, lb, gn)


def _gmlp_mix(uv, vg, vb, ws_ref, bias):
    uv = 0.5 * uv * (1.0 + lax.erf(uv * (2.0 ** -0.5)))
    u = uv[:, :C_DIM]
    v = uv[:, C_DIM:]
    mu = jnp.mean(v, axis=-1, keepdims=True)
    var = jnp.mean(jnp.square(v - mu), axis=-1, keepdims=True)
    vn_b = (((v - mu) * lax.rsqrt(var + EPS)) * vg + vb).astype(BF16)
    r_t = lax.broadcasted_iota(I32, (C_CHUNK, C_CHUNK), 0)
    r_s = lax.broadcasted_iota(I32, (C_CHUNK, C_CHUNK), 1)
    group = lax.broadcasted_iota(I32, (C_CHUNK, C_DIM), 1) // C_GROUP_DIM
    ws = [jnp.where(r_t >= r_s, ws_ref[gi], 0.0).astype(BF16) for gi in range(C_GROUPS)]
    outs = []
    for c in range(uv.shape[0] // C_CHUNK):
        sl = slice(c * C_CHUNK, (c + 1) * C_CHUNK)
        mixed = jnp.zeros((C_CHUNK, C_DIM), F32)
        for gi in range(C_GROUPS):
            m = jnp.dot(ws[gi], vn_b[sl], preferred_element_type=F32)
            mixed = jnp.where(group == gi, m, mixed)
        outs.append(u[sl] * (mixed + bias))
    return jnp.concatenate(outs, axis=0)


def _outproj_kernel(x_ref, a_ref, b_ref, c_ref, wa_ref, wb_ref, wc_ref, g1_ref, o_ref):
    mix = (jnp.dot(a_ref[0], wa_ref[...], preferred_element_type=F32)
           + jnp.dot(b_ref[0], wb_ref[...], preferred_element_type=F32)
           + jnp.dot(c_ref[0], wc_ref[...], preferred_element_type=F32))
    o_ref[0] = x_ref[0] + g1_ref[0] * mix


def _outproj(x, a, b, c, w_out_l, g1, tm):
    bn, s, d = x.shape
    wa = w_out_l[:A_DIM].astype(BF16)
    wb = w_out_l[A_DIM:A_DIM + B_DIM].astype(BF16)
    wc = w_out_l[A_DIM + B_DIM:].astype(BF16)
    tok = lambda w: pl.BlockSpec((1, tm, w), lambda bb, t: (bb, t, 0))
    full = lambda arr: pl.BlockSpec(arr.shape, lambda bb, t: (0, 0))
    return pl.pallas_call(
        _outproj_kernel,
        grid=(bn, s // tm),
        in_specs=[tok(d), tok(A_DIM), tok(B_DIM), tok(C_DIM), full(wa), full(wb), full(wc),
                  pl.BlockSpec((1, 1, d), lambda bb, t: (bb, 0, 0))],
        out_specs=tok(d),
        out_shape=jax.ShapeDtypeStruct((bn, s, d), F32),
        compiler_params=_cparams("parallel", "parallel"),
    )(x, a, b, c, wa, wb, wc, g1)


def _ffn_kernel(x_ref, a_ref, b_ref, c_ref, wa_ref, wb_ref, wc_ref, g1_ref, sh_ref, sc_ref, gate_ref, g_ref,
                wg_ref, wu_ref, wd_ref, o_ref, xn_scr, h_scr, acc_scr):
    f = pl.program_id(2)

    @pl.when(f == 0)
    def _():
        mix = (jnp.dot(a_ref[0], wa_ref[...], preferred_element_type=F32)
               + jnp.dot(b_ref[0], wb_ref[...], preferred_element_type=F32)
               + jnp.dot(c_ref[0], wc_ref[...], preferred_element_type=F32))
        xn = x_ref[0] + g1_ref[0] * mix
        xn_scr[...] = xn
        h_scr[...] = _modulated_rmsnorm(xn, g_ref[...], sc_ref[0], sh_ref[0]).astype(BF16)
        acc_scr[...] = jnp.zeros_like(acc_scr)

    h = h_scr[...]
    a = jnp.dot(h, wg_ref[...], preferred_element_type=F32)
    u = jnp.dot(h, wu_ref[...], preferred_element_type=F32)
    act = (_silu(a) * u).astype(BF16)
    acc_scr[...] += jnp.dot(act, wd_ref[...], preferred_element_type=F32)

    @pl.when(f == pl.num_programs(2) - 1)
    def _():
        o_ref[0] = xn_scr[...] + gate_ref[0] * acc_scr[...]


def _ffn(x, a, b, c, w_out_l, g1, sh, sc, gate, g, wg, wu, wd, tm, tf):
    bn, s, d = x.shape
    fdim = wg.shape[1]
    wa = w_out_l[:A_DIM].astype(BF16)
    wb = w_out_l[A_DIM:A_DIM + B_DIM].astype(BF16)
    wc = w_out_l[A_DIM + B_DIM:].astype(BF16)
    tok = pl.BlockSpec((1, tm, d), lambda b, t, f: (b, t, 0))
    tokw = lambda w: pl.BlockSpec((1, tm, w), lambda b, t, f: (b, t, 0))
    full = lambda arr: pl.BlockSpec(arr.shape, lambda b, t, f: (0, 0))
    vec = pl.BlockSpec((1, 1, d), lambda b, t, f: (b, 0, 0))
    return pl.pallas_call(
        _ffn_kernel,
        grid=(bn, s // tm, fdim // tf),
        in_specs=[tok, tokw(A_DIM), tokw(B_DIM), tokw(C_DIM), full(wa), full(wb), full(wc), vec,
                  vec, vec, vec,
                  pl.BlockSpec((1, d), lambda b, t, f: (0, 0)),
                  pl.BlockSpec((d, tf), lambda b, t, f: (0, f)),
                  pl.BlockSpec((d, tf), lambda b, t, f: (0, f)),
                  pl.BlockSpec((tf, d), lambda b, t, f: (f, 0))],
        out_specs=tok,
        out_shape=jax.ShapeDtypeStruct((bn, s, d), F32),
        scratch_shapes=[pltpu.VMEM((tm, d), F32), pltpu.VMEM((tm, d), BF16), pltpu.VMEM((tm, d), F32)],
        compiler_params=_cparams("parallel", "parallel", "arbitrary"),
    )(x, a, b, c, wa, wb, wc, g1, sh, sc, gate, g, wg, wu, wd)


MOE_RC = 256
MOE_TB = 512


def _moe_kernel(final_norm, x_ref, sh_ref, sc_ref, gate_ref, g_ref, gfin_ref, wr_ref, wg_ref, wu_ref, wd_ref, o_ref,
                h_scr, xe_scr, ye_scr, gcol_scr, gates_scr, rsel_scr, rn_scr, rcol_scr, cnt_smem):
    step_e = pl.program_id(2)
    f = pl.program_id(3)
    odd_tile = (pl.program_id(0) * pl.num_programs(1) + pl.program_id(1)) % 2
    e = jnp.where(odd_tile == 1, pl.num_programs(2) - 1 - step_e, step_e)
    tm = x_ref.shape[1]
    rc_rows = MOE_RC
    tb = MOE_TB
    ntb = tm // tb
    neg_inf = jnp.float32(-jnp.inf)

    @pl.when((step_e == 0) & (f == 0))
    def _route():
        for jb in range(ntb):
            rows = slice(jb * tb, (jb + 1) * tb)
            h = _modulated_rmsnorm(x_ref[0, rows, :], g_ref[...], sc_ref[0], sh_ref[0])
            h_scr[rows, :] = h.astype(BF16)
            rcol_scr[rows, :] = jnp.dot(h, wr_ref[...], precision=lax.Precision.HIGHEST,
                                        preferred_element_type=F32)
            o_ref[0, rows, :] = jnp.zeros((tb, o_ref.shape[2]), F32)
        logits = rcol_scr[...].T[:N_EXPERTS]
        ridx = lax.broadcasted_iota(I32, (N_EXPERTS, tm), 0)
        m1 = jnp.max(logits, axis=0, keepdims=True)
        i1 = jnp.min(jnp.where(logits == m1, ridx, N_EXPERTS), axis=0, keepdims=True)
        rest = jnp.where(ridx == i1, neg_inf, logits)
        m2 = jnp.max(rest, axis=0, keepdims=True)
        i2 = jnp.min(jnp.where(rest == m2, ridx, N_EXPERTS), axis=0, keepdims=True)
        e2 = jnp.exp(m2 - m1)
        den = 1.0 + e2
        gates_scr[...] = jnp.where(ridx == i1, 1.0 / den, jnp.where(ridx == i2, e2 / den, 0.0))
        sel = jnp.where(ridx == i1, 1.0, jnp.where(ridx == i2, 1.0, 0.0))
        ur = lax.broadcasted_iota(I32, (LANES, LANES), 0)
        uc = lax.broadcasted_iota(I32, (LANES, LANES), 1)
        utri = jnp.where(ur <= uc, 1.0, 0.0).astype(BF16)
        carry = jnp.zeros((N_EXPERTS, 1), F32)
        for kb in range(tm // LANES):
            sb = sel[:, kb * LANES:(kb + 1) * LANES]
            pref = jnp.dot(sb.astype(BF16), utri, preferred_element_type=F32) + carry
            rsel_scr[:, kb * LANES:(kb + 1) * LANES] = jnp.where(sb > 0.0, pref, -1.0)
            carry = pref[:, LANES - 1:LANES]
            if (kb + 1) * LANES == tm // 2:
                carry_half = carry
        r8 = lax.broadcasted_iota(I32, (N_EXPERTS, 1), 0)
        for ee in range(N_EXPERTS):
            cnt_smem[ee] = jnp.sum(jnp.where(r8 == ee, carry, 0.0)).astype(I32)
            cnt_smem[N_EXPERTS + ee] = jnp.sum(jnp.where(r8 == ee, carry_half, 0.0)).astype(I32)
        rn_scr[...] = jnp.concatenate(
            [rsel_scr[...], jnp.zeros((LANES - N_EXPERTS, tm), F32)], axis=0).T

    n_rc = (cnt_smem[e] + (rc_rows - 1)) // rc_rows
    cnt_first = cnt_smem[N_EXPERTS + e]
    token_regions = ((0, tm), (0, tm // 2), (tm // 2, tm))

    def chunk_region(r0):
        return jnp.where(r0 + rc_rows <= cnt_first, 1, jnp.where(r0 >= cnt_first, 2, 0))

    @pl.when(f == 0)
    def _gather():
        rsel_row = rsel_scr[pl.ds(e, 1), :]
        gate_row = gates_scr[pl.ds(e, 1), :]
        lane = lax.broadcasted_iota(I32, (tm, LANES), 1)
        rcol = jnp.sum(jnp.where(lane == e, rn_scr[...], 0.0), axis=1, keepdims=True)
        rcol_scr[...] = jnp.broadcast_to(rcol, (tm, LANES))

        def compact(r0, lo, hi):
            want = (r0 + 1 + lax.broadcasted_iota(I32, (rc_rows, hi - lo), 0)).astype(F32)
            pm = rsel_row[:, lo:hi] == want
            pb = jnp.where(pm, 1.0, 0.0).astype(BF16)
            xe_scr[pl.ds(r0, rc_rows), :] = jnp.dot(pb, h_scr[lo:hi, :],
                                                    preferred_element_type=F32).astype(BF16)
            gcol = jnp.sum(jnp.where(pm, gate_row[:, lo:hi], 0.0), axis=1, keepdims=True)
            gcol_scr[pl.ds(r0, rc_rows), :] = jnp.broadcast_to(gcol, (rc_rows, LANES))

        def body(rc, carry):
            r0 = pl.multiple_of(rc * rc_rows, rc_rows)
            for region, (lo, hi) in enumerate(token_regions):
                @pl.when(chunk_region(r0) == region)
                def _():
                    compact(r0, lo, hi)
            ye_scr[pl.ds(r0, rc_rows), :] = jnp.zeros((rc_rows, ye_scr.shape[1]), F32)
            return carry

        lax.fori_loop(0, n_rc, body, 0)

    def ffn_body(rc, carry):
        r0 = pl.multiple_of(rc * rc_rows, rc_rows)
        xe = xe_scr[pl.ds(r0, rc_rows), :]
        a = jnp.dot(xe, wg_ref[0], preferred_element_type=F32)
        u = jnp.dot(xe, wu_ref[0], preferred_element_type=F32)
        act = (_silu(a) * u * gcol_scr[pl.ds(r0, rc_rows), 0:1]).astype(BF16)
        ye_scr[pl.ds(r0, rc_rows), :] += jnp.dot(act, wd_ref[0], preferred_element_type=F32)
        return carry

    lax.fori_loop(0, n_rc, ffn_body, 0)

    @pl.when(f == pl.num_programs(3) - 1)
    def _scatter():
        def body(rc, carry):
            r0 = pl.multiple_of(rc * rc_rows, rc_rows)
            ye = ye_scr[pl.ds(r0, rc_rows), :].astype(BF16)
            for region, (lo, hi) in enumerate(token_regions):
                @pl.when(chunk_region(r0) == region)
                def _():
                    want = (r0 + 1 + lax.broadcasted_iota(I32, (hi - lo, rc_rows), 1)).astype(F32)
                    pt = jnp.where(rcol_scr[lo:hi, 0:1] == want, 1.0, 0.0).astype(BF16)
                    o_ref[0, lo:hi, :] += jnp.dot(pt, ye, preferred_element_type=F32)
            return carry

        lax.fori_loop(0, n_rc, body, 0)

    @pl.when((step_e == pl.num_programs(2) - 1) & (f == pl.num_programs(3) - 1))
    def _residual():
        for jb in range(ntb):
            rows = slice(jb * tb, (jb + 1) * tb)
            y = x_ref[0, rows, :] + gate_ref[0] * o_ref[0, rows, :]
            if final_norm:
                var = jnp.mean(y * y, axis=-1, keepdims=True)
                y = (y * lax.rsqrt(var + EPS)) * gfin_ref[...]
            o_ref[0, rows, :] = y


def _moe(x, sh, sc, gate, g, w_router, wg, wu, wd, tm, tf, g_final=None):
    bn, s, d = x.shape
    ne, _, fdim = wg.shape
    assert ne == N_EXPERTS and tm % MOE_RC == 0 and tm % MOE_TB == 0 and (tm // 2) % LANES == 0
    wr = jnp.zeros((d, LANES), F32).at[:, :ne].set(w_router)
    final_norm = g_final is not None
    gfin = g_final if final_norm else jnp.ones((1, d), F32)
    tok_in = pl.BlockSpec((1, tm, d), lambda b, t, e, f: (b, t, 0), pipeline_mode=pl.Buffered(1))
    tok_out = pl.BlockSpec((1, tm, d), lambda b, t, e, f: (b, t, 0), pipeline_mode=pl.Buffered(1))
    vec = pl.BlockSpec((1, 1, d), lambda b, t, e, f: (b, 0, 0))
    nt, nf = s // tm, fdim // tf

    def walk(b, t, i, n):
        return jnp.where((b * nt + t) % 2 == 1, n - 1 - i, i)

    return pl.pallas_call(
        functools.partial(_moe_kernel, final_norm),
        grid=(bn, s // tm, ne, fdim // tf),
        in_specs=[tok_in, vec, vec, vec,
                  pl.BlockSpec((1, d), lambda b, t, e, f: (0, 0)),
                  pl.BlockSpec((1, d), lambda b, t, e, f: (0, 0)),
                  pl.BlockSpec((d, LANES), lambda b, t, e, f: (0, 0)),
                  pl.BlockSpec((1, d, tf), lambda b, t, e, f: (walk(b, t, e, ne), 0, walk(b, t, f, nf))),
                  pl.BlockSpec((1, d, tf), lambda b, t, e, f: (walk(b, t, e, ne), 0, walk(b, t, f, nf))),
                  pl.BlockSpec((1, tf, d), lambda b, t, e, f: (walk(b, t, e, ne), walk(b, t, f, nf), 0))],
        out_specs=tok_out,
        out_shape=jax.ShapeDtypeStruct((bn, s, d), F32),
        scratch_shapes=[pltpu.VMEM((tm, d), BF16),
                        pltpu.VMEM((tm, d), BF16),
                        pltpu.VMEM((tm, d), F32),
                        pltpu.VMEM((tm, LANES), F32),
                        pltpu.VMEM((ne, tm), F32),
                        pltpu.VMEM((ne, tm), F32),
                        pltpu.VMEM((tm, LANES), F32),
                        pltpu.VMEM((tm, LANES), F32),
                        pltpu.SMEM((2 * ne,), I32)],
        compiler_params=pltpu.CompilerParams(
            dimension_semantics=("parallel", "parallel", "arbitrary", "arbitrary"),
            vmem_limit_bytes=56 * 1024 * 1024),
    )(x, sh, sc, gate, g, gfin, wr, wg, wu, wd)


def _final_norm_kernel(x_ref, g_ref, o_ref):
    x = x_ref[0]
    var = jnp.mean(x * x, axis=-1, keepdims=True)
    o_ref[0] = (x * lax.rsqrt(var + EPS)) * g_ref[...]


def _final_norm(x, g, tm):
    bn, s, d = x.shape
    tok = pl.BlockSpec((1, tm, d), lambda b, t: (b, t, 0))
    return pl.pallas_call(
        _final_norm_kernel,
        grid=(bn, s // tm),
        in_specs=[tok, pl.BlockSpec((1, d), lambda b, t: (0, 0))],
        out_specs=tok,
        out_shape=jax.ShapeDtypeStruct((bn, s, d), F32),
        compiler_params=_cparams("parallel", "parallel"),
    )(x, g)


def _token_tile(s, want):
    tm = min(want, s)
    assert s % tm == 0 and tm % QB == 0
    return tm


def kernel(x, c, positions, w_ada, b_ada, g_norm_mix, g_norm_ffn, w_in, w_out, hgrn_lb_logits, hgrn_out_norm, gmlp_vnorm_g, gmlp_vnorm_b, gmlp_w_s, gmlp_b_s, ffn_w_gate, ffn_w_up, ffn_w_down, moe_w_router, moe_w_gate, moe_w_up, moe_w_down, g_final):
    bn, s, d = x.shape
    depth = w_in.shape[0]
    assert s % QB == 0 and d == sum(IN_WIDTHS[:1]) + B_DIM + C_DIM
    tm_proj = _token_tile(s, 512)
    tm_ffn = _token_tile(s, 1024)

    p_lb = jax.nn.softmax(hgrn_lb_logits.astype(F32), axis=0)
    cum = jnp.cumsum(p_lb, axis=0)
    lower_bounds = cum - cum[0:1]

    mod = _ada_mod(c, w_ada, b_ada)
    cosn, sinn, cost, sint = _rope_tables(positions)

    for l in range(depth):
        sh1, sc1, g1, sh2, sc2, g2 = [mod[l, :, i * d:(i + 1) * d].reshape(bn, 1, d) for i in range(6)]
        w_n, w_t = _prep_in_weights(w_in[l])
        (kz, ik3, bq, bf, bi, bg, c_out, qt, iq3t, vt4, iwt, bit) = _inproj(
            x, sh1, sc1, g_norm_mix[l].reshape(1, d), w_n, w_t, cosn, sinn, cost, sint,
            gmlp_vnorm_g[l].reshape(1, C_DIM), gmlp_vnorm_b[l].reshape(1, C_DIM), gmlp_w_s[l], gmlp_b_s[l],
            tm_proj)
        a_out = _dsa(ik3, kz, vt4, iq3t, iwt, qt)
        b_out = _hgrn(bq, bf, bi, bit, bg, lower_bounds[l].reshape(1, B_DIM),
                      jnp.tile(hgrn_out_norm[l], LANES // B_DV).reshape(1, LANES))
        gf = g_norm_ffn[l].reshape(1, d)
        if l % 2 == 0:
            i = l // 2
            x = _ffn(x, a_out, b_out, c_out, w_out[l], g1, sh2, sc2, g2, gf, ffn_w_gate[i].astype(BF16),
                     ffn_w_up[i].astype(BF16), ffn_w_down[i].astype(BF16), tm_ffn, 256)
        else:
            i = l // 2
            x = _outproj(x, a_out, b_out, c_out, w_out[l], g1, tm_proj)
            fused_final = l == depth - 1
            x = _moe(x, sh2, sc2, g2, gf, moe_w_router[i], moe_w_gate[i].astype(BF16),
                     moe_w_up[i].astype(BF16), moe_w_down[i].astype(BF16), _token_tile(s, 2048), 896,
                     g_final.reshape(1, d) if fused_final else None)
    if depth % 2 == 0:
        return x
    return _final_norm(x, g_final.reshape(1, d), tm_proj)
```

```python
import functools

import numpy as np
import jax
import jax.numpy as jnp
from jax import lax
from jax.experimental import pallas as pl
from jax.experimental.pallas import tpu as pltpu

F32 = jnp.float32
BF16 = jnp.bfloat16
I32 = jnp.int32
I16 = jnp.int16

HEAD_DIM = 64
A_HEADS = 6
A_DIM = A_HEADS * HEAD_DIM
IDX_HEADS = 4
IDX_DIM = 64
TOPK_MAX = 256
B_HEADS = 6
B_DK = 64
B_DV = 64
B_DIM = B_HEADS * B_DV
HGRN_CHUNK = 64
C_GROUPS = 4
C_GROUP_DIM = 64
C_DIM = C_GROUPS * C_GROUP_DIM
C_CHUNK = 128
ROPE_THETA = 10000.0
N_EXPERTS = 8
EPS = 1e-6
IN_WIDTHS = (A_DIM, HEAD_DIM, HEAD_DIM, IDX_HEADS * IDX_DIM, IDX_DIM, IDX_HEADS,
             B_DIM, B_DIM, B_DIM, B_DIM, 2 * C_DIM)

LANES = 128
KC = 128
QB = 2 * KC
PV_ROWS = HEAD_DIM + 16
VMEM_LIMIT = 48 * 1024 * 1024
HGRN_FAST_SPAN = 80.0
HGRN_SUB = 32
HGRN_GROUP = 4
INT_MIN = -2 ** 31

NT_DIMS = (((1,), (1,)), ((), ()))
TN_DIMS = (((0,), (0,)), ((), ()))


def _cparams(*sem):
    return pltpu.CompilerParams(dimension_semantics=sem, vmem_limit_bytes=VMEM_LIMIT)


def _split2(x):
    hi = x.astype(BF16)
    lo = (x - hi.astype(F32)).astype(BF16)
    return hi, lo


def _split3(x):
    hi = x.astype(BF16)
    r1 = x - hi.astype(F32)
    mid = r1.astype(BF16)
    lo = (r1 - mid.astype(F32)).astype(BF16)
    return hi, mid, lo


def _silu(x):
    return x * (1.0 / (1.0 + jnp.exp(-x)))


def _modulated_rmsnorm(x, g, sc, sh):
    var = jnp.mean(x * x, axis=-1, keepdims=True)
    y = x * lax.rsqrt(var + EPS)
    return (y * g) * (1.0 + sc) + sh


def _ada_kernel(c_ref, w_ref, b_ref, o_ref):
    cond = _silu(c_ref[...])
    o_ref[0] = jnp.dot(cond, w_ref[0], precision=lax.Precision.HIGHEST,
                       preferred_element_type=F32) + b_ref[0]


def _ada_mod(c, w_ada, b_ada):
    depth, d, d6 = w_ada.shape
    bn = c.shape[0]
    tn = 1536
    return pl.pallas_call(
        _ada_kernel,
        grid=(depth, d6 // tn),
        in_specs=[pl.BlockSpec((bn, d), lambda l, n: (0, 0)),
                  pl.BlockSpec((1, d, tn), lambda l, n: (l, 0, n)),
                  pl.BlockSpec((1, 1, tn), lambda l, n: (l, 0, n))],
        out_specs=pl.BlockSpec((1, bn, tn), lambda l, n: (l, 0, n)),
        out_shape=jax.ShapeDtypeStruct((depth, bn, d6), F32),
        compiler_params=_cparams("parallel", "parallel"),
    )(c, w_ada, b_ada.reshape(depth, 1, d6))


def _rope_kernel(posn_ref, post_ref, invn_ref, signn_ref, invt_ref,
                 cosn_ref, sinn_ref, cost_ref, sint_ref):
    ang_n = posn_ref[0].astype(F32) * invn_ref[...]
    cosn_ref[0] = jnp.cos(ang_n)
    sinn_ref[0] = jnp.sin(ang_n) * signn_ref[...]
    ang_t = invt_ref[...] * post_ref[0].astype(F32)
    cost_ref[0] = jnp.cos(ang_t)
    sint_ref[0] = jnp.sin(ang_t)


def _rope_tables(positions):
    bn, s = positions.shape
    half = HEAD_DIM // 2
    inv = ROPE_THETA ** (-jnp.arange(0, HEAD_DIM, 2, dtype=F32) / HEAD_DIM)
    inv_n = jnp.tile(inv, LANES // half).reshape(1, LANES)
    sign_n = jnp.tile(jnp.concatenate([-jnp.ones((half,), F32), jnp.ones((half,), F32)]),
                      LANES // HEAD_DIM).reshape(1, LANES)
    inv_t = inv.reshape(half, 1)
    full = lambda shape: pl.BlockSpec(shape, lambda b: (0,) * len(shape))
    return pl.pallas_call(
        _rope_kernel,
        grid=(bn,),
        in_specs=[pl.BlockSpec((1, s, 1), lambda b: (b, 0, 0)),
                  pl.BlockSpec((1, 1, s), lambda b: (b, 0, 0)),
                  full((1, LANES)), full((1, LANES)), full((half, 1))],
        out_specs=[pl.BlockSpec((1, s, LANES), lambda b: (b, 0, 0)),
                   pl.BlockSpec((1, s, LANES), lambda b: (b, 0, 0)),
                   pl.BlockSpec((1, half, s), lambda b: (b, 0, 0)),
                   pl.BlockSpec((1, half, s), lambda b: (b, 0, 0))],
        out_shape=[jax.ShapeDtypeStruct((bn, s, LANES), F32),
                   jax.ShapeDtypeStruct((bn, s, LANES), F32),
                   jax.ShapeDtypeStruct((bn, half, s), F32),
                   jax.ShapeDtypeStruct((bn, half, s), F32)],
        compiler_params=_cparams("parallel"),
    )(positions.reshape(bn, s, 1), positions.reshape(bn, 1, s), inv_n, sign_n, inv_t)


N_KI = 0
N_BQ = 2 * LANES
N_BF = N_BQ + B_DIM
N_BI = N_BF + B_DIM
N_BG = N_BI + B_DIM
N_CUV = N_BG + B_DIM
N_COLS = N_CUV + 2 * C_DIM
T_Q = 0
T_IQ = A_DIM
T_V = T_IQ + IDX_HEADS * IDX_DIM
T_IW = T_V + HEAD_DIM
T_BI = T_IW + 8
T_ROWS = T_BI + B_DIM


def _prep_in_weights(w_in_l):
    offs = np.concatenate([[0], np.cumsum(IN_WIDTHS)])
    sl = lambda i: w_in_l[:, int(offs[i]):int(offs[i + 1])]
    aq, ak, av, iq, ik, iw, bq, bf, bi, bg, cuv = [sl(i) for i in range(11)]
    half = HEAD_DIM // 2
    rot = lambda w: jnp.concatenate([w[:, half:], w[:, :half]], axis=1)
    w_n = jnp.concatenate([ak, ik, rot(ak), rot(ik), bq, bf, bi, bg, cuv], axis=1)
    w_t = jnp.concatenate([aq, iq, av, iw, jnp.zeros((w_in_l.shape[0], 4), w_in_l.dtype), bi], axis=1).T
    return w_n.astype(BF16), w_t.astype(BF16)


def _inproj_kernel(x_ref, sh_ref, sc_ref, g_ref, wn_ref, wt_ref, cosn_ref, sinn_ref, cost_ref, sint_ref,
                   vg_ref, vb_ref, ws_ref, cbias_ref,
                   kz_ref, ik3_ref, bq_ref, bf_ref, bi_ref, bg_ref, c_ref,
                   qt_ref, iq3t_ref, vt_ref, iwt_ref, bit_ref):
    tm = x_ref.shape[1]
    half = HEAD_DIM // 2
    h = _modulated_rmsnorm(x_ref[0], g_ref[...], sc_ref[0], sh_ref[0]).astype(BF16)

    cosn = cosn_ref[0]
    sinn = sinn_ref[0]
    zk = jnp.dot(h, wn_ref[:, N_KI:N_KI + 2 * LANES], preferred_element_type=F32)
    ki = zk[:, :LANES] * cosn + zk[:, LANES:] * sinn
    lane = lax.broadcasted_iota(I32, (tm, LANES), 1)
    kz_ref[0] = jnp.where(lane < HEAD_DIM, ki, 0.0).astype(BF16)
    ik2 = jnp.where(lane < IDX_DIM, pltpu.roll(ki, HEAD_DIM, axis=1), ki)
    hi, lo = _split2(ik2)
    ik3_ref[0, :, :LANES] = jnp.where(lane < IDX_DIM, hi, lo)
    ik3_ref[0, :, LANES:] = hi
    bq_ref[0] = jnp.dot(h, wn_ref[:, N_BQ:N_BF], preferred_element_type=F32).astype(BF16)
    bf_ref[0] = jnp.dot(h, wn_ref[:, N_BF:N_BI], preferred_element_type=F32)
    bi_ref[0] = jnp.dot(h, wn_ref[:, N_BI:N_BG], preferred_element_type=F32).astype(BF16)
    bg_ref[0] = jnp.dot(h, wn_ref[:, N_BG:N_CUV], preferred_element_type=F32).astype(BF16)
    cuv = jnp.dot(h, wn_ref[:, N_CUV:N_COLS], preferred_element_type=F32)
    c_ref[0] = _gmlp_mix(cuv, vg_ref[...], vb_ref[...], ws_ref, cbias_ref[...]).astype(BF16)

    zt = lax.dot_general(wt_ref[...], h, NT_DIMS, preferred_element_type=F32)
    cost = cost_ref[0]
    sint = sint_ref[0]
    qscale = HEAD_DIM ** -0.5
    for hh in range(A_HEADS):
        r0 = T_Q + hh * HEAD_DIM
        x1 = zt[r0:r0 + half]
        x2 = zt[r0 + half:r0 + HEAD_DIM]
        qt_ref[0, hh * HEAD_DIM:hh * HEAD_DIM + half] = ((x1 * cost - x2 * sint) * qscale).astype(BF16)
        qt_ref[0, hh * HEAD_DIM + half:(hh + 1) * HEAD_DIM] = ((x2 * cost + x1 * sint) * qscale).astype(BF16)
    iscale = IDX_DIM ** -0.5
    zero = jnp.zeros((IDX_DIM, tm), BF16)
    for hh in range(IDX_HEADS):
        r0 = T_IQ + hh * IDX_DIM
        x1 = zt[r0:r0 + half]
        x2 = zt[r0 + half:r0 + IDX_DIM]
        y = jnp.concatenate([(x1 * cost - x2 * sint) * iscale, (x2 * cost + x1 * sint) * iscale], axis=0)
        hi, lo = _split2(y)
        iq3t_ref[0, hh, 0 * IDX_DIM:1 * IDX_DIM] = hi
        iq3t_ref[0, hh, 1 * IDX_DIM:2 * IDX_DIM] = hi
        iq3t_ref[0, hh, 2 * IDX_DIM:3 * IDX_DIM] = lo
        iq3t_ref[0, hh, 3 * IDX_DIM:4 * IDX_DIM] = zero
    vt = zt[T_V:T_V + HEAD_DIM].astype(BF16)
    for i in range(tm // KC):
        vt_ref[0, i] = vt[:, i * KC:(i + 1) * KC]
    iwt_ref[0] = zt[T_IW:T_IW + 8] * (IDX_HEADS ** -0.5)
    bit = zt[T_BI:T_BI + B_DIM].astype(BF16)
    for i in range(tm // LANES):
        bit_ref[0, i] = bit[:, i * LANES:(i + 1) * LANES]


def _inproj(x, sh, sc, g, w_n, w_t, cosn, sinn, cost, sint, vg, vb, ws, bs, tm):
    bn, s, d = x.shape
    assert tm % C_CHUNK == 0
    nt = s // tm
    half = HEAD_DIM // 2
    cbias = jnp.repeat(bs.T, C_GROUP_DIM, axis=1)
    tok = lambda w: pl.BlockSpec((1, tm, w), lambda b, t: (b, t, 0))
    vec = pl.BlockSpec((1, 1, d), lambda b, t: (b, 0, 0))
    full2 = lambda a: pl.BlockSpec(a.shape, lambda b, t: (0, 0))
    out_shapes = [
        jax.ShapeDtypeStruct((bn, s, LANES), BF16),
        jax.ShapeDtypeStruct((bn, s, 2 * LANES), BF16),
        jax.ShapeDtypeStruct((bn, s, B_DIM), BF16),
        jax.ShapeDtypeStruct((bn, s, B_DIM), F32),
        jax.ShapeDtypeStruct((bn, s, B_DIM), BF16),
        jax.ShapeDtypeStruct((bn, s, B_DIM), BF16),
        jax.ShapeDtypeStruct((bn, s, C_DIM), BF16),
        jax.ShapeDtypeStruct((bn, A_DIM, s), BF16),
        jax.ShapeDtypeStruct((bn, IDX_HEADS, 4 * IDX_DIM, s), BF16),
        jax.ShapeDtypeStruct((bn, s // KC, HEAD_DIM, KC), BF16),
        jax.ShapeDtypeStruct((bn, 8, s), F32),
        jax.ShapeDtypeStruct((bn, s // LANES, B_DIM, LANES), BF16),
    ]
    out_specs = [
        tok(LANES), tok(2 * LANES), tok(B_DIM), tok(B_DIM), tok(B_DIM), tok(B_DIM), tok(C_DIM),
        pl.BlockSpec((1, A_DIM, tm), lambda b, t: (b, 0, t)),
        pl.BlockSpec((1, IDX_HEADS, 4 * IDX_DIM, tm), lambda b, t: (b, 0, 0, t)),
        pl.BlockSpec((1, tm // KC, HEAD_DIM, KC), lambda b, t: (b, t, 0, 0)),
        pl.BlockSpec((1, 8, tm), lambda b, t: (b, 0, t)),
        pl.BlockSpec((1, tm // LANES, B_DIM, LANES), lambda b, t: (b, t, 0, 0)),
    ]
    return pl.pallas_call(
        _inproj_kernel,
        grid=(bn, nt),
        in_specs=[tok(d), vec, vec, full2(g), full2(w_n), full2(w_t),
                  tok(LANES), tok(LANES),
                  pl.BlockSpec((1, half, tm), lambda b, t: (b, 0, t)),
                  pl.BlockSpec((1, half, tm), lambda b, t: (b, 0, t)),
                  full2(vg), full2(vb), pl.BlockSpec(ws.shape, lambda b, t: (0, 0, 0)), full2(cbias)],
        out_specs=out_specs,
        out_shape=out_shapes,
        compiler_params=_cparams("parallel", "parallel"),
    )(x, sh, sc, g, w_n, w_t, cosn, sinn, cost, sint, vg, vb, ws, cbias)


def _dsa_kernel(n_top, ik3_ref, kz_ref, vt_ref, iq3t_ref, iwt_ref, qt_ref, out_ref,
                key_scr, hi_scr, lo_scr, lg_scr, acc_scr):
    j = pl.program_id(1)
    npair = j + 1
    pair = 2 * KC
    row = lax.broadcasted_iota(I32, (KC, QB), 0)
    col = lax.broadcasted_iota(I32, (KC, QB), 1)
    int_min = jnp.int32(INT_MIN)
    i16_min = jnp.int16(-2 ** 15)

    def score_pair(p, diagonal):
        for u in range(2):
            ks = pl.multiple_of(p * pair + u * KC, KC)
            ikc = ik3_ref[0, pl.ds(ks, KC), :]
            sc = jnp.zeros((KC, QB), F32)
            for hh in range(IDX_HEADS):
                rel = jnp.dot(ikc, iq3t_ref[0, hh], preferred_element_type=F32)
                sc = sc + jnp.maximum(rel, 0.0) * iwt_ref[0, hh:hh + 1, :]
            sc = jnp.where(sc == 0.0, 0.0, sc)
            bits = pltpu.bitcast(sc, I32)
            key = bits ^ ((bits >> 31) & jnp.int32(0x7FFFFFFF))
            if diagonal:
                key = jnp.where((u * KC + row) <= col, key, int_min)
            key_scr[pl.ds(ks, KC), :] = key
            hi_scr[pl.ds(ks, KC), :] = (key >> 16).astype(I16)
            lo_scr[pl.ds(ks, KC), :] = ((key & jnp.int32(0xFFFF)) - 32768).astype(I16)

    def score_body(p, carry):
        score_pair(p, False)
        return carry

    lax.fori_loop(0, j, score_body, 0)
    score_pair(j, True)

    def count16(ref, pred_fn):
        def body(p, acc):
            ks = pl.multiple_of(p * pair, pair)
            m = jnp.where(pred_fn(ref[pl.ds(ks, pair), :]), jnp.int16(1), jnp.int16(0))
            parts = [m[16 * i:16 * (i + 1)] for i in range(pair // 16)]
            while len(parts) > 1:
                parts = [parts[i] + parts[i + 1] for i in range(0, len(parts), 2)]
            return acc + parts[0]
        acc = lax.fori_loop(0, npair, body, jnp.zeros((16, QB), I16))
        return jnp.sum(acc.astype(I32), axis=0, keepdims=True)

    def bisect16(ref, k_needed):
        def bit_body(i, t_u):
            cand = t_u | jnp.left_shift(jnp.int32(1), 15 - i)
            cand16 = (cand - 32768).astype(I16)
            cnt = count16(ref, lambda x: x >= cand16)
            return jnp.where(cnt >= k_needed, cand, t_u)
        return lax.fori_loop(0, 16, bit_body, jnp.zeros((1, QB), I32))

    t_hi = bisect16(hi_scr, n_top)
    t_hi16 = (t_hi - 32768).astype(I16)
    n_hi_gt = count16(hi_scr, lambda x: x > t_hi16)

    def bucket_body(p, carry):
        ks = pl.multiple_of(p * pair, pair)
        in_bucket = hi_scr[pl.ds(ks, pair), :] == t_hi16
        lo_scr[pl.ds(ks, pair), :] = jnp.where(in_bucket, lo_scr[pl.ds(ks, pair), :], i16_min)
        return carry

    lax.fori_loop(0, npair, bucket_body, 0)
    t_lo = bisect16(lo_scr, n_top - n_hi_gt)
    t_lo16 = (t_lo - 32768).astype(I16)
    n_gt = n_hi_gt + count16(lo_scr, lambda x: x > t_lo16)
    thr = jnp.left_shift(t_hi - 32768, 16) | t_lo
    n_tie = (n_top - n_gt).astype(F32)
    thr_valid = jnp.where(thr > int_min, 1.0, 0.0)

    tr = lax.broadcasted_iota(I32, (KC, KC), 0)
    tc = lax.broadcasted_iota(I32, (KC, KC), 1)
    ltri = jnp.where(tr >= tc, 1.0, 0.0).astype(BF16)
    zpad = jnp.zeros((HEAD_DIM, QB), BF16)
    qpad = [jnp.concatenate([qt_ref[0, hh * HEAD_DIM:(hh + 1) * HEAD_DIM, :], zpad], axis=0)
            for hh in range(A_HEADS)]
    neg_inf = jnp.float32(-jnp.inf)

    def pass_a(p, carry):
        tie_cnt, ms = carry
        ms = list(ms)
        for u in range(2):
            ks = pl.multiple_of(p * pair + u * KC, KC)
            kc = key_scr[pl.ds(ks, KC), :]
            eqf = jnp.where(kc == thr, thr_valid, 0.0)
            pref = jnp.dot(ltri, eqf.astype(BF16), preferred_element_type=F32)
            keep_tie = eqf * jnp.where((tie_cnt + pref) <= n_tie, 1.0, 0.0)
            sel = jnp.where(kc > thr, 1.0, keep_tie) > 0.5
            tie_cnt = tie_cnt + pref[KC - 1:KC, :]
            kzc = kz_ref[0, pl.ds(ks, KC), :]
            for hh in range(A_HEADS):
                lt = jnp.dot(kzc, qpad[hh], preferred_element_type=F32)
                lt = jnp.where(sel, lt, neg_inf)
                lg_scr[hh, pl.ds(ks, KC), :] = lt
                ms[hh] = jnp.maximum(ms[hh], jnp.max(lt, axis=0, keepdims=True))
        return tie_cnt, tuple(ms)

    init_m = tuple(jnp.full((1, QB), neg_inf, F32) for _ in range(A_HEADS))
    _, ms = lax.fori_loop(0, npair, pass_a, (jnp.zeros((1, QB), F32), init_m))

    acc_scr[...] = jnp.zeros_like(acc_scr)
    ones_rows = jnp.ones((PV_ROWS - HEAD_DIM, pair), BF16)

    def pass_b(p, carry):
        ks = pl.multiple_of(p * pair, pair)
        vt2 = jnp.concatenate([vt_ref[0, 2 * p], vt_ref[0, 2 * p + 1]], axis=1)
        vt2 = jnp.concatenate([vt2, ones_rows], axis=0)
        for hh in range(A_HEADS):
            pr = jnp.exp(lg_scr[hh, pl.ds(ks, pair), :] - ms[hh])
            acc_scr[hh * PV_ROWS:(hh + 1) * PV_ROWS, :] += jnp.dot(
                vt2, pr.astype(BF16), preferred_element_type=F32)
        return carry

    lax.fori_loop(0, npair, pass_b, 0)
    o_t = jnp.concatenate(
        [acc_scr[hh * PV_ROWS:hh * PV_ROWS + HEAD_DIM, :]
         / acc_scr[hh * PV_ROWS + HEAD_DIM:hh * PV_ROWS + HEAD_DIM + 1, :] for hh in range(A_HEADS)], axis=0)
    out_ref[0] = o_t.T.astype(BF16)


def _dsa(ik3, kz, vt4, iq3t, iwt, qt):
    bn, s, _ = kz.shape
    assert s % QB == 0
    n_top = min(TOPK_MAX, s // 4)
    nq = s // QB
    return pl.pallas_call(
        functools.partial(_dsa_kernel, n_top),
        grid=(bn, nq),
        in_specs=[pl.BlockSpec((1, s, 2 * LANES), lambda b, q: (b, 0, 0)),
                  pl.BlockSpec((1, s, LANES), lambda b, q: (b, 0, 0)),
                  pl.BlockSpec((1, s // KC, HEAD_DIM, KC), lambda b, q: (b, 0, 0, 0)),
                  pl.BlockSpec((1, IDX_HEADS, 4 * IDX_DIM, QB), lambda b, q: (b, 0, 0, q)),
                  pl.BlockSpec((1, 8, QB), lambda b, q: (b, 0, q)),
                  pl.BlockSpec((1, A_DIM, QB), lambda b, q: (b, 0, q))],
        out_specs=pl.BlockSpec((1, QB, A_DIM), lambda b, q: (b, q, 0)),
        out_shape=jax.ShapeDtypeStruct((bn, s, A_DIM), BF16),
        scratch_shapes=[pltpu.VMEM((s, QB), I32), pltpu.VMEM((s, QB), I16), pltpu.VMEM((s, QB), I16),
                        pltpu.VMEM((A_HEADS, s, QB), F32), pltpu.VMEM((A_HEADS * PV_ROWS, QB), F32)],
        compiler_params=_cparams("parallel", "arbitrary"),
    )(ik3, kz, vt4, iq3t, iwt, qt)


def _hgrn_kernel(q_ref, f_ref, i_ref, it_ref, g_ref, lb_ref, gn_ref, o_ref, st_scr, oi_scr):
    s = q_ref.shape[1]
    ch = HGRN_CHUNK
    sb = HGRN_SUB
    nsb = ch // sb
    rows = HGRN_GROUP * ch
    wins_per_group = rows // LANES
    lane = lax.broadcasted_iota(I32, (ch, LANES), 1)
    head0 = lane < B_DK
    lr = lax.broadcasted_iota(I32, (rows, 3 * rows), 0)
    lc = lax.broadcasted_iota(I32, (rows, 3 * rows), 1) % rows
    lcum = jnp.where(((lr // ch) == (lc // ch)) & (lc <= lr), 1.0, 0.0).astype(BF16)
    att_rows = HGRN_GROUP * 2 * ch
    att_cols = HGRN_GROUP * nsb * ch
    ar = lax.broadcasted_iota(I32, (att_rows, att_cols), 0)
    ac = lax.broadcasted_iota(I32, (att_rows, att_cols), 1)
    at = ar % ch
    att_mask = (((ar // (2 * ch)) == (ac // (nsb * ch))) & ((at // sb) == ((ac // ch) % nsb))
                & ((ac % ch) <= at))
    bd_r = lax.broadcasted_iota(I32, (LANES, LANES), 0)
    bd_c = lax.broadcasted_iota(I32, (LANES, LANES), 1)
    same_head = (bd_r < B_DK) == (bd_c < B_DK)
    ones_bd = jnp.where(same_head, 1.0, 0.0).astype(BF16)
    ones_bd2 = jnp.concatenate([ones_bd, ones_bd], axis=0)
    s_iota = lax.broadcasted_iota(I32, (ch, LANES), 0)

    lb = lb_ref[...]
    log_lb = jnp.log(lb)
    log_1mlb = jnp.log1p(-lb)
    gn = gn_ref[...]
    st_scr[...] = jnp.zeros_like(st_scr)

    def group_body(it, carry):
        t0 = pl.multiple_of(it * rows, rows)
        z = f_ref[0, pl.ds(t0, rows), :]
        q = q_ref[0, pl.ds(t0, rows), :].astype(F32)
        v = i_ref[0, pl.ds(t0, rows), :]
        softplus_tail = jnp.log1p(jnp.exp(-jnp.abs(z)))
        log_sig = -(jnp.maximum(-z, 0.0) + softplus_tail)
        x2 = log_1mlb + log_sig
        amax = jnp.maximum(log_lb, x2)
        log_f = amax + jnp.log1p(jnp.exp(-jnp.abs(log_lb - x2)))
        kk = (1.0 - lb) * jnp.exp(-(jnp.maximum(z, 0.0) + softplus_tail))
        f_hi, f_mid, f_lo = _split3(log_f)
        b = jnp.dot(lcum, jnp.concatenate([f_hi, f_mid, f_lo], axis=0), preferred_element_type=F32)
        ref_parts = []
        for c in range(HGRN_GROUP):
            ref_parts.append(jnp.zeros((sb, LANES), F32))
            for i in range(1, nsb):
                r = c * ch + i * sb
                ref_parts.append(jnp.broadcast_to(b[r - 1:r], (sb, LANES)))
        ref = jnp.concatenate(ref_parts, axis=0)
        q_loc = q * jnp.exp(b - ref)
        q_chk = (q * jnp.exp(b)).astype(BF16)
        span = ref - b
        span_max = jnp.max(span)

        k_parts, q_parts, v_parts, kh_cols, decay = [], [], [], [], []
        for c in range(HGRN_GROUP):
            lo_r = c * ch
            b_c = b[lo_r:lo_r + ch]
            kk_c = kk[lo_r:lo_r + ch]
            b_last = b_c[ch - 1:ch]
            for i in range(nsb):
                r = lo_r + i * sb
                k_parts.append(kk_c * jnp.exp(jnp.minimum(ref[r:r + 1] - b_c, HGRN_FAST_SPAN)))
            q_c = q_loc[lo_r:lo_r + ch]
            q_parts += [jnp.where(head0, q_c, 0.0), jnp.where(head0, 0.0, q_c)]
            v_parts += [v[lo_r:lo_r + ch]] * nsb
            kh = (kk_c * jnp.exp(b_last - b_c)).astype(BF16)
            kh_cols.append(jnp.concatenate(
                ([jnp.zeros((lo_r, LANES), BF16)] if lo_r else []) + [kh]
                + ([jnp.zeros((rows - lo_r - ch, LANES), BF16)] if rows - lo_r - ch else []), axis=0))
            decay.append(jnp.exp(b_last))
        att = lax.dot_general(jnp.concatenate(q_parts, axis=0).astype(BF16),
                              jnp.concatenate(k_parts, axis=0).astype(BF16), NT_DIMS,
                              preferred_element_type=F32)
        att = jnp.where(att_mask, att, 0.0).astype(BF16)
        o2 = jnp.dot(att, jnp.concatenate(v_parts, axis=0), preferred_element_type=F32)
        for c in range(HGRN_GROUP):
            r = c * 2 * ch
            oi_scr[c * ch:(c + 1) * ch, :] = jnp.where(head0, o2[r:r + ch], o2[r + ch:r + 2 * ch])

        it_win = jnp.concatenate([it_ref[0, wins_per_group * it + w] for w in range(wins_per_group)], axis=1)
        upd = jnp.dot(it_win, jnp.concatenate(kh_cols, axis=1), preferred_element_type=F32)
        st = st_scr[...]
        states = []
        for c in range(HGRN_GROUP):
            states.append(st.astype(BF16))
            st = st * decay[c] + jnp.where(same_head, upd[:, c * LANES:(c + 1) * LANES], 0.0)
        st_scr[...] = st
        oi_all = lax.dot_general(q_chk, jnp.concatenate(states, axis=0), NT_DIMS,
                                 preferred_element_type=F32)
        o_inter = [oi_all[c * ch:(c + 1) * ch, c * LANES:(c + 1) * LANES] for c in range(HGRN_GROUP)]

        @pl.when(span_max > HGRN_FAST_SPAN)
        def _():
            for c in range(HGRN_GROUP):
                lo_r = c * ch

                @pl.when(jnp.max(span[lo_r:lo_r + ch]) > HGRN_FAST_SPAN)
                def _():
                    b_c = b[lo_r:lo_r + ch]
                    q_c = q[lo_r:lo_r + ch]
                    kk_c = kk[lo_r:lo_r + ch]
                    vf = v[lo_r:lo_r + ch].astype(F32)

                    def t_body(t, carry2):
                        onehot = jnp.where(s_iota == t, 1.0, 0.0)
                        b_t = jnp.sum(onehot * b_c, axis=0, keepdims=True)
                        q_t = jnp.sum(onehot * q_c, axis=0, keepdims=True)
                        dec = jnp.exp(jnp.where(s_iota <= t, b_t - b_c, -jnp.inf))
                        w = q_t * kk_c * dec
                        w0 = jnp.sum(jnp.where(head0, w, 0.0), axis=1, keepdims=True)
                        w1 = jnp.sum(jnp.where(head0, 0.0, w), axis=1, keepdims=True)
                        a_col = jnp.where(head0, w0, w1)
                        oi_scr[pl.ds(lo_r + t, 1), :] = jnp.sum(a_col * vf, axis=0, keepdims=True)
                        return carry2

                    lax.fori_loop(0, ch, t_body, 0)

        o = oi_scr[...] + jnp.concatenate(o_inter, axis=0)
        sq_hi, sq_lo = _split2(o * o)
        ss = jnp.dot(jnp.concatenate([sq_hi, sq_lo], axis=1), ones_bd2, preferred_element_type=F32)
        y = (o * lax.rsqrt(ss * (1.0 / B_DV) + EPS)) * gn
        g = g_ref[0, pl.ds(t0, rows), :].astype(F32)
        o_ref[0, pl.ds(t0, rows), :] = (y * _silu(g)).astype(o_ref.dtype)
        return carry

    lax.fori_loop(0, s // rows, group_body, 0)


def _hgrn(bq, bf, bi, bit, bg, lb, gn):
    bn, s, _ = bq.shape
    assert s % (HGRN_GROUP * HGRN_CHUNK) == 0 and 2 * HGRN_CHUNK == LANES and HGRN_GROUP % 2 == 0
    npair = B_DIM // LANES
    tok = pl.BlockSpec((1, s, LANES), lambda b, p: (b, 0, p))
    return pl.pallas_call(
        _hgrn_kernel,
        grid=(bn, npair),
        in_specs=[tok, tok, tok,
                  pl.BlockSpec((1, s // LANES, LANES, LANES), lambda b, p: (b, 0, p, 0)),
                  tok,
                  pl.BlockSpec((1, LANES), lambda b, p: (0, p)),
                  pl.BlockSpec((1, LANES), lambda b, p: (0, 0))],
        out_specs=tok,
        out_shape=jax.ShapeDtypeStruct((bn, s, B_DIM), BF16),
        scratch_shapes=[pltpu.VMEM((LANES, LANES), F32),
                        pltpu.VMEM((HGRN_GROUP * HGRN_CHUNK, LANES), F32)],
        compiler_params=_cparams("parallel", "parallel"),
    )(bq, bf, bi, bit, bg, lb, gn)


def _gmlp_mix(uv, vg, vb, ws_ref, bias):
    uv = 0.5 * uv * (1.0 + lax.erf(uv * (2.0 ** -0.5)))
    u = uv[:, :C_DIM]
    v = uv[:, C_DIM:]
    mu = jnp.mean(v, axis=-1, keepdims=True)
    var = jnp.mean(jnp.square(v - mu), axis=-1, keepdims=True)
    vn_b = (((v - mu) * lax.rsqrt(var + EPS)) * vg + vb).astype(BF16)
    r_t = lax.broadcasted_iota(I32, (C_CHUNK, C_CHUNK), 0)
    r_s = lax.broadcasted_iota(I32, (C_CHUNK, C_CHUNK), 1)
    group = lax.broadcasted_iota(I32, (C_CHUNK, C_DIM), 1) // C_GROUP_DIM
    ws = [jnp.where(r_t >= r_s, ws_ref[gi], 0.0).astype(BF16) for gi in range(C_GROUPS)]
    outs = []
    for c in range(uv.shape[0] // C_CHUNK):
        sl = slice(c * C_CHUNK, (c + 1) * C_CHUNK)
        mixed = jnp.zeros((C_CHUNK, C_DIM), F32)
        for gi in range(C_GROUPS):
            m = jnp.dot(ws[gi], vn_b[sl], preferred_element_type=F32)
            mixed = jnp.where(group == gi, m, mixed)
        outs.append(u[sl] * (mixed + bias))
    return jnp.concatenate(outs, axis=0)


def _outproj_kernel(x_ref, a_ref, b_ref, c_ref, wa_ref, wb_ref, wc_ref, g1_ref, o_ref):
    mix = (jnp.dot(a_ref[0], wa_ref[...], preferred_element_type=F32)
           + jnp.dot(b_ref[0], wb_ref[...], preferred_element_type=F32)
           + jnp.dot(c_ref[0], wc_ref[...], preferred_element_type=F32))
    o_ref[0] = x_ref[0] + g1_ref[0] * mix


def _outproj(x, a, b, c, w_out_l, g1, tm):
    bn, s, d = x.shape
    wa = w_out_l[:A_DIM].astype(BF16)
    wb = w_out_l[A_DIM:A_DIM + B_DIM].astype(BF16)
    wc = w_out_l[A_DIM + B_DIM:].astype(BF16)
    tok = lambda w: pl.BlockSpec((1, tm, w), lambda bb, t: (bb, t, 0))
    full = lambda arr: pl.BlockSpec(arr.shape, lambda bb, t: (0, 0))
    return pl.pallas_call(
        _outproj_kernel,
        grid=(bn, s // tm),
        in_specs=[tok(d), tok(A_DIM), tok(B_DIM), tok(C_DIM), full(wa), full(wb), full(wc),
                  pl.BlockSpec((1, 1, d), lambda bb, t: (bb, 0, 0))],
        out_specs=tok(d),
        out_shape=jax.ShapeDtypeStruct((bn, s, d), F32),
        compiler_params=_cparams("parallel", "parallel"),
    )(x, a, b, c, wa, wb, wc, g1)


def _ffn_kernel(x_ref, a_ref, b_ref, c_ref, wa_ref, wb_ref, wc_ref, g1_ref, sh_ref, sc_ref, gate_ref, g_ref,
                wg_ref, wu_ref, wd_ref, o_ref, xn_scr, h_scr, acc_scr):
    f = pl.program_id(2)

    @pl.when(f == 0)
    def _():
        mix = (jnp.dot(a_ref[0], wa_ref[...], preferred_element_type=F32)
               + jnp.dot(b_ref[0], wb_ref[...], preferred_element_type=F32)
               + jnp.dot(c_ref[0], wc_ref[...], preferred_element_type=F32))
        xn = x_ref[0] + g1_ref[0] * mix
        xn_scr[...] = xn
        h_scr[...] = _modulated_rmsnorm(xn, g_ref[...], sc_ref[0], sh_ref[0]).astype(BF16)
        acc_scr[...] = jnp.zeros_like(acc_scr)

    h = h_scr[...]
    a = jnp.dot(h, wg_ref[...], preferred_element_type=F32)
    u = jnp.dot(h, wu_ref[...], preferred_element_type=F32)
    act = (_silu(a) * u).astype(BF16)
    acc_scr[...] += jnp.dot(act, wd_ref[...], preferred_element_type=F32)

    @pl.when(f == pl.num_programs(2) - 1)
    def _():
        o_ref[0] = xn_scr[...] + gate_ref[0] * acc_scr[...]


def _ffn(x, a, b, c, w_out_l, g1, sh, sc, gate, g, wg, wu, wd, tm, tf):
    bn, s, d = x.shape
    fdim = wg.shape[1]
    wa = w_out_l[:A_DIM].astype(BF16)
    wb = w_out_l[A_DIM:A_DIM + B_DIM].astype(BF16)
    wc = w_out_l[A_DIM + B_DIM:].astype(BF16)
    tok = pl.BlockSpec((1, tm, d), lambda b, t, f: (b, t, 0))
    tokw = lambda w: pl.BlockSpec((1, tm, w), lambda b, t, f: (b, t, 0))
    full = lambda arr: pl.BlockSpec(arr.shape, lambda b, t, f: (0, 0))
    vec = pl.BlockSpec((1, 1, d), lambda b, t, f: (b, 0, 0))
    return pl.pallas_call(
        _ffn_kernel,
        grid=(bn, s // tm, fdim // tf),
        in_specs=[tok, tokw(A_DIM), tokw(B_DIM), tokw(C_DIM), full(wa), full(wb), full(wc), vec,
                  vec, vec, vec,
                  pl.BlockSpec((1, d), lambda b, t, f: (0, 0)),
                  pl.BlockSpec((d, tf), lambda b, t, f: (0, f)),
                  pl.BlockSpec((d, tf), lambda b, t, f: (0, f)),
                  pl.BlockSpec((tf, d), lambda b, t, f: (f, 0))],
        out_specs=tok,
        out_shape=jax.ShapeDtypeStruct((bn, s, d), F32),
        scratch_shapes=[pltpu.VMEM((tm, d), F32), pltpu.VMEM((tm, d), BF16), pltpu.VMEM((tm, d), F32)],
        compiler_params=_cparams("parallel", "parallel", "arbitrary"),
    )(x, a, b, c, wa, wb, wc, g1, sh, sc, gate, g, wg, wu, wd)


MOE_RC = 256
MOE_TAIL = 128
MOE_TB = 512


def _moe_kernel(final_norm, x_ref, sh_ref, sc_ref, gate_ref, g_ref, gfin_ref, wr_ref, wg_ref, wu_ref, wd_ref, o_ref,
                h_scr, xe_scr, ye_scr, gcol_scr, gates_scr, rsel_scr, rn_scr, rcol_scr, cnt_smem):
    e = pl.program_id(2)
    f = pl.program_id(3)
    tm = x_ref.shape[1]
    rc_rows = MOE_RC
    tb = MOE_TB
    ntb = tm // tb
    neg_inf = jnp.float32(-jnp.inf)

    @pl.when((e == 0) & (f == 0))
    def _route():
        for jb in range(ntb):
            rows = slice(jb * tb, (jb + 1) * tb)
            h = _modulated_rmsnorm(x_ref[0, rows, :], g_ref[...], sc_ref[0], sh_ref[0])
            h_scr[rows, :] = h.astype(BF16)
            rcol_scr[rows, :] = jnp.dot(h, wr_ref[...], precision=lax.Precision.HIGHEST,
                                        preferred_element_type=F32)
            o_ref[0, rows, :] = jnp.zeros((tb, o_ref.shape[2]), F32)
        logits = rcol_scr[...].T[:N_EXPERTS]
        ridx = lax.broadcasted_iota(I32, (N_EXPERTS, tm), 0)
        m1 = jnp.max(logits, axis=0, keepdims=True)
        i1 = jnp.min(jnp.where(logits == m1, ridx, N_EXPERTS), axis=0, keepdims=True)
        rest = jnp.where(ridx == i1, neg_inf, logits)
        m2 = jnp.max(rest, axis=0, keepdims=True)
        i2 = jnp.min(jnp.where(rest == m2, ridx, N_EXPERTS), axis=0, keepdims=True)
        e2 = jnp.exp(m2 - m1)
        den = 1.0 + e2
        gates_scr[...] = jnp.where(ridx == i1, 1.0 / den, jnp.where(ridx == i2, e2 / den, 0.0))
        sel = jnp.where(ridx == i1, 1.0, jnp.where(ridx == i2, 1.0, 0.0))
        ur = lax.broadcasted_iota(I32, (LANES, LANES), 0)
        uc = lax.broadcasted_iota(I32, (LANES, LANES), 1)
        utri = jnp.where(ur <= uc, 1.0, 0.0).astype(BF16)
        carry = jnp.zeros((N_EXPERTS, 1), F32)
        for kb in range(tm // LANES):
            sb = sel[:, kb * LANES:(kb + 1) * LANES]
            pref = jnp.dot(sb.astype(BF16), utri, preferred_element_type=F32) + carry
            rsel_scr[:, kb * LANES:(kb + 1) * LANES] = jnp.where(sb > 0.0, pref, -1.0)
            carry = pref[:, LANES - 1:LANES]
            if (kb + 1) * LANES == tm // 2:
                carry_half = carry
        r8 = lax.broadcasted_iota(I32, (N_EXPERTS, 1), 0)
        for ee in range(N_EXPERTS):
            cnt_smem[ee] = jnp.sum(jnp.where(r8 == ee, carry, 0.0)).astype(I32)
            cnt_smem[N_EXPERTS + ee] = jnp.sum(jnp.where(r8 == ee, carry_half, 0.0)).astype(I32)
        rn_scr[...] = jnp.concatenate(
            [rsel_scr[...], jnp.zeros((LANES - N_EXPERTS, tm), F32)], axis=0).T

    cnt = cnt_smem[e]
    rest = cnt % rc_rows
    has_tail = (rest > 0) & (rest <= MOE_TAIL)
    n_rc = cnt // rc_rows + jnp.where(rest > MOE_TAIL, 1, 0)
    tail_r0 = pl.multiple_of(n_rc * rc_rows, rc_rows)
    cnt_first = cnt_smem[N_EXPERTS + e]
    token_regions = ((0, tm), (0, tm // 2), (tm // 2, tm))

    def for_region(r0, nrows, fn):
        region_of_chunk = jnp.where(r0 + nrows <= cnt_first, 1, jnp.where(r0 >= cnt_first, 2, 0))
        for region, (lo, hi) in enumerate(token_regions):
            @pl.when(region_of_chunk == region)
            def _():
                fn(lo, hi)

    @pl.when(f == 0)
    def _gather():
        rsel_row = rsel_scr[pl.ds(e, 1), :]
        gate_row = gates_scr[pl.ds(e, 1), :]
        lane = lax.broadcasted_iota(I32, (tm, LANES), 1)
        rcol = jnp.sum(jnp.where(lane == e, rn_scr[...], 0.0), axis=1, keepdims=True)
        rcol_scr[...] = jnp.broadcast_to(rcol, (tm, LANES))

        def compact_chunk(r0, nrows):
            def compact(lo, hi):
                want = (r0 + 1 + lax.broadcasted_iota(I32, (nrows, hi - lo), 0)).astype(F32)
                pm = rsel_row[:, lo:hi] == want
                pb = jnp.where(pm, 1.0, 0.0).astype(BF16)
                xe_scr[pl.ds(r0, nrows), :] = jnp.dot(pb, h_scr[lo:hi, :],
                                                      preferred_element_type=F32).astype(BF16)
                gcol = jnp.sum(jnp.where(pm, gate_row[:, lo:hi], 0.0), axis=1, keepdims=True)
                gcol_scr[pl.ds(r0, nrows), :] = jnp.broadcast_to(gcol, (nrows, LANES))

            for_region(r0, nrows, compact)
            ye_scr[pl.ds(r0, nrows), :] = jnp.zeros((nrows, ye_scr.shape[1]), F32)

        def body(rc, carry):
            compact_chunk(pl.multiple_of(rc * rc_rows, rc_rows), rc_rows)
            return carry

        lax.fori_loop(0, n_rc, body, 0)

        @pl.when(has_tail)
        def _():
            compact_chunk(tail_r0, MOE_TAIL)

    def ffn_chunk(r0, nrows):
        xe = xe_scr[pl.ds(r0, nrows), :]
        a = jnp.dot(xe, wg_ref[0], preferred_element_type=F32)
        u = jnp.dot(xe, wu_ref[0], preferred_element_type=F32)
        act = (_silu(a) * u * gcol_scr[pl.ds(r0, nrows), 0:1]).astype(BF16)
        ye_scr[pl.ds(r0, nrows), :] += jnp.dot(act, wd_ref[0], preferred_element_type=F32)

    def ffn_body(rc, carry):
        ffn_chunk(pl.multiple_of(rc * rc_rows, rc_rows), rc_rows)
        return carry

    lax.fori_loop(0, n_rc, ffn_body, 0)

    @pl.when(has_tail)
    def _():
        ffn_chunk(tail_r0, MOE_TAIL)

    @pl.when(f == pl.num_programs(3) - 1)
    def _scatter():
        def scatter_chunk(r0, nrows):
            ye = ye_scr[pl.ds(r0, nrows), :].astype(BF16)

            def scatter(lo, hi):
                want = (r0 + 1 + lax.broadcasted_iota(I32, (hi - lo, nrows), 1)).astype(F32)
                pt = jnp.where(rcol_scr[lo:hi, 0:1] == want, 1.0, 0.0).astype(BF16)
                o_ref[0, lo:hi, :] += jnp.dot(pt, ye, preferred_element_type=F32)

            for_region(r0, nrows, scatter)

        def body(rc, carry):
            scatter_chunk(pl.multiple_of(rc * rc_rows, rc_rows), rc_rows)
            return carry

        lax.fori_loop(0, n_rc, body, 0)

        @pl.when(has_tail)
        def _():
            scatter_chunk(tail_r0, MOE_TAIL)

    @pl.when((e == pl.num_programs(2) - 1) & (f == pl.num_programs(3) - 1))
    def _residual():
        for jb in range(ntb):
            rows = slice(jb * tb, (jb + 1) * tb)
            y = x_ref[0, rows, :] + gate_ref[0] * o_ref[0, rows, :]
            if final_norm:
                var = jnp.mean(y * y, axis=-1, keepdims=True)
                y = (y * lax.rsqrt(var + EPS)) * gfin_ref[...]
            o_ref[0, rows, :] = y


def _moe(x, sh, sc, gate, g, w_router, wg, wu, wd, tm, tf, g_final=None):
    bn, s, d = x.shape
    ne, _, fdim = wg.shape
    assert ne == N_EXPERTS and tm % MOE_RC == 0 and tm % MOE_TB == 0 and (tm // 2) % LANES == 0
    wr = jnp.zeros((d, LANES), F32).at[:, :ne].set(w_router)
    final_norm = g_final is not None
    gfin = g_final if final_norm else jnp.ones((1, d), F32)
    tok_in = pl.BlockSpec((1, tm, d), lambda b, t, e, f: (b, t, 0), pipeline_mode=pl.Buffered(1))
    tok_out = pl.BlockSpec((1, tm, d), lambda b, t, e, f: (b, t, 0), pipeline_mode=pl.Buffered(1))
    vec = pl.BlockSpec((1, 1, d), lambda b, t, e, f: (b, 0, 0))
    return pl.pallas_call(
        functools.partial(_moe_kernel, final_norm),
        grid=(bn, s // tm, ne, fdim // tf),
        in_specs=[tok_in, vec, vec, vec,
                  pl.BlockSpec((1, d), lambda b, t, e, f: (0, 0)),
                  pl.BlockSpec((1, d), lambda b, t, e, f: (0, 0)),
                  pl.BlockSpec((d, LANES), lambda b, t, e, f: (0, 0)),
                  pl.BlockSpec((1, d, tf), lambda b, t, e, f: (e, 0, f)),
                  pl.BlockSpec((1, d, tf), lambda b, t, e, f: (e, 0, f)),
                  pl.BlockSpec((1, tf, d), lambda b, t, e, f: (e, f, 0))],
        out_specs=tok_out,
        out_shape=jax.ShapeDtypeStruct((bn, s, d), F32),
        scratch_shapes=[pltpu.VMEM((tm, d), BF16),
                        pltpu.VMEM((tm, d), BF16),
                        pltpu.VMEM((tm, d), F32),
                        pltpu.VMEM((tm, LANES), F32),
                        pltpu.VMEM((ne, tm), F32),
                        pltpu.VMEM((ne, tm), F32),
                        pltpu.VMEM((tm, LANES), F32),
                        pltpu.VMEM((tm, LANES), F32),
                        pltpu.SMEM((2 * ne,), I32)],
        compiler_params=pltpu.CompilerParams(
            dimension_semantics=("parallel", "parallel", "arbitrary", "arbitrary"),
            vmem_limit_bytes=56 * 1024 * 1024),
    )(x, sh, sc, gate, g, gfin, wr, wg, wu, wd)


def _final_norm_kernel(x_ref, g_ref, o_ref):
    x = x_ref[0]
    var = jnp.mean(x * x, axis=-1, keepdims=True)
    o_ref[0] = (x * lax.rsqrt(var + EPS)) * g_ref[...]


def _final_norm(x, g, tm):
    bn, s, d = x.shape
    tok = pl.BlockSpec((1, tm, d), lambda b, t: (b, t, 0))
    return pl.pallas_call(
        _final_norm_kernel,
        grid=(bn, s // tm),
        in_specs=[tok, pl.BlockSpec((1, d), lambda b, t: (0, 0))],
        out_specs=tok,
        out_shape=jax.ShapeDtypeStruct((bn, s, d), F32),
        compiler_params=_cparams("parallel", "parallel"),
    )(x, g)


def _token_tile(s, want):
    tm = min(want, s)
    assert s % tm == 0 and tm % QB == 0
    return tm


def kernel(x, c, positions, w_ada, b_ada, g_norm_mix, g_norm_ffn, w_in, w_out, hgrn_lb_logits, hgrn_out_norm, gmlp_vnorm_g, gmlp_vnorm_b, gmlp_w_s, gmlp_b_s, ffn_w_gate, ffn_w_up, ffn_w_down, moe_w_router, moe_w_gate, moe_w_up, moe_w_down, g_final):
    bn, s, d = x.shape
    depth = w_in.shape[0]
    assert s % QB == 0 and d == sum(IN_WIDTHS[:1]) + B_DIM + C_DIM
    tm_proj = _token_tile(s, 512)
    tm_ffn = _token_tile(s, 1024)

    p_lb = jax.nn.softmax(hgrn_lb_logits.astype(F32), axis=0)
    cum = jnp.cumsum(p_lb, axis=0)
    lower_bounds = cum - cum[0:1]

    mod = _ada_mod(c, w_ada, b_ada)
    cosn, sinn, cost, sint = _rope_tables(positions)

    for l in range(depth):
        sh1, sc1, g1, sh2, sc2, g2 = [mod[l, :, i * d:(i + 1) * d].reshape(bn, 1, d) for i in range(6)]
        w_n, w_t = _prep_in_weights(w_in[l])
        (kz, ik3, bq, bf, bi, bg, c_out, qt, iq3t, vt4, iwt, bit) = _inproj(
            x, sh1, sc1, g_norm_mix[l].reshape(1, d), w_n, w_t, cosn, sinn, cost, sint,
            gmlp_vnorm_g[l].reshape(1, C_DIM), gmlp_vnorm_b[l].reshape(1, C_DIM), gmlp_w_s[l], gmlp_b_s[l],
            tm_proj)
        a_out = _dsa(ik3, kz, vt4, iq3t, iwt, qt)
        b_out = _hgrn(bq, bf, bi, bit, bg, lower_bounds[l].reshape(1, B_DIM),
                      jnp.tile(hgrn_out_norm[l], LANES // B_DV).reshape(1, LANES))
        gf = g_norm_ffn[l].reshape(1, d)
        if l % 2 == 0:
            i = l // 2
            x = _ffn(x, a_out, b_out, c_out, w_out[l], g1, sh2, sc2, g2, gf, ffn_w_gate[i].astype(BF16),
                     ffn_w_up[i].astype(BF16), ffn_w_down[i].astype(BF16), tm_ffn, 256)
        else:
            i = l // 2
            x = _outproj(x, a_out, b_out, c_out, w_out[l], g1, tm_proj)
            fused_final = l == depth - 1
            x = _moe(x, sh2, sc2, g2, gf, moe_w_router[i], moe_w_gate[i].astype(BF16),
                     moe_w_up[i].astype(BF16), moe_w_down[i].astype(BF16), _token_tile(s, 2048), 896,
                     g_final.reshape(1, d) if fused_final else None)
    if depth % 2 == 0:
        return x
    return _final_norm(x, g_final.reshape(1, d), tm_proj)
```

```python
import functools

import numpy as np
import jax
import jax.numpy as jnp
from jax import lax
from jax.experimental import pallas as pl
from jax.experimental.pallas import tpu as pltpu

F32 = jnp.float32
BF16 = jnp.bfloat16
I32 = jnp.int32
I16 = jnp.int16

HEAD_DIM = 64
A_HEADS = 6
A_DIM = A_HEADS * HEAD_DIM
IDX_HEADS = 4
IDX_DIM = 64
TOPK_MAX = 256
B_HEADS = 6
B_DK = 64
B_DV = 64
B_DIM = B_HEADS * B_DV
HGRN_CHUNK = 64
C_GROUPS = 4
C_GROUP_DIM = 64
C_DIM = C_GROUPS * C_GROUP_DIM
C_CHUNK = 128
ROPE_THETA = 10000.0
N_EXPERTS = 8
EPS = 1e-6
IN_WIDTHS = (A_DIM, HEAD_DIM, HEAD_DIM, IDX_HEADS * IDX_DIM, IDX_DIM, IDX_HEADS,
             B_DIM, B_DIM, B_DIM, B_DIM, 2 * C_DIM)

LANES = 128
KC = 128
QB = 2 * KC
PV_ROWS = HEAD_DIM + 16
VMEM_LIMIT = 48 * 1024 * 1024
HGRN_FAST_SPAN = 80.0
HGRN_SUB = 32
HGRN_GROUP = 4
INT_MIN = -2 ** 31

NT_DIMS = (((1,), (1,)), ((), ()))
TN_DIMS = (((0,), (0,)), ((), ()))


def _cparams(*sem):
    return pltpu.CompilerParams(dimension_semantics=sem, vmem_limit_bytes=VMEM_LIMIT)


def _split2(x):
    hi = x.astype(BF16)
    lo = (x - hi.astype(F32)).astype(BF16)
    return hi, lo


def _split3(x):
    hi = x.astype(BF16)
    r1 = x - hi.astype(F32)
    mid = r1.astype(BF16)
    lo = (r1 - mid.astype(F32)).astype(BF16)
    return hi, mid, lo


def _silu(x):
    return x * (1.0 / (1.0 + jnp.exp(-x)))


def _modulated_rmsnorm(x, g, sc, sh):
    var = jnp.mean(x * x, axis=-1, keepdims=True)
    y = x * lax.rsqrt(var + EPS)
    return (y * g) * (1.0 + sc) + sh


def _ada_kernel(c_ref, w_ref, b_ref, o_ref):
    cond = _silu(c_ref[...])
    o_ref[0] = jnp.dot(cond, w_ref[0], precision=lax.Precision.HIGHEST,
                       preferred_element_type=F32) + b_ref[0]


def _ada_mod(c, w_ada, b_ada):
    depth, d, d6 = w_ada.shape
    bn = c.shape[0]
    tn = 1536
    return pl.pallas_call(
        _ada_kernel,
        grid=(depth, d6 // tn),
        in_specs=[pl.BlockSpec((bn, d), lambda l, n: (0, 0)),
                  pl.BlockSpec((1, d, tn), lambda l, n: (l, 0, n)),
                  pl.BlockSpec((1, 1, tn), lambda l, n: (l, 0, n))],
        out_specs=pl.BlockSpec((1, bn, tn), lambda l, n: (l, 0, n)),
        out_shape=jax.ShapeDtypeStruct((depth, bn, d6), F32),
        compiler_params=_cparams("parallel", "parallel"),
    )(c, w_ada, b_ada.reshape(depth, 1, d6))


def _rope_kernel(posn_ref, post_ref, invn_ref, signn_ref, invt_ref,
                 cosn_ref, sinn_ref, cost_ref, sint_ref):
    ang_n = posn_ref[0].astype(F32) * invn_ref[...]
    cosn_ref[0] = jnp.cos(ang_n)
    sinn_ref[0] = jnp.sin(ang_n) * signn_ref[...]
    ang_t = invt_ref[...] * post_ref[0].astype(F32)
    cost_ref[0] = jnp.cos(ang_t)
    sint_ref[0] = jnp.sin(ang_t)


def _rope_tables(positions):
    bn, s = positions.shape
    half = HEAD_DIM // 2
    inv = ROPE_THETA ** (-jnp.arange(0, HEAD_DIM, 2, dtype=F32) / HEAD_DIM)
    inv_n = jnp.tile(inv, LANES // half).reshape(1, LANES)
    sign_n = jnp.tile(jnp.concatenate([-jnp.ones((half,), F32), jnp.ones((half,), F32)]),
                      LANES // HEAD_DIM).reshape(1, LANES)
    inv_t = inv.reshape(half, 1)
    full = lambda shape: pl.BlockSpec(shape, lambda b: (0,) * len(shape))
    return pl.pallas_call(
        _rope_kernel,
        grid=(bn,),
        in_specs=[pl.BlockSpec((1, s, 1), lambda b: (b, 0, 0)),
                  pl.BlockSpec((1, 1, s), lambda b: (b, 0, 0)),
                  full((1, LANES)), full((1, LANES)), full((half, 1))],
        out_specs=[pl.BlockSpec((1, s, LANES), lambda b: (b, 0, 0)),
                   pl.BlockSpec((1, s, LANES), lambda b: (b, 0, 0)),
                   pl.BlockSpec((1, half, s), lambda b: (b, 0, 0)),
                   pl.BlockSpec((1, half, s), lambda b: (b, 0, 0))],
        out_shape=[jax.ShapeDtypeStruct((bn, s, LANES), F32),
                   jax.ShapeDtypeStruct((bn, s, LANES), F32),
                   jax.ShapeDtypeStruct((bn, half, s), F32),
                   jax.ShapeDtypeStruct((bn, half, s), F32)],
        compiler_params=_cparams("parallel"),
    )(positions.reshape(bn, s, 1), positions.reshape(bn, 1, s), inv_n, sign_n, inv_t)


N_KI = 0
N_BQ = 2 * LANES
N_BF = N_BQ + B_DIM
N_BI = N_BF + B_DIM
N_BG = N_BI + B_DIM
N_CUV = N_BG + B_DIM
N_COLS = N_CUV + 2 * C_DIM
T_Q = 0
T_IQ = A_DIM
T_V = T_IQ + IDX_HEADS * IDX_DIM
T_IW = T_V + HEAD_DIM
T_BI = T_IW + 8
T_ROWS = T_BI + B_DIM


def _prep_in_weights(w_in_l):
    offs = np.concatenate([[0], np.cumsum(IN_WIDTHS)])
    sl = lambda i: w_in_l[:, int(offs[i]):int(offs[i + 1])]
    aq, ak, av, iq, ik, iw, bq, bf, bi, bg, cuv = [sl(i) for i in range(11)]
    half = HEAD_DIM // 2
    rot = lambda w: jnp.concatenate([w[:, half:], w[:, :half]], axis=1)
    w_n = jnp.concatenate([ak, ik, rot(ak), rot(ik), bq, bf, bi, bg, cuv], axis=1)
    w_t = jnp.concatenate([aq, iq, av, iw, jnp.zeros((w_in_l.shape[0], 4), w_in_l.dtype), bi], axis=1).T
    return w_n.astype(BF16), w_t.astype(BF16)


def _inproj_kernel(x_ref, sh_ref, sc_ref, g_ref, wn_ref, wt_ref, cosn_ref, sinn_ref, cost_ref, sint_ref,
                   vg_ref, vb_ref, ws_ref, cbias_ref,
                   kz_ref, ik3_ref, bq_ref, bf_ref, bi_ref, bg_ref, c_ref,
                   qt_ref, iq3t_ref, vt_ref, iwt_ref, bit_ref):
    tm = x_ref.shape[1]
    half = HEAD_DIM // 2
    h = _modulated_rmsnorm(x_ref[0], g_ref[...], sc_ref[0], sh_ref[0]).astype(BF16)

    cosn = cosn_ref[0]
    sinn = sinn_ref[0]
    zk = jnp.dot(h, wn_ref[:, N_KI:N_KI + 2 * LANES], preferred_element_type=F32)
    ki = zk[:, :LANES] * cosn + zk[:, LANES:] * sinn
    lane = lax.broadcasted_iota(I32, (tm, LANES), 1)
    kz_ref[0] = jnp.where(lane < HEAD_DIM, ki, 0.0).astype(BF16)
    ik2 = jnp.where(lane < IDX_DIM, pltpu.roll(ki, HEAD_DIM, axis=1), ki)
    hi, lo = _split2(ik2)
    ik3_ref[0, :, :LANES] = jnp.where(lane < IDX_DIM, hi, lo)
    ik3_ref[0, :, LANES:] = hi
    bq_ref[0] = jnp.dot(h, wn_ref[:, N_BQ:N_BF], preferred_element_type=F32).astype(BF16)
    bf_ref[0] = jnp.dot(h, wn_ref[:, N_BF:N_BI], preferred_element_type=F32)
    bi_ref[0] = jnp.dot(h, wn_ref[:, N_BI:N_BG], preferred_element_type=F32).astype(BF16)
    bg_ref[0] = jnp.dot(h, wn_ref[:, N_BG:N_CUV], preferred_element_type=F32).astype(BF16)
    cuv = jnp.dot(h, wn_ref[:, N_CUV:N_COLS], preferred_element_type=F32)
    c_ref[0] = _gmlp_mix(cuv, vg_ref[...], vb_ref[...], ws_ref, cbias_ref[...]).astype(BF16)

    zt = lax.dot_general(wt_ref[...], h, NT_DIMS, preferred_element_type=F32)
    cost = cost_ref[0]
    sint = sint_ref[0]
    qscale = HEAD_DIM ** -0.5
    for hh in range(A_HEADS):
        r0 = T_Q + hh * HEAD_DIM
        x1 = zt[r0:r0 + half]
        x2 = zt[r0 + half:r0 + HEAD_DIM]
        qt_ref[0, hh * HEAD_DIM:hh * HEAD_DIM + half] = ((x1 * cost - x2 * sint) * qscale).astype(BF16)
        qt_ref[0, hh * HEAD_DIM + half:(hh + 1) * HEAD_DIM] = ((x2 * cost + x1 * sint) * qscale).astype(BF16)
    iscale = IDX_DIM ** -0.5
    zero = jnp.zeros((IDX_DIM, tm), BF16)
    for hh in range(IDX_HEADS):
        r0 = T_IQ + hh * IDX_DIM
        x1 = zt[r0:r0 + half]
        x2 = zt[r0 + half:r0 + IDX_DIM]
        y = jnp.concatenate([(x1 * cost - x2 * sint) * iscale, (x2 * cost + x1 * sint) * iscale], axis=0)
        hi, lo = _split2(y)
        iq3t_ref[0, hh, 0 * IDX_DIM:1 * IDX_DIM] = hi
        iq3t_ref[0, hh, 1 * IDX_DIM:2 * IDX_DIM] = hi
        iq3t_ref[0, hh, 2 * IDX_DIM:3 * IDX_DIM] = lo
        iq3t_ref[0, hh, 3 * IDX_DIM:4 * IDX_DIM] = zero
    vt = zt[T_V:T_V + HEAD_DIM].astype(BF16)
    for i in range(tm // KC):
        vt_ref[0, i] = vt[:, i * KC:(i + 1) * KC]
    iwt_ref[0] = zt[T_IW:T_IW + 8] * (IDX_HEADS ** -0.5)
    bit = zt[T_BI:T_BI + B_DIM].astype(BF16)
    for i in range(tm // LANES):
        bit_ref[0, i] = bit[:, i * LANES:(i + 1) * LANES]


def _inproj(x, sh, sc, g, w_n, w_t, cosn, sinn, cost, sint, vg, vb, ws, bs, tm):
    bn, s, d = x.shape
    assert tm % C_CHUNK == 0
    nt = s // tm
    half = HEAD_DIM // 2
    cbias = jnp.repeat(bs.T, C_GROUP_DIM, axis=1)
    tok = lambda w: pl.BlockSpec((1, tm, w), lambda b, t: (b, t, 0))
    vec = pl.BlockSpec((1, 1, d), lambda b, t: (b, 0, 0))
    full2 = lambda a: pl.BlockSpec(a.shape, lambda b, t: (0, 0))
    out_shapes = [
        jax.ShapeDtypeStruct((bn, s, LANES), BF16),
        jax.ShapeDtypeStruct((bn, s, 2 * LANES), BF16),
        jax.ShapeDtypeStruct((bn, s, B_DIM), BF16),
        jax.ShapeDtypeStruct((bn, s, B_DIM), F32),
        jax.ShapeDtypeStruct((bn, s, B_DIM), BF16),
        jax.ShapeDtypeStruct((bn, s, B_DIM), BF16),
        jax.ShapeDtypeStruct((bn, s, C_DIM), BF16),
        jax.ShapeDtypeStruct((bn, A_DIM, s), BF16),
        jax.ShapeDtypeStruct((bn, IDX_HEADS, 4 * IDX_DIM, s), BF16),
        jax.ShapeDtypeStruct((bn, s // KC, HEAD_DIM, KC), BF16),
        jax.ShapeDtypeStruct((bn, 8, s), F32),
        jax.ShapeDtypeStruct((bn, s // LANES, B_DIM, LANES), BF16),
    ]
    out_specs = [
        tok(LANES), tok(2 * LANES), tok(B_DIM), tok(B_DIM), tok(B_DIM), tok(B_DIM), tok(C_DIM),
        pl.BlockSpec((1, A_DIM, tm), lambda b, t: (b, 0, t)),
        pl.BlockSpec((1, IDX_HEADS, 4 * IDX_DIM, tm), lambda b, t: (b, 0, 0, t)),
        pl.BlockSpec((1, tm // KC, HEAD_DIM, KC), lambda b, t: (b, t, 0, 0)),
        pl.BlockSpec((1, 8, tm), lambda b, t: (b, 0, t)),
        pl.BlockSpec((1, tm // LANES, B_DIM, LANES), lambda b, t: (b, t, 0, 0)),
    ]
    return pl.pallas_call(
        _inproj_kernel,
        grid=(bn, nt),
        in_specs=[tok(d), vec, vec, full2(g), full2(w_n), full2(w_t),
                  tok(LANES), tok(LANES),
                  pl.BlockSpec((1, half, tm), lambda b, t: (b, 0, t)),
                  pl.BlockSpec((1, half, tm), lambda b, t: (b, 0, t)),
                  full2(vg), full2(vb), pl.BlockSpec(ws.shape, lambda b, t: (0, 0, 0)), full2(cbias)],
        out_specs=out_specs,
        out_shape=out_shapes,
        compiler_params=_cparams("parallel", "parallel"),
    )(x, sh, sc, g, w_n, w_t, cosn, sinn, cost, sint, vg, vb, ws, cbias)


def _dsa_kernel(n_top, ik3_ref, kz_ref, vt_ref, iq3t_ref, iwt_ref, qt_ref, out_ref,
                key_scr, hi_scr, lo_scr, lg_scr, acc_scr):
    j = pl.program_id(1)
    npair = j + 1
    pair = 2 * KC
    row = lax.broadcasted_iota(I32, (KC, QB), 0)
    col = lax.broadcasted_iota(I32, (KC, QB), 1)
    int_min = jnp.int32(INT_MIN)
    i16_min = jnp.int16(-2 ** 15)

    def score_pair(p, diagonal):
        for u in range(2):
            ks = pl.multiple_of(p * pair + u * KC, KC)
            ikc = ik3_ref[0, pl.ds(ks, KC), :]
            sc = jnp.zeros((KC, QB), F32)
            for hh in range(IDX_HEADS):
                rel = jnp.dot(ikc, iq3t_ref[0, hh], preferred_element_type=F32)
                sc = sc + jnp.maximum(rel, 0.0) * iwt_ref[0, hh:hh + 1, :]
            sc = jnp.where(sc == 0.0, 0.0, sc)
            bits = pltpu.bitcast(sc, I32)
            key = bits ^ ((bits >> 31) & jnp.int32(0x7FFFFFFF))
            if diagonal:
                key = jnp.where((u * KC + row) <= col, key, int_min)
            key_scr[pl.ds(ks, KC), :] = key
            hi_scr[pl.ds(ks, KC), :] = (key >> 16).astype(I16)
            lo_scr[pl.ds(ks, KC), :] = ((key & jnp.int32(0xFFFF)) - 32768).astype(I16)

    def score_body(p, carry):
        score_pair(p, False)
        return carry

    lax.fori_loop(0, j, score_body, 0)
    score_pair(j, True)

    def count16(ref, pred_fn):
        def body(p, acc):
            ks = pl.multiple_of(p * pair, pair)
            m = jnp.where(pred_fn(ref[pl.ds(ks, pair), :]), jnp.int16(1), jnp.int16(0))
            parts = [m[16 * i:16 * (i + 1)] for i in range(pair // 16)]
            while len(parts) > 1:
                parts = [parts[i] + parts[i + 1] for i in range(0, len(parts), 2)]
            return acc + parts[0]
        acc = lax.fori_loop(0, npair, body, jnp.zeros((16, QB), I16))
        return jnp.sum(acc.astype(I32), axis=0, keepdims=True)

    def bisect16(ref, k_needed):
        def bit_body(i, t_u):
            cand = t_u | jnp.left_shift(jnp.int32(1), 15 - i)
            cand16 = (cand - 32768).astype(I16)
            cnt = count16(ref, lambda x: x >= cand16)
            return jnp.where(cnt >= k_needed, cand, t_u)
        return lax.fori_loop(0, 16, bit_body, jnp.zeros((1, QB), I32))

    t_hi = bisect16(hi_scr, n_top)
    t_hi16 = (t_hi - 32768).astype(I16)
    n_hi_gt = count16(hi_scr, lambda x: x > t_hi16)

    def bucket_body(p, carry):
        ks = pl.multiple_of(p * pair, pair)
        in_bucket = hi_scr[pl.ds(ks, pair), :] == t_hi16
        lo_scr[pl.ds(ks, pair), :] = jnp.where(in_bucket, lo_scr[pl.ds(ks, pair), :], i16_min)
        return carry

    lax.fori_loop(0, npair, bucket_body, 0)
    t_lo = bisect16(lo_scr, n_top - n_hi_gt)
    t_lo16 = (t_lo - 32768).astype(I16)
    n_gt = n_hi_gt + count16(lo_scr, lambda x: x > t_lo16)
    thr = jnp.left_shift(t_hi - 32768, 16) | t_lo
    n_tie = (n_top - n_gt).astype(F32)
    thr_valid = jnp.where(thr > int_min, 1.0, 0.0)

    tr = lax.broadcasted_iota(I32, (KC, KC), 0)
    tc = lax.broadcasted_iota(I32, (KC, KC), 1)
    ltri = jnp.where(tr >= tc, 1.0, 0.0).astype(BF16)
    zpad = jnp.zeros((HEAD_DIM, QB), BF16)
    qpad = [jnp.concatenate([qt_ref[0, hh * HEAD_DIM:(hh + 1) * HEAD_DIM, :], zpad], axis=0)
            for hh in range(A_HEADS)]
    neg_inf = jnp.float32(-jnp.inf)

    def pass_a(p, carry):
        tie_cnt, ms = carry
        ms = list(ms)
        for u in range(2):
            ks = pl.multiple_of(p * pair + u * KC, KC)
            kc = key_scr[pl.ds(ks, KC), :]
            eqf = jnp.where(kc == thr, thr_valid, 0.0)
            pref = jnp.dot(ltri, eqf.astype(BF16), preferred_element_type=F32)
            keep_tie = eqf * jnp.where((tie_cnt + pref) <= n_tie, 1.0, 0.0)
            sel = jnp.where(kc > thr, 1.0, keep_tie) > 0.5
            tie_cnt = tie_cnt + pref[KC - 1:KC, :]
            kzc = kz_ref[0, pl.ds(ks, KC), :]
            for hh in range(A_HEADS):
                lt = jnp.dot(kzc, qpad[hh], preferred_element_type=F32)
                lt = jnp.where(sel, lt, neg_inf)
                lg_scr[hh, pl.ds(ks, KC), :] = lt
                ms[hh] = jnp.maximum(ms[hh], jnp.max(lt, axis=0, keepdims=True))
        return tie_cnt, tuple(ms)

    init_m = tuple(jnp.full((1, QB), neg_inf, F32) for _ in range(A_HEADS))
    _, ms = lax.fori_loop(0, npair, pass_a, (jnp.zeros((1, QB), F32), init_m))

    acc_scr[...] = jnp.zeros_like(acc_scr)
    ones_rows = jnp.ones((PV_ROWS - HEAD_DIM, pair), BF16)

    def pass_b(p, carry):
        ks = pl.multiple_of(p * pair, pair)
        vt2 = jnp.concatenate([vt_ref[0, 2 * p], vt_ref[0, 2 * p + 1]], axis=1)
        vt2 = jnp.concatenate([vt2, ones_rows], axis=0)
        for hh in range(A_HEADS):
            pr = jnp.exp(lg_scr[hh, pl.ds(ks, pair), :] - ms[hh])
            acc_scr[hh * PV_ROWS:(hh + 1) * PV_ROWS, :] += jnp.dot(
                vt2, pr.astype(BF16), preferred_element_type=F32)
        return carry

    lax.fori_loop(0, npair, pass_b, 0)
    o_t = jnp.concatenate(
        [acc_scr[hh * PV_ROWS:hh * PV_ROWS + HEAD_DIM, :]
         / acc_scr[hh * PV_ROWS + HEAD_DIM:hh * PV_ROWS + HEAD_DIM + 1, :] for hh in range(A_HEADS)], axis=0)
    out_ref[0] = o_t.T.astype(BF16)


def _dsa(ik3, kz, vt4, iq3t, iwt, qt):
    bn, s, _ = kz.shape
    assert s % QB == 0
    n_top = min(TOPK_MAX, s // 4)
    nq = s // QB
    return pl.pallas_call(
        functools.partial(_dsa_kernel, n_top),
        grid=(bn, nq),
        in_specs=[pl.BlockSpec((1, s, 2 * LANES), lambda b, q: (b, 0, 0)),
                  pl.BlockSpec((1, s, LANES), lambda b, q: (b, 0, 0)),
                  pl.BlockSpec((1, s // KC, HEAD_DIM, KC), lambda b, q: (b, 0, 0, 0)),
                  pl.BlockSpec((1, IDX_HEADS, 4 * IDX_DIM, QB), lambda b, q: (b, 0, 0, q)),
                  pl.BlockSpec((1, 8, QB), lambda b, q: (b, 0, q)),
                  pl.BlockSpec((1, A_DIM, QB), lambda b, q: (b, 0, q))],
        out_specs=pl.BlockSpec((1, QB, A_DIM), lambda b, q: (b, q, 0)),
        out_shape=jax.ShapeDtypeStruct((bn, s, A_DIM), BF16),
        scratch_shapes=[pltpu.VMEM((s, QB), I32), pltpu.VMEM((s, QB), I16), pltpu.VMEM((s, QB), I16),
                        pltpu.VMEM((A_HEADS, s, QB), F32), pltpu.VMEM((A_HEADS * PV_ROWS, QB), F32)],
        compiler_params=_cparams("parallel", "arbitrary"),
    )(ik3, kz, vt4, iq3t, iwt, qt)


def _hgrn_kernel(q_ref, f_ref, i_ref, it_ref, g_ref, lb_ref, gn_ref, o_ref, st_scr, oi_scr):
    s = q_ref.shape[1]
    ch = HGRN_CHUNK
    sb = HGRN_SUB
    nsb = ch // sb
    rows = HGRN_GROUP * ch
    wins_per_group = rows // LANES
    lane = lax.broadcasted_iota(I32, (ch, LANES), 1)
    head0 = lane < B_DK
    lr = lax.broadcasted_iota(I32, (rows, 3 * rows), 0)
    lc = lax.broadcasted_iota(I32, (rows, 3 * rows), 1) % rows
    lcum = jnp.where(((lr // ch) == (lc // ch)) & (lc <= lr), 1.0, 0.0).astype(BF16)
    att_rows = HGRN_GROUP * 2 * ch
    att_cols = HGRN_GROUP * nsb * ch
    ar = lax.broadcasted_iota(I32, (att_rows, att_cols), 0)
    ac = lax.broadcasted_iota(I32, (att_rows, att_cols), 1)
    at = ar % ch
    att_mask = (((ar // (2 * ch)) == (ac // (nsb * ch))) & ((at // sb) == ((ac // ch) % nsb))
                & ((ac % ch) <= at))
    bd_r = lax.broadcasted_iota(I32, (LANES, LANES), 0)
    bd_c = lax.broadcasted_iota(I32, (LANES, LANES), 1)
    same_head = (bd_r < B_DK) == (bd_c < B_DK)
    ones_bd = jnp.where(same_head, 1.0, 0.0).astype(BF16)
    ones_bd2 = jnp.concatenate([ones_bd, ones_bd], axis=0)
    s_iota = lax.broadcasted_iota(I32, (ch, LANES), 0)

    lb = lb_ref[...]
    log_lb = jnp.log(lb)
    log_1mlb = jnp.log1p(-lb)
    gn = gn_ref[...]
    st_scr[...] = jnp.zeros_like(st_scr)

    def group_body(it, carry):
        t0 = pl.multiple_of(it * rows, rows)
        z = f_ref[0, pl.ds(t0, rows), :]
        q = q_ref[0, pl.ds(t0, rows), :].astype(F32)
        v = i_ref[0, pl.ds(t0, rows), :]
        softplus_tail = jnp.log1p(jnp.exp(-jnp.abs(z)))
        log_sig = -(jnp.maximum(-z, 0.0) + softplus_tail)
        x2 = log_1mlb + log_sig
        amax = jnp.maximum(log_lb, x2)
        log_f = amax + jnp.log1p(jnp.exp(-jnp.abs(log_lb - x2)))
        kk = (1.0 - lb) * jnp.exp(-(jnp.maximum(z, 0.0) + softplus_tail))
        f_hi, f_mid, f_lo = _split3(log_f)
        b = jnp.dot(lcum, jnp.concatenate([f_hi, f_mid, f_lo], axis=0), preferred_element_type=F32)
        ref_parts = []
        for c in range(HGRN_GROUP):
            ref_parts.append(jnp.zeros((sb, LANES), F32))
            for i in range(1, nsb):
                r = c * ch + i * sb
                ref_parts.append(jnp.broadcast_to(b[r - 1:r], (sb, LANES)))
        ref = jnp.concatenate(ref_parts, axis=0)
        q_loc = q * jnp.exp(b - ref)
        q_chk = (q * jnp.exp(b)).astype(BF16)
        span = ref - b
        span_max = jnp.max(span)

        k_parts, q_parts, v_parts, kh_cols, decay = [], [], [], [], []
        for c in range(HGRN_GROUP):
            lo_r = c * ch
            b_c = b[lo_r:lo_r + ch]
            kk_c = kk[lo_r:lo_r + ch]
            b_last = b_c[ch - 1:ch]
            for i in range(nsb):
                r = lo_r + i * sb
                k_parts.append(kk_c * jnp.exp(jnp.minimum(ref[r:r + 1] - b_c, HGRN_FAST_SPAN)))
            q_c = q_loc[lo_r:lo_r + ch]
            q_parts += [jnp.where(head0, q_c, 0.0), jnp.where(head0, 0.0, q_c)]
            v_parts += [v[lo_r:lo_r + ch]] * nsb
            kh = (kk_c * jnp.exp(b_last - b_c)).astype(BF16)
            kh_cols.append(jnp.concatenate(
                ([jnp.zeros((lo_r, LANES), BF16)] if lo_r else []) + [kh]
                + ([jnp.zeros((rows - lo_r - ch, LANES), BF16)] if rows - lo_r - ch else []), axis=0))
            decay.append(jnp.exp(b_last))
        att = lax.dot_general(jnp.concatenate(q_parts, axis=0).astype(BF16),
                              jnp.concatenate(k_parts, axis=0).astype(BF16), NT_DIMS,
                              preferred_element_type=F32)
        att = jnp.where(att_mask, att, 0.0).astype(BF16)
        o2 = jnp.dot(att, jnp.concatenate(v_parts, axis=0), preferred_element_type=F32)
        for c in range(HGRN_GROUP):
            r = c * 2 * ch
            oi_scr[c * ch:(c + 1) * ch, :] = jnp.where(head0, o2[r:r + ch], o2[r + ch:r + 2 * ch])

        it_win = jnp.concatenate([it_ref[0, wins_per_group * it + w] for w in range(wins_per_group)], axis=1)
        upd = jnp.dot(it_win, jnp.concatenate(kh_cols, axis=1), preferred_element_type=F32)
        st = st_scr[...]
        states = []
        for c in range(HGRN_GROUP):
            states.append(st.astype(BF16))
            st = st * decay[c] + jnp.where(same_head, upd[:, c * LANES:(c + 1) * LANES], 0.0)
        st_scr[...] = st
        oi_all = lax.dot_general(q_chk, jnp.concatenate(states, axis=0), NT_DIMS,
                                 preferred_element_type=F32)
        o_inter = [oi_all[c * ch:(c + 1) * ch, c * LANES:(c + 1) * LANES] for c in range(HGRN_GROUP)]

        @pl.when(span_max > HGRN_FAST_SPAN)
        def _():
            for c in range(HGRN_GROUP):
                lo_r = c * ch

                @pl.when(jnp.max(span[lo_r:lo_r + ch]) > HGRN_FAST_SPAN)
                def _():
                    b_c = b[lo_r:lo_r + ch]
                    q_c = q[lo_r:lo_r + ch]
                    kk_c = kk[lo_r:lo_r + ch]
                    vf = v[lo_r:lo_r + ch].astype(F32)

                    def t_body(t, carry2):
                        onehot = jnp.where(s_iota == t, 1.0, 0.0)
                        b_t = jnp.sum(onehot * b_c, axis=0, keepdims=True)
                        q_t = jnp.sum(onehot * q_c, axis=0, keepdims=True)
                        dec = jnp.exp(jnp.where(s_iota <= t, b_t - b_c, -jnp.inf))
                        w = q_t * kk_c * dec
                        w0 = jnp.sum(jnp.where(head0, w, 0.0), axis=1, keepdims=True)
                        w1 = jnp.sum(jnp.where(head0, 0.0, w), axis=1, keepdims=True)
                        a_col = jnp.where(head0, w0, w1)
                        oi_scr[pl.ds(lo_r + t, 1), :] = jnp.sum(a_col * vf, axis=0, keepdims=True)
                        return carry2

                    lax.fori_loop(0, ch, t_body, 0)

        o = oi_scr[...] + jnp.concatenate(o_inter, axis=0)
        sq_hi, sq_lo = _split2(o * o)
        ss = jnp.dot(jnp.concatenate([sq_hi, sq_lo], axis=1), ones_bd2, preferred_element_type=F32)
        y = (o * lax.rsqrt(ss * (1.0 / B_DV) + EPS)) * gn
        g = g_ref[0, pl.ds(t0, rows), :].astype(F32)
        o_ref[0, pl.ds(t0, rows), :] = (y * _silu(g)).astype(o_ref.dtype)
        return carry

    lax.fori_loop(0, s // rows, group_body, 0)


def _hgrn(bq, bf, bi, bit, bg, lb, gn):
    bn, s, _ = bq.shape
    assert s % (HGRN_GROUP * HGRN_CHUNK) == 0 and 2 * HGRN_CHUNK == LANES and HGRN_GROUP % 2 == 0
    npair = B_DIM // LANES
    tok = pl.BlockSpec((1, s, LANES), lambda b, p: (b, 0, p))
    return pl.pallas_call(
        _hgrn_kernel,
        grid=(bn, npair),
        in_specs=[tok, tok, tok,
                  pl.BlockSpec((1, s // LANES, LANES, LANES), lambda b, p: (b, 0, p, 0)),
                  tok,
                  pl.BlockSpec((1, LANES), lambda b, p: (0, p)),
                  pl.BlockSpec((1, LANES), lambda b, p: (0, 0))],
        out_specs=tok,
        out_shape=jax.ShapeDtypeStruct((bn, s, B_DIM), BF16),
        scratch_shapes=[pltpu.VMEM((LANES, LANES), F32),
                        pltpu.VMEM((HGRN_GROUP * HGRN_CHUNK, LANES), F32)],
        compiler_params=_cparams("parallel", "parallel"),
    )(bq, bf, bi, bit, bg, lb, gn)


def _gmlp_mix(uv, vg, vb, ws_ref, bias):
    uv = 0.5 * uv * (1.0 + lax.erf(uv * (2.0 ** -0.5)))
    u = uv[:, :C_DIM]
    v = uv[:, C_DIM:]
    mu = jnp.mean(v, axis=-1, keepdims=True)
    var = jnp.mean(jnp.square(v - mu), axis=-1, keepdims=True)
    vn_b = (((v - mu) * lax.rsqrt(var + EPS)) * vg + vb).astype(BF16)
    r_t = lax.broadcasted_iota(I32, (C_CHUNK, C_CHUNK), 0)
    r_s = lax.broadcasted_iota(I32, (C_CHUNK, C_CHUNK), 1)
    group = lax.broadcasted_iota(I32, (C_CHUNK, C_DIM), 1) // C_GROUP_DIM
    ws = [jnp.where(r_t >= r_s, ws_ref[gi], 0.0).astype(BF16) for gi in range(C_GROUPS)]
    outs = []
    for c in range(uv.shape[0] // C_CHUNK):
        sl = slice(c * C_CHUNK, (c + 1) * C_CHUNK)
        mixed = jnp.zeros((C_CHUNK, C_DIM), F32)
        for gi in range(C_GROUPS):
            m = jnp.dot(ws[gi], vn_b[sl], preferred_element_type=F32)
            mixed = jnp.where(group == gi, m, mixed)
        outs.append(u[sl] * (mixed + bias))
    return jnp.concatenate(outs, axis=0)


def _outproj_kernel(x_ref, a_ref, b_ref, c_ref, wa_ref, wb_ref, wc_ref, g1_ref, o_ref):
    mix = (jnp.dot(a_ref[0], wa_ref[...], preferred_element_type=F32)
           + jnp.dot(b_ref[0], wb_ref[...], preferred_element_type=F32)
           + jnp.dot(c_ref[0], wc_ref[...], preferred_element_type=F32))
    o_ref[0] = x_ref[0] + g1_ref[0] * mix


def _outproj(x, a, b, c, w_out_l, g1, tm):
    bn, s, d = x.shape
    wa = w_out_l[:A_DIM].astype(BF16)
    wb = w_out_l[A_DIM:A_DIM + B_DIM].astype(BF16)
    wc = w_out_l[A_DIM + B_DIM:].astype(BF16)
    tok = lambda w: pl.BlockSpec((1, tm, w), lambda bb, t: (bb, t, 0))
    full = lambda arr: pl.BlockSpec(arr.shape, lambda bb, t: (0, 0))
    return pl.pallas_call(
        _outproj_kernel,
        grid=(bn, s // tm),
        in_specs=[tok(d), tok(A_DIM), tok(B_DIM), tok(C_DIM), full(wa), full(wb), full(wc),
                  pl.BlockSpec((1, 1, d), lambda bb, t: (bb, 0, 0))],
        out_specs=tok(d),
        out_shape=jax.ShapeDtypeStruct((bn, s, d), F32),
        compiler_params=_cparams("parallel", "parallel"),
    )(x, a, b, c, wa, wb, wc, g1)


def _ffn_kernel(x_ref, a_ref, b_ref, c_ref, wa_ref, wb_ref, wc_ref, g1_ref, sh_ref, sc_ref, gate_ref, g_ref,
                wg_ref, wu_ref, wd_ref, o_ref, xn_scr, h_scr, acc_scr):
    f = pl.program_id(2)

    @pl.when(f == 0)
    def _():
        mix = (jnp.dot(a_ref[0], wa_ref[...], preferred_element_type=F32)
               + jnp.dot(b_ref[0], wb_ref[...], preferred_element_type=F32)
               + jnp.dot(c_ref[0], wc_ref[...], preferred_element_type=F32))
        xn = x_ref[0] + g1_ref[0] * mix
        xn_scr[...] = xn
        h_scr[...] = _modulated_rmsnorm(xn, g_ref[...], sc_ref[0], sh_ref[0]).astype(BF16)
        acc_scr[...] = jnp.zeros_like(acc_scr)

    h = h_scr[...]
    a = jnp.dot(h, wg_ref[...], preferred_element_type=F32)
    u = jnp.dot(h, wu_ref[...], preferred_element_type=F32)
    act = (_silu(a) * u).astype(BF16)
    acc_scr[...] += jnp.dot(act, wd_ref[...], preferred_element_type=F32)

    @pl.when(f == pl.num_programs(2) - 1)
    def _():
        o_ref[0] = xn_scr[...] + gate_ref[0] * acc_scr[...]


def _ffn(x, a, b, c, w_out_l, g1, sh, sc, gate, g, wg, wu, wd, tm, tf):
    bn, s, d = x.shape
    fdim = wg.shape[1]
    wa = w_out_l[:A_DIM].astype(BF16)
    wb = w_out_l[A_DIM:A_DIM + B_DIM].astype(BF16)
    wc = w_out_l[A_DIM + B_DIM:].astype(BF16)
    tok = pl.BlockSpec((1, tm, d), lambda b, t, f: (b, t, 0))
    tokw = lambda w: pl.BlockSpec((1, tm, w), lambda b, t, f: (b, t, 0))
    full = lambda arr: pl.BlockSpec(arr.shape, lambda b, t, f: (0, 0))
    vec = pl.BlockSpec((1, 1, d), lambda b, t, f: (b, 0, 0))
    return pl.pallas_call(
        _ffn_kernel,
        grid=(bn, s // tm, fdim // tf),
        in_specs=[tok, tokw(A_DIM), tokw(B_DIM), tokw(C_DIM), full(wa), full(wb), full(wc), vec,
                  vec, vec, vec,
                  pl.BlockSpec((1, d), lambda b, t, f: (0, 0)),
                  pl.BlockSpec((d, tf), lambda b, t, f: (0, f)),
                  pl.BlockSpec((d, tf), lambda b, t, f: (0, f)),
                  pl.BlockSpec((tf, d), lambda b, t, f: (f, 0))],
        out_specs=tok,
        out_shape=jax.ShapeDtypeStruct((bn, s, d), F32),
        scratch_shapes=[pltpu.VMEM((tm, d), F32), pltpu.VMEM((tm, d), BF16), pltpu.VMEM((tm, d), F32)],
        compiler_params=_cparams("parallel", "parallel", "arbitrary"),
    )(x, a, b, c, wa, wb, wc, g1, sh, sc, gate, g, wg, wu, wd)


MOE_RC = 256
MOE_TAIL = 128
MOE_TB = 512


def _moe_kernel(final_norm, x_ref, sh_ref, sc_ref, gate_ref, g_ref, gfin_ref, wr_ref, wg_ref, wu_ref, wd_ref, o_ref,
                h_scr, xe_scr, ye_scr, gcol_scr, gates_scr, rsel_scr, rn_scr, rcol_scr, cnt_smem):
    e = pl.program_id(2)
    f = pl.program_id(3)
    tm = x_ref.shape[1]
    rc_rows = MOE_RC
    tb = MOE_TB
    ntb = tm // tb
    neg_inf = jnp.float32(-jnp.inf)

    @pl.when((e == 0) & (f == 0))
    def _route():
        for jb in range(ntb):
            rows = slice(jb * tb, (jb + 1) * tb)
            h = _modulated_rmsnorm(x_ref[0, rows, :], g_ref[...], sc_ref[0], sh_ref[0])
            h_scr[rows, :] = h.astype(BF16)
            h_hi, h_lo = _split2(h)
            rcol_scr[rows, :] = jnp.dot(jnp.concatenate([h_hi, h_hi, h_lo], axis=1), wr_ref[...],
                                        preferred_element_type=F32)
            o_ref[0, rows, :] = jnp.zeros((tb, o_ref.shape[2]), F32)
        logits = rcol_scr[...].T[:N_EXPERTS]
        ridx = lax.broadcasted_iota(I32, (N_EXPERTS, tm), 0)
        m1 = jnp.max(logits, axis=0, keepdims=True)
        i1 = jnp.min(jnp.where(logits == m1, ridx, N_EXPERTS), axis=0, keepdims=True)
        rest = jnp.where(ridx == i1, neg_inf, logits)
        m2 = jnp.max(rest, axis=0, keepdims=True)
        i2 = jnp.min(jnp.where(rest == m2, ridx, N_EXPERTS), axis=0, keepdims=True)
        e2 = jnp.exp(m2 - m1)
        den = 1.0 + e2
        gates_scr[...] = jnp.where(ridx == i1, 1.0 / den, jnp.where(ridx == i2, e2 / den, 0.0))
        sel = jnp.where(ridx == i1, 1.0, jnp.where(ridx == i2, 1.0, 0.0))
        ur = lax.broadcasted_iota(I32, (LANES, LANES), 0)
        uc = lax.broadcasted_iota(I32, (LANES, LANES), 1)
        utri = jnp.where(ur <= uc, 1.0, 0.0).astype(BF16)
        carry = jnp.zeros((N_EXPERTS, 1), F32)
        for kb in range(tm // LANES):
            sb = sel[:, kb * LANES:(kb + 1) * LANES]
            pref = jnp.dot(sb.astype(BF16), utri, preferred_element_type=F32) + carry
            rsel_scr[:, kb * LANES:(kb + 1) * LANES] = jnp.where(sb > 0.0, pref, -1.0)
            carry = pref[:, LANES - 1:LANES]
            if (kb + 1) * LANES == tm // 2:
                carry_half = carry
        r8 = lax.broadcasted_iota(I32, (N_EXPERTS, 1), 0)
        for ee in range(N_EXPERTS):
            cnt_smem[ee] = jnp.sum(jnp.where(r8 == ee, carry, 0.0)).astype(I32)
            cnt_smem[N_EXPERTS + ee] = jnp.sum(jnp.where(r8 == ee, carry_half, 0.0)).astype(I32)
        rn_scr[...] = jnp.concatenate(
            [rsel_scr[...], jnp.zeros((LANES - N_EXPERTS, tm), F32)], axis=0).T

    cnt = cnt_smem[e]
    rest = cnt % rc_rows
    has_tail = (rest > 0) & (rest <= MOE_TAIL)
    n_rc = cnt // rc_rows + jnp.where(rest > MOE_TAIL, 1, 0)
    tail_r0 = pl.multiple_of(n_rc * rc_rows, rc_rows)
    cnt_first = cnt_smem[N_EXPERTS + e]
    token_regions = ((0, tm), (0, tm // 2), (tm // 2, tm))

    def for_region(r0, nrows, fn):
        region_of_chunk = jnp.where(r0 + nrows <= cnt_first, 1, jnp.where(r0 >= cnt_first, 2, 0))
        for region, (lo, hi) in enumerate(token_regions):
            @pl.when(region_of_chunk == region)
            def _():
                fn(lo, hi)

    @pl.when(f == 0)
    def _gather():
        rsel_row = rsel_scr[pl.ds(e, 1), :]
        gate_row = gates_scr[pl.ds(e, 1), :]
        lane = lax.broadcasted_iota(I32, (tm, LANES), 1)
        rcol = jnp.sum(jnp.where(lane == e, rn_scr[...], 0.0), axis=1, keepdims=True)
        rcol_scr[...] = jnp.broadcast_to(rcol, (tm, LANES))

        def compact_chunk(r0, nrows):
            def compact(lo, hi):
                want = (r0 + 1 + lax.broadcasted_iota(I32, (nrows, hi - lo), 0)).astype(F32)
                pm = rsel_row[:, lo:hi] == want
                pb = jnp.where(pm, 1.0, 0.0).astype(BF16)
                xe_scr[pl.ds(r0, nrows), :] = jnp.dot(pb, h_scr[lo:hi, :],
                                                      preferred_element_type=F32).astype(BF16)
                gcol = jnp.sum(jnp.where(pm, gate_row[:, lo:hi], 0.0), axis=1, keepdims=True)
                gcol_scr[pl.ds(r0, nrows), :] = jnp.broadcast_to(gcol, (nrows, LANES))

            for_region(r0, nrows, compact)
            ye_scr[pl.ds(r0, nrows), :] = jnp.zeros((nrows, ye_scr.shape[1]), F32)

        def body(rc, carry):
            compact_chunk(pl.multiple_of(rc * rc_rows, rc_rows), rc_rows)
            return carry

        lax.fori_loop(0, n_rc, body, 0)

        @pl.when(has_tail)
        def _():
            compact_chunk(tail_r0, MOE_TAIL)

    def ffn_chunk(r0, nrows):
        xe = xe_scr[pl.ds(r0, nrows), :]
        a = jnp.dot(xe, wg_ref[0], preferred_element_type=F32)
        u = jnp.dot(xe, wu_ref[0], preferred_element_type=F32)
        act = (_silu(a) * u * gcol_scr[pl.ds(r0, nrows), 0:1]).astype(BF16)
        ye_scr[pl.ds(r0, nrows), :] += jnp.dot(act, wd_ref[0], preferred_element_type=F32)

    def ffn_body(rc, carry):
        ffn_chunk(pl.multiple_of(rc * rc_rows, rc_rows), rc_rows)
        return carry

    lax.fori_loop(0, n_rc, ffn_body, 0)

    @pl.when(has_tail)
    def _():
        ffn_chunk(tail_r0, MOE_TAIL)

    @pl.when(f == pl.num_programs(3) - 1)
    def _scatter():
        def scatter_chunk(r0, nrows):
            ye = ye_scr[pl.ds(r0, nrows), :].astype(BF16)

            def scatter(lo, hi):
                want = (r0 + 1 + lax.broadcasted_iota(I32, (hi - lo, nrows), 1)).astype(F32)
                pt = jnp.where(rcol_scr[lo:hi, 0:1] == want, 1.0, 0.0).astype(BF16)
                o_ref[0, lo:hi, :] += jnp.dot(pt, ye, preferred_element_type=F32)

            for_region(r0, nrows, scatter)

        def body(rc, carry):
            scatter_chunk(pl.multiple_of(rc * rc_rows, rc_rows), rc_rows)
            return carry

        lax.fori_loop(0, n_rc, body, 0)

        @pl.when(has_tail)
        def _():
            scatter_chunk(tail_r0, MOE_TAIL)

    @pl.when((e == pl.num_programs(2) - 1) & (f == pl.num_programs(3) - 1))
    def _residual():
        for jb in range(ntb):
            rows = slice(jb * tb, (jb + 1) * tb)
            y = x_ref[0, rows, :] + gate_ref[0] * o_ref[0, rows, :]
            if final_norm:
                var = jnp.mean(y * y, axis=-1, keepdims=True)
                y = (y * lax.rsqrt(var + EPS)) * gfin_ref[...]
            o_ref[0, rows, :] = y


def _moe(x, sh, sc, gate, g, w_router, wg, wu, wd, tm, tf, g_final=None):
    bn, s, d = x.shape
    ne, _, fdim = wg.shape
    assert ne == N_EXPERTS and tm % MOE_RC == 0 and tm % MOE_TB == 0 and (tm // 2) % LANES == 0
    wr_hi, wr_lo = _split2(jnp.zeros((d, LANES), F32).at[:, :ne].set(w_router))
    wr = jnp.concatenate([wr_hi, wr_lo, wr_hi], axis=0)
    final_norm = g_final is not None
    gfin = g_final if final_norm else jnp.ones((1, d), F32)
    tok_in = pl.BlockSpec((1, tm, d), lambda b, t, e, f: (b, t, 0), pipeline_mode=pl.Buffered(1))
    tok_out = pl.BlockSpec((1, tm, d), lambda b, t, e, f: (b, t, 0), pipeline_mode=pl.Buffered(1))
    vec = pl.BlockSpec((1, 1, d), lambda b, t, e, f: (b, 0, 0))
    return pl.pallas_call(
        functools.partial(_moe_kernel, final_norm),
        grid=(bn, s // tm, ne, fdim // tf),
        in_specs=[tok_in, vec, vec, vec,
                  pl.BlockSpec((1, d), lambda b, t, e, f: (0, 0)),
                  pl.BlockSpec((1, d), lambda b, t, e, f: (0, 0)),
                  pl.BlockSpec((3 * d, LANES), lambda b, t, e, f: (0, 0)),
                  pl.BlockSpec((1, d, tf), lambda b, t, e, f: (e, 0, f)),
                  pl.BlockSpec((1, d, tf), lambda b, t, e, f: (e, 0, f)),
                  pl.BlockSpec((1, tf, d), lambda b, t, e, f: (e, f, 0))],
        out_specs=tok_out,
        out_shape=jax.ShapeDtypeStruct((bn, s, d), F32),
        scratch_shapes=[pltpu.VMEM((tm, d), BF16),
                        pltpu.VMEM((tm, d), BF16),
                        pltpu.VMEM((tm, d), F32),
                        pltpu.VMEM((tm, LANES), F32),
                        pltpu.VMEM((ne, tm), F32),
                        pltpu.VMEM((ne, tm), F32),
                        pltpu.VMEM((tm, LANES), F32),
                        pltpu.VMEM((tm, LANES), F32),
                        pltpu.SMEM((2 * ne,), I32)],
        compiler_params=pltpu.CompilerParams(
            dimension_semantics=("parallel", "parallel", "arbitrary", "arbitrary"),
            vmem_limit_bytes=56 * 1024 * 1024),
    )(x, sh, sc, gate, g, gfin, wr, wg, wu, wd)


def _final_norm_kernel(x_ref, g_ref, o_ref):
    x = x_ref[0]
    var = jnp.mean(x * x, axis=-1, keepdims=True)
    o_ref[0] = (x * lax.rsqrt(var + EPS)) * g_ref[...]


def _final_norm(x, g, tm):
    bn, s, d = x.shape
    tok = pl.BlockSpec((1, tm, d), lambda b, t: (b, t, 0))
    return pl.pallas_call(
        _final_norm_kernel,
        grid=(bn, s // tm),
        in_specs=[tok, pl.BlockSpec((1, d), lambda b, t: (0, 0))],
        out_specs=tok,
        out_shape=jax.ShapeDtypeStruct((bn, s, d), F32),
        compiler_params=_cparams("parallel", "parallel"),
    )(x, g)


def _token_tile(s, want):
    tm = min(want, s)
    assert s % tm == 0 and tm % QB == 0
    return tm


def kernel(x, c, positions, w_ada, b_ada, g_norm_mix, g_norm_ffn, w_in, w_out, hgrn_lb_logits, hgrn_out_norm, gmlp_vnorm_g, gmlp_vnorm_b, gmlp_w_s, gmlp_b_s, ffn_w_gate, ffn_w_up, ffn_w_down, moe_w_router, moe_w_gate, moe_w_up, moe_w_down, g_final):
    bn, s, d = x.shape
    depth = w_in.shape[0]
    assert s % QB == 0 and d == sum(IN_WIDTHS[:1]) + B_DIM + C_DIM
    tm_proj = _token_tile(s, 512)
    tm_ffn = _token_tile(s, 1024)

    p_lb = jax.nn.softmax(hgrn_lb_logits.astype(F32), axis=0)
    cum = jnp.cumsum(p_lb, axis=0)
    lower_bounds = cum - cum[0:1]

    mod = _ada_mod(c, w_ada, b_ada)
    cosn, sinn, cost, sint = _rope_tables(positions)

    for l in range(depth):
        sh1, sc1, g1, sh2, sc2, g2 = [mod[l, :, i * d:(i + 1) * d].reshape(bn, 1, d) for i in range(6)]
        w_n, w_t = _prep_in_weights(w_in[l])
        (kz, ik3, bq, bf, bi, bg, c_out, qt, iq3t, vt4, iwt, bit) = _inproj(
            x, sh1, sc1, g_norm_mix[l].reshape(1, d), w_n, w_t, cosn, sinn, cost, sint,
            gmlp_vnorm_g[l].reshape(1, C_DIM), gmlp_vnorm_b[l].reshape(1, C_DIM), gmlp_w_s[l], gmlp_b_s[l],
            tm_proj)
        a_out = _dsa(ik3, kz, vt4, iq3t, iwt, qt)
        b_out = _hgrn(bq, bf, bi, bit, bg, lower_bounds[l].reshape(1, B_DIM),
                      jnp.tile(hgrn_out_norm[l], LANES // B_DV).reshape(1, LANES))
        gf = g_norm_ffn[l].reshape(1, d)
        if l % 2 == 0:
            i = l // 2
            x = _ffn(x, a_out, b_out, c_out, w_out[l], g1, sh2, sc2, g2, gf, ffn_w_gate[i].astype(BF16),
                     ffn_w_up[i].astype(BF16), ffn_w_down[i].astype(BF16), tm_ffn, 256)
        else:
            i = l // 2
            x = _outproj(x, a_out, b_out, c_out, w_out[l], g1, tm_proj)
            fused_final = l == depth - 1
            x = _moe(x, sh2, sc2, g2, gf, moe_w_router[i], moe_w_gate[i].astype(BF16),
                     moe_w_up[i].astype(BF16), moe_w_down[i].astype(BF16), _token_tile(s, 2048), 896,
                     g_final.reshape(1, d) if fused_final else None)
    if depth % 2 == 0:
        return x
    return _final_norm(x, g_final.reshape(1, d), tm_proj)
```
